```python
import functools
import jax
import jax.numpy as jnp
from jax import lax
import numpy as np

D_MODEL = 1024
BATCH = 16
SEQ = 4096
DEPTH = 2
DEC_BATCH = 8
DEC_SEQ = 32
PAST_LEN = 2048

CHUNK = 64
LEFT_CHUNKS = 8
LEFT_CONTEXT = LEFT_CHUNKS * CHUNK
BAND = LEFT_CONTEXT + CHUNK
ATT_WIDTH = D_MODEL // 2
ATT_HEADS = 8
ATT_HEAD_DIM = ATT_WIDTH // ATT_HEADS
MAX_REL = 2 * CHUNK
HGRN_WIDTH = D_MODEL - ATT_WIDTH
HGRN_EXPAND = 128
HGRN_HEADS = HGRN_WIDTH // HGRN_EXPAND
HGRN_DK = HGRN_EXPAND
HGRN_DV = HGRN_WIDTH // HGRN_HEADS
PROJ_SIZES = (ATT_WIDTH, ATT_WIDTH, ATT_WIDTH, HGRN_WIDTH, HGRN_WIDTH, HGRN_WIDTH, HGRN_WIDTH)
PROJ_WIDTH = 3 * ATT_WIDTH + 4 * HGRN_WIDTH
MIX_WIDTH = ATT_WIDTH + HGRN_WIDTH
N_EXPERTS = 32
TOP_K = 4
D_EXPERT = D_MODEL
SWIGLU_ALPHA = 1.702
SWIGLU_LIMIT = 7.0
MOE_BLOCK = 256
N_MOD = 6
EPS = 1e-6

kernel_name = 'streaming_hymba_hgrn2_moe_step'


def _rmsnorm(x, g):
    xf = x.astype(jnp.float32)
    y = xf * lax.rsqrt(jnp.mean(xf * xf, axis=-1, keepdims=True) + EPS)
    return (y * g.astype(jnp.float32)).astype(x.dtype)


def _modulation(c, w_ada_l, b_ada_l):
    m = jax.nn.silu(c) @ w_ada_l + b_ada_l
    return jnp.split(m[:, None, :], N_MOD, axis=-1)


def _rel_bias(table, rel):
    return table[:, jnp.clip(rel, -MAX_REL, MAX_REL) + MAX_REL]


def _band_attend(q, k, v, bias, valid):
    s = jnp.einsum('bqhd,bkhd->bhqk', q, k).astype(jnp.float32) * (ATT_HEAD_DIM ** -0.5)
    s = jnp.where(valid, s + bias.astype(jnp.float32), -jnp.inf)
    p = jax.nn.softmax(s, axis=-1).astype(v.dtype)
    return jnp.einsum('bhqk,bkhd->bqhd', p, v)


def _attn_prompt(q, k, v, table):
    b, s, h, d = q.shape
    nc = s // CHUNK
    pad = ((0, 0), (LEFT_CONTEXT, 0), (0, 0), (0, 0))
    kp = jnp.pad(k, pad)
    vp = jnp.pad(v, pad)
    t = jnp.arange(CHUNK)
    j = jnp.arange(BAND)
    bias = _rel_bias(table, LEFT_CONTEXT + t[:, None] - j[None, :])
    qc = q.reshape(b, nc, CHUNK, h, d).swapaxes(0, 1)

    def one_chunk(args):
        n, qn = args
        kn = lax.dynamic_slice_in_dim(kp, n * CHUNK, BAND, axis=1)
        vn = lax.dynamic_slice_in_dim(vp, n * CHUNK, BAND, axis=1)
        valid = (n * CHUNK + j >= LEFT_CONTEXT)[None, :]
        return _band_attend(qn, kn, vn, bias, valid)

    out = lax.map(one_chunk, (jnp.arange(nc), qc))
    return out.swapaxes(0, 1).reshape(b, s, h, d)


def _hgrn_chunk(state, q, k, v, logf):
    length = q.shape[1]
    cum = jnp.cumsum(logf, axis=1)
    causal = jnp.tril(jnp.ones((length, length), bool))[None, :, :, None, None]
    decay = jnp.exp(jnp.where(causal, cum[:, :, None] - cum[:, None, :], -jnp.inf))
    scores = jnp.einsum('bthk,btshk,bshk->bhts', q, decay, k)
    o = (jnp.einsum('bhts,bshv->bthv', scores, v)
         + jnp.einsum('bthk,bhkv->bthv', q * jnp.exp(cum), state))
    last = cum[:, -1]
    state = (jnp.exp(last)[..., None] * state
             + jnp.einsum('bshk,bshv->bhkv', k * jnp.exp(last[:, None] - cum), v))
    return state, o


def _hgrn_prompt(q, k, v, logf):
    b, s = q.shape[:2]
    nc = s // CHUNK

    def to_chunks(a):
        return a.reshape(b, nc, CHUNK, *a.shape[2:]).swapaxes(0, 1)

    s0 = jnp.zeros((b, HGRN_HEADS, HGRN_DK, HGRN_DV), jnp.float32)
    s_fin, o = lax.scan(lambda st, xs: _hgrn_chunk(st, *xs), s0,
                        (to_chunks(q), to_chunks(k), to_chunks(v), to_chunks(logf)))
    return o.swapaxes(0, 1).reshape(b, s, HGRN_HEADS, HGRN_DV), s_fin


def _mixer_inputs(h, w_in_l, lb_l):
    b, l = h.shape[:2]
    p = h @ w_in_l
    qa, ka, va, qh, fh, ih, gh = jnp.split(p, np.cumsum(PROJ_SIZES)[:-1].tolist(), axis=-1)

    def att(a):
        return a.reshape(b, l, ATT_HEADS, ATT_HEAD_DIM)

    def rec(a):
        return a.reshape(b, l, HGRN_HEADS, -1)

    f32 = fh.astype(jnp.float32)
    logf = jnp.log(lb_l + (1.0 - lb_l) * jax.nn.sigmoid(f32))
    k_h = (1.0 - lb_l) * jax.nn.sigmoid(-f32)
    q_h = jax.nn.silu(qh.astype(jnp.float32))
    return att(qa), att(ka), att(va), rec(q_h), rec(k_h), rec(logf), rec(ih.astype(jnp.float32)), gh


def _merge(o_att, o_rec, gh, g_att_l, g_rec_l, w_out_l):
    b, l = gh.shape[:2]
    a = _rmsnorm(o_att, g_att_l.reshape(ATT_HEADS, ATT_HEAD_DIM)).reshape(b, l, ATT_WIDTH)
    r = _rmsnorm(o_rec.astype(gh.dtype), g_rec_l.reshape(HGRN_HEADS, HGRN_DV)).reshape(b, l, HGRN_WIDTH)
    r = r * jax.nn.silu(gh)
    return jnp.concatenate([a, r], axis=-1) @ w_out_l


def _mixer_prompt(h, w_in_l, lb_l, table_l, g_att_l, g_rec_l, w_out_l):
    qa, ka, va, qh, kh, lf, vh, gh = _mixer_inputs(h, w_in_l, lb_l)
    o_att = _attn_prompt(qa, ka, va, table_l)
    o_rec, s_fin = _hgrn_prompt(qh, kh, vh, lf)
    w = min(LEFT_CONTEXT, h.shape[1])
    return _merge(o_att, o_rec, gh, g_att_l, g_rec_l, w_out_l), (ka[:, -w:], va[:, -w:], s_fin)


def _mixer_sample(h, cache_k_l, cache_v_l, state_l, w_in_l, lb_l, table_l, g_att_l, g_rec_l, w_out_l):
    qa, ka, va, qh, kh, lf, vh, gh = _mixer_inputs(h, w_in_l, lb_l)
    w = cache_k_l.shape[1]
    l = h.shape[1]
    keys = jnp.concatenate([cache_k_l.astype(ka.dtype), ka], axis=1)
    vals = jnp.concatenate([cache_v_l.astype(va.dtype), va], axis=1)
    rel = (w + jnp.arange(l))[:, None] - jnp.arange(w + l)[None, :]
    o_att = _band_attend(qa, keys, vals, _rel_bias(table_l, rel), jnp.ones((l, w + l), bool))
    s_new, o_rec = _hgrn_chunk(state_l.astype(jnp.float32), qh, kh, vh, lf)
    return _merge(o_att, o_rec, gh, g_att_l, g_rec_l, w_out_l), (ka, va, s_new)


def _moe(h2, w_r, b_r, w1, b1, w2, b2):
    n, d = h2.shape
    logits = (h2 @ w_r).astype(jnp.float32) + b_r.astype(jnp.float32)
    top_val, top_idx = lax.top_k(logits, TOP_K)
    gate = jax.nn.softmax(top_val, axis=-1)
    m = n * TOP_K
    e_flat = top_idx.reshape(m)
    tok_flat = jnp.arange(m, dtype=jnp.int32) // TOP_K
    order = jnp.argsort(e_flat)
    e_sorted = e_flat[order]
    counts = jnp.bincount(e_flat, length=N_EXPERTS)
    start = jnp.cumsum(counts) - counts
    padded = (counts + MOE_BLOCK - 1) // MOE_BLOCK * MOE_BLOCK
    pad_end = jnp.cumsum(padded)
    pad_start = pad_end - padded
    dest = pad_start[e_sorted] + (jnp.arange(m) - start[e_sorted])
    nb = (m + N_EXPERTS * (MOE_BLOCK - 1) + MOE_BLOCK - 1) // MOE_BLOCK
    p = nb * MOE_BLOCK
    slot_tok = jnp.full((p,), n, jnp.int32).at[dest].set(tok_flat[order])
    slot_gate = jnp.zeros((p,), jnp.float32).at[dest].set(gate.reshape(m)[order])
    block_expert = jnp.minimum(
        jnp.searchsorted(pad_end, jnp.arange(nb) * MOE_BLOCK, side='right'), N_EXPERTS - 1)
    h_ext = jnp.concatenate([h2, jnp.zeros((1, d), h2.dtype)], axis=0)

    def run_block(args):
        e, tok, g = args
        gu = h_ext[tok] @ w1[e] + b1[e]
        gt, up = jnp.split(gu, 2, axis=-1)
        gt = jnp.minimum(gt, SWIGLU_LIMIT)
        up = jnp.clip(up, -SWIGLU_LIMIT, SWIGLU_LIMIT)
        act = gt * jax.nn.sigmoid(SWIGLU_ALPHA * gt) * (up + 1.0)
        return (act @ w2[e] + b2[e]) * g[:, None]

    out = lax.map(run_block, (block_expert, slot_tok.reshape(nb, MOE_BLOCK),
                              slot_gate.reshape(nb, MOE_BLOCK)))
    y = jax.ops.segment_sum(out.reshape(p, d), slot_tok, num_segments=n + 1)[:n]
    return y.astype(h2.dtype)


def _block(x, c, mixer, norm_mix_l, norm_ffn_l, w_ada_l, b_ada_l, moe_w):
    sh1, sc1, g1, sh2, sc2, g2 = _modulation(c, w_ada_l, b_ada_l)
    h = _rmsnorm(x, norm_mix_l) * (1.0 + sc1) + sh1
    mix_out, new_state = mixer(h)
    x = x + g1 * mix_out
    h = _rmsnorm(x, norm_ffn_l) * (1.0 + sc2) + sh2
    b, l, d = h.shape
    x = x + g2 * _moe(h.reshape(b * l, d), *moe_w).reshape(b, l, d)
    return x, new_state


def setup_inputs(seed: int = 0) -> dict:
    key = jax.random.key(seed)
    ks = jax.random.split(key, 24)
    f32 = jnp.float32

    def nrm(k, shape, scale):
        return scale * jax.random.normal(k, shape, f32)

    w_cache = min(LEFT_CONTEXT, PAST_LEN)
    return {
        'x_prompt': nrm(ks[0], (BATCH, SEQ, D_MODEL), 1.0),
        'x_sample': nrm(ks[1], (DEC_BATCH, DEC_SEQ, D_MODEL), 1.0),
        'c_prompt': nrm(ks[2], (BATCH, D_MODEL), 1.0),
        'c_sample': nrm(ks[3], (DEC_BATCH, D_MODEL), 1.0),
        'cache_k': nrm(ks[4], (DEPTH, DEC_BATCH, w_cache, ATT_HEADS, ATT_HEAD_DIM), 1.0),
        'cache_v': nrm(ks[5], (DEPTH, DEC_BATCH, w_cache, ATT_HEADS, ATT_HEAD_DIM), 1.0),
        'state_hgrn': nrm(ks[6], (DEPTH, DEC_BATCH, HGRN_HEADS, HGRN_DK, HGRN_DV), 0.5),
        'norm_mix': 1.0 + nrm(ks[7], (DEPTH, D_MODEL), 0.05),
        'norm_ffn': 1.0 + nrm(ks[8], (DEPTH, D_MODEL), 0.05),
        'norm_final': 1.0 + nrm(ks[9], (D_MODEL,), 0.05),
        'w_ada': nrm(ks[10], (DEPTH, D_MODEL, N_MOD * D_MODEL), 0.5 * D_MODEL ** -0.5),
        'b_ada': nrm(ks[11], (DEPTH, N_MOD * D_MODEL), 0.02),
        'w_in': nrm(ks[12], (DEPTH, D_MODEL, PROJ_WIDTH), D_MODEL ** -0.5),
        'rel_bias': nrm(ks[13], (DEPTH, ATT_HEADS, 2 * MAX_REL + 1), 0.1),
        'hgrn_lb_logits': nrm(ks[14], (DEPTH, HGRN_WIDTH), 1.0),
        'g_attn_out': 1.0 + nrm(ks[15], (DEPTH, ATT_WIDTH), 0.05),
        'g_hgrn_out': 1.0 + nrm(ks[16], (DEPTH, HGRN_WIDTH), 0.05),
        'w_out': nrm(ks[17], (DEPTH, MIX_WIDTH, D_MODEL), MIX_WIDTH ** -0.5),
        'w_router': nrm(ks[18], (DEPTH, D_MODEL, N_EXPERTS), D_MODEL ** -0.5),
        'b_router': nrm(ks[19], (DEPTH, N_EXPERTS), 0.01),
        'w_e1': nrm(ks[20], (DEPTH, N_EXPERTS, D_MODEL, 2 * D_EXPERT), D_MODEL ** -0.5),
        'b_e1': nrm(ks[21], (DEPTH, N_EXPERTS, 2 * D_EXPERT), 0.02),
        'w_e2': nrm(ks[22], (DEPTH, N_EXPERTS, D_EXPERT, D_MODEL), D_EXPERT ** -0.5),
        'b_e2': nrm(ks[23], (DEPTH, N_EXPERTS, D_MODEL), 0.02),
    }


def reference(x_prompt, x_sample, c_prompt, c_sample, cache_k, cache_v, state_hgrn,
              norm_mix, norm_ffn, norm_final, w_ada, b_ada, w_in, rel_bias, hgrn_lb_logits,
              g_attn_out, g_hgrn_out, w_out, w_router, b_router, w_e1, b_e1, w_e2, b_e2):
    lbs = jax.nn.softmax(hgrn_lb_logits.astype(jnp.float32), axis=0)
    lbs = jnp.cumsum(lbs, axis=0) - lbs[0]

    y_p, y_s = x_prompt, x_sample
    kp_l, vp_l, sp_l, ks_l, vs_l, ss_l = [], [], [], [], [], []
    for layer in range(DEPTH):
        shared = dict(w_in_l=w_in[layer], lb_l=lbs[layer], table_l=rel_bias[layer],
                      g_att_l=g_attn_out[layer], g_rec_l=g_hgrn_out[layer], w_out_l=w_out[layer])
        moe_w = (w_router[layer], b_router[layer], w_e1[layer], b_e1[layer], w_e2[layer], b_e2[layer])
        mix_p = functools.partial(_mixer_prompt, **shared)
        mix_s = functools.partial(_mixer_sample, cache_k_l=cache_k[layer], cache_v_l=cache_v[layer],
                                  state_l=state_hgrn[layer], **shared)
        y_p, (kp, vp, sp) = _block(y_p, c_prompt, mix_p, norm_mix[layer], norm_ffn[layer],
                                   w_ada[layer], b_ada[layer], moe_w)
        y_s, (kn, vn, sn) = _block(y_s, c_sample, mix_s, norm_mix[layer], norm_ffn[layer],
                                   w_ada[layer], b_ada[layer], moe_w)
        kp_l.append(kp)
        vp_l.append(vp)
        sp_l.append(sp)
        ks_l.append(kn)
        vs_l.append(vn)
        ss_l.append(sn)

    y_prompt = _rmsnorm(y_p, norm_final)
    y_sample = _rmsnorm(y_s, norm_final)
    new_k_prompt = jnp.stack(kp_l)
    new_v_prompt = jnp.stack(vp_l)
    new_state_prompt = jnp.stack(sp_l)
    new_k_sample = jnp.stack(ks_l)
    new_v_sample = jnp.stack(vs_l)
    new_state_sample = jnp.stack(ss_l)
    return (y_prompt, y_sample, new_k_prompt, new_v_prompt, new_state_prompt,
            new_k_sample, new_v_sample, new_state_sample)
```

```python
import functools

import jax
import jax.numpy as jnp
from jax import lax
from jax.experimental import pallas as pl
from jax.experimental.pallas import tpu as pltpu

F32 = jnp.float32
BF16 = jnp.bfloat16
I32 = jnp.int32

EPS = 1e-6
CHUNK = 64
LEFT_CHUNKS = 8
LEFT = LEFT_CHUNKS * CHUNK
MAX_REL = 2 * CHUNK
ATT_HEADS = 8
ATT_DIM = 64
ATT_W = ATT_HEADS * ATT_DIM
REC_HEADS = 4
REC_DIM = 128
REC_W = REC_HEADS * REC_DIM
N_EXPERTS = 32
TOP_K = 4
SWIGLU_ALPHA = 1.702
SWIGLU_LIMIT = 7.0
N_MOD = 6
NEG = -1e30

LANES = 128
ATT_QSUB = 128
ATT_WIN = LEFT + ATT_QSUB
SEQ_TILE = 512
MOE_ROWS = 512
VMEM_LIMIT = 56 * 1024 * 1024


def _cparams(*sem):
    return pltpu.CompilerParams(dimension_semantics=sem, vmem_limit_bytes=VMEM_LIMIT)


def _split3(x):
    hi = x.astype(BF16)
    r1 = x - hi.astype(F32)
    mid = r1.astype(BF16)
    lo = (r1 - mid.astype(F32)).astype(BF16)
    return hi, mid, lo


def _mod_kernel(c_ref, w_ref, b_ref, o_ref):
    c = c_ref[...]
    act = (c * jax.nn.sigmoid(c)).astype(BF16)
    o_ref[0] = jnp.dot(act, w_ref[0].astype(BF16), preferred_element_type=F32) + b_ref[0]


def _modulation(c_all, w_ada, b_ada):
    depth, d, n6 = w_ada.shape
    rows = c_all.shape[0]
    tn = 1536
    return pl.pallas_call(
        _mod_kernel,
        grid=(depth, n6 // tn),
        in_specs=[pl.BlockSpec((rows, d), lambda l, j: (0, 0)),
                  pl.BlockSpec((1, d, tn), lambda l, j: (l, 0, j)),
                  pl.BlockSpec((1, 1, tn), lambda l, j: (l, 0, j))],
        out_specs=pl.BlockSpec((1, rows, tn), lambda l, j: (l, 0, j)),
        out_shape=jax.ShapeDtypeStruct((depth, rows, n6), F32),
        compiler_params=_cparams("parallel", "parallel"),
        name="adaln_modulation",
    )(c_all, w_ada, b_ada.reshape(depth, 1, n6))


def _inproj_kernel(has_res, *refs):
    if has_res:
        (x_ref, moe_ref, g2_ref, nw_ref, sc_ref, sh_ref, w_ref, lb_ref,
         xo_ref, qa_ref, ka_ref, va_ref, qh_ref, kh_ref, lf_ref, vh_ref, gh_ref) = refs
        x = x_ref[0] + g2_ref[0] * moe_ref[0]
        xo_ref[0] = x
    else:
        (x_ref, nw_ref, sc_ref, sh_ref, w_ref, lb_ref,
         qa_ref, ka_ref, va_ref, qh_ref, kh_ref, lf_ref, vh_ref, gh_ref) = refs
        x = x_ref[0]
    ms = jnp.mean(x * x, axis=-1, keepdims=True)
    h = (x * lax.rsqrt(ms + EPS) * nw_ref[...]) * (1.0 + sc_ref[0]) + sh_ref[0]
    hb = h.astype(BF16)

    def proj(g):
        return jnp.dot(hb, w_ref[:, g * ATT_W:(g + 1) * ATT_W], preferred_element_type=F32)

    qa_ref[0] = (proj(0) * (ATT_DIM ** -0.5)).astype(BF16)
    ka_ref[0] = proj(1)
    va_ref[0] = proj(2)
    qh = proj(3)
    qh_ref[0] = qh * jax.nn.sigmoid(qh)
    f = proj(4)
    lb = lb_ref[...]
    lf_ref[0] = jnp.log(lb + (1.0 - lb) * jax.nn.sigmoid(f))
    kh_ref[0] = (1.0 - lb) * jax.nn.sigmoid(-f)
    vh_ref[0] = proj(5)
    gh_ref[0] = proj(6)


def _inproj(x, moe, g2, nw, sc, sh, w_bf, lb, ts):
    b, s, d = x.shape
    pw = w_bf.shape[1]
    has_res = moe is not None
    row = pl.BlockSpec((1, ts, d), lambda i, j: (i, j, 0))
    per_b = pl.BlockSpec((1, 1, d), lambda i, j: (i, 0, 0))
    half = pl.BlockSpec((1, ts, ATT_W), lambda i, j: (i, j, 0))
    in_specs = [row] + ([row, per_b] if has_res else []) + [
        pl.BlockSpec((1, d), lambda i, j: (0, 0)), per_b, per_b,
        pl.BlockSpec((d, pw), lambda i, j: (0, 0)),
        pl.BlockSpec((1, REC_W), lambda i, j: (0, 0))]
    args = [x] + ([moe, g2] if has_res else []) + [nw.reshape(1, d), sc, sh, w_bf, lb.reshape(1, REC_W)]
    hs = jax.ShapeDtypeStruct((b, s, ATT_W), F32)
    out_shape = ([jax.ShapeDtypeStruct((b, s, d), F32)] if has_res else []) + [
        jax.ShapeDtypeStruct((b, s, ATT_W), BF16), hs, hs, hs, hs, hs, hs, hs]
    out_specs = ([row] if has_res else []) + [half] * 8
    outs = pl.pallas_call(
        functools.partial(_inproj_kernel, has_res),
        grid=(b, s // ts),
        in_specs=in_specs, out_specs=out_specs, out_shape=out_shape,
        compiler_params=_cparams("parallel", "parallel"),
        name="norm_inproj",
    )(*args)
    if not has_res:
        outs = [x] + list(outs)
    return outs


def _attend(q, kw, vw, bias_ref, col_thr):
    lq, lk = q.shape[0], kw.shape[0]
    lane = lax.broadcasted_iota(I32, (lq, LANES), 1)
    col = lax.broadcasted_iota(I32, (lq, lk), 1)
    keep = col >= col_thr
    outs = []
    for p in range(ATT_HEADS // 2):
        sl = slice(p * LANES, (p + 1) * LANES)
        qp, kp, vp = q[:, sl], kw[:, sl], vw[:, sl]
        o_pair = None
        for half in range(2):
            in_head = (lane >= ATT_DIM) if half else (lane < ATT_DIM)
            qm = jnp.where(in_head, qp, jnp.zeros_like(qp))
            s = lax.dot_general(qm, kp, (((1,), (1,)), ((), ())), preferred_element_type=F32)
            s = jnp.where(keep, s + bias_ref[2 * p + half], NEG)
            m = jnp.max(s, axis=-1, keepdims=True)
            e = jnp.exp(s - m)
            l = jnp.sum(e, axis=-1, keepdims=True)
            o = jnp.dot(e.astype(BF16), vp, preferred_element_type=F32) / l
            o_pair = o if o_pair is None else jnp.where(in_head, o, o_pair)
        outs.append(o_pair)
    return jnp.concatenate(outs, axis=-1)


def _attn_prompt_kernel(q_ref, kp_ref, kc_ref, vp_ref, vc_ref, bias_ref, o_ref, kw_ref, vw_ref):
    i = pl.program_id(1)
    ts = q_ref.shape[1]
    kw_ref[0:ts, :] = kp_ref[0].astype(BF16)
    kw_ref[ts:2 * ts, :] = kc_ref[0].astype(BF16)
    vw_ref[0:ts, :] = vp_ref[0].astype(BF16)
    vw_ref[ts:2 * ts, :] = vc_ref[0].astype(BF16)

    def body(r, carry):
        r0 = pl.multiple_of(r * ATT_QSUB, ATT_QSUB)
        q = q_ref[0, pl.ds(r0, ATT_QSUB), :]
        kw = kw_ref[pl.ds(r0, ATT_WIN), :]
        vw = vw_ref[pl.ds(r0, ATT_WIN), :]
        thr = jnp.where(i > 0, 0, ts - r0)
        o_ref[0, pl.ds(r0, ATT_QSUB), :] = _attend(q, kw, vw, bias_ref, thr)
        return carry

    lax.fori_loop(0, ts // ATT_QSUB, body, 0)


def _attn_prompt(qa, ka, va, bias):
    b, s, w = qa.shape
    ts = LEFT
    cur = lambda i, j: (i, j, 0)
    prev = lambda i, j: (i, jnp.maximum(j - 1, 0), 0)
    blk = lambda im: pl.BlockSpec((1, ts, w), im)
    return pl.pallas_call(
        _attn_prompt_kernel,
        grid=(b, s // ts),
        in_specs=[blk(cur), blk(prev), blk(cur), blk(prev), blk(cur),
                  pl.BlockSpec(bias.shape, lambda i, j: (0, 0, 0))],
        out_specs=blk(cur),
        out_shape=jax.ShapeDtypeStruct((b, s, w), F32),
        scratch_shapes=[pltpu.VMEM((2 * ts, w), BF16), pltpu.VMEM((2 * ts, w), BF16)],
        compiler_params=_cparams("parallel", "parallel"),
        name="band_attention",
    )(qa, ka, ka, va, va, bias)


def _attn_sample_kernel(q_ref, k_ref, v_ref, bias_ref, o_ref):
    o_ref[0] = _attend(q_ref[0], k_ref[0], v_ref[0], bias_ref, 0)


def _attn_sample(qa, keys, vals, bias):
    b, l, w = qa.shape
    lk = keys.shape[1]
    return pl.pallas_call(
        _attn_sample_kernel,
        grid=(b,),
        in_specs=[pl.BlockSpec((1, l, w), lambda i: (i, 0, 0)),
                  pl.BlockSpec((1, lk, w), lambda i: (i, 0, 0)),
                  pl.BlockSpec((1, lk, w), lambda i: (i, 0, 0)),
                  pl.BlockSpec(bias.shape, lambda i: (0, 0, 0))],
        out_specs=pl.BlockSpec((1, l, w), lambda i: (i, 0, 0)),
        out_shape=jax.ShapeDtypeStruct((b, l, w), F32),
        compiler_params=_cparams("parallel"),
        name="cache_attention",
    )(qa, keys, vals, bias)


def _prompt_bias(table):
    t = jnp.arange(ATT_QSUB)[:, None]
    j = jnp.arange(ATT_WIN)[None, :]
    start = (t // CHUNK) * CHUNK
    in_band = (j >= start) & (j < start + LEFT + CHUNK)
    rel = jnp.clip(LEFT + t - j, -MAX_REL, MAX_REL) + MAX_REL
    return jnp.where(in_band[None], table[:, rel], NEG).astype(F32)


def _sample_bias(table, l, w, lk):
    t = jnp.arange(l)[:, None]
    j = jnp.arange(lk)[None, :]
    rel = jnp.clip(w + t - j, -MAX_REL, MAX_REL) + MAX_REL
    return jnp.where((j < w + l)[None], table[:, rel], NEG).astype(F32)


def _hgrn_kernel(chunk, q_ref, k_ref, lf_ref, v_ref, s0_ref, o_ref, sf_ref, st_ref):
    i = pl.program_id(1)
    ts = q_ref.shape[1]
    n_chunks = ts // chunk

    @pl.when(i == 0)
    def _():
        st_ref[...] = s0_ref[0]

    row = lax.broadcasted_iota(I32, (chunk, chunk), 0)
    colm = lax.broadcasted_iota(I32, (chunk, chunk), 1)
    tri = (colm <= row).astype(BF16)
    xor = jnp.bitwise_xor(row, colm)
    lower = colm < row
    rid = lax.broadcasted_iota(I32, (chunk, REC_W), 0)

    levels = []
    b = chunk // 2
    while b >= 1:
        levels.append(b)
        b //= 2

    def boundary(cum, b):
        if 2 * b >= 8:
            pieces = [jnp.broadcast_to(cum[m * 2 * b + b - 1:m * 2 * b + b, :], (2 * b, REC_W))
                      for m in range(chunk // (2 * b))]
            return pieces[0] if len(pieces) == 1 else jnp.concatenate(pieces, axis=0)
        if b == 2:
            ph = jnp.bitwise_and(rid, 3)
            up1 = pltpu.roll(cum, chunk - 1, 0)
            dn1 = pltpu.roll(cum, 1, 0)
            dn2 = pltpu.roll(cum, 2, 0)
            return jnp.where(ph == 0, up1, jnp.where(ph == 1, cum, jnp.where(ph == 2, dn1, dn2)))
        odd = jnp.bitwise_and(rid, 1) == 1
        return jnp.where(odd, pltpu.roll(cum, 1, 0), cum)

    def one_chunk(c, carry):
        r0 = pl.multiple_of(c * chunk, chunk)
        rows = pl.ds(r0, chunk)
        q = q_ref[0, rows, :]
        k = k_ref[0, rows, :]
        v = v_ref[0, rows, :].astype(BF16)
        lf = lf_ref[0, rows, :]
        cum = sum(jnp.dot(tri, part, preferred_element_type=F32) for part in _split3(lf))
        last = cum[chunk - 1:chunk, :]
        q_in = (q * jnp.exp(cum)).astype(BF16)
        k_out = (k * jnp.exp(last - cum)).astype(BF16)
        decay = jnp.exp(last)
        qb = [q.astype(BF16)]
        kb = [k.astype(BF16)]
        for b in levels:
            w = jnp.exp(-jnp.abs(cum - boundary(cum, b)))
            qb.append((q * w).astype(BF16))
            kb.append((k * w).astype(BF16))
        for h in range(REC_HEADS):
            sl = slice(h * REC_DIM, (h + 1) * REC_DIM)
            nt = (((1,), (1,)), ((), ()))
            a = jnp.where(xor == 0, lax.dot_general(qb[0][:, sl], kb[0][:, sl], nt, preferred_element_type=F32), 0.0)
            for n, b in enumerate(levels):
                d = lax.dot_general(qb[n + 1][:, sl], kb[n + 1][:, sl], nt, preferred_element_type=F32)
                a = jnp.where(lower & (xor >= b) & (xor < 2 * b), d, a)
            st = st_ref[h]
            o = jnp.dot(a.astype(BF16), v[:, sl], preferred_element_type=F32)
            o += lax.dot_general(q_in[:, sl], st.astype(BF16), nt, preferred_element_type=F32)
            o_ref[0, rows, sl] = o
            upd = lax.dot_general(v[:, sl], k_out[:, sl], (((0,), (0,)), ((), ())), preferred_element_type=F32)
            st_ref[h] = decay[:, sl] * st + upd
        return carry

    lax.fori_loop(0, n_chunks, one_chunk, 0)

    @pl.when(i == pl.num_programs(1) - 1)
    def _():
        sf_ref[0] = st_ref[...]


def _hgrn(q, k, lf, v, s0_t, ts, chunk):
    b, s, w = q.shape
    blk = pl.BlockSpec((1, ts, w), lambda i, j: (i, j, 0))
    st = pl.BlockSpec((1, REC_HEADS, REC_DIM, REC_DIM), lambda i, j: (i, 0, 0, 0))
    return pl.pallas_call(
        functools.partial(_hgrn_kernel, chunk),
        grid=(b, s // ts),
        in_specs=[blk, blk, blk, blk, st],
        out_specs=[blk, st],
        out_shape=[jax.ShapeDtypeStruct((b, s, w), F32),
                   jax.ShapeDtypeStruct((b, REC_HEADS, REC_DIM, REC_DIM), F32)],
        scratch_shapes=[pltpu.VMEM((REC_HEADS, REC_DIM, REC_DIM), F32)],
        compiler_params=_cparams("parallel", "arbitrary"),
        name="hgrn_recurrence",
    )(q, k, lf, v, s0_t)


def _group_mean_sq(x, ones_ref, width):
    hi, mid, _ = _split3(x * x)
    tot = jnp.dot(hi, ones_ref[...], preferred_element_type=F32) + jnp.dot(mid, ones_ref[...], preferred_element_type=F32)
    return tot * (1.0 / width)


def _merge_kernel(oa_ref, or_ref, gh_ref, x_ref, ga_ref, gr_ref, wo_ref, g1_ref, nf_ref, sc_ref, sh_ref,
                  wr_ref, br_ref, onesa_ref, onesr_ref, xo_ref, h2_ref, lg_ref):
    oa = oa_ref[0]
    a = oa * lax.rsqrt(_group_mean_sq(oa, onesa_ref, ATT_DIM) + EPS) * ga_ref[...]
    orr = or_ref[0]
    gh = gh_ref[0]
    r = orr * lax.rsqrt(_group_mean_sq(orr, onesr_ref, REC_DIM) + EPS) * gr_ref[...]
    r = r * (gh * jax.nn.sigmoid(gh))
    mix = jnp.dot(a.astype(BF16), wo_ref[0:ATT_W, :], preferred_element_type=F32)
    mix += jnp.dot(r.astype(BF16), wo_ref[ATT_W:ATT_W + REC_W, :], preferred_element_type=F32)
    x = x_ref[0] + g1_ref[0] * mix
    xo_ref[0] = x
    ms = jnp.mean(x * x, axis=-1, keepdims=True)
    h2 = (x * lax.rsqrt(ms + EPS) * nf_ref[...]) * (1.0 + sc_ref[0]) + sh_ref[0]
    h2_ref[0] = h2.astype(BF16)
    hh, hm, hl = _split3(h2)
    wh, wm, wl = wr_ref[0], wr_ref[1], wr_ref[2]
    lg = br_ref[...]
    for xa, wa in ((hh, wh), (hh, wm), (hm, wh), (hh, wl), (hm, wm), (hl, wh)):
        lg = lg + jnp.dot(xa, wa, preferred_element_type=F32)
    lg_ref[0] = lg


def _merge(o_att, o_rec, gh, x, g_att, g_rec, wo_bf, g1, nf, sc, sh, wr3, br, ones_a, ones_r, ts):
    b, s, d = x.shape
    row = pl.BlockSpec((1, ts, d), lambda i, j: (i, j, 0))
    half = pl.BlockSpec((1, ts, ATT_W), lambda i, j: (i, j, 0))
    per_b = pl.BlockSpec((1, 1, d), lambda i, j: (i, 0, 0))
    const2 = lambda shp: pl.BlockSpec(shp, lambda i, j: (0,) * len(shp))
    return pl.pallas_call(
        _merge_kernel,
        grid=(b, s // ts),
        in_specs=[half, half, half, row, const2((1, ATT_W)), const2((1, REC_W)), const2(wo_bf.shape),
                  per_b, const2((1, d)), per_b, per_b, const2(wr3.shape), const2((1, N_EXPERTS)),
                  const2(ones_a.shape), const2(ones_r.shape)],
        out_specs=[row, row, pl.BlockSpec((1, ts, N_EXPERTS), lambda i, j: (i, j, 0))],
        out_shape=[jax.ShapeDtypeStruct((b, s, d), F32), jax.ShapeDtypeStruct((b, s, d), BF16),
                   jax.ShapeDtypeStruct((b, s, N_EXPERTS), F32)],
        compiler_params=_cparams("parallel", "parallel"),
        name="merge_outproj_router",
    )(o_att, o_rec, gh, x, g_att.reshape(1, ATT_W), g_rec.reshape(1, REC_W), wo_bf, g1, nf.reshape(1, d),
      sc, sh, wr3, br.reshape(1, N_EXPERTS), ones_a, ones_r)


def _moe_kernel(be_ref, x_ref, w1_ref, b1_ref, w2_ref, b2_ref, o_ref):
    del be_ref
    x = x_ref[...]
    f = w2_ref.shape[1]
    fc = 512
    acc = None
    for c in range(f // fc):
        gt = jnp.dot(x, w1_ref[0, :, c * fc:(c + 1) * fc], preferred_element_type=F32) + b1_ref[0, :, c * fc:(c + 1) * fc]
        up = jnp.dot(x, w1_ref[0, :, f + c * fc:f + (c + 1) * fc], preferred_element_type=F32) + b1_ref[0, :, f + c * fc:f + (c + 1) * fc]
        gt = jnp.minimum(gt, SWIGLU_LIMIT)
        up = jnp.clip(up, -SWIGLU_LIMIT, SWIGLU_LIMIT)
        act = (gt * jax.nn.sigmoid(SWIGLU_ALPHA * gt) * (up + 1.0)).astype(BF16)
        part = jnp.dot(act, w2_ref[0, c * fc:(c + 1) * fc, :], preferred_element_type=F32)
        acc = part if acc is None else acc + part
    o_ref[...] = acc + b2_ref[0]


def _moe_experts(block_expert, xs, w1_bf, b1, w2_bf, b2):
    p, d = xs.shape
    e, _, f2 = w1_bf.shape
    f = f2 // 2
    nb = p // MOE_ROWS
    grid_spec = pltpu.PrefetchScalarGridSpec(
        num_scalar_prefetch=1,
        grid=(nb,),
        in_specs=[pl.BlockSpec((MOE_ROWS, d), lambda i, be: (i, 0)),
                  pl.BlockSpec((1, d, f2), lambda i, be: (be[i], 0, 0)),
                  pl.BlockSpec((1, 1, f2), lambda i, be: (be[i], 0, 0)),
                  pl.BlockSpec((1, f, d), lambda i, be: (be[i], 0, 0)),
                  pl.BlockSpec((1, 1, d), lambda i, be: (be[i], 0, 0))],
        out_specs=pl.BlockSpec((MOE_ROWS, d), lambda i, be: (i, 0)),
    )
    return pl.pallas_call(
        _moe_kernel,
        grid_spec=grid_spec,
        out_shape=jax.ShapeDtypeStruct((p, d), F32),
        compiler_params=_cparams("arbitrary"),
        name="moe_experts",
    )(block_expert, xs, w1_bf, b1.reshape(e, 1, f2), w2_bf, b2.reshape(e, 1, d))


def _route(logits):
    n = logits.shape[0]
    top_val, top_idx = lax.top_k(logits, TOP_K)
    gate = jax.nn.softmax(top_val, axis=-1)
    m = n * TOP_K
    e_flat = top_idx.reshape(m)
    onehot = (e_flat[:, None] == jnp.arange(N_EXPERTS, dtype=I32)[None, :]).astype(I32)
    csum = jnp.cumsum(onehot, axis=0)
    rank = jnp.sum(onehot * csum, axis=1) - 1
    counts = csum[-1]
    padded = (counts + MOE_ROWS - 1) // MOE_ROWS * MOE_ROWS
    pad_end = jnp.cumsum(padded)
    pad_start = pad_end - padded
    dest = pad_start[e_flat] + rank
    nb = (m + N_EXPERTS * (MOE_ROWS - 1) + MOE_ROWS - 1) // MOE_ROWS
    slot_tok = jnp.zeros((nb * MOE_ROWS,), I32).at[dest].set(jnp.arange(m, dtype=I32) // TOP_K)
    block_expert = jnp.minimum(
        jnp.searchsorted(pad_end, jnp.arange(nb, dtype=I32) * MOE_ROWS, side='right'), N_EXPERTS - 1).astype(I32)
    return gate, dest.reshape(n, TOP_K), slot_tok, block_expert


def _moe(h2, logits, w1_bf, b1, w2_bf, b2):
    gate, dest, slot_tok, block_expert = _route(logits)
    xs = jnp.take(h2, slot_tok, axis=0)
    ys = _moe_experts(block_expert, xs, w1_bf, b1, w2_bf, b2)
    y = gate[:, 0:1] * jnp.take(ys, dest[:, 0], axis=0)
    for k in range(1, TOP_K):
        y = y + gate[:, k:k + 1] * jnp.take(ys, dest[:, k], axis=0)
    return y


def _final_kernel(x_ref, moe_ref, g2_ref, nw_ref, o_ref):
    x = x_ref[0] + g2_ref[0] * moe_ref[0]
    ms = jnp.mean(x * x, axis=-1, keepdims=True)
    o_ref[0] = x * lax.rsqrt(ms + EPS) * nw_ref[...]


def _final(x, moe, g2, nw, ts):
    b, s, d = x.shape
    row = pl.BlockSpec((1, ts, d), lambda i, j: (i, j, 0))
    return pl.pallas_call(
        _final_kernel,
        grid=(b, s // ts),
        in_specs=[row, row, pl.BlockSpec((1, 1, d), lambda i, j: (i, 0, 0)),
                  pl.BlockSpec((1, d), lambda i, j: (0, 0))],
        out_specs=row,
        out_shape=jax.ShapeDtypeStruct((b, s, d), F32),
        compiler_params=_cparams("parallel", "parallel"),
        name="final_norm",
    )(x, moe, g2, nw.reshape(1, d))


def _block_diag_ones(width, group):
    i = jnp.arange(width)
    return (i[:, None] // group == i[None, :] // group).astype(BF16)


def kernel(x_prompt, x_sample, c_prompt, c_sample, cache_k, cache_v, state_hgrn, norm_mix, norm_ffn, norm_final, w_ada, b_ada, w_in, rel_bias, hgrn_lb_logits, g_attn_out, g_hgrn_out, w_out, w_router, b_router, w_e1, b_e1, w_e2, b_e2):
    depth = w_in.shape[0]
    bp, sp, d = x_prompt.shape
    bs, ls, _ = x_sample.shape
    wc = cache_k.shape[2]
    n_p, n_s = bp * sp, bs * ls

    lbs = jax.nn.softmax(hgrn_lb_logits.astype(F32), axis=0)
    lbs = jnp.cumsum(lbs, axis=0) - lbs[0]

    mods = _modulation(jnp.concatenate([c_prompt, c_sample], axis=0), w_ada, b_ada)
    ones_a = _block_diag_ones(ATT_W, ATT_DIM)
    ones_r = _block_diag_ones(REC_W, REC_DIM)
    lk_s = -(-(wc + ls) // LANES) * LANES
    tail = min(LEFT, sp)

    x_p, x_s = x_prompt, x_sample
    moe_p = moe_s = g2_p = g2_s = None
    kp_l, vp_l, sp_l, ks_l, vs_l, ss_l = [], [], [], [], [], []
    for layer in range(depth):
        m6 = mods[layer].reshape(bp + bs, 1, N_MOD, d)
        mod_p = [m6[:bp, :, n, :] for n in range(N_MOD)]
        mod_s = [m6[bp:, :, n, :] for n in range(N_MOD)]
        w_in_bf = w_in[layer].astype(BF16)
        w_out_bf = w_out[layer].astype(BF16)
        wr3 = jnp.stack(_split3(w_router[layer]))
        w1_bf = w_e1[layer].astype(BF16)
        w2_bf = w_e2[layer].astype(BF16)
        table = rel_bias[layer]

        x_p, qa, ka, va, qh, kh, lf, vh, gh = _inproj(
            x_p, moe_p, g2_p, norm_mix[layer], mod_p[1], mod_p[0], w_in_bf, lbs[layer], SEQ_TILE)
        o_att = _attn_prompt(qa, ka, va, _prompt_bias(table))
        s0 = jnp.zeros((bp, REC_HEADS, REC_DIM, REC_DIM), F32)
        o_rec, s_fin = _hgrn(qh, kh, lf, vh, s0, SEQ_TILE, CHUNK)
        x_p, h2_p, lg_p = _merge(o_att, o_rec, gh, x_p, g_attn_out[layer], g_hgrn_out[layer], w_out_bf,
                                 mod_p[2], norm_ffn[layer], mod_p[4], mod_p[3], wr3, b_router[layer],
                                 ones_a, ones_r, SEQ_TILE)
        kp_l.append(ka[:, sp - tail:].reshape(bp, tail, ATT_HEADS, ATT_DIM))
        vp_l.append(va[:, sp - tail:].reshape(bp, tail, ATT_HEADS, ATT_DIM))
        sp_l.append(jnp.swapaxes(s_fin, -1, -2))

        x_s, qa, ka, va, qh, kh, lf, vh, gh = _inproj(
            x_s, moe_s, g2_s, norm_mix[layer], mod_s[1], mod_s[0], w_in_bf, lbs[layer], ls)
        zpad = jnp.zeros((bs, lk_s - wc - ls, ATT_W), BF16)
        keys = jnp.concatenate([cache_k[layer].reshape(bs, wc, ATT_W).astype(BF16), ka.astype(BF16), zpad], axis=1)
        vals = jnp.concatenate([cache_v[layer].reshape(bs, wc, ATT_W).astype(BF16), va.astype(BF16), zpad], axis=1)
        o_att = _attn_sample(qa, keys, vals, _sample_bias(table, ls, wc, lk_s))
        s0 = jnp.swapaxes(state_hgrn[layer].astype(F32), -1, -2)
        o_rec, s_new = _hgrn(qh, kh, lf, vh, s0, ls, ls)
        x_s, h2_s, lg_s = _merge(o_att, o_rec, gh, x_s, g_attn_out[layer], g_hgrn_out[layer], w_out_bf,
                                 mod_s[2], norm_ffn[layer], mod_s[4], mod_s[3], wr3, b_router[layer],
                                 ones_a, ones_r, ls)
        ks_l.append(ka.reshape(bs, ls, ATT_HEADS, ATT_DIM))
        vs_l.append(va.reshape(bs, ls, ATT_HEADS, ATT_DIM))
        ss_l.append(jnp.swapaxes(s_new, -1, -2))

        h2 = jnp.concatenate([h2_p.reshape(n_p, d), h2_s.reshape(n_s, d)], axis=0)
        lg = jnp.concatenate([lg_p.reshape(n_p, N_EXPERTS), lg_s.reshape(n_s, N_EXPERTS)], axis=0)
        y = _moe(h2, lg, w1_bf, b_e1[layer], w2_bf, b_e2[layer])
        moe_p = y[:n_p].reshape(bp, sp, d)
        moe_s = y[n_p:].reshape(bs, ls, d)
        g2_p, g2_s = mod_p[5], mod_s[5]

    y_prompt = _final(x_p, moe_p, g2_p, norm_final, SEQ_TILE)
    y_sample = _final(x_s, moe_s, g2_s, norm_final, ls)
    return (y_prompt, y_sample, jnp.stack(kp_l), jnp.stack(vp_l), jnp.stack(sp_l),
            jnp.stack(ks_l), jnp.stack(vs_l), jnp.stack(ss_l))
```

```python
import functools
import math

import jax
import jax.numpy as jnp
import numpy as np
from jax import lax
from jax.experimental import pallas as pl
from jax.experimental.pallas import tpu as pltpu
from jax.experimental.pallas import tpu_sc as plsc

F32 = jnp.float32
BF16 = jnp.bfloat16
I32 = jnp.int32
U32 = jnp.uint32

EPS = 1e-6
CHUNK = 64
LEFT_CHUNKS = 8
LEFT = LEFT_CHUNKS * CHUNK
MAX_REL = 2 * CHUNK
ATT_HEADS = 8
ATT_DIM = 64
ATT_W = ATT_HEADS * ATT_DIM
REC_HEADS = 4
REC_DIM = 128
REC_W = REC_HEADS * REC_DIM
N_EXPERTS = 32
TOP_K = 4
SWIGLU_ALPHA = 1.702
SWIGLU_LIMIT = 7.0
N_MOD = 6
NEG = -1e30
LOG2E = 1.4426950408889634

LANES = 128
ATT_QSUB = 128
ATT_WIN = LEFT + ATT_QSUB
SEQ_TILE = 512
REC_CHUNK = 128
MOE_ROWS = 512
ROUTE_TILE = 256
VMEM_LIMIT = 56 * 1024 * 1024
SC_WORKERS = 32
SC_SCATTER_WIN = 64
SC_GATHER_WIN = 16
SC_GATHER_BUFS = 6


def _cparams(*sem):
    return pltpu.CompilerParams(dimension_semantics=sem, vmem_limit_bytes=VMEM_LIMIT)


def _split3(x):
    hi = x.astype(BF16)
    r1 = x - hi.astype(F32)
    mid = r1.astype(BF16)
    lo = (r1 - mid.astype(F32)).astype(BF16)
    return hi, mid, lo


def _bits(x):
    return lax.bitcast_convert_type(x, U32)


def _mod_kernel(c_ref, w_ref, b_ref, o_ref):
    c = c_ref[...]
    act = (c * jax.nn.sigmoid(c)).astype(BF16)
    o_ref[0] = jnp.dot(act, w_ref[0].astype(BF16), preferred_element_type=F32) + b_ref[0]


def _modulation(c_all, w_ada, b_ada):
    depth, d, n6 = w_ada.shape
    rows = c_all.shape[0]
    tn = 1536
    return pl.pallas_call(
        _mod_kernel,
        grid=(depth, n6 // tn),
        in_specs=[pl.BlockSpec((rows, d), lambda l, j: (0, 0)),
                  pl.BlockSpec((1, d, tn), lambda l, j: (l, 0, j)),
                  pl.BlockSpec((1, 1, tn), lambda l, j: (l, 0, j))],
        out_specs=pl.BlockSpec((1, rows, tn), lambda l, j: (l, 0, j)),
        out_shape=jax.ShapeDtypeStruct((depth, rows, n6), F32),
        compiler_params=_cparams("parallel", "parallel"),
        name="adaln_modulation",
    )(c_all, w_ada, b_ada.reshape(depth, 1, n6))


def _combine(yg_ref, gate_ref):
    g = gate_ref[0]
    acc = yg_ref[0] * g[:, 0:1]
    for k in range(1, TOP_K):
        acc = acc + yg_ref[k] * g[:, k:k + 1]
    return acc


def _inproj_kernel(has_res, *refs):
    if has_res:
        (x_ref, yg_ref, gate_ref, g2_ref, nw_ref, sc_ref, sh_ref, w_ref, lb_ref,
         xo_ref, qa_ref, ka_ref, va_ref, qh_ref, kh_ref, lf_ref, vh_ref, gh_ref) = refs
        x = x_ref[0] + g2_ref[0] * _combine(yg_ref, gate_ref)
        xo_ref[0] = x
    else:
        (x_ref, nw_ref, sc_ref, sh_ref, w_ref, lb_ref,
         qa_ref, ka_ref, va_ref, qh_ref, kh_ref, lf_ref, vh_ref, gh_ref) = refs
        x = x_ref[0]
    ms = jnp.mean(x * x, axis=-1, keepdims=True)
    h = (x * lax.rsqrt(ms + EPS) * nw_ref[...]) * (1.0 + sc_ref[0]) + sh_ref[0]
    hb = h.astype(BF16)

    def proj(g):
        return jnp.dot(hb, w_ref[:, g * ATT_W:(g + 1) * ATT_W], preferred_element_type=F32)

    qa_ref[0] = (proj(0) * (ATT_DIM ** -0.5)).astype(BF16)
    ka_ref[0] = proj(1)
    va_ref[0] = proj(2)
    qh = proj(3)
    qh_ref[0] = qh * jax.nn.sigmoid(qh)
    f = proj(4)
    lb = lb_ref[...]
    lf_ref[0] = jnp.log(lb + (1.0 - lb) * jax.nn.sigmoid(f))
    kh_ref[0] = (1.0 - lb) * jax.nn.sigmoid(-f)
    vh_ref[0] = proj(5).astype(BF16)
    gh_ref[0] = proj(6)


def _inproj(x, res, nw, sc, sh, w_bf, lb, ts):
    b, s, d = x.shape
    pw = w_bf.shape[1]
    nj = s // ts
    has_res = res is not None
    row = pl.BlockSpec((1, ts, d), lambda i, j: (i, j, 0))
    per_b = pl.BlockSpec((1, 1, d), lambda i, j: (i, 0, 0))
    half = pl.BlockSpec((1, ts, ATT_W), lambda i, j: (i, j, 0))
    in_specs, args = [row], [x]
    if has_res:
        yg, gate, g2, first = res
        blk0 = first // ts
        in_specs += [pl.BlockSpec((TOP_K, ts, d), lambda i, j: (0, blk0 + i * nj + j, 0)),
                     pl.BlockSpec((1, ts, TOP_K), lambda i, j: (i, j, 0)), per_b]
        args += [yg, gate, g2]
    in_specs += [pl.BlockSpec((1, d), lambda i, j: (0, 0)), per_b, per_b,
                 pl.BlockSpec((d, pw), lambda i, j: (0, 0)),
                 pl.BlockSpec((1, REC_W), lambda i, j: (0, 0))]
    args += [nw.reshape(1, d), sc, sh, w_bf, lb.reshape(1, REC_W)]
    hs = jax.ShapeDtypeStruct((b, s, ATT_W), F32)
    hb = jax.ShapeDtypeStruct((b, s, ATT_W), BF16)
    out_shape = ([jax.ShapeDtypeStruct((b, s, d), F32)] if has_res else []) + [hb, hs, hs, hs, hs, hs, hb, hs]
    out_specs = ([row] if has_res else []) + [half] * 8
    outs = pl.pallas_call(
        functools.partial(_inproj_kernel, has_res),
        grid=(b, nj),
        in_specs=in_specs, out_specs=out_specs, out_shape=out_shape,
        compiler_params=_cparams("parallel", "parallel"),
        name="norm_inproj",
    )(*args)
    if not has_res:
        outs = [x] + list(outs)
    return outs


def _attend(q, kw, vw, bias_ref, s_ref, col_thr):
    lq, lk = q.shape[0], kw.shape[0]
    lane = lax.broadcasted_iota(I32, (lq, LANES), 1)
    lo = lane < ATT_DIM
    nt = (((1,), (1,)), ((), ()))
    if col_thr is not None:
        keep = lax.broadcasted_iota(I32, (2 * lq, lk), 1) >= col_thr
    ms = []
    for p in range(ATT_HEADS // 2):
        sl = slice(p * LANES, (p + 1) * LANES)
        qp = q[:, sl]
        z = jnp.zeros_like(qp)
        lhs = jnp.concatenate([jnp.where(lo, qp, z), jnp.where(lo, z, qp)], axis=0)
        s = lax.dot_general(lhs, kw[:, sl], nt, preferred_element_type=F32) + bias_ref[p]
        if col_thr is not None:
            s = jnp.where(keep, s, NEG)
        ms.append(jnp.max(s, axis=-1, keepdims=True))
        s_ref[p] = s
    outs = []
    for p in range(ATT_HEADS // 2):
        sl = slice(p * LANES, (p + 1) * LANES)
        e = jnp.exp(s_ref[p] - ms[p])
        l = jnp.sum(e, axis=-1, keepdims=True)
        o = jnp.dot(e.astype(BF16), vw[:, sl], preferred_element_type=F32) / l
        outs.append(jnp.where(lo, o[:lq], o[lq:]))
    return jnp.concatenate(outs, axis=-1)


def _attn_prompt_kernel(q_ref, kp_ref, kc_ref, vp_ref, vc_ref, bias_ref, o_ref, kw_ref, vw_ref, s_ref):
    i = pl.program_id(1)
    ts = q_ref.shape[1]
    kw_ref[0:ts, :] = kp_ref[0].astype(BF16)
    kw_ref[ts:2 * ts, :] = kc_ref[0].astype(BF16)
    vw_ref[0:ts, :] = vp_ref[0].astype(BF16)
    vw_ref[ts:2 * ts, :] = vc_ref[0].astype(BF16)

    def body(masked, r, carry):
        r0 = pl.multiple_of(r * ATT_QSUB, ATT_QSUB)
        q = q_ref[0, pl.ds(r0, ATT_QSUB), :]
        kw = kw_ref[pl.ds(r0, ATT_WIN), :]
        vw = vw_ref[pl.ds(r0, ATT_WIN), :]
        o_ref[0, pl.ds(r0, ATT_QSUB), :] = _attend(q, kw, vw, bias_ref, s_ref, (ts - r0) if masked else None)
        return carry

    @pl.when(i == 0)
    def _():
        lax.fori_loop(0, ts // ATT_QSUB, functools.partial(body, True), 0)

    @pl.when(i > 0)
    def _():
        lax.fori_loop(0, ts // ATT_QSUB, functools.partial(body, False), 0)


def _attn_prompt(qa, ka, va, bias):
    b, s, w = qa.shape
    ts = LEFT
    cur = lambda i, j: (i, j, 0)
    prev = lambda i, j: (i, jnp.maximum(j - 1, 0), 0)
    blk = lambda im: pl.BlockSpec((1, ts, w), im)
    return pl.pallas_call(
        _attn_prompt_kernel,
        grid=(b, s // ts),
        in_specs=[blk(cur), blk(prev), blk(cur), blk(prev), blk(cur),
                  pl.BlockSpec(bias.shape, lambda i, j: (0, 0, 0))],
        out_specs=blk(cur),
        out_shape=jax.ShapeDtypeStruct((b, s, w), F32),
        scratch_shapes=[pltpu.VMEM((2 * ts, w), BF16), pltpu.VMEM((2 * ts, w), BF16),
                        pltpu.VMEM((ATT_HEADS // 2, 2 * ATT_QSUB, ATT_WIN), F32)],
        compiler_params=_cparams("parallel", "parallel"),
        name="band_attention",
    )(qa, ka, ka, va, va, bias)


def _attn_sample_kernel(q_ref, k_ref, v_ref, bias_ref, o_ref, s_ref):
    o_ref[0] = _attend(q_ref[0], k_ref[0], v_ref[0], bias_ref, s_ref, None)


def _attn_sample(qa, keys, vals, bias):
    b, l, w = qa.shape
    lk = keys.shape[1]
    return pl.pallas_call(
        _attn_sample_kernel,
        grid=(b,),
        in_specs=[pl.BlockSpec((1, l, w), lambda i: (i, 0, 0)),
                  pl.BlockSpec((1, lk, w), lambda i: (i, 0, 0)),
                  pl.BlockSpec((1, lk, w), lambda i: (i, 0, 0)),
                  pl.BlockSpec(bias.shape, lambda i: (0, 0, 0))],
        out_specs=pl.BlockSpec((1, l, w), lambda i: (i, 0, 0)),
        out_shape=jax.ShapeDtypeStruct((b, l, w), F32),
        scratch_shapes=[pltpu.VMEM((ATT_HEADS // 2, 2 * l, lk), F32)],
        compiler_params=_cparams("parallel"),
        name="cache_attention",
    )(qa, keys, vals, bias)


def _toeplitz_bias(table, lq, lk, offset, valid):
    i = np.arange(lq + lk - 1)
    gvec = table[:, np.clip(offset + (lq - 1) - i, -MAX_REL, MAX_REL) + MAX_REL]
    rows = [gvec[:, lq - 1 - t:lq - 1 - t + lk] for t in range(lq)]
    bias = jnp.where(jnp.asarray(valid)[None], jnp.stack(rows, axis=1), NEG).astype(F32)
    return bias.reshape(ATT_HEADS // 2, 2 * lq, lk)


def _prompt_bias(table):
    t = np.arange(ATT_QSUB)[:, None]
    j = np.arange(ATT_WIN)[None, :]
    start = (t // CHUNK) * CHUNK
    return _toeplitz_bias(table, ATT_QSUB, ATT_WIN, LEFT, (j >= start) & (j < start + LEFT + CHUNK))


def _sample_bias(table, l, w, lk):
    valid = np.broadcast_to(np.arange(lk)[None, :] < w + l, (l, lk))
    return _toeplitz_bias(table, l, lk, w, valid)


def _neg_abs(x):
    return lax.bitcast_convert_type(jnp.bitwise_or(_bits(x), jnp.uint32(0x80000000)), F32)


def _hgrn_kernel(chunk, q_ref, k_ref, lf_ref, v_ref, s0_ref, o_ref, sf_ref, st_ref):
    i = pl.program_id(1)
    ts = q_ref.shape[1]
    n_chunks = ts // chunk

    @pl.when(i == 0)
    def _():
        st_ref[...] = s0_ref[0]

    row = lax.broadcasted_iota(I32, (chunk, chunk), 0)
    colm = lax.broadcasted_iota(I32, (chunk, chunk), 1)
    tri = (colm <= row).astype(BF16)
    xor = jnp.bitwise_xor(row, colm)
    lower = colm < row
    rid = lax.broadcasted_iota(I32, (chunk, REC_W), 0)

    levels = []
    b = chunk // 2
    while b >= 1:
        levels.append(b)
        b //= 2
    pair_mask = [xor == 0] + [lower & (xor >= b) & (xor < 2 * b) for b in levels]
    upper = [jnp.bitwise_and(rid, b) != 0 for b in levels]
    ph4 = jnp.bitwise_and(rid, 3)
    odd = jnp.bitwise_and(rid, 1) == 1
    nt = (((1,), (1,)), ((), ()))
    tn = (((0,), (0,)), ((), ()))
    sls = [slice(h * REC_DIM, (h + 1) * REC_DIM) for h in range(REC_HEADS)]

    def boundary(cum, b):
        if 2 * b >= 8:
            pieces = [jnp.broadcast_to(cum[m * 2 * b + b - 1:m * 2 * b + b, :], (2 * b, REC_W))
                      for m in range(chunk // (2 * b))]
            return pieces[0] if len(pieces) == 1 else jnp.concatenate(pieces, axis=0)
        if b == 2:
            up1 = pltpu.roll(cum, chunk - 1, 0)
            dn1 = pltpu.roll(cum, 1, 0)
            dn2 = pltpu.roll(cum, 2, 0)
            return jnp.where(ph4 == 0, up1, jnp.where(ph4 == 1, cum, jnp.where(ph4 == 2, dn1, dn2)))
        return jnp.where(odd, pltpu.roll(cum, 1, 0), cum)

    def one_chunk(c, carry):
        r0 = pl.multiple_of(c * chunk, chunk)
        rows = pl.ds(r0, chunk)
        q = q_ref[0, rows, :]
        k = k_ref[0, rows, :]
        v = v_ref[0, rows, :]
        lf = lf_ref[0, rows, :]
        cum = sum(jnp.dot(tri, part, preferred_element_type=F32) for part in _split3(lf)) * LOG2E
        last = cum[chunk - 1:chunk, :]
        q_in = (q * jnp.exp2(cum)).astype(BF16)
        k_out = (k * jnp.exp2(last - cum)).astype(BF16)
        decay = jnp.exp2(last)
        qb = q.astype(BF16)
        kb = k.astype(BF16)
        zs = []
        for n, b in enumerate(levels):
            w = jnp.exp2(_neg_abs(cum - boundary(cum, b)))
            zs.append((jnp.where(upper[n], q, k) * w).astype(BF16))
        sts = [st_ref[h] for h in range(REC_HEADS)]
        d0 = [lax.dot_general(qb[:, sl], kb[:, sl], nt, preferred_element_type=F32) for sl in sls]
        dl = [[lax.dot_general(z[:, sl], z[:, sl], nt, preferred_element_type=F32) for z in zs] for sl in sls]
        oi = [lax.dot_general(q_in[:, sl], st.astype(BF16), nt, preferred_element_type=F32)
              for sl, st in zip(sls, sts)]
        upd = [lax.dot_general(v[:, sl], k_out[:, sl], tn, preferred_element_type=F32) for sl in sls]
        outs = []
        for h, sl in enumerate(sls):
            a = jnp.where(pair_mask[0], d0[h], 0.0)
            for n in range(len(levels)):
                a = jnp.where(pair_mask[n + 1], dl[h][n], a)
            outs.append(jnp.dot(a.astype(BF16), v[:, sl], preferred_element_type=F32) + oi[h])
        o_ref[0, rows, :] = jnp.concatenate(outs, axis=-1)
        for h, sl in enumerate(sls):
            st_ref[h] = decay[:, sl] * sts[h] + upd[h]
        return carry

    lax.fori_loop(0, n_chunks, one_chunk, 0)

    @pl.when(i == pl.num_programs(1) - 1)
    def _():
        sf_ref[0] = st_ref[...]


def _hgrn(q, k, lf, v, s0_t, ts, chunk):
    b, s, w = q.shape
    blk = pl.BlockSpec((1, ts, w), lambda i, j: (i, j, 0))
    st = pl.BlockSpec((1, REC_HEADS, REC_DIM, REC_DIM), lambda i, j: (i, 0, 0, 0))
    return pl.pallas_call(
        functools.partial(_hgrn_kernel, chunk),
        grid=(b, s // ts),
        in_specs=[blk, blk, blk, blk, st],
        out_specs=[blk, st],
        out_shape=[jax.ShapeDtypeStruct((b, s, w), F32),
                   jax.ShapeDtypeStruct((b, REC_HEADS, REC_DIM, REC_DIM), F32)],
        scratch_shapes=[pltpu.VMEM((REC_HEADS, REC_DIM, REC_DIM), F32)],
        compiler_params=_cparams("parallel", "arbitrary"),
        name="hgrn_recurrence",
    )(q, k, lf, v, s0_t)


def _head_rms(x, width):
    lane = lax.broadcasted_iota(I32, (x.shape[0], LANES), 1)
    lo = lane < ATT_DIM
    outs = []
    for p in range(x.shape[1] // LANES):
        xp = x[:, p * LANES:(p + 1) * LANES]
        sq = xp * xp
        tot = jnp.sum(sq, axis=-1, keepdims=True)
        if width == LANES:
            ms = tot * (1.0 / LANES)
        else:
            s_lo = jnp.sum(jnp.where(lo, sq, 0.0), axis=-1, keepdims=True)
            ms = jnp.where(lo, s_lo, tot - s_lo) * (1.0 / ATT_DIM)
        outs.append(xp * lax.rsqrt(ms + EPS))
    return jnp.concatenate(outs, axis=-1)


def _merge_kernel(oa_ref, or_ref, gh_ref, x_ref, ga_ref, gr_ref, wo_ref, g1_ref, nf_ref, sc_ref, sh_ref,
                  wr_ref, br_ref, xo_ref, h2_ref, lg_ref):
    a = _head_rms(oa_ref[0], ATT_DIM) * ga_ref[...]
    gh = gh_ref[0]
    r = _head_rms(or_ref[0], REC_DIM) * gr_ref[...] * (gh * jax.nn.sigmoid(gh))
    mix = jnp.dot(a.astype(BF16), wo_ref[0:ATT_W, :], preferred_element_type=F32)
    mix += jnp.dot(r.astype(BF16), wo_ref[ATT_W:ATT_W + REC_W, :], preferred_element_type=F32)
    x = x_ref[0] + g1_ref[0] * mix
    xo_ref[0] = x
    ms = jnp.mean(x * x, axis=-1, keepdims=True)
    h2 = (x * lax.rsqrt(ms + EPS) * nf_ref[...]) * (1.0 + sc_ref[0]) + sh_ref[0]
    hh = h2.astype(BF16)
    hr = _bits(hh.astype(F32))
    half = h2.shape[1] // 2
    h2_ref[0] = jnp.bitwise_or(jnp.bitwise_and(hr[:, half:], jnp.uint32(0xFFFF0000)),
                               jnp.right_shift(hr[:, :half], jnp.uint32(16)))
    hm = (h2 - hh.astype(F32)).astype(BF16)
    nt = (((1,), (1,)), ((), ()))
    l1 = lax.dot_general(wr_ref[...], hh, nt, preferred_element_type=F32)
    l2 = lax.dot_general(wr_ref[0:N_EXPERTS, :], hm, nt, preferred_element_type=F32)
    lg = l1[0:N_EXPERTS] + l1[N_EXPERTS:2 * N_EXPERTS] + l2 + br_ref[...]
    if len(lg_ref.shape) == 3:
        lg_ref[0] = lg
    else:
        lg_ref[...] = lg


def _merge(o_att, o_rec, gh, x, g_att, g_rec, wo_bf, g1, nf, sc, sh, wr2, br, ts):
    b, s, d = x.shape
    nj = s // ts
    row = pl.BlockSpec((1, ts, d), lambda i, j: (i, j, 0))
    half = pl.BlockSpec((1, ts, ATT_W), lambda i, j: (i, j, 0))
    per_b = pl.BlockSpec((1, 1, d), lambda i, j: (i, 0, 0))
    const2 = lambda shp: pl.BlockSpec(shp, lambda i, j: (0,) * len(shp))
    if ts % LANES == 0:
        lg_spec = pl.BlockSpec((N_EXPERTS, ts), lambda i, j: (0, i * nj + j))
        lg_shape = jax.ShapeDtypeStruct((N_EXPERTS, b * s), F32)
    else:
        lg_spec = pl.BlockSpec((1, N_EXPERTS, ts), lambda i, j: (i, 0, j))
        lg_shape = jax.ShapeDtypeStruct((b, N_EXPERTS, s), F32)
    return pl.pallas_call(
        _merge_kernel,
        grid=(b, nj),
        in_specs=[half, half, half, row, const2((1, ATT_W)), const2((1, REC_W)), const2(wo_bf.shape),
                  per_b, const2((1, d)), per_b, per_b, const2(wr2.shape), const2((N_EXPERTS, 1))],
        out_specs=[row, pl.BlockSpec((1, ts, d // 2), lambda i, j: (i, j, 0)), lg_spec],
        out_shape=[jax.ShapeDtypeStruct((b, s, d), F32), jax.ShapeDtypeStruct((b, s, d // 2), U32), lg_shape],
        compiler_params=_cparams("parallel", "parallel"),
        name="merge_outproj_router",
    )(o_att, o_rec, gh, x, g_att.reshape(1, ATT_W), g_rec.reshape(1, REC_W), wo_bf, g1, nf.reshape(1, d),
      sc, sh, wr2, br.reshape(N_EXPERTS, 1))


def _route_kernel(lg_ref, idx_ref, gate_ref, rank_ref, cnt_ref, run_ref):
    i = pl.program_id(0)
    t = lg_ref.shape[1]

    @pl.when(i == 0)
    def _():
        run_ref[...] = jnp.zeros_like(run_ref)

    x = lg_ref[...]
    row = lax.broadcasted_iota(I32, x.shape, 0)
    vals, idxs, hits = [], [], []
    for _ in range(TOP_K):
        m = jnp.max(x, axis=0, keepdims=True)
        ik = jnp.min(jnp.where(x == m, row, N_EXPERTS), axis=0, keepdims=True)
        hit = row == ik
        x = jnp.where(hit, -jnp.inf, x)
        vals.append(m)
        idxs.append(ik)
        hits.append(hit)
    es = [jnp.exp(v - vals[0]) for v in vals]
    tot = es[0] + es[1] + es[2] + es[3]
    gate_ref[...] = jnp.concatenate([e / tot for e in es], axis=0)
    idx_ref[...] = jnp.concatenate(idxs, axis=0)
    chosen = (hits[0] | hits[1] | hits[2] | hits[3])
    onehot = jnp.where(chosen, 1.0, 0.0)
    earlier = (lax.broadcasted_iota(I32, (t, t), 0) < lax.broadcasted_iota(I32, (t, t), 1)).astype(BF16)
    run = run_ref[:, 0:1]
    before = jnp.dot(onehot.astype(BF16), earlier, preferred_element_type=F32) + run
    rank_ref[...] = jnp.concatenate(
        [jnp.sum(jnp.where(h, before, 0.0), axis=0, keepdims=True) for h in hits], axis=0).astype(I32)
    run = run + jnp.sum(onehot, axis=1, keepdims=True)
    run_ref[...] = jnp.broadcast_to(run, run_ref.shape)
    cnt_ref[...] = jnp.broadcast_to(run, cnt_ref.shape)


def _route(lg_t):
    e, n = lg_t.shape
    t = ROUTE_TILE
    tok = pl.BlockSpec((TOP_K, t), lambda i: (0, i))
    idx, gate, rank, cnt = pl.pallas_call(
        _route_kernel,
        grid=(n // t,),
        in_specs=[pl.BlockSpec((e, t), lambda i: (0, i))],
        out_specs=[tok, tok, tok, pl.BlockSpec((e, LANES), lambda i: (0, 0))],
        out_shape=[jax.ShapeDtypeStruct((TOP_K, n), I32), jax.ShapeDtypeStruct((TOP_K, n), F32),
                   jax.ShapeDtypeStruct((TOP_K, n), I32), jax.ShapeDtypeStruct((e, LANES), F32)],
        scratch_shapes=[pltpu.VMEM((e, LANES), F32)],
        compiler_params=_cparams("arbitrary"),
        name="moe_route",
    )(lg_t)
    counts = cnt[:, 0].astype(I32)
    padded = (counts + MOE_ROWS - 1) // MOE_ROWS * MOE_ROWS
    pad_end = jnp.cumsum(padded)
    pad_start = pad_end - padded
    m = n * TOP_K
    nb = (m + N_EXPERTS * (MOE_ROWS - 1) + MOE_ROWS - 1) // MOE_ROWS
    block_expert = jnp.minimum(
        jnp.searchsorted(pad_end, jnp.arange(nb, dtype=I32) * MOE_ROWS, side='right'), N_EXPERTS - 1).astype(I32)
    start_of = jnp.sum(jnp.where(idx[:, :, None] == jnp.arange(N_EXPERTS, dtype=I32), pad_start, 0), axis=-1)
    dest = start_of + rank
    return gate, dest, block_expert, nb * MOE_ROWS


def _windows(dest, w, multiple, fill):
    kk, n = dest.shape
    nwin = n // w
    per = -(-nwin // SC_WORKERS)
    per = -(-per // multiple) * multiple
    idx = dest.reshape(kk, nwin, w).transpose(1, 0, 2)
    pad = jnp.broadcast_to(fill[None, None, :], (SC_WORKERS * per - nwin, kk, w)).astype(I32)
    return jnp.concatenate([idx, pad], axis=0)


def _sc_scatter_rows(x, idx, p_rows):
    n, c = x.shape
    nwin, kk, w = idx.shape
    per = nwin // SC_WORKERS
    nreal = n // w
    mesh = plsc.VectorSubcoreMesh(core_axis_name="c", subcore_axis_name="s")

    @functools.partial(
        pl.kernel, mesh=mesh,
        out_type=jax.ShapeDtypeStruct((p_rows, c), x.dtype),
        scratch_types=[pltpu.VMEM((per, kk, w), I32), pltpu.VMEM((2, w, c), x.dtype),
                       pltpu.SemaphoreType.DMA((2,)), pltpu.SemaphoreType.DMA((2,))],
        name="sc_scatter_rows",
    )
    def k(x_hbm, idx_hbm, out_hbm, idx_v, rows_v, lsem, ssem):
        wid = lax.axis_index("s") * 2 + lax.axis_index("c")
        g0 = wid * per
        pltpu.sync_copy(idx_hbm.at[pl.ds(g0, per)], idx_v)

        def load(j, b):
            g = jnp.minimum(g0 + j, nreal - 1)
            return pltpu.make_async_copy(x_hbm.at[pl.ds(g * w, w)], rows_v.at[b], lsem.at[b])

        def scat(j, b, q):
            return pltpu.make_async_copy(rows_v.at[b], out_hbm.at[idx_v.at[j, q]], ssem.at[b])

        @pl.loop(0, per + 2, step=2)
        def _(j0):
            for b in range(2):
                j = j0 + b

                @pl.when(jnp.logical_and(j >= 2, j < per + 2))
                def _():
                    for q in range(kk):
                        scat(j - 2, b, q).wait()

                @pl.when(j < per)
                def _():
                    load(j, b).start()

                @pl.when(jnp.logical_and(j >= 1, j < per + 1))
                def _():
                    load(j - 1, 1 - b).wait()
                    for q in range(kk):
                        scat(j - 1, 1 - b, q).start()

    return k(x, idx)


def _sc_gather_rows(y, idx):
    p, c = y.shape
    nwin, kk, w = idx.shape
    nb = SC_GATHER_BUFS
    per = nwin // SC_WORKERS
    group = math.lcm(nb, kk)
    wins = group // kk
    n_items = per * kk
    mesh = plsc.VectorSubcoreMesh(core_axis_name="c", subcore_axis_name="s")

    @functools.partial(
        pl.kernel, mesh=mesh,
        out_type=jax.ShapeDtypeStruct((kk, nwin * w, c), y.dtype),
        scratch_types=[pltpu.VMEM((2, wins, kk, w), I32), pltpu.VMEM((nb, w, c), y.dtype),
                       pltpu.SemaphoreType.DMA((nb,)), pltpu.SemaphoreType.DMA((nb,))],
        name="sc_gather_rows",
    )
    def k(y_hbm, idx_hbm, out_hbm, idx_v, rows_v, gsem, wsem):
        wid = lax.axis_index("s") * 2 + lax.axis_index("c")
        g0 = wid * per

        def gath(slot, jj, q, b):
            return pltpu.make_async_copy(y_hbm.at[idx_v.at[slot, jj, q]], rows_v.at[b], gsem.at[b])

        def wr(i, b):
            return pltpu.make_async_copy(rows_v.at[b], out_hbm.at[i % kk, pl.ds((g0 + i // kk) * w, w)], wsem.at[b])

        n_groups = -(-(n_items + nb) // group)

        @pl.loop(0, n_groups)
        def _(gi):
            slot = gi % 2

            @pl.when(gi * wins < per)
            def _():
                pltpu.sync_copy(idx_hbm.at[pl.ds(g0 + gi * wins, wins)], idx_v.at[slot])

            for l in range(group):
                i = gi * group + l
                b = l % nb
                bw = (l - (nb - 1)) % nb

                @pl.when(jnp.logical_and(i >= nb, i < n_items + nb))
                def _():
                    wr(i - nb, b).wait()

                @pl.when(i < n_items)
                def _():
                    gath(slot, l // kk, l % kk, b).start()

                @pl.when(jnp.logical_and(i >= nb - 1, i < n_items + nb - 1))
                def _():
                    gath(slot, 0, 0, bw).wait()
                    wr(i - (nb - 1), bw).start()

    return k(y, idx)


def _moe_kernel(be_ref, x_ref, w1_ref, b1_ref, w2_ref, b2_ref, o_ref, w1b_ref, w2b_ref):
    i = pl.program_id(0)
    changed = jnp.logical_or(i == 0, be_ref[i] != be_ref[jnp.maximum(i - 1, 0)])

    @pl.when(changed)
    def _():
        w1b_ref[...] = w1_ref[0, 0].astype(BF16)
        w2b_ref[...] = w2_ref[0, 0].astype(BF16)

    word = x_ref[...]
    lo = lax.bitcast_convert_type(jnp.left_shift(word, jnp.uint32(16)), F32).astype(BF16)
    hi = lax.bitcast_convert_type(jnp.bitwise_and(word, jnp.uint32(0xFFFF0000)), F32).astype(BF16)
    x = jnp.concatenate([lo, hi], axis=-1)
    f = w2b_ref.shape[0]
    fc = 512
    acc = None
    for c in range(f // fc):
        gt = jnp.dot(x, w1b_ref[:, c * fc:(c + 1) * fc], preferred_element_type=F32) + b1_ref[0, 0, :, c * fc:(c + 1) * fc]
        up = jnp.dot(x, w1b_ref[:, f + c * fc:f + (c + 1) * fc], preferred_element_type=F32) + b1_ref[0, 0, :, f + c * fc:f + (c + 1) * fc]
        gt = jnp.minimum(gt, SWIGLU_LIMIT)
        up = jnp.clip(up, -SWIGLU_LIMIT, SWIGLU_LIMIT)
        act = (gt * jax.nn.sigmoid(SWIGLU_ALPHA * gt) * (up + 1.0)).astype(BF16)
        part = jnp.dot(act, w2b_ref[c * fc:(c + 1) * fc, :], preferred_element_type=F32)
        acc = part if acc is None else acc + part
    o_ref[...] = acc + b2_ref[0, 0]


def _moe_experts(layer, block_expert, xs, p_rows, w1, b1, w2, b2):
    _, e, d, f2 = w1.shape
    f = f2 // 2
    nb = p_rows // MOE_ROWS
    grid_spec = pltpu.PrefetchScalarGridSpec(
        num_scalar_prefetch=1,
        grid=(nb,),
        in_specs=[pl.BlockSpec((MOE_ROWS, d // 2), lambda i, be: (i, 0)),
                  pl.BlockSpec((1, 1, d, f2), lambda i, be: (layer, be[i], 0, 0)),
                  pl.BlockSpec((1, 1, 1, f2), lambda i, be: (layer, be[i], 0, 0)),
                  pl.BlockSpec((1, 1, f, d), lambda i, be: (layer, be[i], 0, 0)),
                  pl.BlockSpec((1, 1, 1, d), lambda i, be: (layer, be[i], 0, 0))],
        out_specs=pl.BlockSpec((MOE_ROWS, d), lambda i, be: (i, 0)),
        scratch_shapes=[pltpu.VMEM((d, f2), BF16), pltpu.VMEM((f, d), BF16)],
    )
    return pl.pallas_call(
        _moe_kernel,
        grid_spec=grid_spec,
        out_shape=jax.ShapeDtypeStruct((p_rows, d), F32),
        compiler_params=_cparams("arbitrary"),
        name="moe_experts",
    )(block_expert, xs, w1, b1.reshape(b1.shape[0], e, 1, f2), w2, b2.reshape(b2.shape[0], e, 1, d))


def _moe(layer, h2_words, lg_t, w1, b1, w2, b2):
    n = h2_words.shape[0]
    gate, dest, block_expert, p_rows = _route(lg_t)
    spare = p_rows + jnp.arange(SC_SCATTER_WIN, dtype=I32)
    xs = _sc_scatter_rows(h2_words, _windows(dest, SC_SCATTER_WIN, 2, spare), p_rows + SC_SCATTER_WIN)
    ys = _moe_experts(layer, block_expert, xs, p_rows, w1, b1, w2, b2)
    group_wins = math.lcm(SC_GATHER_BUFS, TOP_K) // TOP_K
    yg = _sc_gather_rows(ys, _windows(dest, SC_GATHER_WIN, group_wins, jnp.zeros((SC_GATHER_WIN,), I32)))
    return yg, gate.T


def _final_kernel(x_ref, yg_ref, gate_ref, g2_ref, nw_ref, o_ref):
    x = x_ref[0] + g2_ref[0] * _combine(yg_ref, gate_ref)
    ms = jnp.mean(x * x, axis=-1, keepdims=True)
    o_ref[0] = x * lax.rsqrt(ms + EPS) * nw_ref[...]


def _final(x, res, nw, ts):
    b, s, d = x.shape
    nj = s // ts
    yg, gate, g2, first = res
    blk0 = first // ts
    row = pl.BlockSpec((1, ts, d), lambda i, j: (i, j, 0))
    return pl.pallas_call(
        _final_kernel,
        grid=(b, nj),
        in_specs=[row, pl.BlockSpec((TOP_K, ts, d), lambda i, j: (0, blk0 + i * nj + j, 0)),
                  pl.BlockSpec((1, ts, TOP_K), lambda i, j: (i, j, 0)),
                  pl.BlockSpec((1, 1, d), lambda i, j: (i, 0, 0)),
                  pl.BlockSpec((1, d), lambda i, j: (0, 0))],
        out_specs=row,
        out_shape=jax.ShapeDtypeStruct((b, s, d), F32),
        compiler_params=_cparams("parallel", "parallel"),
        name="final_norm",
    )(x, yg, gate, g2, nw.reshape(1, d))


def kernel(x_prompt, x_sample, c_prompt, c_sample, cache_k, cache_v, state_hgrn, norm_mix, norm_ffn, norm_final, w_ada, b_ada, w_in, rel_bias, hgrn_lb_logits, g_attn_out, g_hgrn_out, w_out, w_router, b_router, w_e1, b_e1, w_e2, b_e2):
    depth = w_in.shape[0]
    bp, sp, d = x_prompt.shape
    bs, ls, _ = x_sample.shape
    wc = cache_k.shape[2]
    n_p, n_s = bp * sp, bs * ls

    lbs = jax.nn.softmax(hgrn_lb_logits.astype(F32), axis=0)
    lbs = jnp.cumsum(lbs, axis=0) - lbs[0]

    mods = _modulation(jnp.concatenate([c_prompt, c_sample], axis=0), w_ada, b_ada)
    lk_s = -(-(wc + ls) // LANES) * LANES
    tail = min(LEFT, sp)
    rec_chunk = min(REC_CHUNK, sp)

    x_p, x_s = x_prompt, x_sample
    res_p = res_s = None
    kp_l, vp_l, sp_l, ks_l, vs_l, ss_l = [], [], [], [], [], []
    for layer in range(depth):
        m6 = mods[layer].reshape(bp + bs, 1, N_MOD, d)
        mod_p = [m6[:bp, :, n, :] for n in range(N_MOD)]
        mod_s = [m6[bp:, :, n, :] for n in range(N_MOD)]
        w_in_bf = w_in[layer].astype(BF16)
        w_out_bf = w_out[layer].astype(BF16)
        wr_hi, wr_mid, _ = _split3(w_router[layer].T)
        wr2 = jnp.concatenate([wr_hi, wr_mid], axis=0)
        table = rel_bias[layer]

        x_p, qa, ka, va, qh, kh, lf, vh, gh = _inproj(
            x_p, res_p, norm_mix[layer], mod_p[1], mod_p[0], w_in_bf, lbs[layer],
            SEQ_TILE if res_p is None else SEQ_TILE // 2)
        o_att = _attn_prompt(qa, ka, va, _prompt_bias(table))
        s0 = jnp.zeros((bp, REC_HEADS, REC_DIM, REC_DIM), F32)
        o_rec, s_fin = _hgrn(qh, kh, lf, vh, s0, SEQ_TILE, rec_chunk)
        x_p, h2_p, lg_p = _merge(o_att, o_rec, gh, x_p, g_attn_out[layer], g_hgrn_out[layer], w_out_bf,
                                 mod_p[2], norm_ffn[layer], mod_p[4], mod_p[3], wr2, b_router[layer], SEQ_TILE)
        kp_l.append(ka[:, sp - tail:].reshape(bp, tail, ATT_HEADS, ATT_DIM))
        vp_l.append(va[:, sp - tail:].reshape(bp, tail, ATT_HEADS, ATT_DIM))
        sp_l.append(jnp.swapaxes(s_fin, -1, -2))

        x_s, qa, ka, va, qh, kh, lf, vh, gh = _inproj(
            x_s, res_s, norm_mix[layer], mod_s[1], mod_s[0], w_in_bf, lbs[layer], ls)
        zpad = jnp.zeros((bs, lk_s - wc - ls, ATT_W), BF16)
        keys = jnp.concatenate([cache_k[layer].reshape(bs, wc, ATT_W).astype(BF16), ka.astype(BF16), zpad], axis=1)
        vals = jnp.concatenate([cache_v[layer].reshape(bs, wc, ATT_W).astype(BF16), va.astype(BF16), zpad], axis=1)
        o_att = _attn_sample(qa, keys, vals, _sample_bias(table, ls, wc, lk_s))
        s0 = jnp.swapaxes(state_hgrn[layer].astype(F32), -1, -2)
        o_rec, s_new = _hgrn(qh, kh, lf, vh, s0, ls, ls)
        x_s, h2_s, lg_s = _merge(o_att, o_rec, gh, x_s, g_attn_out[layer], g_hgrn_out[layer], w_out_bf,
                                 mod_s[2], norm_ffn[layer], mod_s[4], mod_s[3], wr2, b_router[layer], ls)
        ks_l.append(ka.reshape(bs, ls, ATT_HEADS, ATT_DIM))
        vs_l.append(va.reshape(bs, ls, ATT_HEADS, ATT_DIM))
        ss_l.append(jnp.swapaxes(s_new, -1, -2))

        h2 = jnp.concatenate([h2_p.reshape(n_p, d // 2), h2_s.reshape(n_s, d // 2)], axis=0)
        lg_t = jnp.concatenate([lg_p, jnp.swapaxes(lg_s, 0, 1).reshape(N_EXPERTS, n_s)], axis=1)
        yg, gate = _moe(layer, h2, lg_t, w_e1, b_e1, w_e2, b_e2)
        res_p = (yg, gate[:n_p].reshape(bp, sp, TOP_K), mod_p[5], 0)
        res_s = (yg, gate[n_p:].reshape(bs, ls, TOP_K), mod_s[5], n_p)

    y_prompt = _final(x_p, res_p, norm_final, SEQ_TILE)
    y_sample = _final(x_s, res_s, norm_final, ls)
    return (y_prompt, y_sample, jnp.stack(kp_l), jnp.stack(vp_l), jnp.stack(sp_l),
            jnp.stack(ks_l), jnp.stack(vs_l), jnp.stack(ss_l))
```

```python
import functools
import math

import jax
import jax.numpy as jnp
import numpy as np
from jax import lax
from jax.experimental import pallas as pl
from jax.experimental.pallas import tpu as pltpu
from jax.experimental.pallas import tpu_sc as plsc

F32 = jnp.float32
BF16 = jnp.bfloat16
I32 = jnp.int32
U32 = jnp.uint32

EPS = 1e-6
CHUNK = 64
LEFT_CHUNKS = 8
LEFT = LEFT_CHUNKS * CHUNK
MAX_REL = 2 * CHUNK
ATT_HEADS = 8
ATT_DIM = 64
ATT_W = ATT_HEADS * ATT_DIM
REC_HEADS = 4
REC_DIM = 128
REC_W = REC_HEADS * REC_DIM
N_EXPERTS = 32
TOP_K = 4
SWIGLU_ALPHA = 1.702
SWIGLU_LIMIT = 7.0
N_MOD = 6
NEG = -1e30
LOG2E = 1.4426950408889634

LANES = 128
ATT_QSUB = 128
ATT_WIN = LEFT + ATT_QSUB
SEQ_TILE = 512
REC_CHUNK = 128
MOE_ROWS = 512
ROUTE_TILE = 256
PROMPT_GROUPS = 2
VMEM_LIMIT = 56 * 1024 * 1024
SC_WORKERS = 32
SC_SCATTER_WIN = 64
SC_GATHER_WIN = 16
SC_GATHER_BUFS = 6


def _cparams(*sem):
    return pltpu.CompilerParams(dimension_semantics=sem, vmem_limit_bytes=VMEM_LIMIT)


def _split3(x):
    hi = x.astype(BF16)
    r1 = x - hi.astype(F32)
    mid = r1.astype(BF16)
    lo = (r1 - mid.astype(F32)).astype(BF16)
    return hi, mid, lo


def _bits(x):
    return lax.bitcast_convert_type(x, U32)


def _mod_kernel(c_ref, w_ref, b_ref, o_ref):
    c = c_ref[...]
    act = (c * jax.nn.sigmoid(c)).astype(BF16)
    o_ref[0] = jnp.dot(act, w_ref[0].astype(BF16), preferred_element_type=F32) + b_ref[0]


def _modulation(c_all, w_ada, b_ada):
    depth, d, n6 = w_ada.shape
    rows = c_all.shape[0]
    tn = 1536
    return pl.pallas_call(
        _mod_kernel,
        grid=(depth, n6 // tn),
        in_specs=[pl.BlockSpec((rows, d), lambda l, j: (0, 0)),
                  pl.BlockSpec((1, d, tn), lambda l, j: (l, 0, j)),
                  pl.BlockSpec((1, 1, tn), lambda l, j: (l, 0, j))],
        out_specs=pl.BlockSpec((1, rows, tn), lambda l, j: (l, 0, j)),
        out_shape=jax.ShapeDtypeStruct((depth, rows, n6), F32),
        compiler_params=_cparams("parallel", "parallel"),
        name="adaln_modulation",
    )(c_all, w_ada, b_ada.reshape(depth, 1, n6))


def _combine(yg_ref, gate_ref):
    g = gate_ref[0]
    acc = yg_ref[0] * g[:, 0:1]
    for k in range(1, TOP_K):
        acc = acc + yg_ref[k] * g[:, k:k + 1]
    return acc


def _inproj_kernel(has_res, *refs):
    if has_res:
        (x_ref, yg_ref, gate_ref, g2_ref, nw_ref, sc_ref, sh_ref, w_ref, lb_ref,
         xo_ref, qa_ref, ka_ref, va_ref, qh_ref, kh_ref, lf_ref, vh_ref, gh_ref) = refs
        x = x_ref[0] + g2_ref[0] * _combine(yg_ref, gate_ref)
        xo_ref[0] = x
    else:
        (x_ref, nw_ref, sc_ref, sh_ref, w_ref, lb_ref,
         qa_ref, ka_ref, va_ref, qh_ref, kh_ref, lf_ref, vh_ref, gh_ref) = refs
        x = x_ref[0]
    ms = jnp.mean(x * x, axis=-1, keepdims=True)
    h = (x * lax.rsqrt(ms + EPS) * nw_ref[...]) * (1.0 + sc_ref[0]) + sh_ref[0]
    hb = h.astype(BF16)

    def proj(g):
        return jnp.dot(hb, w_ref[:, g * ATT_W:(g + 1) * ATT_W], preferred_element_type=F32)

    qa_ref[0] = (proj(0) * (ATT_DIM ** -0.5)).astype(BF16)
    ka_ref[0] = proj(1)
    va_ref[0] = proj(2)
    qh = proj(3)
    qh_ref[0] = qh * jax.nn.sigmoid(qh)
    f = proj(4)
    lb = lb_ref[...]
    lf_ref[0] = jnp.log(lb + (1.0 - lb) * jax.nn.sigmoid(f))
    kh_ref[0] = (1.0 - lb) * jax.nn.sigmoid(-f)
    vh_ref[0] = proj(5).astype(BF16)
    gh_ref[0] = proj(6)


def _inproj(x, xb0, res, nw, sc, sh, w_bf, lb, ts):
    _, s, d = x.shape
    b = sc.shape[0]
    pw = w_bf.shape[1]
    nj = s // ts
    has_res = res is not None
    row = pl.BlockSpec((1, ts, d), lambda i, j: (i, j, 0))
    per_b = pl.BlockSpec((1, 1, d), lambda i, j: (i, 0, 0))
    half = pl.BlockSpec((1, ts, ATT_W), lambda i, j: (i, j, 0))
    in_specs, args = [pl.BlockSpec((1, ts, d), lambda i, j: (i + xb0, j, 0))], [x]
    if has_res:
        yg, gate, g2, first = res
        blk0 = first // ts
        in_specs += [pl.BlockSpec((TOP_K, ts, d), lambda i, j: (0, blk0 + i * nj + j, 0)),
                     pl.BlockSpec((1, ts, TOP_K), lambda i, j: (i, j, 0)), per_b]
        args += [yg, gate, g2]
    in_specs += [pl.BlockSpec((1, d), lambda i, j: (0, 0)), per_b, per_b,
                 pl.BlockSpec((d, pw), lambda i, j: (0, 0)),
                 pl.BlockSpec((1, REC_W), lambda i, j: (0, 0))]
    args += [nw.reshape(1, d), sc, sh, w_bf, lb.reshape(1, REC_W)]
    hs = jax.ShapeDtypeStruct((b, s, ATT_W), F32)
    hb = jax.ShapeDtypeStruct((b, s, ATT_W), BF16)
    out_shape = ([jax.ShapeDtypeStruct((b, s, d), F32)] if has_res else []) + [hb, hs, hs, hs, hs, hs, hb, hs]
    out_specs = ([row] if has_res else []) + [half] * 8
    outs = pl.pallas_call(
        functools.partial(_inproj_kernel, has_res),
        grid=(b, nj),
        in_specs=in_specs, out_specs=out_specs, out_shape=out_shape,
        compiler_params=_cparams("parallel", "parallel"),
        name="norm_inproj",
    )(*args)
    if has_res:
        return [(outs[0], 0)] + list(outs[1:])
    return [(x, xb0)] + list(outs)


def _attend(q, kw, vw, bias_ref, s_ref, col_thr):
    lq, lk = q.shape[0], kw.shape[0]
    lane = lax.broadcasted_iota(I32, (lq, LANES), 1)
    lo = lane < ATT_DIM
    nt = (((1,), (1,)), ((), ()))
    if col_thr is not None:
        keep = lax.broadcasted_iota(I32, (2 * lq, lk), 1) >= col_thr
    ms = []
    for p in range(ATT_HEADS // 2):
        sl = slice(p * LANES, (p + 1) * LANES)
        qp = q[:, sl]
        z = jnp.zeros_like(qp)
        lhs = jnp.concatenate([jnp.where(lo, qp, z), jnp.where(lo, z, qp)], axis=0)
        s = lax.dot_general(lhs, kw[:, sl], nt, preferred_element_type=F32) + bias_ref[p]
        if col_thr is not None:
            s = jnp.where(keep, s, NEG)
        ms.append(jnp.max(s, axis=-1, keepdims=True))
        s_ref[p] = s
    outs = []
    for p in range(ATT_HEADS // 2):
        sl = slice(p * LANES, (p + 1) * LANES)
        e = jnp.exp(s_ref[p] - ms[p])
        l = jnp.sum(e, axis=-1, keepdims=True)
        o = jnp.dot(e.astype(BF16), vw[:, sl], preferred_element_type=F32) / l
        outs.append(jnp.where(lo, o[:lq], o[lq:]))
    return jnp.concatenate(outs, axis=-1)


def _attn_prompt_kernel(q_ref, kp_ref, kc_ref, vp_ref, vc_ref, bias_ref, o_ref, kw_ref, vw_ref, s_ref):
    i = pl.program_id(1)
    ts = q_ref.shape[1]
    kw_ref[0:ts, :] = kp_ref[0].astype(BF16)
    kw_ref[ts:2 * ts, :] = kc_ref[0].astype(BF16)
    vw_ref[0:ts, :] = vp_ref[0].astype(BF16)
    vw_ref[ts:2 * ts, :] = vc_ref[0].astype(BF16)

    def body(masked, r, carry):
        r0 = pl.multiple_of(r * ATT_QSUB, ATT_QSUB)
        q = q_ref[0, pl.ds(r0, ATT_QSUB), :]
        kw = kw_ref[pl.ds(r0, ATT_WIN), :]
        vw = vw_ref[pl.ds(r0, ATT_WIN), :]
        o_ref[0, pl.ds(r0, ATT_QSUB), :] = _attend(q, kw, vw, bias_ref, s_ref, (ts - r0) if masked else None)
        return carry

    @pl.when(i == 0)
    def _():
        lax.fori_loop(0, ts // ATT_QSUB, functools.partial(body, True), 0)

    @pl.when(i > 0)
    def _():
        lax.fori_loop(0, ts // ATT_QSUB, functools.partial(body, False), 0)


def _attn_prompt(qa, ka, va, bias):
    b, s, w = qa.shape
    ts = LEFT
    cur = lambda i, j: (i, j, 0)
    prev = lambda i, j: (i, jnp.maximum(j - 1, 0), 0)
    blk = lambda im: pl.BlockSpec((1, ts, w), im)
    return pl.pallas_call(
        _attn_prompt_kernel,
        grid=(b, s // ts),
        in_specs=[blk(cur), blk(prev), blk(cur), blk(prev), blk(cur),
                  pl.BlockSpec(bias.shape, lambda i, j: (0, 0, 0))],
        out_specs=blk(cur),
        out_shape=jax.ShapeDtypeStruct((b, s, w), F32),
        scratch_shapes=[pltpu.VMEM((2 * ts, w), BF16), pltpu.VMEM((2 * ts, w), BF16),
                        pltpu.VMEM((ATT_HEADS // 2, 2 * ATT_QSUB, ATT_WIN), F32)],
        compiler_params=_cparams("parallel", "parallel"),
        name="band_attention",
    )(qa, ka, ka, va, va, bias)


def _attn_sample_kernel(q_ref, k_ref, v_ref, bias_ref, o_ref, s_ref):
    o_ref[0] = _attend(q_ref[0], k_ref[0], v_ref[0], bias_ref, s_ref, None)


def _attn_sample(qa, keys, vals, bias):
    b, l, w = qa.shape
    lk = keys.shape[1]
    return pl.pallas_call(
        _attn_sample_kernel,
        grid=(b,),
        in_specs=[pl.BlockSpec((1, l, w), lambda i: (i, 0, 0)),
                  pl.BlockSpec((1, lk, w), lambda i: (i, 0, 0)),
                  pl.BlockSpec((1, lk, w), lambda i: (i, 0, 0)),
                  pl.BlockSpec(bias.shape, lambda i: (0, 0, 0))],
        out_specs=pl.BlockSpec((1, l, w), lambda i: (i, 0, 0)),
        out_shape=jax.ShapeDtypeStruct((b, l, w), F32),
        scratch_shapes=[pltpu.VMEM((ATT_HEADS // 2, 2 * l, lk), F32)],
        compiler_params=_cparams("parallel"),
        name="cache_attention",
    )(qa, keys, vals, bias)


def _toeplitz_bias(table, lq, lk, offset, valid):
    i = np.arange(lq + lk - 1)
    gvec = table[:, np.clip(offset + (lq - 1) - i, -MAX_REL, MAX_REL) + MAX_REL]
    rows = [gvec[:, lq - 1 - t:lq - 1 - t + lk] for t in range(lq)]
    bias = jnp.where(jnp.asarray(valid)[None], jnp.stack(rows, axis=1), NEG).astype(F32)
    return bias.reshape(ATT_HEADS // 2, 2 * lq, lk)


def _prompt_bias(table):
    t = np.arange(ATT_QSUB)[:, None]
    j = np.arange(ATT_WIN)[None, :]
    start = (t // CHUNK) * CHUNK
    return _toeplitz_bias(table, ATT_QSUB, ATT_WIN, LEFT, (j >= start) & (j < start + LEFT + CHUNK))


def _sample_bias(table, l, w, lk):
    valid = np.broadcast_to(np.arange(lk)[None, :] < w + l, (l, lk))
    return _toeplitz_bias(table, l, lk, w, valid)


def _neg_abs(x):
    return lax.bitcast_convert_type(jnp.bitwise_or(_bits(x), jnp.uint32(0x80000000)), F32)


def _hgrn_kernel(chunk, q_ref, k_ref, lf_ref, v_ref, s0_ref, o_ref, sf_ref, st_ref):
    i = pl.program_id(1)
    ts = q_ref.shape[1]
    n_chunks = ts // chunk

    @pl.when(i == 0)
    def _():
        st_ref[...] = s0_ref[0]

    row = lax.broadcasted_iota(I32, (chunk, chunk), 0)
    colm = lax.broadcasted_iota(I32, (chunk, chunk), 1)
    tri = (colm <= row).astype(BF16)
    xor = jnp.bitwise_xor(row, colm)
    lower = colm < row
    rid = lax.broadcasted_iota(I32, (chunk, REC_W), 0)

    levels = []
    b = chunk // 2
    while b >= 1:
        levels.append(b)
        b //= 2
    pair_mask = [xor == 0] + [lower & (xor >= b) & (xor < 2 * b) for b in levels]
    upper = [jnp.bitwise_and(rid, b) != 0 for b in levels]
    ph4 = jnp.bitwise_and(rid, 3)
    odd = jnp.bitwise_and(rid, 1) == 1
    nt = (((1,), (1,)), ((), ()))
    tn = (((0,), (0,)), ((), ()))
    sls = [slice(h * REC_DIM, (h + 1) * REC_DIM) for h in range(REC_HEADS)]

    def boundary(cum, b):
        if 2 * b >= 8:
            pieces = [jnp.broadcast_to(cum[m * 2 * b + b - 1:m * 2 * b + b, :], (2 * b, REC_W))
                      for m in range(chunk // (2 * b))]
            return pieces[0] if len(pieces) == 1 else jnp.concatenate(pieces, axis=0)
        if b == 2:
            up1 = pltpu.roll(cum, chunk - 1, 0)
            dn1 = pltpu.roll(cum, 1, 0)
            dn2 = pltpu.roll(cum, 2, 0)
            return jnp.where(ph4 == 0, up1, jnp.where(ph4 == 1, cum, jnp.where(ph4 == 2, dn1, dn2)))
        return jnp.where(odd, pltpu.roll(cum, 1, 0), cum)

    def one_chunk(c, carry):
        r0 = pl.multiple_of(c * chunk, chunk)
        rows = pl.ds(r0, chunk)
        q = q_ref[0, rows, :]
        k = k_ref[0, rows, :]
        v = v_ref[0, rows, :]
        lf = lf_ref[0, rows, :]
        cum = sum(jnp.dot(tri, part, preferred_element_type=F32) for part in _split3(lf)) * LOG2E
        last = cum[chunk - 1:chunk, :]
        q_in = (q * jnp.exp2(cum)).astype(BF16)
        k_out = (k * jnp.exp2(last - cum)).astype(BF16)
        decay = jnp.exp2(last)
        qb = q.astype(BF16)
        kb = k.astype(BF16)
        zs = []
        for n, b in enumerate(levels):
            w = jnp.exp2(_neg_abs(cum - boundary(cum, b)))
            zs.append((jnp.where(upper[n], q, k) * w).astype(BF16))
        sts = [st_ref[h] for h in range(REC_HEADS)]
        d0 = [lax.dot_general(qb[:, sl], kb[:, sl], nt, preferred_element_type=F32) for sl in sls]
        dl = [[lax.dot_general(z[:, sl], z[:, sl], nt, preferred_element_type=F32) for z in zs] for sl in sls]
        oi = [lax.dot_general(q_in[:, sl], st.astype(BF16), nt, preferred_element_type=F32)
              for sl, st in zip(sls, sts)]
        upd = [lax.dot_general(v[:, sl], k_out[:, sl], tn, preferred_element_type=F32) for sl in sls]
        outs = []
        for h, sl in enumerate(sls):
            a = jnp.where(pair_mask[0], d0[h], 0.0)
            for n in range(len(levels)):
                a = jnp.where(pair_mask[n + 1], dl[h][n], a)
            outs.append(jnp.dot(a.astype(BF16), v[:, sl], preferred_element_type=F32) + oi[h])
        o_ref[0, rows, :] = jnp.concatenate(outs, axis=-1)
        for h, sl in enumerate(sls):
            st_ref[h] = decay[:, sl] * sts[h] + upd[h]
        return carry

    lax.fori_loop(0, n_chunks, one_chunk, 0)

    @pl.when(i == pl.num_programs(1) - 1)
    def _():
        sf_ref[0] = st_ref[...]


def _hgrn(q, k, lf, v, s0_t, ts, chunk):
    b, s, w = q.shape
    blk = pl.BlockSpec((1, ts, w), lambda i, j: (i, j, 0))
    st = pl.BlockSpec((1, REC_HEADS, REC_DIM, REC_DIM), lambda i, j: (i, 0, 0, 0))
    return pl.pallas_call(
        functools.partial(_hgrn_kernel, chunk),
        grid=(b, s // ts),
        in_specs=[blk, blk, blk, blk, st],
        out_specs=[blk, st],
        out_shape=[jax.ShapeDtypeStruct((b, s, w), F32),
                   jax.ShapeDtypeStruct((b, REC_HEADS, REC_DIM, REC_DIM), F32)],
        scratch_shapes=[pltpu.VMEM((REC_HEADS, REC_DIM, REC_DIM), F32)],
        compiler_params=_cparams("parallel", "arbitrary"),
        name="hgrn_recurrence",
    )(q, k, lf, v, s0_t)


def _head_rms(x, width):
    lane = lax.broadcasted_iota(I32, (x.shape[0], LANES), 1)
    lo = lane < ATT_DIM
    outs = []
    for p in range(x.shape[1] // LANES):
        xp = x[:, p * LANES:(p + 1) * LANES]
        sq = xp * xp
        tot = jnp.sum(sq, axis=-1, keepdims=True)
        if width == LANES:
            ms = tot * (1.0 / LANES)
        else:
            s_lo = jnp.sum(jnp.where(lo, sq, 0.0), axis=-1, keepdims=True)
            ms = jnp.where(lo, s_lo, tot - s_lo) * (1.0 / ATT_DIM)
        outs.append(xp * lax.rsqrt(ms + EPS))
    return jnp.concatenate(outs, axis=-1)


def _merge_kernel(oa_ref, or_ref, gh_ref, x_ref, ga_ref, gr_ref, wo_ref, g1_ref, nf_ref, sc_ref, sh_ref,
                  wr_ref, br_ref, xo_ref, h2_ref, lg_ref):
    a = _head_rms(oa_ref[0], ATT_DIM) * ga_ref[...]
    gh = gh_ref[0]
    r = _head_rms(or_ref[0], REC_DIM) * gr_ref[...] * (gh * jax.nn.sigmoid(gh))
    mix = jnp.dot(a.astype(BF16), wo_ref[0:ATT_W, :], preferred_element_type=F32)
    mix += jnp.dot(r.astype(BF16), wo_ref[ATT_W:ATT_W + REC_W, :], preferred_element_type=F32)
    x = x_ref[0] + g1_ref[0] * mix
    xo_ref[0] = x
    ms = jnp.mean(x * x, axis=-1, keepdims=True)
    h2 = (x * lax.rsqrt(ms + EPS) * nf_ref[...]) * (1.0 + sc_ref[0]) + sh_ref[0]
    hh = h2.astype(BF16)
    hr = _bits(hh.astype(F32))
    half = h2.shape[1] // 2
    h2_ref[0] = jnp.bitwise_or(jnp.bitwise_and(hr[:, half:], jnp.uint32(0xFFFF0000)),
                               jnp.right_shift(hr[:, :half], jnp.uint32(16)))
    hm = (h2 - hh.astype(F32)).astype(BF16)
    nt = (((1,), (1,)), ((), ()))
    l1 = lax.dot_general(wr_ref[...], hh, nt, preferred_element_type=F32)
    l2 = lax.dot_general(wr_ref[0:N_EXPERTS, :], hm, nt, preferred_element_type=F32)
    lg = l1[0:N_EXPERTS] + l1[N_EXPERTS:2 * N_EXPERTS] + l2 + br_ref[...]
    if len(lg_ref.shape) == 3:
        lg_ref[0] = lg
    else:
        lg_ref[...] = lg


def _merge(o_att, o_rec, gh, x, xb0, g_att, g_rec, wo_bf, g1, nf, sc, sh, wr2, br, ts):
    _, s, d = x.shape
    b = o_att.shape[0]
    nj = s // ts
    row = pl.BlockSpec((1, ts, d), lambda i, j: (i, j, 0))
    xrow = pl.BlockSpec((1, ts, d), lambda i, j: (i + xb0, j, 0))
    half = pl.BlockSpec((1, ts, ATT_W), lambda i, j: (i, j, 0))
    per_b = pl.BlockSpec((1, 1, d), lambda i, j: (i, 0, 0))
    const2 = lambda shp: pl.BlockSpec(shp, lambda i, j: (0,) * len(shp))
    if ts % LANES == 0:
        lg_spec = pl.BlockSpec((N_EXPERTS, ts), lambda i, j: (0, i * nj + j))
        lg_shape = jax.ShapeDtypeStruct((N_EXPERTS, b * s), F32)
    else:
        lg_spec = pl.BlockSpec((1, N_EXPERTS, ts), lambda i, j: (i, 0, j))
        lg_shape = jax.ShapeDtypeStruct((b, N_EXPERTS, s), F32)
    return pl.pallas_call(
        _merge_kernel,
        grid=(b, nj),
        in_specs=[half, half, half, xrow, const2((1, ATT_W)), const2((1, REC_W)), const2(wo_bf.shape),
                  per_b, const2((1, d)), per_b, per_b, const2(wr2.shape), const2((N_EXPERTS, 1))],
        out_specs=[row, pl.BlockSpec((1, ts, d // 2), lambda i, j: (i, j, 0)), lg_spec],
        out_shape=[jax.ShapeDtypeStruct((b, s, d), F32), jax.ShapeDtypeStruct((b, s, d // 2), U32), lg_shape],
        compiler_params=_cparams("parallel", "parallel"),
        name="merge_outproj_router",
    )(o_att, o_rec, gh, x, g_att.reshape(1, ATT_W), g_rec.reshape(1, REC_W), wo_bf, g1, nf.reshape(1, d),
      sc, sh, wr2, br.reshape(N_EXPERTS, 1))


def _route_kernel(lg_ref, idx_ref, gate_ref, rank_ref, cnt_ref, run_ref):
    i = pl.program_id(0)
    t = lg_ref.shape[1]

    @pl.when(i == 0)
    def _():
        run_ref[...] = jnp.zeros_like(run_ref)

    x = lg_ref[...]
    row = lax.broadcasted_iota(I32, x.shape, 0)
    vals, idxs, hits = [], [], []
    for _ in range(TOP_K):
        m = jnp.max(x, axis=0, keepdims=True)
        ik = jnp.min(jnp.where(x == m, row, N_EXPERTS), axis=0, keepdims=True)
        hit = row == ik
        x = jnp.where(hit, -jnp.inf, x)
        vals.append(m)
        idxs.append(ik)
        hits.append(hit)
    es = [jnp.exp(v - vals[0]) for v in vals]
    tot = es[0] + es[1] + es[2] + es[3]
    gate_ref[...] = jnp.concatenate([e / tot for e in es], axis=0)
    idx_ref[...] = jnp.concatenate(idxs, axis=0)
    chosen = (hits[0] | hits[1] | hits[2] | hits[3])
    onehot = jnp.where(chosen, 1.0, 0.0)
    earlier = (lax.broadcasted_iota(I32, (t, t), 0) < lax.broadcasted_iota(I32, (t, t), 1)).astype(BF16)
    run = run_ref[:, 0:1]
    before = jnp.dot(onehot.astype(BF16), earlier, preferred_element_type=F32) + run
    rank_ref[...] = jnp.concatenate(
        [jnp.sum(jnp.where(h, before, 0.0), axis=0, keepdims=True) for h in hits], axis=0).astype(I32)
    run = run + jnp.sum(onehot, axis=1, keepdims=True)
    run_ref[...] = jnp.broadcast_to(run, run_ref.shape)
    cnt_ref[...] = jnp.broadcast_to(run, cnt_ref.shape)


def _route(lg_t):
    e, n = lg_t.shape
    t = ROUTE_TILE
    tok = pl.BlockSpec((TOP_K, t), lambda i: (0, i))
    idx, gate, rank, cnt = pl.pallas_call(
        _route_kernel,
        grid=(n // t,),
        in_specs=[pl.BlockSpec((e, t), lambda i: (0, i))],
        out_specs=[tok, tok, tok, pl.BlockSpec((e, LANES), lambda i: (0, 0))],
        out_shape=[jax.ShapeDtypeStruct((TOP_K, n), I32), jax.ShapeDtypeStruct((TOP_K, n), F32),
                   jax.ShapeDtypeStruct((TOP_K, n), I32), jax.ShapeDtypeStruct((e, LANES), F32)],
        scratch_shapes=[pltpu.VMEM((e, LANES), F32)],
        compiler_params=_cparams("arbitrary"),
        name="moe_route",
    )(lg_t)
    counts = cnt[:, 0].astype(I32)
    padded = (counts + MOE_ROWS - 1) // MOE_ROWS * MOE_ROWS
    pad_end = jnp.cumsum(padded)
    pad_start = pad_end - padded
    m = n * TOP_K
    nb = (m + N_EXPERTS * (MOE_ROWS - 1) + MOE_ROWS - 1) // MOE_ROWS
    block_start = jnp.arange(nb, dtype=I32) * MOE_ROWS
    block_expert = jnp.minimum(jnp.sum((pad_end[None, :] <= block_start[:, None]).astype(I32), axis=1), N_EXPERTS - 1)
    block_info = jnp.concatenate([block_expert, pad_end[-1:] // MOE_ROWS]).astype(I32)
    start_of = jnp.sum(jnp.where(idx[:, :, None] == jnp.arange(N_EXPERTS, dtype=I32), pad_start, 0), axis=-1)
    dest = start_of + rank
    return gate, dest, block_info, nb * MOE_ROWS


def _windows(dest, w, multiple, fill):
    kk, n = dest.shape
    nwin = n // w
    per = -(-nwin // SC_WORKERS)
    per = -(-per // multiple) * multiple
    idx = dest.reshape(kk, nwin, w).transpose(1, 0, 2)
    pad = jnp.broadcast_to(fill[None, None, :], (SC_WORKERS * per - nwin, kk, w)).astype(I32)
    return jnp.concatenate([idx, pad], axis=0)


def _sc_scatter_rows(x, idx, p_rows):
    n, c = x.shape
    nwin, kk, w = idx.shape
    per = nwin // SC_WORKERS
    nreal = n // w
    mesh = plsc.VectorSubcoreMesh(core_axis_name="c", subcore_axis_name="s")

    @functools.partial(
        pl.kernel, mesh=mesh,
        out_type=jax.ShapeDtypeStruct((p_rows, c), x.dtype),
        scratch_types=[pltpu.VMEM((per, kk, w), I32), pltpu.VMEM((2, w, c), x.dtype),
                       pltpu.SemaphoreType.DMA((2,)), pltpu.SemaphoreType.DMA((2,))],
        name="sc_scatter_rows",
    )
    def k(x_hbm, idx_hbm, out_hbm, idx_v, rows_v, lsem, ssem):
        wid = lax.axis_index("s") * 2 + lax.axis_index("c")
        g0 = wid * per
        pltpu.sync_copy(idx_hbm.at[pl.ds(g0, per)], idx_v)

        def load(j, b):
            g = jnp.minimum(g0 + j, nreal - 1)
            return pltpu.make_async_copy(x_hbm.at[pl.ds(g * w, w)], rows_v.at[b], lsem.at[b])

        def scat(j, b, q):
            return pltpu.make_async_copy(rows_v.at[b], out_hbm.at[idx_v.at[j, q]], ssem.at[b])

        @pl.loop(0, per + 2, step=2)
        def _(j0):
            for b in range(2):
                j = j0 + b

                @pl.when(jnp.logical_and(j >= 2, j < per + 2))
                def _():
                    for q in range(kk):
                        scat(j - 2, b, q).wait()

                @pl.when(j < per)
                def _():
                    load(j, b).start()

                @pl.when(jnp.logical_and(j >= 1, j < per + 1))
                def _():
                    load(j - 1, 1 - b).wait()
                    for q in range(kk):
                        scat(j - 1, 1 - b, q).start()

    return k(x, idx)


def _sc_gather_rows(y, idx):
    p, c = y.shape
    nwin, kk, w = idx.shape
    nb = SC_GATHER_BUFS
    per = nwin // SC_WORKERS
    group = math.lcm(nb, kk)
    wins = group // kk
    n_items = per * kk
    mesh = plsc.VectorSubcoreMesh(core_axis_name="c", subcore_axis_name="s")

    @functools.partial(
        pl.kernel, mesh=mesh,
        out_type=jax.ShapeDtypeStruct((kk, nwin * w, c), y.dtype),
        scratch_types=[pltpu.VMEM((2, wins, kk, w), I32), pltpu.VMEM((nb, w, c), y.dtype),
                       pltpu.SemaphoreType.DMA((nb,)), pltpu.SemaphoreType.DMA((nb,))],
        name="sc_gather_rows",
    )
    def k(y_hbm, idx_hbm, out_hbm, idx_v, rows_v, gsem, wsem):
        wid = lax.axis_index("s") * 2 + lax.axis_index("c")
        g0 = wid * per

        def gath(slot, jj, q, b):
            return pltpu.make_async_copy(y_hbm.at[idx_v.at[slot, jj, q]], rows_v.at[b], gsem.at[b])

        def wr(i, b):
            return pltpu.make_async_copy(rows_v.at[b], out_hbm.at[i % kk, pl.ds((g0 + i // kk) * w, w)], wsem.at[b])

        n_groups = -(-(n_items + nb) // group)

        @pl.loop(0, n_groups)
        def _(gi):
            slot = gi % 2

            @pl.when(gi * wins < per)
            def _():
                pltpu.sync_copy(idx_hbm.at[pl.ds(g0 + gi * wins, wins)], idx_v.at[slot])

            for l in range(group):
                i = gi * group + l
                b = l % nb
                bw = (l - (nb - 1)) % nb

                @pl.when(jnp.logical_and(i >= nb, i < n_items + nb))
                def _():
                    wr(i - nb, b).wait()

                @pl.when(i < n_items)
                def _():
                    gath(slot, l // kk, l % kk, b).start()

                @pl.when(jnp.logical_and(i >= nb - 1, i < n_items + nb - 1))
                def _():
                    gath(slot, 0, 0, bw).wait()
                    wr(i - (nb - 1), bw).start()

    return k(y, idx)


def _moe_kernel(be_ref, x_ref, w1_ref, b1_ref, w2_ref, b2_ref, o_ref, w1b_ref, w2b_ref):
    i = pl.program_id(0)
    changed = jnp.logical_or(i == 0, be_ref[i] != be_ref[jnp.maximum(i - 1, 0)])

    @pl.when(changed)
    def _():
        w1b_ref[...] = w1_ref[0, 0].astype(BF16)
        w2b_ref[...] = w2_ref[0, 0].astype(BF16)

    @pl.when(i < be_ref[pl.num_programs(0)])
    def _():
        _expert_block(x_ref, w1b_ref, b1_ref, w2b_ref, b2_ref, o_ref)


def _expert_block(x_ref, w1b_ref, b1_ref, w2b_ref, b2_ref, o_ref):
    word = x_ref[...]
    lo = lax.bitcast_convert_type(jnp.left_shift(word, jnp.uint32(16)), F32).astype(BF16)
    hi = lax.bitcast_convert_type(jnp.bitwise_and(word, jnp.uint32(0xFFFF0000)), F32).astype(BF16)
    x = jnp.concatenate([lo, hi], axis=-1)
    f = w2b_ref.shape[0]
    fc = 512
    acc = None
    for c in range(f // fc):
        gt = jnp.dot(x, w1b_ref[:, c * fc:(c + 1) * fc], preferred_element_type=F32) + b1_ref[0, 0, :, c * fc:(c + 1) * fc]
        up = jnp.dot(x, w1b_ref[:, f + c * fc:f + (c + 1) * fc], preferred_element_type=F32) + b1_ref[0, 0, :, f + c * fc:f + (c + 1) * fc]
        gt = jnp.minimum(gt, SWIGLU_LIMIT)
        up = jnp.clip(up, -SWIGLU_LIMIT, SWIGLU_LIMIT)
        act = (gt * jax.nn.sigmoid(SWIGLU_ALPHA * gt) * (up + 1.0)).astype(BF16)
        part = jnp.dot(act, w2b_ref[c * fc:(c + 1) * fc, :], preferred_element_type=F32)
        acc = part if acc is None else acc + part
    o_ref[...] = acc + b2_ref[0, 0]


def _moe_experts(layer, block_expert, xs, p_rows, w1, b1, w2, b2):
    _, e, d, f2 = w1.shape
    f = f2 // 2
    nb = p_rows // MOE_ROWS
    grid_spec = pltpu.PrefetchScalarGridSpec(
        num_scalar_prefetch=1,
        grid=(nb,),
        in_specs=[pl.BlockSpec((MOE_ROWS, d // 2), lambda i, be: (i, 0)),
                  pl.BlockSpec((1, 1, d, f2), lambda i, be: (layer, be[i], 0, 0)),
                  pl.BlockSpec((1, 1, 1, f2), lambda i, be: (layer, be[i], 0, 0)),
                  pl.BlockSpec((1, 1, f, d), lambda i, be: (layer, be[i], 0, 0)),
                  pl.BlockSpec((1, 1, 1, d), lambda i, be: (layer, be[i], 0, 0))],
        out_specs=pl.BlockSpec((MOE_ROWS, d), lambda i, be: (i, 0)),
        scratch_shapes=[pltpu.VMEM((d, f2), BF16), pltpu.VMEM((f, d), BF16)],
    )
    return pl.pallas_call(
        _moe_kernel,
        grid_spec=grid_spec,
        out_shape=jax.ShapeDtypeStruct((p_rows, d), F32),
        compiler_params=_cparams("arbitrary"),
        name="moe_experts",
    )(block_expert, xs, w1, b1.reshape(b1.shape[0], e, 1, f2), w2, b2.reshape(b2.shape[0], e, 1, d))


def _moe_dispatch(h2_words, lg_t):
    gate, dest, block_info, p_rows = _route(lg_t)
    spare = p_rows + jnp.arange(SC_SCATTER_WIN, dtype=I32)
    xs = _sc_scatter_rows(h2_words, _windows(dest, SC_SCATTER_WIN, 2, spare), p_rows + SC_SCATTER_WIN)
    return xs, gate.T, dest, block_info, p_rows


def _moe_compute(layer, dispatched, w1, b1, w2, b2):
    xs, gate, dest, block_info, p_rows = dispatched
    ys = _moe_experts(layer, block_info, xs, p_rows, w1, b1, w2, b2)
    group_wins = math.lcm(SC_GATHER_BUFS, TOP_K) // TOP_K
    yg = _sc_gather_rows(ys, _windows(dest, SC_GATHER_WIN, group_wins, jnp.zeros((SC_GATHER_WIN,), I32)))
    return yg, gate


def _final_kernel(x_ref, yg_ref, gate_ref, g2_ref, nw_ref, *rest):
    o_ref = rest[-1]
    x = x_ref[0] + g2_ref[0] * _combine(yg_ref, gate_ref)
    ms = jnp.mean(x * x, axis=-1, keepdims=True)
    o_ref[0] = x * lax.rsqrt(ms + EPS) * nw_ref[...]


def _final(x, res, nw, ts, out_buf, ob0, b_total):
    b, s, d = x.shape
    nj = s // ts
    yg, gate, g2, first = res
    blk0 = first // ts
    row = pl.BlockSpec((1, ts, d), lambda i, j: (i, j, 0))
    in_specs = [row, pl.BlockSpec((TOP_K, ts, d), lambda i, j: (0, blk0 + i * nj + j, 0)),
                pl.BlockSpec((1, ts, TOP_K), lambda i, j: (i, j, 0)),
                pl.BlockSpec((1, 1, d), lambda i, j: (i, 0, 0)),
                pl.BlockSpec((1, d), lambda i, j: (0, 0))]
    args = [x, yg, gate, g2, nw.reshape(1, d)]
    aliases = {}
    if out_buf is not None:
        in_specs.append(pl.BlockSpec(memory_space=pl.ANY))
        args.append(out_buf)
        aliases = {len(args) - 1: 0}
    return pl.pallas_call(
        _final_kernel,
        grid=(b, nj),
        in_specs=in_specs,
        out_specs=pl.BlockSpec((1, ts, d), lambda i, j: (i + ob0, j, 0)),
        out_shape=jax.ShapeDtypeStruct((b_total, s, d), F32),
        input_output_aliases=aliases,
        compiler_params=_cparams("parallel", "parallel"),
        name="final_norm",
    )(*args)


def kernel(x_prompt, x_sample, c_prompt, c_sample, cache_k, cache_v, state_hgrn, norm_mix, norm_ffn, norm_final, w_ada, b_ada, w_in, rel_bias, hgrn_lb_logits, g_attn_out, g_hgrn_out, w_out, w_router, b_router, w_e1, b_e1, w_e2, b_e2):
    depth = w_in.shape[0]
    bp, sp, d = x_prompt.shape
    bs, ls, _ = x_sample.shape
    wc = cache_k.shape[2]
    n_p, n_s = bp * sp, bs * ls

    lbs = jax.nn.softmax(hgrn_lb_logits.astype(F32), axis=0)
    lbs = jnp.cumsum(lbs, axis=0) - lbs[0]

    mods = _modulation(jnp.concatenate([c_prompt, c_sample], axis=0), w_ada, b_ada)
    lk_s = -(-(wc + ls) // LANES) * LANES
    tail = min(LEFT, sp)
    rec_chunk = min(REC_CHUNK, sp)

    n_groups = PROMPT_GROUPS if bp % PROMPT_GROUPS == 0 else 1
    gb = bp // n_groups
    n_g = gb * sp
    x_g = [(x_prompt, g * gb) for g in range(n_groups)]
    x_s = (x_sample, 0)
    res_g = [None] * n_groups
    res_s = None
    kp_l, vp_l, sp_l, ks_l, vs_l, ss_l = [], [], [], [], [], []
    for layer in range(depth):
        m6 = mods[layer].reshape(bp + bs, 1, N_MOD, d)
        mod_s = [m6[bp:, :, n, :] for n in range(N_MOD)]
        w_in_bf = w_in[layer].astype(BF16)
        w_out_bf = w_out[layer].astype(BF16)
        wr_hi, wr_mid, _ = _split3(w_router[layer].T)
        wr2 = jnp.concatenate([wr_hi, wr_mid], axis=0)
        table = rel_bias[layer]
        bias_p = _prompt_bias(table)

        def mixer_prompt(g):
            mod = [m6[g * gb:(g + 1) * gb, :, n, :] for n in range(N_MOD)]
            x, qa, ka, va, qh, kh, lf, vh, gh = _inproj(
                x_g[g][0], x_g[g][1], res_g[g], norm_mix[layer], mod[1], mod[0], w_in_bf, lbs[layer],
                SEQ_TILE if res_g[g] is None else SEQ_TILE // 2)
            o_att = _attn_prompt(qa, ka, va, bias_p)
            s0 = jnp.zeros((gb, REC_HEADS, REC_DIM, REC_DIM), F32)
            o_rec, s_fin = _hgrn(qh, kh, lf, vh, s0, SEQ_TILE, rec_chunk)
            x_new, h2, lg = _merge(o_att, o_rec, gh, x[0], x[1], g_attn_out[layer], g_hgrn_out[layer], w_out_bf,
                                   mod[2], norm_ffn[layer], mod[4], mod[3], wr2, b_router[layer], SEQ_TILE)
            x_g[g] = (x_new, 0)
            return h2.reshape(n_g, d // 2), lg, ka[:, sp - tail:], va[:, sp - tail:], s_fin, mod[5]

        def mixer_sample():
            x, qa, ka, va, qh, kh, lf, vh, gh = _inproj(
                x_s[0], x_s[1], res_s, norm_mix[layer], mod_s[1], mod_s[0], w_in_bf, lbs[layer], ls)
            zpad = jnp.zeros((bs, lk_s - wc - ls, ATT_W), BF16)
            keys = jnp.concatenate([cache_k[layer].reshape(bs, wc, ATT_W).astype(BF16), ka.astype(BF16), zpad], axis=1)
            vals = jnp.concatenate([cache_v[layer].reshape(bs, wc, ATT_W).astype(BF16), va.astype(BF16), zpad], axis=1)
            o_att = _attn_sample(qa, keys, vals, _sample_bias(table, ls, wc, lk_s))
            s0 = jnp.swapaxes(state_hgrn[layer].astype(F32), -1, -2)
            o_rec, s_new = _hgrn(qh, kh, lf, vh, s0, ls, ls)
            x_new, h2, lg = _merge(o_att, o_rec, gh, x[0], x[1], g_attn_out[layer], g_hgrn_out[layer], w_out_bf,
                                   mod_s[2], norm_ffn[layer], mod_s[4], mod_s[3], wr2, b_router[layer], ls)
            return (x_new, 0), h2.reshape(n_s, d // 2), jnp.swapaxes(lg, 0, 1).reshape(N_EXPERTS, n_s), ka, va, s_new

        x_s, h2_s, lg_s, ka_s, va_s, s_new = mixer_sample()
        ks_l.append(ka_s.reshape(bs, ls, ATT_HEADS, ATT_DIM))
        vs_l.append(va_s.reshape(bs, ls, ATT_HEADS, ATT_DIM))
        ss_l.append(jnp.swapaxes(s_new, -1, -2))

        dispatched, g2s, k_t, v_t, s_t = [], [], [], [], []
        for g in range(n_groups):
            h2, lg, kt, vt, s_fin, g2 = mixer_prompt(g)
            if g == n_groups - 1:
                h2 = jnp.concatenate([h2, h2_s], axis=0)
                lg = jnp.concatenate([lg, lg_s], axis=1)
            dispatched.append(_moe_dispatch(h2, lg))
            g2s.append(g2)
            k_t.append(kt)
            v_t.append(vt)
            s_t.append(s_fin)
        kp_l.append(jnp.concatenate(k_t, axis=0).reshape(bp, tail, ATT_HEADS, ATT_DIM))
        vp_l.append(jnp.concatenate(v_t, axis=0).reshape(bp, tail, ATT_HEADS, ATT_DIM))
        sp_l.append(jnp.swapaxes(jnp.concatenate(s_t, axis=0), -1, -2))

        for g in range(n_groups):
            yg, gate = _moe_compute(layer, dispatched[g], w_e1, b_e1, w_e2, b_e2)
            res_g[g] = (yg, gate[:n_g].reshape(gb, sp, TOP_K), g2s[g], 0)
            if g == n_groups - 1:
                res_s = (yg, gate[n_g:].reshape(bs, ls, TOP_K), mod_s[5], n_g)

    y_prompt = None
    for g in range(n_groups):
        y_prompt = _final(x_g[g][0], res_g[g], norm_final, SEQ_TILE, y_prompt, g * gb, bp)
    y_sample = _final(x_s[0], res_s, norm_final, ls, None, 0, bs)
    return (y_prompt, y_sample, jnp.stack(kp_l), jnp.stack(vp_l), jnp.stack(sp_l),
            jnp.stack(ks_l), jnp.stack(vs_l), jnp.stack(ss_l))
```

```python
import functools
import math

import jax
import jax.numpy as jnp
import numpy as np
from jax import lax
from jax.experimental import pallas as pl
from jax.experimental.pallas import tpu as pltpu
from jax.experimental.pallas import tpu_sc as plsc

F32 = jnp.float32
BF16 = jnp.bfloat16
I32 = jnp.int32
U32 = jnp.uint32

EPS = 1e-6
CHUNK = 64
LEFT_CHUNKS = 8
LEFT = LEFT_CHUNKS * CHUNK
MAX_REL = 2 * CHUNK
ATT_HEADS = 8
ATT_DIM = 64
ATT_W = ATT_HEADS * ATT_DIM
REC_HEADS = 4
REC_DIM = 128
REC_W = REC_HEADS * REC_DIM
N_EXPERTS = 32
TOP_K = 4
SWIGLU_ALPHA = 1.702
SWIGLU_LIMIT = 7.0
N_MOD = 6
NEG = -1e30
LOG2E = 1.4426950408889634

LANES = 128
ATT_QSUB = 128
ATT_WIN = LEFT + ATT_QSUB
SEQ_TILE = 512
REC_CHUNK = 128
MOE_ROWS = 512
ROUTE_TILE = 256
PROMPT_GROUPS = 2
VMEM_LIMIT = 56 * 1024 * 1024
SC_WORKERS = 32
SC_SCATTER_WIN = 64
SC_GATHER_WIN = 32
SC_GATHER_BUFS = 6


def _cparams(*sem):
    return pltpu.CompilerParams(dimension_semantics=sem, vmem_limit_bytes=VMEM_LIMIT)


def _split3(x):
    hi = x.astype(BF16)
    r1 = x - hi.astype(F32)
    mid = r1.astype(BF16)
    lo = (r1 - mid.astype(F32)).astype(BF16)
    return hi, mid, lo


def _bits(x):
    return lax.bitcast_convert_type(x, U32)


def _pack_pairs(x):
    r = _bits(x.astype(BF16).astype(F32))
    half = x.shape[1] // 2
    return jnp.bitwise_or(jnp.bitwise_and(r[:, half:], jnp.uint32(0xFFFF0000)),
                          jnp.right_shift(r[:, :half], jnp.uint32(16)))


def _unpack_pairs(word):
    lo = lax.bitcast_convert_type(jnp.left_shift(word, jnp.uint32(16)), F32)
    hi = lax.bitcast_convert_type(jnp.bitwise_and(word, jnp.uint32(0xFFFF0000)), F32)
    return lo, hi


def _mod_kernel(c_ref, w_ref, b_ref, o_ref):
    c = c_ref[...]
    act = (c * jax.nn.sigmoid(c)).astype(BF16)
    o_ref[0] = jnp.dot(act, w_ref[0].astype(BF16), preferred_element_type=F32) + b_ref[0]


def _modulation(c_all, w_ada, b_ada):
    depth, d, n6 = w_ada.shape
    rows = c_all.shape[0]
    tn = 1536
    return pl.pallas_call(
        _mod_kernel,
        grid=(depth, n6 // tn),
        in_specs=[pl.BlockSpec((rows, d), lambda l, j: (0, 0)),
                  pl.BlockSpec((1, d, tn), lambda l, j: (l, 0, j)),
                  pl.BlockSpec((1, 1, tn), lambda l, j: (l, 0, j))],
        out_specs=pl.BlockSpec((1, rows, tn), lambda l, j: (l, 0, j)),
        out_shape=jax.ShapeDtypeStruct((depth, rows, n6), F32),
        compiler_params=_cparams("parallel", "parallel"),
        name="adaln_modulation",
    )(c_all, w_ada, b_ada.reshape(depth, 1, n6))


def _combine(yg_ref, gate_ref):
    g = gate_ref[0]
    lo = hi = None
    for k in range(TOP_K):
        l, h = _unpack_pairs(yg_ref[k])
        gk = g[:, k:k + 1]
        lo = l * gk if lo is None else lo + l * gk
        hi = h * gk if hi is None else hi + h * gk
    return jnp.concatenate([lo, hi], axis=-1)


def _inproj_kernel(has_res, *refs):
    if has_res:
        (x_ref, yg_ref, gate_ref, g2_ref, nw_ref, sc_ref, sh_ref, w_ref, lb_ref,
         xo_ref, qa_ref, ka_ref, va_ref, kt_ref, vt_ref, qh_ref, kh_ref, lf_ref, vh_ref, gh_ref) = refs
        x = x_ref[0] + g2_ref[0] * _combine(yg_ref, gate_ref)
        xo_ref[0] = x
    else:
        (x_ref, nw_ref, sc_ref, sh_ref, w_ref, lb_ref,
         qa_ref, ka_ref, va_ref, kt_ref, vt_ref, qh_ref, kh_ref, lf_ref, vh_ref, gh_ref) = refs
        x = x_ref[0]
    ms = jnp.mean(x * x, axis=-1, keepdims=True)
    h = (x * lax.rsqrt(ms + EPS) * nw_ref[...]) * (1.0 + sc_ref[0]) + sh_ref[0]
    hb = h.astype(BF16)

    def proj(g):
        return jnp.dot(hb, w_ref[:, g * ATT_W:(g + 1) * ATT_W], preferred_element_type=F32)

    qa_ref[0] = (proj(0) * (ATT_DIM ** -0.5)).astype(BF16)
    k = proj(1)
    ka_ref[0] = k.astype(BF16)
    kt_ref[0] = k
    v = proj(2)
    va_ref[0] = v.astype(BF16)
    vt_ref[0] = v
    qh = proj(3)
    qh_ref[0] = qh * jax.nn.sigmoid(qh)
    f = proj(4)
    lb = lb_ref[...]
    lf_ref[0] = jnp.log(lb + (1.0 - lb) * jax.nn.sigmoid(f))
    kh_ref[0] = (1.0 - lb) * jax.nn.sigmoid(-f)
    vh_ref[0] = proj(5).astype(BF16)
    gh_ref[0] = proj(6)


def _inproj(x, xb0, res, nw, sc, sh, w_bf, lb, ts, tail):
    _, s, d = x.shape
    b = sc.shape[0]
    pw = w_bf.shape[1]
    nj = s // ts
    n_tail = tail // ts
    has_res = res is not None
    tail_spec = pl.BlockSpec((1, ts, ATT_W), lambda i, j: (i, jnp.maximum(j - (nj - n_tail), 0), 0))
    ht = jax.ShapeDtypeStruct((b, tail, ATT_W), F32)
    row = pl.BlockSpec((1, ts, d), lambda i, j: (i, j, 0))
    per_b = pl.BlockSpec((1, 1, d), lambda i, j: (i, 0, 0))
    half = pl.BlockSpec((1, ts, ATT_W), lambda i, j: (i, j, 0))
    in_specs, args = [pl.BlockSpec((1, ts, d), lambda i, j: (i + xb0, j, 0))], [x]
    if has_res:
        yg, gate, g2, first = res
        blk0 = first // ts
        in_specs += [pl.BlockSpec((TOP_K, ts, d // 2), lambda i, j: (0, blk0 + i * nj + j, 0)),
                     pl.BlockSpec((1, ts, TOP_K), lambda i, j: (i, j, 0)), per_b]
        args += [yg, gate, g2]
    in_specs += [pl.BlockSpec((1, d), lambda i, j: (0, 0)), per_b, per_b,
                 pl.BlockSpec((d, pw), lambda i, j: (0, 0)),
                 pl.BlockSpec((1, REC_W), lambda i, j: (0, 0))]
    args += [nw.reshape(1, d), sc, sh, w_bf, lb.reshape(1, REC_W)]
    hs = jax.ShapeDtypeStruct((b, s, ATT_W), F32)
    hb = jax.ShapeDtypeStruct((b, s, ATT_W), BF16)
    out_shape = ([jax.ShapeDtypeStruct((b, s, d), F32)] if has_res else []) + [hb, hb, hb, ht, ht, hs, hs, hs, hb, hs]
    out_specs = ([row] if has_res else []) + [half] * 3 + [tail_spec] * 2 + [half] * 5
    outs = pl.pallas_call(
        functools.partial(_inproj_kernel, has_res),
        grid=(b, nj),
        in_specs=in_specs, out_specs=out_specs, out_shape=out_shape,
        compiler_params=_cparams("parallel", "arbitrary"),
        name="norm_inproj",
    )(*args)
    if has_res:
        return [(outs[0], 0)] + list(outs[1:])
    return [(x, xb0)] + list(outs)


def _attend(q, kw, vw, bias_ref, s_ref, col_thr):
    lq, lk = q.shape[0], kw.shape[0]
    lane = lax.broadcasted_iota(I32, (lq, LANES), 1)
    lo = lane < ATT_DIM
    nt = (((1,), (1,)), ((), ()))
    if col_thr is not None:
        keep = lax.broadcasted_iota(I32, (2 * lq, lk), 1) >= col_thr
    ms = []
    for p in range(ATT_HEADS // 2):
        sl = slice(p * LANES, (p + 1) * LANES)
        qp = q[:, sl]
        z = jnp.zeros_like(qp)
        lhs = jnp.concatenate([jnp.where(lo, qp, z), jnp.where(lo, z, qp)], axis=0)
        s = lax.dot_general(lhs, kw[:, sl], nt, preferred_element_type=F32) + bias_ref[p]
        if col_thr is not None:
            s = jnp.where(keep, s, NEG)
        ms.append(jnp.max(s, axis=-1, keepdims=True))
        s_ref[p] = s
    outs = []
    for p in range(ATT_HEADS // 2):
        sl = slice(p * LANES, (p + 1) * LANES)
        e = jnp.exp(s_ref[p] - ms[p])
        l = jnp.sum(e, axis=-1, keepdims=True)
        o = jnp.dot(e.astype(BF16), vw[:, sl], preferred_element_type=F32) / l
        outs.append(jnp.where(lo, o[:lq], o[lq:]))
    return jnp.concatenate(outs, axis=-1)


def _attn_prompt_kernel(q_ref, kp_ref, kc_ref, vp_ref, vc_ref, bias_ref, o_ref, kw_ref, vw_ref, s_ref):
    i = pl.program_id(1)
    ts = q_ref.shape[1]
    kw_ref[0:ts, :] = kp_ref[0].astype(BF16)
    kw_ref[ts:2 * ts, :] = kc_ref[0].astype(BF16)
    vw_ref[0:ts, :] = vp_ref[0].astype(BF16)
    vw_ref[ts:2 * ts, :] = vc_ref[0].astype(BF16)

    def body(masked, r, carry):
        r0 = pl.multiple_of(r * ATT_QSUB, ATT_QSUB)
        q = q_ref[0, pl.ds(r0, ATT_QSUB), :]
        kw = kw_ref[pl.ds(r0, ATT_WIN), :]
        vw = vw_ref[pl.ds(r0, ATT_WIN), :]
        o_ref[0, pl.ds(r0, ATT_QSUB), :] = _attend(q, kw, vw, bias_ref, s_ref, (ts - r0) if masked else None)
        return carry

    @pl.when(i == 0)
    def _():
        lax.fori_loop(0, ts // ATT_QSUB, functools.partial(body, True), 0)

    @pl.when(i > 0)
    def _():
        lax.fori_loop(0, ts // ATT_QSUB, functools.partial(body, False), 0)


def _attn_prompt(qa, ka, va, bias):
    b, s, w = qa.shape
    ts = LEFT
    cur = lambda i, j: (i, j, 0)
    prev = lambda i, j: (i, jnp.maximum(j - 1, 0), 0)
    blk = lambda im: pl.BlockSpec((1, ts, w), im)
    return pl.pallas_call(
        _attn_prompt_kernel,
        grid=(b, s // ts),
        in_specs=[blk(cur), blk(prev), blk(cur), blk(prev), blk(cur),
                  pl.BlockSpec(bias.shape, lambda i, j: (0, 0, 0))],
        out_specs=blk(cur),
        out_shape=jax.ShapeDtypeStruct((b, s, w), F32),
        scratch_shapes=[pltpu.VMEM((2 * ts, w), BF16), pltpu.VMEM((2 * ts, w), BF16),
                        pltpu.VMEM((ATT_HEADS // 2, 2 * ATT_QSUB, ATT_WIN), F32)],
        compiler_params=_cparams("parallel", "parallel"),
        name="band_attention",
    )(qa, ka, ka, va, va, bias)


def _attn_sample_kernel(q_ref, k_ref, v_ref, bias_ref, o_ref, s_ref):
    o_ref[0] = _attend(q_ref[0], k_ref[0], v_ref[0], bias_ref, s_ref, None)


def _attn_sample(qa, keys, vals, bias):
    b, l, w = qa.shape
    lk = keys.shape[1]
    return pl.pallas_call(
        _attn_sample_kernel,
        grid=(b,),
        in_specs=[pl.BlockSpec((1, l, w), lambda i: (i, 0, 0)),
                  pl.BlockSpec((1, lk, w), lambda i: (i, 0, 0)),
                  pl.BlockSpec((1, lk, w), lambda i: (i, 0, 0)),
                  pl.BlockSpec(bias.shape, lambda i: (0, 0, 0))],
        out_specs=pl.BlockSpec((1, l, w), lambda i: (i, 0, 0)),
        out_shape=jax.ShapeDtypeStruct((b, l, w), F32),
        scratch_shapes=[pltpu.VMEM((ATT_HEADS // 2, 2 * l, lk), F32)],
        compiler_params=_cparams("parallel"),
        name="cache_attention",
    )(qa, keys, vals, bias)


def _toeplitz_bias(table, lq, lk, offset, valid):
    i = np.arange(lq + lk - 1)
    gvec = table[:, np.clip(offset + (lq - 1) - i, -MAX_REL, MAX_REL) + MAX_REL]
    rows = [gvec[:, lq - 1 - t:lq - 1 - t + lk] for t in range(lq)]
    bias = jnp.where(jnp.asarray(valid)[None], jnp.stack(rows, axis=1), NEG).astype(F32)
    return bias.reshape(ATT_HEADS // 2, 2 * lq, lk)


def _prompt_bias(table):
    t = np.arange(ATT_QSUB)[:, None]
    j = np.arange(ATT_WIN)[None, :]
    start = (t // CHUNK) * CHUNK
    return _toeplitz_bias(table, ATT_QSUB, ATT_WIN, LEFT, (j >= start) & (j < start + LEFT + CHUNK))


def _sample_bias(table, l, w, lk):
    valid = np.broadcast_to(np.arange(lk)[None, :] < w + l, (l, lk))
    return _toeplitz_bias(table, l, lk, w, valid)


def _neg_abs(x):
    return lax.bitcast_convert_type(jnp.bitwise_or(_bits(x), jnp.uint32(0x80000000)), F32)


def _hgrn_kernel(chunk, q_ref, k_ref, lf_ref, v_ref, s0_ref, o_ref, sf_ref, st_ref):
    i = pl.program_id(1)
    ts = q_ref.shape[1]
    n_chunks = ts // chunk

    @pl.when(i == 0)
    def _():
        st_ref[...] = s0_ref[0]

    row = lax.broadcasted_iota(I32, (chunk, chunk), 0)
    colm = lax.broadcasted_iota(I32, (chunk, chunk), 1)
    tri = (colm <= row).astype(BF16)
    xor = jnp.bitwise_xor(row, colm)
    lower = colm < row
    rid = lax.broadcasted_iota(I32, (chunk, REC_W), 0)

    levels = []
    b = chunk // 2
    while b >= 1:
        levels.append(b)
        b //= 2
    pair_mask = [xor == 0] + [lower & (xor >= b) & (xor < 2 * b) for b in levels]
    upper = [jnp.bitwise_and(rid, b) != 0 for b in levels]
    ph4 = jnp.bitwise_and(rid, 3)
    odd = jnp.bitwise_and(rid, 1) == 1
    nt = (((1,), (1,)), ((), ()))
    tn = (((0,), (0,)), ((), ()))
    sls = [slice(h * REC_DIM, (h + 1) * REC_DIM) for h in range(REC_HEADS)]

    def boundary(cum, b):
        if 2 * b >= 8:
            pieces = [jnp.broadcast_to(cum[m * 2 * b + b - 1:m * 2 * b + b, :], (2 * b, REC_W))
                      for m in range(chunk // (2 * b))]
            return pieces[0] if len(pieces) == 1 else jnp.concatenate(pieces, axis=0)
        if b == 2:
            up1 = pltpu.roll(cum, chunk - 1, 0)
            dn1 = pltpu.roll(cum, 1, 0)
            dn2 = pltpu.roll(cum, 2, 0)
            return jnp.where(ph4 == 0, up1, jnp.where(ph4 == 1, cum, jnp.where(ph4 == 2, dn1, dn2)))
        return jnp.where(odd, pltpu.roll(cum, 1, 0), cum)

    def one_chunk(c, carry):
        r0 = pl.multiple_of(c * chunk, chunk)
        rows = pl.ds(r0, chunk)
        q = q_ref[0, rows, :]
        k = k_ref[0, rows, :]
        v = v_ref[0, rows, :]
        lf = lf_ref[0, rows, :]
        cum = sum(jnp.dot(tri, part, preferred_element_type=F32) for part in _split3(lf)) * LOG2E
        last = cum[chunk - 1:chunk, :]
        q_in = (q * jnp.exp2(cum)).astype(BF16)
        k_out = (k * jnp.exp2(last - cum)).astype(BF16)
        decay = jnp.exp2(last)
        qb = q.astype(BF16)
        kb = k.astype(BF16)
        zs = []
        for n, b in enumerate(levels):
            w = jnp.exp2(_neg_abs(cum - boundary(cum, b)))
            zs.append((jnp.where(upper[n], q, k) * w).astype(BF16))
        sts = [st_ref[h] for h in range(REC_HEADS)]
        d0 = [lax.dot_general(qb[:, sl], kb[:, sl], nt, preferred_element_type=F32) for sl in sls]
        dl = [[lax.dot_general(z[:, sl], z[:, sl], nt, preferred_element_type=F32) for z in zs] for sl in sls]
        oi = [lax.dot_general(q_in[:, sl], st.astype(BF16), nt, preferred_element_type=F32)
              for sl, st in zip(sls, sts)]
        upd = [lax.dot_general(v[:, sl], k_out[:, sl], tn, preferred_element_type=F32) for sl in sls]
        outs = []
        for h, sl in enumerate(sls):
            a = jnp.where(pair_mask[0], d0[h], 0.0)
            for n in range(len(levels)):
                a = jnp.where(pair_mask[n + 1], dl[h][n], a)
            outs.append(jnp.dot(a.astype(BF16), v[:, sl], preferred_element_type=F32) + oi[h])
        o_ref[0, rows, :] = jnp.concatenate(outs, axis=-1)
        for h, sl in enumerate(sls):
            st_ref[h] = decay[:, sl] * sts[h] + upd[h]
        return carry

    lax.fori_loop(0, n_chunks, one_chunk, 0, unroll=2 if n_chunks % 2 == 0 else 1)

    @pl.when(i == pl.num_programs(1) - 1)
    def _():
        sf_ref[0] = st_ref[...]


def _hgrn(q, k, lf, v, s0_t, ts, chunk):
    b, s, w = q.shape
    blk = pl.BlockSpec((1, ts, w), lambda i, j: (i, j, 0))
    st = pl.BlockSpec((1, REC_HEADS, REC_DIM, REC_DIM), lambda i, j: (i, 0, 0, 0))
    return pl.pallas_call(
        functools.partial(_hgrn_kernel, chunk),
        grid=(b, s // ts),
        in_specs=[blk, blk, blk, blk, st],
        out_specs=[blk, st],
        out_shape=[jax.ShapeDtypeStruct((b, s, w), F32),
                   jax.ShapeDtypeStruct((b, REC_HEADS, REC_DIM, REC_DIM), F32)],
        scratch_shapes=[pltpu.VMEM((REC_HEADS, REC_DIM, REC_DIM), F32)],
        compiler_params=_cparams("parallel", "arbitrary"),
        name="hgrn_recurrence",
    )(q, k, lf, v, s0_t)


def _head_rms(x, width):
    lane = lax.broadcasted_iota(I32, (x.shape[0], LANES), 1)
    lo = lane < ATT_DIM
    outs = []
    for p in range(x.shape[1] // LANES):
        xp = x[:, p * LANES:(p + 1) * LANES]
        sq = xp * xp
        tot = jnp.sum(sq, axis=-1, keepdims=True)
        if width == LANES:
            ms = tot * (1.0 / LANES)
        else:
            s_lo = jnp.sum(jnp.where(lo, sq, 0.0), axis=-1, keepdims=True)
            ms = jnp.where(lo, s_lo, tot - s_lo) * (1.0 / ATT_DIM)
        outs.append(xp * lax.rsqrt(ms + EPS))
    return jnp.concatenate(outs, axis=-1)


def _merge_kernel(oa_ref, or_ref, gh_ref, x_ref, ga_ref, gr_ref, wo_ref, g1_ref, nf_ref, sc_ref, sh_ref,
                  wr_ref, br_ref, xo_ref, h2_ref, lg_ref):
    a = _head_rms(oa_ref[0], ATT_DIM) * ga_ref[...]
    gh = gh_ref[0]
    r = _head_rms(or_ref[0], REC_DIM) * gr_ref[...] * (gh * jax.nn.sigmoid(gh))
    mix = jnp.dot(a.astype(BF16), wo_ref[0:ATT_W, :], preferred_element_type=F32)
    mix += jnp.dot(r.astype(BF16), wo_ref[ATT_W:ATT_W + REC_W, :], preferred_element_type=F32)
    x = x_ref[0] + g1_ref[0] * mix
    xo_ref[0] = x
    ms = jnp.mean(x * x, axis=-1, keepdims=True)
    h2 = (x * lax.rsqrt(ms + EPS) * nf_ref[...]) * (1.0 + sc_ref[0]) + sh_ref[0]
    hh = h2.astype(BF16)
    h2_ref[0] = _pack_pairs(h2)
    hm = (h2 - hh.astype(F32)).astype(BF16)
    nt = (((1,), (1,)), ((), ()))
    l1 = lax.dot_general(wr_ref[...], hh, nt, preferred_element_type=F32)
    l2 = lax.dot_general(wr_ref[0:N_EXPERTS, :], hm, nt, preferred_element_type=F32)
    lg = l1[0:N_EXPERTS] + l1[N_EXPERTS:2 * N_EXPERTS] + l2 + br_ref[...]
    if len(lg_ref.shape) == 3:
        lg_ref[0] = lg
    else:
        lg_ref[...] = lg


def _merge(o_att, o_rec, gh, x, xb0, g_att, g_rec, wo_bf, g1, nf, sc, sh, wr2, br, ts):
    _, s, d = x.shape
    b = o_att.shape[0]
    nj = s // ts
    row = pl.BlockSpec((1, ts, d), lambda i, j: (i, j, 0))
    xrow = pl.BlockSpec((1, ts, d), lambda i, j: (i + xb0, j, 0))
    half = pl.BlockSpec((1, ts, ATT_W), lambda i, j: (i, j, 0))
    per_b = pl.BlockSpec((1, 1, d), lambda i, j: (i, 0, 0))
    const2 = lambda shp: pl.BlockSpec(shp, lambda i, j: (0,) * len(shp))
    if ts % LANES == 0:
        lg_spec = pl.BlockSpec((N_EXPERTS, ts), lambda i, j: (0, i * nj + j))
        lg_shape = jax.ShapeDtypeStruct((N_EXPERTS, b * s), F32)
    else:
        lg_spec = pl.BlockSpec((1, N_EXPERTS, ts), lambda i, j: (i, 0, j))
        lg_shape = jax.ShapeDtypeStruct((b, N_EXPERTS, s), F32)
    return pl.pallas_call(
        _merge_kernel,
        grid=(b, nj),
        in_specs=[half, half, half, xrow, const2((1, ATT_W)), const2((1, REC_W)), const2(wo_bf.shape),
                  per_b, const2((1, d)), per_b, per_b, const2(wr2.shape), const2((N_EXPERTS, 1))],
        out_specs=[row, pl.BlockSpec((1, ts, d // 2), lambda i, j: (i, j, 0)), lg_spec],
        out_shape=[jax.ShapeDtypeStruct((b, s, d), F32), jax.ShapeDtypeStruct((b, s, d // 2), U32), lg_shape],
        compiler_params=_cparams("parallel", "parallel"),
        name="merge_outproj_router",
    )(o_att, o_rec, gh, x, g_att.reshape(1, ATT_W), g_rec.reshape(1, REC_W), wo_bf, g1, nf.reshape(1, d),
      sc, sh, wr2, br.reshape(N_EXPERTS, 1))


def _route_kernel(lg_ref, idx_ref, gate_ref, rank_ref, cnt_ref, run_ref):
    i = pl.program_id(0)
    t = lg_ref.shape[1]

    @pl.when(i == 0)
    def _():
        run_ref[...] = jnp.zeros_like(run_ref)

    x = lg_ref[...]
    row = lax.broadcasted_iota(I32, x.shape, 0)
    vals, idxs, hits = [], [], []
    for _ in range(TOP_K):
        m = jnp.max(x, axis=0, keepdims=True)
        ik = jnp.min(jnp.where(x == m, row, N_EXPERTS), axis=0, keepdims=True)
        hit = row == ik
        x = jnp.where(hit, -jnp.inf, x)
        vals.append(m)
        idxs.append(ik)
        hits.append(hit)
    es = [jnp.exp(v - vals[0]) for v in vals]
    tot = es[0] + es[1] + es[2] + es[3]
    gate_ref[...] = jnp.concatenate([e / tot for e in es], axis=0)
    idx_ref[...] = jnp.concatenate(idxs, axis=0)
    chosen = (hits[0] | hits[1] | hits[2] | hits[3])
    onehot = jnp.where(chosen, 1.0, 0.0)
    earlier = (lax.broadcasted_iota(I32, (t, t), 0) < lax.broadcasted_iota(I32, (t, t), 1)).astype(BF16)
    run = run_ref[:, 0:1]
    before = jnp.dot(onehot.astype(BF16), earlier, preferred_element_type=F32) + run
    rank_ref[...] = jnp.concatenate(
        [jnp.sum(jnp.where(h, before, 0.0), axis=0, keepdims=True) for h in hits], axis=0).astype(I32)
    run = run + jnp.sum(onehot, axis=1, keepdims=True)
    run_ref[...] = jnp.broadcast_to(run, run_ref.shape)
    cnt_ref[...] = jnp.broadcast_to(run, cnt_ref.shape)


def _route(lg_t):
    e, n = lg_t.shape
    t = ROUTE_TILE
    tok = pl.BlockSpec((TOP_K, t), lambda i: (0, i))
    idx, gate, rank, cnt = pl.pallas_call(
        _route_kernel,
        grid=(n // t,),
        in_specs=[pl.BlockSpec((e, t), lambda i: (0, i))],
        out_specs=[tok, tok, tok, pl.BlockSpec((e, LANES), lambda i: (0, 0))],
        out_shape=[jax.ShapeDtypeStruct((TOP_K, n), I32), jax.ShapeDtypeStruct((TOP_K, n), F32),
                   jax.ShapeDtypeStruct((TOP_K, n), I32), jax.ShapeDtypeStruct((e, LANES), F32)],
        scratch_shapes=[pltpu.VMEM((e, LANES), F32)],
        compiler_params=_cparams("arbitrary"),
        name="moe_route",
    )(lg_t)
    counts = cnt[:, 0].astype(I32)
    padded = (counts + MOE_ROWS - 1) // MOE_ROWS * MOE_ROWS
    pad_end = jnp.cumsum(padded)
    pad_start = pad_end - padded
    m = n * TOP_K
    nb = (m + N_EXPERTS * (MOE_ROWS - 1) + MOE_ROWS - 1) // MOE_ROWS
    block_start = jnp.arange(nb, dtype=I32) * MOE_ROWS
    block_expert = jnp.minimum(jnp.sum((pad_end[None, :] <= block_start[:, None]).astype(I32), axis=1), N_EXPERTS - 1)
    block_info = jnp.concatenate([block_expert, pad_end[-1:] // MOE_ROWS]).astype(I32)
    start_of = jnp.sum(jnp.where(idx[:, :, None] == jnp.arange(N_EXPERTS, dtype=I32), pad_start, 0), axis=-1)
    dest = start_of + rank
    return gate, dest, block_info, nb * MOE_ROWS


def _windows(dest, w, multiple, fill):
    kk, n = dest.shape
    nwin = n // w
    per = -(-nwin // SC_WORKERS)
    per = -(-per // multiple) * multiple
    idx = dest.reshape(kk, nwin, w).transpose(1, 0, 2)
    pad = jnp.broadcast_to(fill[None, None, :], (SC_WORKERS * per - nwin, kk, w)).astype(I32)
    return jnp.concatenate([idx, pad], axis=0)


def _sc_scatter_rows(x, idx, p_rows):
    n, c = x.shape
    nwin, kk, w = idx.shape
    per = nwin // SC_WORKERS
    nreal = n // w
    mesh = plsc.VectorSubcoreMesh(core_axis_name="c", subcore_axis_name="s")

    @functools.partial(
        pl.kernel, mesh=mesh,
        out_type=jax.ShapeDtypeStruct((p_rows, c), x.dtype),
        scratch_types=[pltpu.VMEM((per, kk, w), I32), pltpu.VMEM((2, w, c), x.dtype),
                       pltpu.SemaphoreType.DMA((2,)), pltpu.SemaphoreType.DMA((2,))],
        name="sc_scatter_rows",
    )
    def k(x_hbm, idx_hbm, out_hbm, idx_v, rows_v, lsem, ssem):
        wid = lax.axis_index("s") * 2 + lax.axis_index("c")
        g0 = wid * per
        pltpu.sync_copy(idx_hbm.at[pl.ds(g0, per)], idx_v)

        def load(j, b):
            g = jnp.minimum(g0 + j, nreal - 1)
            return pltpu.make_async_copy(x_hbm.at[pl.ds(g * w, w)], rows_v.at[b], lsem.at[b])

        def scat(j, b, q):
            return pltpu.make_async_copy(rows_v.at[b], out_hbm.at[idx_v.at[j, q]], ssem.at[b])

        @pl.loop(0, per + 2, step=2)
        def _(j0):
            for b in range(2):
                j = j0 + b

                @pl.when(jnp.logical_and(j >= 2, j < per + 2))
                def _():
                    for q in range(kk):
                        scat(j - 2, b, q).wait()

                @pl.when(j < per)
                def _():
                    load(j, b).start()

                @pl.when(jnp.logical_and(j >= 1, j < per + 1))
                def _():
                    load(j - 1, 1 - b).wait()
                    for q in range(kk):
                        scat(j - 1, 1 - b, q).start()

    return k(x, idx)


def _sc_gather_rows(y, idx):
    p, c = y.shape
    nwin, kk, w = idx.shape
    nb = SC_GATHER_BUFS
    per = nwin // SC_WORKERS
    group = math.lcm(nb, kk)
    wins = group // kk
    n_items = per * kk
    mesh = plsc.VectorSubcoreMesh(core_axis_name="c", subcore_axis_name="s")

    @functools.partial(
        pl.kernel, mesh=mesh,
        out_type=jax.ShapeDtypeStruct((kk, nwin * w, c), y.dtype),
        scratch_types=[pltpu.VMEM((2, wins, kk, w), I32), pltpu.VMEM((nb, w, c), y.dtype),
                       pltpu.SemaphoreType.DMA((nb,)), pltpu.SemaphoreType.DMA((nb,))],
        name="sc_gather_rows",
    )
    def k(y_hbm, idx_hbm, out_hbm, idx_v, rows_v, gsem, wsem):
        wid = lax.axis_index("s") * 2 + lax.axis_index("c")
        g0 = wid * per

        def gath(slot, jj, q, b):
            return pltpu.make_async_copy(y_hbm.at[idx_v.at[slot, jj, q]], rows_v.at[b], gsem.at[b])

        def wr(i, b):
            return pltpu.make_async_copy(rows_v.at[b], out_hbm.at[i % kk, pl.ds((g0 + i // kk) * w, w)], wsem.at[b])

        n_groups = -(-(n_items + nb) // group)

        @pl.loop(0, n_groups)
        def _(gi):
            slot = gi % 2

            @pl.when(gi * wins < per)
            def _():
                pltpu.sync_copy(idx_hbm.at[pl.ds(g0 + gi * wins, wins)], idx_v.at[slot])

            for l in range(group):
                i = gi * group + l
                b = l % nb
                bw = (l - (nb - 1)) % nb

                @pl.when(jnp.logical_and(i >= nb, i < n_items + nb))
                def _():
                    wr(i - nb, b).wait()

                @pl.when(i < n_items)
                def _():
                    gath(slot, l // kk, l % kk, b).start()

                @pl.when(jnp.logical_and(i >= nb - 1, i < n_items + nb - 1))
                def _():
                    gath(slot, 0, 0, bw).wait()
                    wr(i - (nb - 1), bw).start()

    return k(y, idx)


def _moe_kernel(be_ref, x_ref, w1_ref, b1_ref, w2_ref, b2_ref, o_ref, w1b_ref, w2b_ref):
    i = pl.program_id(0)
    changed = jnp.logical_or(i == 0, be_ref[i] != be_ref[jnp.maximum(i - 1, 0)])

    @pl.when(changed)
    def _():
        w1b_ref[...] = w1_ref[0, 0].astype(BF16)
        w2b_ref[...] = w2_ref[0, 0].astype(BF16)

    @pl.when(i < be_ref[pl.num_programs(0)])
    def _():
        _expert_block(x_ref, w1b_ref, b1_ref, w2b_ref, b2_ref, o_ref)


def _expert_block(x_ref, w1b_ref, b1_ref, w2b_ref, b2_ref, o_ref):
    lo, hi = _unpack_pairs(x_ref[...])
    x = jnp.concatenate([lo.astype(BF16), hi.astype(BF16)], axis=-1)
    f = w2b_ref.shape[0]
    fc = 512
    acc = None
    for c in range(f // fc):
        gt = jnp.dot(x, w1b_ref[:, c * fc:(c + 1) * fc], preferred_element_type=F32) + b1_ref[0, 0, :, c * fc:(c + 1) * fc]
        up = jnp.dot(x, w1b_ref[:, f + c * fc:f + (c + 1) * fc], preferred_element_type=F32) + b1_ref[0, 0, :, f + c * fc:f + (c + 1) * fc]
        gt = jnp.minimum(gt, SWIGLU_LIMIT)
        up = jnp.clip(up, -SWIGLU_LIMIT, SWIGLU_LIMIT)
        act = (gt * jax.nn.sigmoid(SWIGLU_ALPHA * gt) * (up + 1.0)).astype(BF16)
        part = jnp.dot(act, w2b_ref[c * fc:(c + 1) * fc, :], preferred_element_type=F32)
        acc = part if acc is None else acc + part
    o_ref[...] = _pack_pairs(acc + b2_ref[0, 0])


def _moe_experts(layer, block_expert, xs, p_rows, w1, b1, w2, b2):
    _, e, d, f2 = w1.shape
    f = f2 // 2
    nb = p_rows // MOE_ROWS
    grid_spec = pltpu.PrefetchScalarGridSpec(
        num_scalar_prefetch=1,
        grid=(nb,),
        in_specs=[pl.BlockSpec((MOE_ROWS, d // 2), lambda i, be: (i, 0)),
                  pl.BlockSpec((1, 1, d, f2), lambda i, be: (layer, be[i], 0, 0)),
                  pl.BlockSpec((1, 1, 1, f2), lambda i, be: (layer, be[i], 0, 0)),
                  pl.BlockSpec((1, 1, f, d), lambda i, be: (layer, be[i], 0, 0)),
                  pl.BlockSpec((1, 1, 1, d), lambda i, be: (layer, be[i], 0, 0))],
        out_specs=pl.BlockSpec((MOE_ROWS, d // 2), lambda i, be: (i, 0)),
        scratch_shapes=[pltpu.VMEM((d, f2), BF16), pltpu.VMEM((f, d), BF16)],
    )
    return pl.pallas_call(
        _moe_kernel,
        grid_spec=grid_spec,
        out_shape=jax.ShapeDtypeStruct((p_rows, d // 2), U32),
        compiler_params=_cparams("arbitrary"),
        name="moe_experts",
    )(block_expert, xs, w1, b1.reshape(b1.shape[0], e, 1, f2), w2, b2.reshape(b2.shape[0], e, 1, d))


def _moe_dispatch(h2_words, lg_t):
    gate, dest, block_info, p_rows = _route(lg_t)
    spare = p_rows + jnp.arange(SC_SCATTER_WIN, dtype=I32)
    xs = _sc_scatter_rows(h2_words, _windows(dest, SC_SCATTER_WIN, 2, spare), p_rows + SC_SCATTER_WIN)
    return xs, gate.T, dest, block_info, p_rows


def _moe_compute(layer, dispatched, w1, b1, w2, b2):
    xs, gate, dest, block_info, p_rows = dispatched
    ys = _moe_experts(layer, block_info, xs, p_rows, w1, b1, w2, b2)
    group_wins = math.lcm(SC_GATHER_BUFS, TOP_K) // TOP_K
    yg = _sc_gather_rows(ys, _windows(dest, SC_GATHER_WIN, group_wins, jnp.zeros((SC_GATHER_WIN,), I32)))
    return yg, gate


def _final_kernel(x_ref, yg_ref, gate_ref, g2_ref, nw_ref, *rest):
    o_ref = rest[-1]
    x = x_ref[0] + g2_ref[0] * _combine(yg_ref, gate_ref)
    ms = jnp.mean(x * x, axis=-1, keepdims=True)
    o_ref[0] = x * lax.rsqrt(ms + EPS) * nw_ref[...]


def _final(x, res, nw, ts, out_buf, ob0, b_total):
    b, s, d = x.shape
    nj = s // ts
    yg, gate, g2, first = res
    blk0 = first // ts
    row = pl.BlockSpec((1, ts, d), lambda i, j: (i, j, 0))
    in_specs = [row, pl.BlockSpec((TOP_K, ts, d // 2), lambda i, j: (0, blk0 + i * nj + j, 0)),
                pl.BlockSpec((1, ts, TOP_K), lambda i, j: (i, j, 0)),
                pl.BlockSpec((1, 1, d), lambda i, j: (i, 0, 0)),
                pl.BlockSpec((1, d), lambda i, j: (0, 0))]
    args = [x, yg, gate, g2, nw.reshape(1, d)]
    aliases = {}
    if out_buf is not None:
        in_specs.append(pl.BlockSpec(memory_space=pl.ANY))
        args.append(out_buf)
        aliases = {len(args) - 1: 0}
    return pl.pallas_call(
        _final_kernel,
        grid=(b, nj),
        in_specs=in_specs,
        out_specs=pl.BlockSpec((1, ts, d), lambda i, j: (i + ob0, j, 0)),
        out_shape=jax.ShapeDtypeStruct((b_total, s, d), F32),
        input_output_aliases=aliases,
        compiler_params=_cparams("parallel", "parallel"),
        name="final_norm",
    )(*args)


def kernel(x_prompt, x_sample, c_prompt, c_sample, cache_k, cache_v, state_hgrn, norm_mix, norm_ffn, norm_final, w_ada, b_ada, w_in, rel_bias, hgrn_lb_logits, g_attn_out, g_hgrn_out, w_out, w_router, b_router, w_e1, b_e1, w_e2, b_e2):
    depth = w_in.shape[0]
    bp, sp, d = x_prompt.shape
    bs, ls, _ = x_sample.shape
    wc = cache_k.shape[2]
    n_p, n_s = bp * sp, bs * ls

    lbs = jax.nn.softmax(hgrn_lb_logits.astype(F32), axis=0)
    lbs = jnp.cumsum(lbs, axis=0) - lbs[0]

    mods = _modulation(jnp.concatenate([c_prompt, c_sample], axis=0), w_ada, b_ada)
    lk_s = -(-(wc + ls) // LANES) * LANES
    tail = min(LEFT, sp)
    rec_chunk = min(REC_CHUNK, sp)

    n_groups = PROMPT_GROUPS if bp % PROMPT_GROUPS == 0 else 1
    gb = bp // n_groups
    n_g = gb * sp
    x_g = [(x_prompt, g * gb) for g in range(n_groups)]
    x_s = (x_sample, 0)
    res_g = [None] * n_groups
    res_s = None
    kp_l, vp_l, sp_l, ks_l, vs_l, ss_l = [], [], [], [], [], []
    for layer in range(depth):
        m6 = mods[layer].reshape(bp + bs, 1, N_MOD, d)
        mod_s = [m6[bp:, :, n, :] for n in range(N_MOD)]
        w_in_bf = w_in[layer].astype(BF16)
        w_out_bf = w_out[layer].astype(BF16)
        wr_hi, wr_mid, _ = _split3(w_router[layer].T)
        wr2 = jnp.concatenate([wr_hi, wr_mid], axis=0)
        table = rel_bias[layer]
        bias_p = _prompt_bias(table)

        def mixer_prompt(g):
            mod = [m6[g * gb:(g + 1) * gb, :, n, :] for n in range(N_MOD)]
            x, qa, ka, va, kt, vt, qh, kh, lf, vh, gh = _inproj(
                x_g[g][0], x_g[g][1], res_g[g], norm_mix[layer], mod[1], mod[0], w_in_bf, lbs[layer],
                SEQ_TILE, tail)
            o_att = _attn_prompt(qa, ka, va, bias_p)
            s0 = jnp.zeros((gb, REC_HEADS, REC_DIM, REC_DIM), F32)
            o_rec, s_fin = _hgrn(qh, kh, lf, vh, s0, SEQ_TILE, rec_chunk)
            x_new, h2, lg = _merge(o_att, o_rec, gh, x[0], x[1], g_attn_out[layer], g_hgrn_out[layer], w_out_bf,
                                   mod[2], norm_ffn[layer], mod[4], mod[3], wr2, b_router[layer], SEQ_TILE)
            x_g[g] = (x_new, 0)
            return h2.reshape(n_g, d // 2), lg, kt, vt, s_fin, mod[5]

        def mixer_sample():
            x, qa, ka, va, kt, vt, qh, kh, lf, vh, gh = _inproj(
                x_s[0], x_s[1], res_s, norm_mix[layer], mod_s[1], mod_s[0], w_in_bf, lbs[layer], ls, ls)
            zpad = jnp.zeros((bs, lk_s - wc - ls, ATT_W), BF16)
            keys = jnp.concatenate([cache_k[layer].reshape(bs, wc, ATT_W).astype(BF16), ka, zpad], axis=1)
            vals = jnp.concatenate([cache_v[layer].reshape(bs, wc, ATT_W).astype(BF16), va, zpad], axis=1)
            o_att = _attn_sample(qa, keys, vals, _sample_bias(table, ls, wc, lk_s))
            s0 = jnp.swapaxes(state_hgrn[layer].astype(F32), -1, -2)
            o_rec, s_new = _hgrn(qh, kh, lf, vh, s0, ls, ls)
            x_new, h2, lg = _merge(o_att, o_rec, gh, x[0], x[1], g_attn_out[layer], g_hgrn_out[layer], w_out_bf,
                                   mod_s[2], norm_ffn[layer], mod_s[4], mod_s[3], wr2, b_router[layer], ls)
            return (x_new, 0), h2.reshape(n_s, d // 2), jnp.swapaxes(lg, 0, 1).reshape(N_EXPERTS, n_s), kt, vt, s_new

        x_s, h2_s, lg_s, ka_s, va_s, s_new = mixer_sample()
        ks_l.append(ka_s.reshape(bs, ls, ATT_HEADS, ATT_DIM))
        vs_l.append(va_s.reshape(bs, ls, ATT_HEADS, ATT_DIM))
        ss_l.append(jnp.swapaxes(s_new, -1, -2))

        dispatched, g2s, k_t, v_t, s_t = [], [], [], [], []
        for g in range(n_groups):
            h2, lg, kt, vt, s_fin, g2 = mixer_prompt(g)
            if g == n_groups - 1:
                h2 = jnp.concatenate([h2, h2_s], axis=0)
                lg = jnp.concatenate([lg, lg_s], axis=1)
            dispatched.append(_moe_dispatch(h2, lg))
            g2s.append(g2)
            k_t.append(kt)
            v_t.append(vt)
            s_t.append(s_fin)
        kp_l.append(jnp.concatenate(k_t, axis=0).reshape(bp, tail, ATT_HEADS, ATT_DIM))
        vp_l.append(jnp.concatenate(v_t, axis=0).reshape(bp, tail, ATT_HEADS, ATT_DIM))
        sp_l.append(jnp.swapaxes(jnp.concatenate(s_t, axis=0), -1, -2))

        for g in range(n_groups):
            yg, gate = _moe_compute(layer, dispatched[g], w_e1, b_e1, w_e2, b_e2)
            res_g[g] = (yg, gate[:n_g].reshape(gb, sp, TOP_K), g2s[g], 0)
            if g == n_groups - 1:
                res_s = (yg, gate[n_g:].reshape(bs, ls, TOP_K), mod_s[5], n_g)

    y_prompt = None
    for g in range(n_groups):
        y_prompt = _final(x_g[g][0], res_g[g], norm_final, SEQ_TILE, y_prompt, g * gb, bp)
    y_sample = _final(x_s[0], res_s, norm_final, ls, None, 0, bs)
    return (y_prompt, y_sample, jnp.stack(kp_l), jnp.stack(vp_l), jnp.stack(sp_l),
            jnp.stack(ks_l), jnp.stack(vs_l), jnp.stack(ss_l))
```

```python
import functools
import math

import jax
import jax.numpy as jnp
import numpy as np
from jax import lax
from jax.experimental import pallas as pl
from jax.experimental.pallas import tpu as pltpu
from jax.experimental.pallas import tpu_sc as plsc

F32 = jnp.float32
BF16 = jnp.bfloat16
I32 = jnp.int32
U32 = jnp.uint32

EPS = 1e-6
CHUNK = 64
LEFT_CHUNKS = 8
LEFT = LEFT_CHUNKS * CHUNK
MAX_REL = 2 * CHUNK
ATT_HEADS = 8
ATT_DIM = 64
ATT_W = ATT_HEADS * ATT_DIM
REC_HEADS = 4
REC_DIM = 128
REC_W = REC_HEADS * REC_DIM
N_EXPERTS = 32
TOP_K = 4
SWIGLU_ALPHA = 1.702
SWIGLU_LIMIT = 7.0
N_MOD = 6
NEG = -1e30
LOG2E = 1.4426950408889634

LANES = 128
ATT_QSUB = 128
ATT_WIN = LEFT + ATT_QSUB
SEQ_TILE = 512
REC_CHUNK = 128
MOE_ROWS = 512
ROUTE_TILE = 256
PROMPT_GROUPS = 2
VMEM_LIMIT = 56 * 1024 * 1024
SC_WORKERS = 32
SC_SCATTER_WIN = 64
SC_GATHER_WIN = 32
SC_GATHER_BUFS = 6


def _cparams(*sem):
    return pltpu.CompilerParams(dimension_semantics=sem, vmem_limit_bytes=VMEM_LIMIT)


def _split3(x):
    hi = x.astype(BF16)
    r1 = x - hi.astype(F32)
    mid = r1.astype(BF16)
    lo = (r1 - mid.astype(F32)).astype(BF16)
    return hi, mid, lo


def _bits(x):
    return lax.bitcast_convert_type(x, U32)


def _pack_pairs(x):
    r = _bits(x.astype(BF16).astype(F32))
    half = x.shape[1] // 2
    return jnp.bitwise_or(jnp.bitwise_and(r[:, half:], jnp.uint32(0xFFFF0000)),
                          jnp.right_shift(r[:, :half], jnp.uint32(16)))


def _unpack_pairs(word):
    lo = lax.bitcast_convert_type(jnp.left_shift(word, jnp.uint32(16)), F32)
    hi = lax.bitcast_convert_type(jnp.bitwise_and(word, jnp.uint32(0xFFFF0000)), F32)
    return lo, hi


def _mod_kernel(c_ref, w_ref, b_ref, o_ref):
    c = c_ref[...]
    act = (c * jax.nn.sigmoid(c)).astype(BF16)
    o_ref[0] = jnp.dot(act, w_ref[0].astype(BF16), preferred_element_type=F32) + b_ref[0]


def _modulation(c_all, w_ada, b_ada):
    depth, d, n6 = w_ada.shape
    rows = c_all.shape[0]
    tn = 1536
    return pl.pallas_call(
        _mod_kernel,
        grid=(depth, n6 // tn),
        in_specs=[pl.BlockSpec((rows, d), lambda l, j: (0, 0)),
                  pl.BlockSpec((1, d, tn), lambda l, j: (l, 0, j)),
                  pl.BlockSpec((1, 1, tn), lambda l, j: (l, 0, j))],
        out_specs=pl.BlockSpec((1, rows, tn), lambda l, j: (l, 0, j)),
        out_shape=jax.ShapeDtypeStruct((depth, rows, n6), F32),
        compiler_params=_cparams("parallel", "parallel"),
        name="adaln_modulation",
    )(c_all, w_ada, b_ada.reshape(depth, 1, n6))


def _combine(yg_ref, gate_ref):
    g = gate_ref[0]
    lo = hi = None
    for k in range(TOP_K):
        l, h = _unpack_pairs(yg_ref[k])
        gk = g[:, k:k + 1]
        lo = l * gk if lo is None else lo + l * gk
        hi = h * gk if hi is None else hi + h * gk
    return jnp.concatenate([lo, hi], axis=-1)


def _inproj_kernel(has_res, *refs):
    if has_res:
        (x_ref, yg_ref, gate_ref, g2_ref, nw_ref, sc_ref, sh_ref, w_ref, lb_ref,
         xo_ref, qa_ref, ka_ref, va_ref, kt_ref, vt_ref, qh_ref, kh_ref, lf_ref, vh_ref, gh_ref) = refs
        x = x_ref[0] + g2_ref[0] * _combine(yg_ref, gate_ref)
        xo_ref[0] = x
    else:
        (x_ref, nw_ref, sc_ref, sh_ref, w_ref, lb_ref,
         qa_ref, ka_ref, va_ref, kt_ref, vt_ref, qh_ref, kh_ref, lf_ref, vh_ref, gh_ref) = refs
        x = x_ref[0]
    ms = jnp.mean(x * x, axis=-1, keepdims=True)
    h = (x * lax.rsqrt(ms + EPS) * nw_ref[...]) * (1.0 + sc_ref[0]) + sh_ref[0]
    hb = h.astype(BF16)

    def proj(g):
        return jnp.dot(hb, w_ref[:, g * ATT_W:(g + 1) * ATT_W], preferred_element_type=F32)

    qa_ref[0] = (proj(0) * (ATT_DIM ** -0.5)).astype(BF16)
    k = proj(1)
    ka_ref[0] = k.astype(BF16)
    kt_ref[0] = k
    v = proj(2)
    va_ref[0] = v.astype(BF16)
    vt_ref[0] = v
    qh = proj(3)
    qh_ref[0] = qh * jax.nn.sigmoid(qh)
    f = proj(4)
    lb = lb_ref[...]
    lf_ref[0] = jnp.log(lb + (1.0 - lb) * jax.nn.sigmoid(f))
    kh_ref[0] = (1.0 - lb) * jax.nn.sigmoid(-f)
    vh_ref[0] = proj(5).astype(BF16)
    gh_ref[0] = proj(6)


def _inproj(x, xb0, res, nw, sc, sh, w_bf, lb, ts, tail):
    _, s, d = x.shape
    b = sc.shape[0]
    pw = w_bf.shape[1]
    nj = s // ts
    n_tail = tail // ts
    has_res = res is not None
    tail_spec = pl.BlockSpec((1, ts, ATT_W), lambda i, j: (i, jnp.maximum(j - (nj - n_tail), 0), 0))
    ht = jax.ShapeDtypeStruct((b, tail, ATT_W), F32)
    row = pl.BlockSpec((1, ts, d), lambda i, j: (i, j, 0))
    per_b = pl.BlockSpec((1, 1, d), lambda i, j: (i, 0, 0))
    half = pl.BlockSpec((1, ts, ATT_W), lambda i, j: (i, j, 0))
    in_specs, args = [pl.BlockSpec((1, ts, d), lambda i, j: (i + xb0, j, 0))], [x]
    if has_res:
        yg, gate, g2, first = res
        blk0 = first // ts
        in_specs += [pl.BlockSpec((TOP_K, ts, d // 2), lambda i, j: (0, blk0 + i * nj + j, 0)),
                     pl.BlockSpec((1, ts, TOP_K), lambda i, j: (i, j, 0)), per_b]
        args += [yg, gate, g2]
    in_specs += [pl.BlockSpec((1, d), lambda i, j: (0, 0)), per_b, per_b,
                 pl.BlockSpec((d, pw), lambda i, j: (0, 0)),
                 pl.BlockSpec((1, REC_W), lambda i, j: (0, 0))]
    args += [nw.reshape(1, d), sc, sh, w_bf, lb.reshape(1, REC_W)]
    hs = jax.ShapeDtypeStruct((b, s, ATT_W), F32)
    hb = jax.ShapeDtypeStruct((b, s, ATT_W), BF16)
    out_shape = ([jax.ShapeDtypeStruct((b, s, d), F32)] if has_res else []) + [hb, hb, hb, ht, ht, hs, hs, hs, hb, hs]
    out_specs = ([row] if has_res else []) + [half] * 3 + [tail_spec] * 2 + [half] * 5
    outs = pl.pallas_call(
        functools.partial(_inproj_kernel, has_res),
        grid=(b, nj),
        in_specs=in_specs, out_specs=out_specs, out_shape=out_shape,
        compiler_params=_cparams("parallel", "arbitrary"),
        name="norm_inproj",
    )(*args)
    if has_res:
        return [(outs[0], 0)] + list(outs[1:])
    return [(x, xb0)] + list(outs)


def _attend(q, kw, vw, bias_ref, s_ref, col_thr):
    lq, lk = q.shape[0], kw.shape[0]
    lane = lax.broadcasted_iota(I32, (lq, LANES), 1)
    lo = lane < ATT_DIM
    nt = (((1,), (1,)), ((), ()))
    if col_thr is not None:
        keep = lax.broadcasted_iota(I32, (2 * lq, lk), 1) >= col_thr
    ms = []
    for p in range(ATT_HEADS // 2):
        sl = slice(p * LANES, (p + 1) * LANES)
        qp = q[:, sl]
        z = jnp.zeros_like(qp)
        lhs = jnp.concatenate([jnp.where(lo, qp, z), jnp.where(lo, z, qp)], axis=0)
        s = lax.dot_general(lhs, kw[:, sl], nt, preferred_element_type=F32) + bias_ref[p]
        if col_thr is not None:
            s = jnp.where(keep, s, NEG)
        ms.append(jnp.max(s, axis=-1, keepdims=True))
        s_ref[p] = s
    outs = []
    for p in range(ATT_HEADS // 2):
        sl = slice(p * LANES, (p + 1) * LANES)
        e = jnp.exp(s_ref[p] - ms[p])
        l = jnp.sum(e, axis=-1, keepdims=True)
        o = jnp.dot(e.astype(BF16), vw[:, sl], preferred_element_type=F32) / l
        outs.append(jnp.where(lo, o[:lq], o[lq:]))
    return jnp.concatenate(outs, axis=-1)


def _attn_prompt_kernel(q_ref, kp_ref, kc_ref, vp_ref, vc_ref, bias_ref, o_ref, kw_ref, vw_ref, s_ref):
    i = pl.program_id(1)
    ts = q_ref.shape[1]
    kw_ref[0:ts, :] = kp_ref[0].astype(BF16)
    kw_ref[ts:2 * ts, :] = kc_ref[0].astype(BF16)
    vw_ref[0:ts, :] = vp_ref[0].astype(BF16)
    vw_ref[ts:2 * ts, :] = vc_ref[0].astype(BF16)

    def body(masked, r, carry):
        r0 = pl.multiple_of(r * ATT_QSUB, ATT_QSUB)
        q = q_ref[0, pl.ds(r0, ATT_QSUB), :]
        kw = kw_ref[pl.ds(r0, ATT_WIN), :]
        vw = vw_ref[pl.ds(r0, ATT_WIN), :]
        o_ref[0, pl.ds(r0, ATT_QSUB), :] = _attend(q, kw, vw, bias_ref, s_ref, (ts - r0) if masked else None)
        return carry

    @pl.when(i == 0)
    def _():
        lax.fori_loop(0, ts // ATT_QSUB, functools.partial(body, True), 0)

    @pl.when(i > 0)
    def _():
        lax.fori_loop(0, ts // ATT_QSUB, functools.partial(body, False), 0)


def _attn_prompt(qa, ka, va, bias):
    b, s, w = qa.shape
    ts = LEFT
    cur = lambda i, j: (i, j, 0)
    prev = lambda i, j: (i, jnp.maximum(j - 1, 0), 0)
    blk = lambda im: pl.BlockSpec((1, ts, w), im)
    return pl.pallas_call(
        _attn_prompt_kernel,
        grid=(b, s // ts),
        in_specs=[blk(cur), blk(prev), blk(cur), blk(prev), blk(cur),
                  pl.BlockSpec(bias.shape, lambda i, j: (0, 0, 0))],
        out_specs=blk(cur),
        out_shape=jax.ShapeDtypeStruct((b, s, w), F32),
        scratch_shapes=[pltpu.VMEM((2 * ts, w), BF16), pltpu.VMEM((2 * ts, w), BF16),
                        pltpu.VMEM((ATT_HEADS // 2, 2 * ATT_QSUB, ATT_WIN), F32)],
        compiler_params=_cparams("parallel", "parallel"),
        name="band_attention",
    )(qa, ka, ka, va, va, bias)


def _attn_sample_kernel(q_ref, k_ref, v_ref, bias_ref, o_ref, s_ref):
    o_ref[0] = _attend(q_ref[0], k_ref[0], v_ref[0], bias_ref, s_ref, None)


def _attn_sample(qa, keys, vals, bias):
    b, l, w = qa.shape
    lk = keys.shape[1]
    return pl.pallas_call(
        _attn_sample_kernel,
        grid=(b,),
        in_specs=[pl.BlockSpec((1, l, w), lambda i: (i, 0, 0)),
                  pl.BlockSpec((1, lk, w), lambda i: (i, 0, 0)),
                  pl.BlockSpec((1, lk, w), lambda i: (i, 0, 0)),
                  pl.BlockSpec(bias.shape, lambda i: (0, 0, 0))],
        out_specs=pl.BlockSpec((1, l, w), lambda i: (i, 0, 0)),
        out_shape=jax.ShapeDtypeStruct((b, l, w), F32),
        scratch_shapes=[pltpu.VMEM((ATT_HEADS // 2, 2 * l, lk), F32)],
        compiler_params=_cparams("parallel"),
        name="cache_attention",
    )(qa, keys, vals, bias)


def _toeplitz_bias(table, lq, lk, offset, valid):
    m = lq + lk - 1
    k = np.arange(m)
    diff = np.where(k < lk, k, k - m)
    g = table[:, np.clip(offset - diff, -MAX_REL, MAX_REL) + MAX_REL]
    skew = jnp.tile(g, (1, lq))[:, :lq * (m - 1)].reshape(-1, lq, m - 1)[:, :, :lk]
    bias = jnp.where(jnp.asarray(valid)[None], skew, NEG).astype(F32)
    return bias.reshape(ATT_HEADS // 2, 2 * lq, lk)


def _prompt_bias(table):
    t = np.arange(ATT_QSUB)[:, None]
    j = np.arange(ATT_WIN)[None, :]
    start = (t // CHUNK) * CHUNK
    return _toeplitz_bias(table, ATT_QSUB, ATT_WIN, LEFT, (j >= start) & (j < start + LEFT + CHUNK))


def _sample_bias(table, l, w, lk):
    valid = np.broadcast_to(np.arange(lk)[None, :] < w + l, (l, lk))
    return _toeplitz_bias(table, l, lk, w, valid)


def _neg_abs(x):
    return lax.bitcast_convert_type(jnp.bitwise_or(_bits(x), jnp.uint32(0x80000000)), F32)


def _hgrn_kernel(chunk, q_ref, k_ref, lf_ref, v_ref, s0_ref, o_ref, sf_ref, st_ref):
    i = pl.program_id(1)
    ts = q_ref.shape[1]
    n_chunks = ts // chunk

    @pl.when(i == 0)
    def _():
        st_ref[...] = s0_ref[0]

    row = lax.broadcasted_iota(I32, (chunk, chunk), 0)
    colm = lax.broadcasted_iota(I32, (chunk, chunk), 1)
    tri = (colm <= row).astype(BF16)
    xor = jnp.bitwise_xor(row, colm)
    lower = colm < row
    rid = lax.broadcasted_iota(I32, (chunk, REC_W), 0)

    levels = []
    b = chunk // 2
    while b >= 1:
        levels.append(b)
        b //= 2
    pair_mask = [xor == 0] + [lower & (xor >= b) & (xor < 2 * b) for b in levels]
    upper = [jnp.bitwise_and(rid, b) != 0 for b in levels]
    ph4 = jnp.bitwise_and(rid, 3)
    odd = jnp.bitwise_and(rid, 1) == 1
    nt = (((1,), (1,)), ((), ()))
    tn = (((0,), (0,)), ((), ()))
    sls = [slice(h * REC_DIM, (h + 1) * REC_DIM) for h in range(REC_HEADS)]

    def boundary(cum, b):
        if 2 * b >= 8:
            pieces = [jnp.broadcast_to(cum[m * 2 * b + b - 1:m * 2 * b + b, :], (2 * b, REC_W))
                      for m in range(chunk // (2 * b))]
            return pieces[0] if len(pieces) == 1 else jnp.concatenate(pieces, axis=0)
        if b == 2:
            up1 = pltpu.roll(cum, chunk - 1, 0)
            dn1 = pltpu.roll(cum, 1, 0)
            dn2 = pltpu.roll(cum, 2, 0)
            return jnp.where(ph4 == 0, up1, jnp.where(ph4 == 1, cum, jnp.where(ph4 == 2, dn1, dn2)))
        return jnp.where(odd, pltpu.roll(cum, 1, 0), cum)

    def one_chunk(c, carry):
        r0 = pl.multiple_of(c * chunk, chunk)
        rows = pl.ds(r0, chunk)
        q = q_ref[0, rows, :]
        k = k_ref[0, rows, :]
        v = v_ref[0, rows, :]
        lf = lf_ref[0, rows, :]
        cum = sum(jnp.dot(tri, part, preferred_element_type=F32) for part in _split3(lf)) * LOG2E
        last = cum[chunk - 1:chunk, :]
        q_in = (q * jnp.exp2(cum)).astype(BF16)
        k_out = (k * jnp.exp2(last - cum)).astype(BF16)
        decay = jnp.exp2(last)
        qb = q.astype(BF16)
        kb = k.astype(BF16)
        zs = []
        for n, b in enumerate(levels):
            w = jnp.exp2(_neg_abs(cum - boundary(cum, b)))
            zs.append((jnp.where(upper[n], q, k) * w).astype(BF16))
        sts = [st_ref[h] for h in range(REC_HEADS)]
        d0 = [lax.dot_general(qb[:, sl], kb[:, sl], nt, preferred_element_type=F32) for sl in sls]
        dl = [[lax.dot_general(z[:, sl], z[:, sl], nt, preferred_element_type=F32) for z in zs] for sl in sls]
        oi = [lax.dot_general(q_in[:, sl], st.astype(BF16), nt, preferred_element_type=F32)
              for sl, st in zip(sls, sts)]
        upd = [lax.dot_general(v[:, sl], k_out[:, sl], tn, preferred_element_type=F32) for sl in sls]
        outs = []
        for h, sl in enumerate(sls):
            a = jnp.where(pair_mask[0], d0[h], 0.0)
            for n in range(len(levels)):
                a = jnp.where(pair_mask[n + 1], dl[h][n], a)
            outs.append(jnp.dot(a.astype(BF16), v[:, sl], preferred_element_type=F32) + oi[h])
        o_ref[0, rows, :] = jnp.concatenate(outs, axis=-1)
        for h, sl in enumerate(sls):
            st_ref[h] = decay[:, sl] * sts[h] + upd[h]
        return carry

    lax.fori_loop(0, n_chunks, one_chunk, 0, unroll=2 if n_chunks % 2 == 0 else 1)

    @pl.when(i == pl.num_programs(1) - 1)
    def _():
        sf_ref[0] = st_ref[...]


def _hgrn(q, k, lf, v, s0_t, ts, chunk):
    b, s, w = q.shape
    blk = pl.BlockSpec((1, ts, w), lambda i, j: (i, j, 0))
    st = pl.BlockSpec((1, REC_HEADS, REC_DIM, REC_DIM), lambda i, j: (i, 0, 0, 0))
    return pl.pallas_call(
        functools.partial(_hgrn_kernel, chunk),
        grid=(b, s // ts),
        in_specs=[blk, blk, blk, blk, st],
        out_specs=[blk, st],
        out_shape=[jax.ShapeDtypeStruct((b, s, w), F32),
                   jax.ShapeDtypeStruct((b, REC_HEADS, REC_DIM, REC_DIM), F32)],
        scratch_shapes=[pltpu.VMEM((REC_HEADS, REC_DIM, REC_DIM), F32)],
        compiler_params=_cparams("parallel", "arbitrary"),
        name="hgrn_recurrence",
    )(q, k, lf, v, s0_t)


def _head_rms(x, width):
    lane = lax.broadcasted_iota(I32, (x.shape[0], LANES), 1)
    lo = lane < ATT_DIM
    outs = []
    for p in range(x.shape[1] // LANES):
        xp = x[:, p * LANES:(p + 1) * LANES]
        sq = xp * xp
        tot = jnp.sum(sq, axis=-1, keepdims=True)
        if width == LANES:
            ms = tot * (1.0 / LANES)
        else:
            s_lo = jnp.sum(jnp.where(lo, sq, 0.0), axis=-1, keepdims=True)
            ms = jnp.where(lo, s_lo, tot - s_lo) * (1.0 / ATT_DIM)
        outs.append(xp * lax.rsqrt(ms + EPS))
    return jnp.concatenate(outs, axis=-1)


def _merge_kernel(oa_ref, or_ref, gh_ref, x_ref, ga_ref, gr_ref, wo_ref, g1_ref, nf_ref, sc_ref, sh_ref,
                  wr_ref, br_ref, xo_ref, h2_ref, lg_ref):
    a = _head_rms(oa_ref[0], ATT_DIM) * ga_ref[...]
    gh = gh_ref[0]
    r = _head_rms(or_ref[0], REC_DIM) * gr_ref[...] * (gh * jax.nn.sigmoid(gh))
    mix = jnp.dot(a.astype(BF16), wo_ref[0:ATT_W, :], preferred_element_type=F32)
    mix += jnp.dot(r.astype(BF16), wo_ref[ATT_W:ATT_W + REC_W, :], preferred_element_type=F32)
    x = x_ref[0] + g1_ref[0] * mix
    xo_ref[0] = x
    ms = jnp.mean(x * x, axis=-1, keepdims=True)
    h2 = (x * lax.rsqrt(ms + EPS) * nf_ref[...]) * (1.0 + sc_ref[0]) + sh_ref[0]
    hh = h2.astype(BF16)
    h2_ref[0] = _pack_pairs(h2)
    hm = (h2 - hh.astype(F32)).astype(BF16)
    nt = (((1,), (1,)), ((), ()))
    l1 = lax.dot_general(wr_ref[...], hh, nt, preferred_element_type=F32)
    l2 = lax.dot_general(wr_ref[0:N_EXPERTS, :], hm, nt, preferred_element_type=F32)
    lg = l1[0:N_EXPERTS] + l1[N_EXPERTS:2 * N_EXPERTS] + l2 + br_ref[...]
    if len(lg_ref.shape) == 3:
        lg_ref[0] = lg
    else:
        lg_ref[...] = lg


def _merge(o_att, o_rec, gh, x, xb0, g_att, g_rec, wo_bf, g1, nf, sc, sh, wr2, br, ts):
    _, s, d = x.shape
    b = o_att.shape[0]
    nj = s // ts
    row = pl.BlockSpec((1, ts, d), lambda i, j: (i, j, 0))
    xrow = pl.BlockSpec((1, ts, d), lambda i, j: (i + xb0, j, 0))
    half = pl.BlockSpec((1, ts, ATT_W), lambda i, j: (i, j, 0))
    per_b = pl.BlockSpec((1, 1, d), lambda i, j: (i, 0, 0))
    const2 = lambda shp: pl.BlockSpec(shp, lambda i, j: (0,) * len(shp))
    if ts % LANES == 0:
        lg_spec = pl.BlockSpec((N_EXPERTS, ts), lambda i, j: (0, i * nj + j))
        lg_shape = jax.ShapeDtypeStruct((N_EXPERTS, b * s), F32)
    else:
        lg_spec = pl.BlockSpec((1, N_EXPERTS, ts), lambda i, j: (i, 0, j))
        lg_shape = jax.ShapeDtypeStruct((b, N_EXPERTS, s), F32)
    return pl.pallas_call(
        _merge_kernel,
        grid=(b, nj),
        in_specs=[half, half, half, xrow, const2((1, ATT_W)), const2((1, REC_W)), const2(wo_bf.shape),
                  per_b, const2((1, d)), per_b, per_b, const2(wr2.shape), const2((N_EXPERTS, 1))],
        out_specs=[row, pl.BlockSpec((1, ts, d // 2), lambda i, j: (i, j, 0)), lg_spec],
        out_shape=[jax.ShapeDtypeStruct((b, s, d), F32), jax.ShapeDtypeStruct((b, s, d // 2), U32), lg_shape],
        compiler_params=_cparams("parallel", "parallel"),
        name="merge_outproj_router",
    )(o_att, o_rec, gh, x, g_att.reshape(1, ATT_W), g_rec.reshape(1, REC_W), wo_bf, g1, nf.reshape(1, d),
      sc, sh, wr2, br.reshape(N_EXPERTS, 1))


def _route_kernel(lg_ref, idx_ref, gate_ref, rank_ref, cnt_ref, run_ref):
    i = pl.program_id(0)
    t = lg_ref.shape[1]

    @pl.when(i == 0)
    def _():
        run_ref[...] = jnp.zeros_like(run_ref)

    x = lg_ref[...]
    row = lax.broadcasted_iota(I32, x.shape, 0)
    vals, idxs, hits = [], [], []
    for _ in range(TOP_K):
        m = jnp.max(x, axis=0, keepdims=True)
        ik = jnp.min(jnp.where(x == m, row, N_EXPERTS), axis=0, keepdims=True)
        hit = row == ik
        x = jnp.where(hit, -jnp.inf, x)
        vals.append(m)
        idxs.append(ik)
        hits.append(hit)
    es = [jnp.exp(v - vals[0]) for v in vals]
    tot = es[0] + es[1] + es[2] + es[3]
    gate_ref[...] = jnp.concatenate([e / tot for e in es], axis=0)
    idx_ref[...] = jnp.concatenate(idxs, axis=0)
    chosen = (hits[0] | hits[1] | hits[2] | hits[3])
    onehot = jnp.where(chosen, 1.0, 0.0)
    earlier = (lax.broadcasted_iota(I32, (t, t), 0) < lax.broadcasted_iota(I32, (t, t), 1)).astype(BF16)
    run = run_ref[:, 0:1]
    before = jnp.dot(onehot.astype(BF16), earlier, preferred_element_type=F32) + run
    rank_ref[...] = jnp.concatenate(
        [jnp.sum(jnp.where(h, before, 0.0), axis=0, keepdims=True) for h in hits], axis=0).astype(I32)
    run = run + jnp.sum(onehot, axis=1, keepdims=True)
    run_ref[...] = jnp.broadcast_to(run, run_ref.shape)
    cnt_ref[...] = jnp.broadcast_to(run, cnt_ref.shape)


def _route(lg_t):
    e, n = lg_t.shape
    t = ROUTE_TILE
    tok = pl.BlockSpec((TOP_K, t), lambda i: (0, i))
    idx, gate, rank, cnt = pl.pallas_call(
        _route_kernel,
        grid=(n // t,),
        in_specs=[pl.BlockSpec((e, t), lambda i: (0, i))],
        out_specs=[tok, tok, tok, pl.BlockSpec((e, LANES), lambda i: (0, 0))],
        out_shape=[jax.ShapeDtypeStruct((TOP_K, n), I32), jax.ShapeDtypeStruct((TOP_K, n), F32),
                   jax.ShapeDtypeStruct((TOP_K, n), I32), jax.ShapeDtypeStruct((e, LANES), F32)],
        scratch_shapes=[pltpu.VMEM((e, LANES), F32)],
        compiler_params=_cparams("arbitrary"),
        name="moe_route",
    )(lg_t)
    counts = cnt[:, 0].astype(I32)
    padded = (counts + MOE_ROWS - 1) // MOE_ROWS * MOE_ROWS
    pad_end = jnp.cumsum(padded)
    pad_start = pad_end - padded
    m = n * TOP_K
    nb = (m + N_EXPERTS * (MOE_ROWS - 1) + MOE_ROWS - 1) // MOE_ROWS
    block_start = jnp.arange(nb, dtype=I32) * MOE_ROWS
    block_expert = jnp.minimum(jnp.sum((pad_end[None, :] <= block_start[:, None]).astype(I32), axis=1), N_EXPERTS - 1)
    experts = jnp.arange(N_EXPERTS, dtype=I32)
    owns = jnp.any(block_expert[:, None] == experts[None, :], axis=0)
    later = owns[None, :] & (experts[None, :] > block_expert[:, None])
    nxt = jnp.min(jnp.where(later, experts[None, :], N_EXPERTS), axis=1)
    nxt = jnp.where(nxt == N_EXPERTS, -1, nxt)
    run = jnp.sum((owns[None, :] & (experts[None, :] < block_expert[:, None])).astype(I32), axis=1)
    block_info = jnp.concatenate([block_expert, nxt, run % 2, pad_end[-1:] // MOE_ROWS]).astype(I32)
    start_of = jnp.sum(jnp.where(idx[:, :, None] == jnp.arange(N_EXPERTS, dtype=I32), pad_start, 0), axis=-1)
    dest = start_of + rank
    return gate, dest, block_info, nb * MOE_ROWS


def _windows(dest, w, multiple, fill):
    kk, n = dest.shape
    nwin = n // w
    per = -(-nwin // SC_WORKERS)
    per = -(-per // multiple) * multiple
    idx = dest.reshape(kk, nwin, w).transpose(1, 0, 2)
    pad = jnp.broadcast_to(fill[None, None, :], (SC_WORKERS * per - nwin, kk, w)).astype(I32)
    return jnp.concatenate([idx, pad], axis=0)


def _sc_scatter_rows(x, idx, p_rows):
    n, c = x.shape
    nwin, kk, w = idx.shape
    per = nwin // SC_WORKERS
    nreal = n // w
    mesh = plsc.VectorSubcoreMesh(core_axis_name="c", subcore_axis_name="s")

    @functools.partial(
        pl.kernel, mesh=mesh,
        out_type=jax.ShapeDtypeStruct((p_rows, c), x.dtype),
        scratch_types=[pltpu.VMEM((per, kk, w), I32), pltpu.VMEM((2, w, c), x.dtype),
                       pltpu.SemaphoreType.DMA((2,)), pltpu.SemaphoreType.DMA((2,))],
        name="sc_scatter_rows",
    )
    def k(x_hbm, idx_hbm, out_hbm, idx_v, rows_v, lsem, ssem):
        wid = lax.axis_index("s") * 2 + lax.axis_index("c")
        g0 = wid * per
        pltpu.sync_copy(idx_hbm.at[pl.ds(g0, per)], idx_v)

        def load(j, b):
            g = jnp.minimum(g0 + j, nreal - 1)
            return pltpu.make_async_copy(x_hbm.at[pl.ds(g * w, w)], rows_v.at[b], lsem.at[b])

        def scat(j, b, q):
            return pltpu.make_async_copy(rows_v.at[b], out_hbm.at[idx_v.at[j, q]], ssem.at[b])

        @pl.loop(0, per + 2, step=2)
        def _(j0):
            for b in range(2):
                j = j0 + b

                @pl.when(jnp.logical_and(j >= 2, j < per + 2))
                def _():
                    for q in range(kk):
                        scat(j - 2, b, q).wait()

                @pl.when(j < per)
                def _():
                    load(j, b).start()

                @pl.when(jnp.logical_and(j >= 1, j < per + 1))
                def _():
                    load(j - 1, 1 - b).wait()
                    for q in range(kk):
                        scat(j - 1, 1 - b, q).start()

    return k(x, idx)


def _sc_gather_rows(y, idx):
    p, c = y.shape
    nwin, kk, w = idx.shape
    nb = SC_GATHER_BUFS
    per = nwin // SC_WORKERS
    group = math.lcm(nb, kk)
    wins = group // kk
    n_items = per * kk
    mesh = plsc.VectorSubcoreMesh(core_axis_name="c", subcore_axis_name="s")

    @functools.partial(
        pl.kernel, mesh=mesh,
        out_type=jax.ShapeDtypeStruct((kk, nwin * w, c), y.dtype),
        scratch_types=[pltpu.VMEM((2, wins, kk, w), I32), pltpu.VMEM((nb, w, c), y.dtype),
                       pltpu.SemaphoreType.DMA((nb,)), pltpu.SemaphoreType.DMA((nb,))],
        name="sc_gather_rows",
    )
    def k(y_hbm, idx_hbm, out_hbm, idx_v, rows_v, gsem, wsem):
        wid = lax.axis_index("s") * 2 + lax.axis_index("c")
        g0 = wid * per

        def gath(slot, jj, q, b):
            return pltpu.make_async_copy(y_hbm.at[idx_v.at[slot, jj, q]], rows_v.at[b], gsem.at[b])

        def wr(i, b):
            return pltpu.make_async_copy(rows_v.at[b], out_hbm.at[i % kk, pl.ds((g0 + i // kk) * w, w)], wsem.at[b])

        n_groups = -(-(n_items + nb) // group)

        @pl.loop(0, n_groups)
        def _(gi):
            slot = gi % 2

            @pl.when(gi * wins < per)
            def _():
                pltpu.sync_copy(idx_hbm.at[pl.ds(g0 + gi * wins, wins)], idx_v.at[slot])

            for l in range(group):
                i = gi * group + l
                b = l % nb
                bw = (l - (nb - 1)) % nb

                @pl.when(jnp.logical_and(i >= nb, i < n_items + nb))
                def _():
                    wr(i - nb, b).wait()

                @pl.when(i < n_items)
                def _():
                    gath(slot, l // kk, l % kk, b).start()

                @pl.when(jnp.logical_and(i >= nb - 1, i < n_items + nb - 1))
                def _():
                    gath(slot, 0, 0, bw).wait()
                    wr(i - (nb - 1), bw).start()

    return k(y, idx)


def _moe_kernel(layer, info_ref, x_ref, w1_hbm, b1_ref, w2_hbm, b2_ref, o_ref,
                w1f_ref, w2f_ref, w1b_ref, w2b_ref, sem):
    i = pl.program_id(0)
    nb = pl.num_programs(0)
    expert = info_ref[i]
    nxt = info_ref[nb + i]
    slot = info_ref[2 * nb + i]
    first = jnp.logical_or(i == 0, expert != info_ref[jnp.maximum(i - 1, 0)])

    def fetch(e, s):
        return (pltpu.make_async_copy(w1_hbm.at[layer, e], w1f_ref.at[s], sem.at[0, s]),
                pltpu.make_async_copy(w2_hbm.at[layer, e], w2f_ref.at[s], sem.at[1, s]))

    @pl.when(i == 0)
    def _():
        for cp in fetch(expert, slot):
            cp.start()

    @pl.when(first)
    def _():
        for cp in fetch(expert, slot):
            cp.wait()

        @pl.when(nxt >= 0)
        def _():
            for cp in fetch(nxt, 1 - slot):
                cp.start()

        w1b_ref[...] = w1f_ref[slot].astype(BF16)
        w2b_ref[...] = w2f_ref[slot].astype(BF16)

    @pl.when(i < info_ref[3 * nb])
    def _():
        _expert_block(x_ref, w1b_ref, b1_ref, w2b_ref, b2_ref, o_ref)


def _expert_block(x_ref, w1b_ref, b1_ref, w2b_ref, b2_ref, o_ref):
    lo, hi = _unpack_pairs(x_ref[...])
    x = jnp.concatenate([lo.astype(BF16), hi.astype(BF16)], axis=-1)
    f = w2b_ref.shape[0]
    fc = 512
    acc = None
    for c in range(f // fc):
        gt = jnp.dot(x, w1b_ref[:, c * fc:(c + 1) * fc], preferred_element_type=F32) + b1_ref[0, 0, :, c * fc:(c + 1) * fc]
        up = jnp.dot(x, w1b_ref[:, f + c * fc:f + (c + 1) * fc], preferred_element_type=F32) + b1_ref[0, 0, :, f + c * fc:f + (c + 1) * fc]
        gt = jnp.minimum(gt, SWIGLU_LIMIT)
        up = jnp.clip(up, -SWIGLU_LIMIT, SWIGLU_LIMIT)
        act = (gt * jax.nn.sigmoid(SWIGLU_ALPHA * gt) * (up + 1.0)).astype(BF16)
        part = jnp.dot(act, w2b_ref[c * fc:(c + 1) * fc, :], preferred_element_type=F32)
        acc = part if acc is None else acc + part
    o_ref[...] = _pack_pairs(acc + b2_ref[0, 0])


def _moe_experts(layer, block_expert, xs, p_rows, w1, b1, w2, b2):
    _, e, d, f2 = w1.shape
    f = f2 // 2
    nb = p_rows // MOE_ROWS
    grid_spec = pltpu.PrefetchScalarGridSpec(
        num_scalar_prefetch=1,
        grid=(nb,),
        in_specs=[pl.BlockSpec((MOE_ROWS, d // 2), lambda i, be: (i, 0)),
                  pl.BlockSpec(memory_space=pl.ANY),
                  pl.BlockSpec((1, 1, 1, f2), lambda i, be: (layer, be[i], 0, 0)),
                  pl.BlockSpec(memory_space=pl.ANY),
                  pl.BlockSpec((1, 1, 1, d), lambda i, be: (layer, be[i], 0, 0))],
        out_specs=pl.BlockSpec((MOE_ROWS, d // 2), lambda i, be: (i, 0)),
        scratch_shapes=[pltpu.VMEM((2, d, f2), F32), pltpu.VMEM((2, f, d), F32),
                        pltpu.VMEM((d, f2), BF16), pltpu.VMEM((f, d), BF16),
                        pltpu.SemaphoreType.DMA((2, 2))],
    )
    return pl.pallas_call(
        functools.partial(_moe_kernel, layer),
        grid_spec=grid_spec,
        out_shape=jax.ShapeDtypeStruct((p_rows, d // 2), U32),
        compiler_params=_cparams("arbitrary"),
        name="moe_experts",
    )(block_expert, xs, w1, b1.reshape(b1.shape[0], e, 1, f2), w2, b2.reshape(b2.shape[0], e, 1, d))


def _moe_dispatch(h2_words, lg_t):
    gate, dest, block_info, p_rows = _route(lg_t)
    spare = p_rows + jnp.arange(SC_SCATTER_WIN, dtype=I32)
    xs = _sc_scatter_rows(h2_words, _windows(dest, SC_SCATTER_WIN, 2, spare), p_rows + SC_SCATTER_WIN)
    return xs, gate.T, dest, block_info, p_rows


def _moe_compute(layer, dispatched, w1, b1, w2, b2):
    xs, gate, dest, block_info, p_rows = dispatched
    ys = _moe_experts(layer, block_info, xs, p_rows, w1, b1, w2, b2)
    group_wins = math.lcm(SC_GATHER_BUFS, TOP_K) // TOP_K
    yg = _sc_gather_rows(ys, _windows(dest, SC_GATHER_WIN, group_wins, jnp.zeros((SC_GATHER_WIN,), I32)))
    return yg, gate


def _final_kernel(x_ref, yg_ref, gate_ref, g2_ref, nw_ref, *rest):
    o_ref = rest[-1]
    x = x_ref[0] + g2_ref[0] * _combine(yg_ref, gate_ref)
    ms = jnp.mean(x * x, axis=-1, keepdims=True)
    o_ref[0] = x * lax.rsqrt(ms + EPS) * nw_ref[...]


def _final(x, res, nw, ts, out_buf, ob0, b_total):
    b, s, d = x.shape
    nj = s // ts
    yg, gate, g2, first = res
    blk0 = first // ts
    row = pl.BlockSpec((1, ts, d), lambda i, j: (i, j, 0))
    in_specs = [row, pl.BlockSpec((TOP_K, ts, d // 2), lambda i, j: (0, blk0 + i * nj + j, 0)),
                pl.BlockSpec((1, ts, TOP_K), lambda i, j: (i, j, 0)),
                pl.BlockSpec((1, 1, d), lambda i, j: (i, 0, 0)),
                pl.BlockSpec((1, d), lambda i, j: (0, 0))]
    args = [x, yg, gate, g2, nw.reshape(1, d)]
    aliases = {}
    if out_buf is not None:
        in_specs.append(pl.BlockSpec(memory_space=pl.ANY))
        args.append(out_buf)
        aliases = {len(args) - 1: 0}
    return pl.pallas_call(
        _final_kernel,
        grid=(b, nj),
        in_specs=in_specs,
        out_specs=pl.BlockSpec((1, ts, d), lambda i, j: (i + ob0, j, 0)),
        out_shape=jax.ShapeDtypeStruct((b_total, s, d), F32),
        input_output_aliases=aliases,
        compiler_params=_cparams("parallel", "parallel"),
        name="final_norm",
    )(*args)


def kernel(x_prompt, x_sample, c_prompt, c_sample, cache_k, cache_v, state_hgrn, norm_mix, norm_ffn, norm_final, w_ada, b_ada, w_in, rel_bias, hgrn_lb_logits, g_attn_out, g_hgrn_out, w_out, w_router, b_router, w_e1, b_e1, w_e2, b_e2):
    depth = w_in.shape[0]
    bp, sp, d = x_prompt.shape
    bs, ls, _ = x_sample.shape
    wc = cache_k.shape[2]
    n_p, n_s = bp * sp, bs * ls

    lbs = jax.nn.softmax(hgrn_lb_logits.astype(F32), axis=0)
    lbs = jnp.cumsum(lbs, axis=0) - lbs[0]

    mods = _modulation(jnp.concatenate([c_prompt, c_sample], axis=0), w_ada, b_ada)
    lk_s = -(-(wc + ls) // LANES) * LANES
    tail = min(LEFT, sp)
    rec_chunk = min(REC_CHUNK, sp)

    n_groups = PROMPT_GROUPS if bp % PROMPT_GROUPS == 0 else 1
    gb = bp // n_groups
    n_g = gb * sp
    x_g = [(x_prompt, g * gb) for g in range(n_groups)]
    x_s = (x_sample, 0)
    res_g = [None] * n_groups
    res_s = None
    kp_l, vp_l, sp_l, ks_l, vs_l, ss_l = [], [], [], [], [], []
    for layer in range(depth):
        m6 = mods[layer].reshape(bp + bs, 1, N_MOD, d)
        mod_s = [m6[bp:, :, n, :] for n in range(N_MOD)]
        w_in_bf = w_in[layer].astype(BF16)
        w_out_bf = w_out[layer].astype(BF16)
        wr_hi, wr_mid, _ = _split3(w_router[layer].T)
        wr2 = jnp.concatenate([wr_hi, wr_mid], axis=0)
        table = rel_bias[layer]
        bias_p = _prompt_bias(table)

        def mixer_prompt(g):
            mod = [m6[g * gb:(g + 1) * gb, :, n, :] for n in range(N_MOD)]
            x, qa, ka, va, kt, vt, qh, kh, lf, vh, gh = _inproj(
                x_g[g][0], x_g[g][1], res_g[g], norm_mix[layer], mod[1], mod[0], w_in_bf, lbs[layer],
                SEQ_TILE, tail)
            o_att = _attn_prompt(qa, ka, va, bias_p)
            s0 = jnp.zeros((gb, REC_HEADS, REC_DIM, REC_DIM), F32)
            o_rec, s_fin = _hgrn(qh, kh, lf, vh, s0, SEQ_TILE, rec_chunk)
            x_new, h2, lg = _merge(o_att, o_rec, gh, x[0], x[1], g_attn_out[layer], g_hgrn_out[layer], w_out_bf,
                                   mod[2], norm_ffn[layer], mod[4], mod[3], wr2, b_router[layer], SEQ_TILE)
            x_g[g] = (x_new, 0)
            return h2.reshape(n_g, d // 2), lg, kt, vt, s_fin, mod[5]

        def mixer_sample():
            x, qa, ka, va, kt, vt, qh, kh, lf, vh, gh = _inproj(
                x_s[0], x_s[1], res_s, norm_mix[layer], mod_s[1], mod_s[0], w_in_bf, lbs[layer], ls, ls)
            zpad = jnp.zeros((bs, lk_s - wc - ls, ATT_W), BF16)
            keys = jnp.concatenate([cache_k[layer].reshape(bs, wc, ATT_W).astype(BF16), ka, zpad], axis=1)
            vals = jnp.concatenate([cache_v[layer].reshape(bs, wc, ATT_W).astype(BF16), va, zpad], axis=1)
            o_att = _attn_sample(qa, keys, vals, _sample_bias(table, ls, wc, lk_s))
            s0 = jnp.swapaxes(state_hgrn[layer].astype(F32), -1, -2)
            o_rec, s_new = _hgrn(qh, kh, lf, vh, s0, ls, ls)
            x_new, h2, lg = _merge(o_att, o_rec, gh, x[0], x[1], g_attn_out[layer], g_hgrn_out[layer], w_out_bf,
                                   mod_s[2], norm_ffn[layer], mod_s[4], mod_s[3], wr2, b_router[layer], ls)
            return (x_new, 0), h2.reshape(n_s, d // 2), jnp.swapaxes(lg, 0, 1).reshape(N_EXPERTS, n_s), kt, vt, s_new

        x_s, h2_s, lg_s, ka_s, va_s, s_new = mixer_sample()
        ks_l.append(ka_s.reshape(bs, ls, ATT_HEADS, ATT_DIM))
        vs_l.append(va_s.reshape(bs, ls, ATT_HEADS, ATT_DIM))
        ss_l.append(jnp.swapaxes(s_new, -1, -2))

        dispatched, g2s, k_t, v_t, s_t = [], [], [], [], []
        for g in range(n_groups):
            h2, lg, kt, vt, s_fin, g2 = mixer_prompt(g)
            if g == n_groups - 1:
                h2 = jnp.concatenate([h2, h2_s], axis=0)
                lg = jnp.concatenate([lg, lg_s], axis=1)
            dispatched.append(_moe_dispatch(h2, lg))
            g2s.append(g2)
            k_t.append(kt)
            v_t.append(vt)
            s_t.append(s_fin)
        kp_l.append(jnp.concatenate(k_t, axis=0).reshape(bp, tail, ATT_HEADS, ATT_DIM))
        vp_l.append(jnp.concatenate(v_t, axis=0).reshape(bp, tail, ATT_HEADS, ATT_DIM))
        sp_l.append(jnp.swapaxes(jnp.concatenate(s_t, axis=0), -1, -2))

        for g in range(n_groups):
            yg, gate = _moe_compute(layer, dispatched[g], w_e1, b_e1, w_e2, b_e2)
            res_g[g] = (yg, gate[:n_g].reshape(gb, sp, TOP_K), g2s[g], 0)
            if g == n_groups - 1:
                res_s = (yg, gate[n_g:].reshape(bs, ls, TOP_K), mod_s[5], n_g)

    y_prompt = None
    for g in range(n_groups):
        y_prompt = _final(x_g[g][0], res_g[g], norm_final, SEQ_TILE, y_prompt, g * gb, bp)
    y_sample = _final(x_s[0], res_s, norm_final, ls, None, 0, bs)
    return (y_prompt, y_sample, jnp.stack(kp_l), jnp.stack(vp_l), jnp.stack(sp_l),
            jnp.stack(ks_l), jnp.stack(vs_l), jnp.stack(ss_l))
```

```python
import functools
import math

import jax
import jax.numpy as jnp
import numpy as np
from jax import lax
from jax.experimental import pallas as pl
from jax.experimental.pallas import tpu as pltpu
from jax.experimental.pallas import tpu_sc as plsc

F32 = jnp.float32
BF16 = jnp.bfloat16
I32 = jnp.int32
U32 = jnp.uint32

EPS = 1e-6
CHUNK = 64
LEFT_CHUNKS = 8
LEFT = LEFT_CHUNKS * CHUNK
MAX_REL = 2 * CHUNK
ATT_HEADS = 8
ATT_DIM = 64
ATT_W = ATT_HEADS * ATT_DIM
REC_HEADS = 4
REC_DIM = 128
REC_W = REC_HEADS * REC_DIM
N_EXPERTS = 32
TOP_K = 4
SWIGLU_ALPHA = 1.702
SWIGLU_LIMIT = 7.0
N_MOD = 6
NEG = -1e30
LOG2E = 1.4426950408889634

LANES = 128
ATT_QSUB = 128
ATT_WIN = LEFT + ATT_QSUB
SEQ_TILE = 512
REC_CHUNK = 128
MOE_ROWS = 512
ROUTE_TILE = 256
PROMPT_GROUPS = 2
VMEM_LIMIT = 56 * 1024 * 1024
SC_WORKERS = 32
SC_SCATTER_WIN = 64
SC_GATHER_WIN = 32
SC_GATHER_BUFS = 6


def _cparams(*sem):
    return pltpu.CompilerParams(dimension_semantics=sem, vmem_limit_bytes=VMEM_LIMIT)


def _split3(x):
    hi = x.astype(BF16)
    r1 = x - hi.astype(F32)
    mid = r1.astype(BF16)
    lo = (r1 - mid.astype(F32)).astype(BF16)
    return hi, mid, lo


def _bits(x):
    return lax.bitcast_convert_type(x, U32)


def _pack_pairs(x):
    r = _bits(x.astype(BF16).astype(F32))
    half = x.shape[1] // 2
    return jnp.bitwise_or(jnp.bitwise_and(r[:, half:], jnp.uint32(0xFFFF0000)),
                          jnp.right_shift(r[:, :half], jnp.uint32(16)))


def _unpack_pairs(word):
    lo = lax.bitcast_convert_type(jnp.left_shift(word, jnp.uint32(16)), F32)
    hi = lax.bitcast_convert_type(jnp.bitwise_and(word, jnp.uint32(0xFFFF0000)), F32)
    return lo, hi


def _mod_kernel(c_ref, w_ref, b_ref, o_ref):
    c = c_ref[...]
    act = (c * jax.nn.sigmoid(c)).astype(BF16)
    o_ref[0] = jnp.dot(act, w_ref[0].astype(BF16), preferred_element_type=F32) + b_ref[0]


def _modulation(c_all, w_ada, b_ada):
    depth, d, n6 = w_ada.shape
    rows = c_all.shape[0]
    tn = 1536
    return pl.pallas_call(
        _mod_kernel,
        grid=(depth, n6 // tn),
        in_specs=[pl.BlockSpec((rows, d), lambda l, j: (0, 0)),
                  pl.BlockSpec((1, d, tn), lambda l, j: (l, 0, j)),
                  pl.BlockSpec((1, 1, tn), lambda l, j: (l, 0, j))],
        out_specs=pl.BlockSpec((1, rows, tn), lambda l, j: (l, 0, j)),
        out_shape=jax.ShapeDtypeStruct((depth, rows, n6), F32),
        compiler_params=_cparams("parallel", "parallel"),
        name="adaln_modulation",
    )(c_all, w_ada, b_ada.reshape(depth, 1, n6))


def _combine(yg_ref, gate_ref):
    g = gate_ref[0]
    lo = hi = None
    for k in range(TOP_K):
        l, h = _unpack_pairs(yg_ref[k])
        gk = g[:, k:k + 1]
        lo = l * gk if lo is None else lo + l * gk
        hi = h * gk if hi is None else hi + h * gk
    return jnp.concatenate([lo, hi], axis=-1)


def _inproj_kernel(has_res, *refs):
    if has_res:
        (x_ref, yg_ref, gate_ref, g2_ref, nw_ref, sc_ref, sh_ref, w_ref, lb_ref,
         xo_ref, qa_ref, ka_ref, va_ref, kt_ref, vt_ref, qh_ref, kh_ref, lf_ref, vh_ref, gh_ref) = refs
        x = x_ref[0] + g2_ref[0] * _combine(yg_ref, gate_ref)
        xo_ref[0] = x
    else:
        (x_ref, nw_ref, sc_ref, sh_ref, w_ref, lb_ref,
         qa_ref, ka_ref, va_ref, kt_ref, vt_ref, qh_ref, kh_ref, lf_ref, vh_ref, gh_ref) = refs
        x = x_ref[0]
    _project_rows(x, nw_ref, sc_ref, sh_ref, w_ref, lb_ref, qa_ref.at[0], ka_ref.at[0], va_ref.at[0],
                  kt_ref.at[0], vt_ref.at[0], qh_ref.at[0], kh_ref.at[0], lf_ref.at[0], vh_ref.at[0], gh_ref.at[0])


def _project_rows(x, nw_ref, sc_ref, sh_ref, w_ref, lb_ref,
                  qa_ref, ka_ref, va_ref, kt_ref, vt_ref, qh_ref, kh_ref, lf_ref, vh_ref, gh_ref):
    ms = jnp.mean(x * x, axis=-1, keepdims=True)
    h = (x * lax.rsqrt(ms + EPS) * nw_ref[...]) * (1.0 + sc_ref[0]) + sh_ref[0]
    hb = h.astype(BF16)

    def proj(g):
        return jnp.dot(hb, w_ref[:, g * ATT_W:(g + 1) * ATT_W], preferred_element_type=F32)

    qa_ref[...] = (proj(0) * (ATT_DIM ** -0.5)).astype(BF16)
    k = proj(1)
    ka_ref[...] = k.astype(BF16)
    kt_ref[...] = k
    v = proj(2)
    va_ref[...] = v.astype(BF16)
    vt_ref[...] = v
    qh = proj(3)
    qh_ref[...] = qh * jax.nn.sigmoid(qh)
    f = proj(4)
    lb = lb_ref[...]
    lf_ref[...] = jnp.log(lb + (1.0 - lb) * jax.nn.sigmoid(f))
    kh_ref[...] = (1.0 - lb) * jax.nn.sigmoid(-f)
    vh_ref[...] = proj(5).astype(BF16)
    gh_ref[...] = proj(6)


def _inproj(x, xb0, res, nw, sc, sh, w_bf, lb, ts, tail):
    _, s, d = x.shape
    b = sc.shape[0]
    pw = w_bf.shape[1]
    nj = s // ts
    n_tail = tail // ts
    has_res = res is not None
    tail_spec = pl.BlockSpec((1, ts, ATT_W), lambda i, j: (i, jnp.maximum(j - (nj - n_tail), 0), 0))
    ht = jax.ShapeDtypeStruct((b, tail, ATT_W), F32)
    row = pl.BlockSpec((1, ts, d), lambda i, j: (i, j, 0))
    per_b = pl.BlockSpec((1, 1, d), lambda i, j: (i, 0, 0))
    half = pl.BlockSpec((1, ts, ATT_W), lambda i, j: (i, j, 0))
    in_specs, args = [pl.BlockSpec((1, ts, d), lambda i, j: (i + xb0, j, 0))], [x]
    if has_res:
        yg, gate, g2, first = res
        blk0 = first // ts
        in_specs += [pl.BlockSpec((TOP_K, ts, d // 2), lambda i, j: (0, blk0 + i * nj + j, 0)),
                     pl.BlockSpec((1, ts, TOP_K), lambda i, j: (i, j, 0)), per_b]
        args += [yg, gate, g2]
    in_specs += [pl.BlockSpec((1, d), lambda i, j: (0, 0)), per_b, per_b,
                 pl.BlockSpec((d, pw), lambda i, j: (0, 0)),
                 pl.BlockSpec((1, REC_W), lambda i, j: (0, 0))]
    args += [nw.reshape(1, d), sc, sh, w_bf, lb.reshape(1, REC_W)]
    hs = jax.ShapeDtypeStruct((b, s, ATT_W), F32)
    hb = jax.ShapeDtypeStruct((b, s, ATT_W), BF16)
    out_shape = ([jax.ShapeDtypeStruct((b, s, d), F32)] if has_res else []) + [hb, hb, hb, ht, ht, hs, hs, hs, hb, hs]
    out_specs = ([row] if has_res else []) + [half] * 3 + [tail_spec] * 2 + [half] * 5
    outs = pl.pallas_call(
        functools.partial(_inproj_kernel, has_res),
        grid=(b, nj),
        in_specs=in_specs, out_specs=out_specs, out_shape=out_shape,
        compiler_params=_cparams("parallel", "arbitrary"),
        name="norm_inproj",
    )(*args)
    if has_res:
        return [(outs[0], 0)] + list(outs[1:])
    return [(x, xb0)] + list(outs)


def _attend(q, kw, vw, bias_ref, s_ref, col_thr):
    lq, lk = q.shape[0], kw.shape[0]
    lane = lax.broadcasted_iota(I32, (lq, LANES), 1)
    lo = lane < ATT_DIM
    nt = (((1,), (1,)), ((), ()))
    if col_thr is not None:
        keep = lax.broadcasted_iota(I32, (2 * lq, lk), 1) >= col_thr
    ms = []
    for p in range(ATT_HEADS // 2):
        sl = slice(p * LANES, (p + 1) * LANES)
        qp = q[:, sl]
        z = jnp.zeros_like(qp)
        lhs = jnp.concatenate([jnp.where(lo, qp, z), jnp.where(lo, z, qp)], axis=0)
        s = lax.dot_general(lhs, kw[:, sl], nt, preferred_element_type=F32) + bias_ref[p]
        if col_thr is not None:
            s = jnp.where(keep, s, NEG)
        ms.append(jnp.max(s, axis=-1, keepdims=True))
        s_ref[p] = s
    outs = []
    for p in range(ATT_HEADS // 2):
        sl = slice(p * LANES, (p + 1) * LANES)
        e = jnp.exp(s_ref[p] - ms[p])
        l = jnp.sum(e, axis=-1, keepdims=True)
        o = jnp.dot(e.astype(BF16), vw[:, sl], preferred_element_type=F32) / l
        outs.append(jnp.where(lo, o[:lq], o[lq:]))
    return jnp.concatenate(outs, axis=-1)


def _band_attention(q_ref, kw_ref, vw_ref, bias_ref, s_ref, o_ref, first_tile):
    ts = q_ref.shape[0]

    def body(masked, r, carry):
        r0 = pl.multiple_of(r * ATT_QSUB, ATT_QSUB)
        q = q_ref[pl.ds(r0, ATT_QSUB), :]
        kw = kw_ref[pl.ds(r0, ATT_WIN), :]
        vw = vw_ref[pl.ds(r0, ATT_WIN), :]
        o_ref[pl.ds(r0, ATT_QSUB), :] = _attend(q, kw, vw, bias_ref, s_ref, (ts - r0) if masked else None)
        return carry

    @pl.when(first_tile)
    def _():
        lax.fori_loop(0, ts // ATT_QSUB, functools.partial(body, True), 0)

    @pl.when(jnp.logical_not(first_tile))
    def _():
        lax.fori_loop(0, ts // ATT_QSUB, functools.partial(body, False), 0)


def _attn_sample_kernel(q_ref, k_ref, v_ref, bias_ref, o_ref, s_ref):
    o_ref[0] = _attend(q_ref[0], k_ref[0], v_ref[0], bias_ref, s_ref, None)


def _attn_sample(qa, keys, vals, bias):
    b, l, w = qa.shape
    lk = keys.shape[1]
    return pl.pallas_call(
        _attn_sample_kernel,
        grid=(b,),
        in_specs=[pl.BlockSpec((1, l, w), lambda i: (i, 0, 0)),
                  pl.BlockSpec((1, lk, w), lambda i: (i, 0, 0)),
                  pl.BlockSpec((1, lk, w), lambda i: (i, 0, 0)),
                  pl.BlockSpec(bias.shape, lambda i: (0, 0, 0))],
        out_specs=pl.BlockSpec((1, l, w), lambda i: (i, 0, 0)),
        out_shape=jax.ShapeDtypeStruct((b, l, w), F32),
        scratch_shapes=[pltpu.VMEM((ATT_HEADS // 2, 2 * l, lk), F32)],
        compiler_params=_cparams("parallel"),
        name="cache_attention",
    )(qa, keys, vals, bias)


def _toeplitz_bias(table, lq, lk, offset, valid):
    m = lq + lk - 1
    k = np.arange(m)
    diff = np.where(k < lk, k, k - m)
    g = table[:, np.clip(offset - diff, -MAX_REL, MAX_REL) + MAX_REL]
    skew = jnp.tile(g, (1, lq))[:, :lq * (m - 1)].reshape(-1, lq, m - 1)[:, :, :lk]
    bias = jnp.where(jnp.asarray(valid)[None], skew, NEG).astype(F32)
    return bias.reshape(ATT_HEADS // 2, 2 * lq, lk)


def _prompt_bias(table):
    t = np.arange(ATT_QSUB)[:, None]
    j = np.arange(ATT_WIN)[None, :]
    start = (t // CHUNK) * CHUNK
    return _toeplitz_bias(table, ATT_QSUB, ATT_WIN, LEFT, (j >= start) & (j < start + LEFT + CHUNK))


def _sample_bias(table, l, w, lk):
    valid = np.broadcast_to(np.arange(lk)[None, :] < w + l, (l, lk))
    return _toeplitz_bias(table, l, lk, w, valid)


def _neg_abs(x):
    return lax.bitcast_convert_type(jnp.bitwise_or(_bits(x), jnp.uint32(0x80000000)), F32)


def _hgrn_kernel(chunk, q_ref, k_ref, lf_ref, v_ref, s0_ref, o_ref, sf_ref, st_ref):
    i = pl.program_id(1)

    @pl.when(i == 0)
    def _():
        st_ref[...] = s0_ref[0]

    _hgrn_rows(chunk, q_ref.at[0], k_ref.at[0], lf_ref.at[0], v_ref.at[0], o_ref.at[0], st_ref)

    @pl.when(i == pl.num_programs(1) - 1)
    def _():
        sf_ref[0] = st_ref[...]


def _hgrn_rows(chunk, q_ref, k_ref, lf_ref, v_ref, o_ref, st_ref):
    ts = q_ref.shape[0]
    n_chunks = ts // chunk

    row = lax.broadcasted_iota(I32, (chunk, chunk), 0)
    colm = lax.broadcasted_iota(I32, (chunk, chunk), 1)
    tri = (colm <= row).astype(BF16)
    xor = jnp.bitwise_xor(row, colm)
    lower = colm < row
    rid = lax.broadcasted_iota(I32, (chunk, REC_W), 0)

    levels = []
    b = chunk // 2
    while b >= 1:
        levels.append(b)
        b //= 2
    pair_mask = [xor == 0] + [lower & (xor >= b) & (xor < 2 * b) for b in levels]
    upper = [jnp.bitwise_and(rid, b) != 0 for b in levels]
    ph4 = jnp.bitwise_and(rid, 3)
    odd = jnp.bitwise_and(rid, 1) == 1
    nt = (((1,), (1,)), ((), ()))
    tn = (((0,), (0,)), ((), ()))
    sls = [slice(h * REC_DIM, (h + 1) * REC_DIM) for h in range(REC_HEADS)]

    def boundary(cum, b):
        if 2 * b >= 8:
            pieces = [jnp.broadcast_to(cum[m * 2 * b + b - 1:m * 2 * b + b, :], (2 * b, REC_W))
                      for m in range(chunk // (2 * b))]
            return pieces[0] if len(pieces) == 1 else jnp.concatenate(pieces, axis=0)
        if b == 2:
            up1 = pltpu.roll(cum, chunk - 1, 0)
            dn1 = pltpu.roll(cum, 1, 0)
            dn2 = pltpu.roll(cum, 2, 0)
            return jnp.where(ph4 == 0, up1, jnp.where(ph4 == 1, cum, jnp.where(ph4 == 2, dn1, dn2)))
        return jnp.where(odd, pltpu.roll(cum, 1, 0), cum)

    def one_chunk(c, carry):
        r0 = pl.multiple_of(c * chunk, chunk)
        rows = pl.ds(r0, chunk)
        q = q_ref[rows, :]
        k = k_ref[rows, :]
        v = v_ref[rows, :]
        lf = lf_ref[rows, :]
        cum = sum(jnp.dot(tri, part, preferred_element_type=F32) for part in _split3(lf)) * LOG2E
        last = cum[chunk - 1:chunk, :]
        q_in = (q * jnp.exp2(cum)).astype(BF16)
        k_out = (k * jnp.exp2(last - cum)).astype(BF16)
        decay = jnp.exp2(last)
        qb = q.astype(BF16)
        kb = k.astype(BF16)
        zs = []
        for n, b in enumerate(levels):
            w = jnp.exp2(_neg_abs(cum - boundary(cum, b)))
            zs.append((jnp.where(upper[n], q, k) * w).astype(BF16))
        sts = [st_ref[h] for h in range(REC_HEADS)]
        d0 = [lax.dot_general(qb[:, sl], kb[:, sl], nt, preferred_element_type=F32) for sl in sls]
        dl = [[lax.dot_general(z[:, sl], z[:, sl], nt, preferred_element_type=F32) for z in zs] for sl in sls]
        oi = [lax.dot_general(q_in[:, sl], st.astype(BF16), nt, preferred_element_type=F32)
              for sl, st in zip(sls, sts)]
        upd = [lax.dot_general(v[:, sl], k_out[:, sl], tn, preferred_element_type=F32) for sl in sls]
        outs = []
        for h, sl in enumerate(sls):
            a = jnp.where(pair_mask[0], d0[h], 0.0)
            for n in range(len(levels)):
                a = jnp.where(pair_mask[n + 1], dl[h][n], a)
            outs.append(jnp.dot(a.astype(BF16), v[:, sl], preferred_element_type=F32) + oi[h])
        o_ref[rows, :] = jnp.concatenate(outs, axis=-1)
        for h, sl in enumerate(sls):
            st_ref[h] = decay[:, sl] * sts[h] + upd[h]
        return carry

    lax.fori_loop(0, n_chunks, one_chunk, 0, unroll=2 if n_chunks % 2 == 0 else 1)


def _hgrn(q, k, lf, v, s0_t, ts, chunk):
    b, s, w = q.shape
    blk = pl.BlockSpec((1, ts, w), lambda i, j: (i, j, 0))
    st = pl.BlockSpec((1, REC_HEADS, REC_DIM, REC_DIM), lambda i, j: (i, 0, 0, 0))
    return pl.pallas_call(
        functools.partial(_hgrn_kernel, chunk),
        grid=(b, s // ts),
        in_specs=[blk, blk, blk, blk, st],
        out_specs=[blk, st],
        out_shape=[jax.ShapeDtypeStruct((b, s, w), F32),
                   jax.ShapeDtypeStruct((b, REC_HEADS, REC_DIM, REC_DIM), F32)],
        scratch_shapes=[pltpu.VMEM((REC_HEADS, REC_DIM, REC_DIM), F32)],
        compiler_params=_cparams("parallel", "arbitrary"),
        name="hgrn_recurrence",
    )(q, k, lf, v, s0_t)


def _head_rms(x, width):
    lane = lax.broadcasted_iota(I32, (x.shape[0], LANES), 1)
    lo = lane < ATT_DIM
    outs = []
    for p in range(x.shape[1] // LANES):
        xp = x[:, p * LANES:(p + 1) * LANES]
        sq = xp * xp
        tot = jnp.sum(sq, axis=-1, keepdims=True)
        if width == LANES:
            ms = tot * (1.0 / LANES)
        else:
            s_lo = jnp.sum(jnp.where(lo, sq, 0.0), axis=-1, keepdims=True)
            ms = jnp.where(lo, s_lo, tot - s_lo) * (1.0 / ATT_DIM)
        outs.append(xp * lax.rsqrt(ms + EPS))
    return jnp.concatenate(outs, axis=-1)


def _merge_rows(oa, orr, gh, x, ga_ref, gr_ref, wo_ref, g1_ref, nf_ref, sc_ref, sh_ref, wr_ref, br_ref):
    a = _head_rms(oa, ATT_DIM) * ga_ref[...]
    r = _head_rms(orr, REC_DIM) * gr_ref[...] * (gh * jax.nn.sigmoid(gh))
    mix = jnp.dot(a.astype(BF16), wo_ref[0:ATT_W, :], preferred_element_type=F32)
    mix += jnp.dot(r.astype(BF16), wo_ref[ATT_W:ATT_W + REC_W, :], preferred_element_type=F32)
    x = x + g1_ref[0] * mix
    ms = jnp.mean(x * x, axis=-1, keepdims=True)
    h2 = (x * lax.rsqrt(ms + EPS) * nf_ref[...]) * (1.0 + sc_ref[0]) + sh_ref[0]
    hh = h2.astype(BF16)
    words = _pack_pairs(h2)
    hm = (h2 - hh.astype(F32)).astype(BF16)
    nt = (((1,), (1,)), ((), ()))
    l1 = lax.dot_general(wr_ref[...], hh, nt, preferred_element_type=F32)
    l2 = lax.dot_general(wr_ref[0:N_EXPERTS, :], hm, nt, preferred_element_type=F32)
    lg = l1[0:N_EXPERTS] + l1[N_EXPERTS:2 * N_EXPERTS] + l2 + br_ref[...]
    return x, words, lg


def _merge_kernel(oa_ref, or_ref, gh_ref, x_ref, ga_ref, gr_ref, wo_ref, g1_ref, nf_ref, sc_ref, sh_ref,
                  wr_ref, br_ref, xo_ref, h2_ref, lg_ref):
    x, words, lg = _merge_rows(oa_ref[0], or_ref[0], gh_ref[0], x_ref[0], ga_ref, gr_ref, wo_ref, g1_ref,
                               nf_ref, sc_ref, sh_ref, wr_ref, br_ref)
    xo_ref[0] = x
    h2_ref[0] = words
    if len(lg_ref.shape) == 3:
        lg_ref[0] = lg
    else:
        lg_ref[...] = lg


def _merge(o_att, o_rec, gh, x, xb0, g_att, g_rec, wo_bf, g1, nf, sc, sh, wr2, br, ts):
    _, s, d = x.shape
    b = o_att.shape[0]
    nj = s // ts
    row = pl.BlockSpec((1, ts, d), lambda i, j: (i, j, 0))
    xrow = pl.BlockSpec((1, ts, d), lambda i, j: (i + xb0, j, 0))
    half = pl.BlockSpec((1, ts, ATT_W), lambda i, j: (i, j, 0))
    per_b = pl.BlockSpec((1, 1, d), lambda i, j: (i, 0, 0))
    const2 = lambda shp: pl.BlockSpec(shp, lambda i, j: (0,) * len(shp))
    if ts % LANES == 0:
        lg_spec = pl.BlockSpec((N_EXPERTS, ts), lambda i, j: (0, i * nj + j))
        lg_shape = jax.ShapeDtypeStruct((N_EXPERTS, b * s), F32)
    else:
        lg_spec = pl.BlockSpec((1, N_EXPERTS, ts), lambda i, j: (i, 0, j))
        lg_shape = jax.ShapeDtypeStruct((b, N_EXPERTS, s), F32)
    return pl.pallas_call(
        _merge_kernel,
        grid=(b, nj),
        in_specs=[half, half, half, xrow, const2((1, ATT_W)), const2((1, REC_W)), const2(wo_bf.shape),
                  per_b, const2((1, d)), per_b, per_b, const2(wr2.shape), const2((N_EXPERTS, 1))],
        out_specs=[row, pl.BlockSpec((1, ts, d // 2), lambda i, j: (i, j, 0)), lg_spec],
        out_shape=[jax.ShapeDtypeStruct((b, s, d), F32), jax.ShapeDtypeStruct((b, s, d // 2), U32), lg_shape],
        compiler_params=_cparams("parallel", "parallel"),
        name="merge_outproj_router",
    )(o_att, o_rec, gh, x, g_att.reshape(1, ATT_W), g_rec.reshape(1, REC_W), wo_bf, g1, nf.reshape(1, d),
      sc, sh, wr2, br.reshape(N_EXPERTS, 1))


def _mixer_kernel(has_res, chunk, *refs):
    n_in = 4 if has_res else 1
    x_ref = refs[0]
    (nw_ref, sc1_ref, sh1_ref, lb_ref, ga_ref, gr_ref, g1_ref, nf_ref, sc2_ref, sh2_ref, wr_ref, br_ref,
     win_hbm, wout_hbm, bias_hbm,
     xo_ref, h2_ref, lg_ref, kt_ref, vt_ref, sf_ref,
     win_v, wout_v, bias_v, q_s, kw_s, vw_s, s_s, qh_s, kh_s, lf_s, vh_s, gh_s, oa_s, or_s, st_s, sem) = refs[n_in:]
    j = pl.program_id(1)
    ts = x_ref.shape[1]

    @pl.when(j == 0)
    def _():
        copies = [pltpu.make_async_copy(src, dst, sem.at[n])
                  for n, (src, dst) in enumerate(((win_hbm, win_v), (wout_hbm, wout_v), (bias_hbm, bias_v)))]
        for cp in copies:
            cp.start()
        for cp in copies:
            cp.wait()
        st_s[...] = jnp.zeros_like(st_s)
        kw_s[0:ts, :] = jnp.zeros((ts, ATT_W), BF16)
        vw_s[0:ts, :] = jnp.zeros((ts, ATT_W), BF16)

    if has_res:
        yg_ref, gate_ref, g2_ref = refs[1:4]
        x = x_ref[0] + g2_ref[0] * _combine(yg_ref, gate_ref)
        xo_ref[0] = x
    else:
        x = x_ref[0]
    cur = pl.ds(ts, ts)
    _project_rows(x, nw_ref, sc1_ref, sh1_ref, win_v, lb_ref, q_s, kw_s.at[cur], vw_s.at[cur],
                  kt_ref.at[0], vt_ref.at[0], qh_s, kh_s, lf_s, vh_s, gh_s)
    _band_attention(q_s, kw_s, vw_s, bias_v, s_s, oa_s, j == 0)
    _hgrn_rows(chunk, qh_s, kh_s, lf_s, vh_s, or_s, st_s)
    x_in = xo_ref[0] if has_res else x_ref[0]
    x_new, words, lg = _merge_rows(oa_s[...], or_s[...], gh_s[...], x_in, ga_ref, gr_ref, wout_v, g1_ref,
                                   nf_ref, sc2_ref, sh2_ref, wr_ref, br_ref)
    xo_ref[0] = x_new
    h2_ref[0] = words
    lg_ref[...] = lg
    kw_s[0:ts, :] = kw_s[ts:2 * ts, :]
    vw_s[0:ts, :] = vw_s[ts:2 * ts, :]

    @pl.when(j == pl.num_programs(1) - 1)
    def _():
        sf_ref[0] = st_s[...]


def _mixer_prompt(x, xb0, res, nw, sc1, sh1, w_in_bf, lb, bias, g_att, g_rec, w_out_bf, g1, nf, sc2, sh2,
                  wr2, br, tail, chunk):
    _, s, d = x.shape
    b = sc1.shape[0]
    ts = LEFT
    nj = s // ts
    n_tail = tail // ts
    has_res = res is not None
    row = pl.BlockSpec((1, ts, d), lambda i, j: (i, j, 0))
    per_b = pl.BlockSpec((1, 1, d), lambda i, j: (i, 0, 0))
    const = lambda shp: pl.BlockSpec(shp, lambda i, j: (0,) * len(shp))
    hbm = pl.BlockSpec(memory_space=pl.ANY)
    in_specs, args = [pl.BlockSpec((1, ts, d), lambda i, j: (i + xb0, j, 0))], [x]
    if has_res:
        yg, gate, g2, first = res
        blk0 = first // ts
        in_specs += [pl.BlockSpec((TOP_K, ts, d // 2), lambda i, j: (0, blk0 + i * nj + j, 0)),
                     pl.BlockSpec((1, ts, TOP_K), lambda i, j: (i, j, 0)), per_b]
        args += [yg, gate, g2]
    in_specs += [const((1, d)), per_b, per_b, const((1, REC_W)), const((1, ATT_W)), const((1, REC_W)),
                 per_b, const((1, d)), per_b, per_b, const(wr2.shape), const((N_EXPERTS, 1)), hbm, hbm, hbm]
    args += [nw.reshape(1, d), sc1, sh1, lb.reshape(1, REC_W), g_att.reshape(1, ATT_W), g_rec.reshape(1, REC_W),
             g1, nf.reshape(1, d), sc2, sh2, wr2, br.reshape(N_EXPERTS, 1), w_in_bf, w_out_bf, bias]
    tail_spec = pl.BlockSpec((1, ts, ATT_W), lambda i, j: (i, jnp.maximum(j - (nj - n_tail), 0), 0))
    state_spec = pl.BlockSpec((1, REC_HEADS, REC_DIM, REC_DIM), lambda i, j: (i, 0, 0, 0))
    tile = lambda dt: pltpu.VMEM((ts, ATT_W), dt)
    return pl.pallas_call(
        functools.partial(_mixer_kernel, has_res, chunk),
        grid=(b, nj),
        in_specs=in_specs,
        out_specs=[row, pl.BlockSpec((1, ts, d // 2), lambda i, j: (i, j, 0)),
                   pl.BlockSpec((N_EXPERTS, ts), lambda i, j: (0, i * nj + j)), tail_spec, tail_spec, state_spec],
        out_shape=[jax.ShapeDtypeStruct((b, s, d), F32), jax.ShapeDtypeStruct((b, s, d // 2), U32),
                   jax.ShapeDtypeStruct((N_EXPERTS, b * s), F32),
                   jax.ShapeDtypeStruct((b, tail, ATT_W), F32), jax.ShapeDtypeStruct((b, tail, ATT_W), F32),
                   jax.ShapeDtypeStruct((b, REC_HEADS, REC_DIM, REC_DIM), F32)],
        scratch_shapes=[pltpu.VMEM(w_in_bf.shape, BF16), pltpu.VMEM(w_out_bf.shape, BF16), pltpu.VMEM(bias.shape, F32),
                        tile(BF16), pltpu.VMEM((2 * ts, ATT_W), BF16), pltpu.VMEM((2 * ts, ATT_W), BF16),
                        pltpu.VMEM((ATT_HEADS // 2, 2 * ATT_QSUB, ATT_WIN), F32),
                        tile(F32), tile(F32), tile(F32), tile(BF16), tile(F32), tile(F32), tile(F32),
                        pltpu.VMEM((REC_HEADS, REC_DIM, REC_DIM), F32), pltpu.SemaphoreType.DMA((3,))],
        compiler_params=_cparams("parallel", "arbitrary"),
        name="prompt_mixer",
    )(*args)


def _route_kernel(lg_ref, idx_ref, gate_ref, rank_ref, cnt_ref, run_ref):
    i = pl.program_id(0)
    t = lg_ref.shape[1]

    @pl.when(i == 0)
    def _():
        run_ref[...] = jnp.zeros_like(run_ref)

    x = lg_ref[...]
    row = lax.broadcasted_iota(I32, x.shape, 0)
    vals, idxs, hits = [], [], []
    for _ in range(TOP_K):
        m = jnp.max(x, axis=0, keepdims=True)
        ik = jnp.min(jnp.where(x == m, row, N_EXPERTS), axis=0, keepdims=True)
        hit = row == ik
        x = jnp.where(hit, -jnp.inf, x)
        vals.append(m)
        idxs.append(ik)
        hits.append(hit)
    es = [jnp.exp(v - vals[0]) for v in vals]
    tot = es[0] + es[1] + es[2] + es[3]
    gate_ref[...] = jnp.concatenate([e / tot for e in es], axis=0)
    idx_ref[...] = jnp.concatenate(idxs, axis=0)
    chosen = (hits[0] | hits[1] | hits[2] | hits[3])
    onehot = jnp.where(chosen, 1.0, 0.0)
    earlier = (lax.broadcasted_iota(I32, (t, t), 0) < lax.broadcasted_iota(I32, (t, t), 1)).astype(BF16)
    run = run_ref[:, 0:1]
    before = jnp.dot(onehot.astype(BF16), earlier, preferred_element_type=F32) + run
    rank_ref[...] = jnp.concatenate(
        [jnp.sum(jnp.where(h, before, 0.0), axis=0, keepdims=True) for h in hits], axis=0).astype(I32)
    run = run + jnp.sum(onehot, axis=1, keepdims=True)
    run_ref[...] = jnp.broadcast_to(run, run_ref.shape)
    cnt_ref[...] = jnp.broadcast_to(run, cnt_ref.shape)


def _route(lg_t):
    e, n = lg_t.shape
    t = ROUTE_TILE
    tok = pl.BlockSpec((TOP_K, t), lambda i: (0, i))
    idx, gate, rank, cnt = pl.pallas_call(
        _route_kernel,
        grid=(n // t,),
        in_specs=[pl.BlockSpec((e, t), lambda i: (0, i))],
        out_specs=[tok, tok, tok, pl.BlockSpec((e, LANES), lambda i: (0, 0))],
        out_shape=[jax.ShapeDtypeStruct((TOP_K, n), I32), jax.ShapeDtypeStruct((TOP_K, n), F32),
                   jax.ShapeDtypeStruct((TOP_K, n), I32), jax.ShapeDtypeStruct((e, LANES), F32)],
        scratch_shapes=[pltpu.VMEM((e, LANES), F32)],
        compiler_params=_cparams("arbitrary"),
        name="moe_route",
    )(lg_t)
    counts = cnt[:, 0].astype(I32)
    padded = (counts + MOE_ROWS - 1) // MOE_ROWS * MOE_ROWS
    pad_end = jnp.cumsum(padded)
    pad_start = pad_end - padded
    m = n * TOP_K
    nb = (m + N_EXPERTS * (MOE_ROWS - 1) + MOE_ROWS - 1) // MOE_ROWS
    block_start = jnp.arange(nb, dtype=I32) * MOE_ROWS
    block_expert = jnp.minimum(jnp.sum((pad_end[None, :] <= block_start[:, None]).astype(I32), axis=1), N_EXPERTS - 1)
    experts = jnp.arange(N_EXPERTS, dtype=I32)
    owns = jnp.any(block_expert[:, None] == experts[None, :], axis=0)
    later = owns[None, :] & (experts[None, :] > block_expert[:, None])
    nxt = jnp.min(jnp.where(later, experts[None, :], N_EXPERTS), axis=1)
    nxt = jnp.where(nxt == N_EXPERTS, -1, nxt)
    run = jnp.sum((owns[None, :] & (experts[None, :] < block_expert[:, None])).astype(I32), axis=1)
    block_info = jnp.concatenate([block_expert, nxt, run % 2, pad_end[-1:] // MOE_ROWS]).astype(I32)
    start_of = jnp.sum(jnp.where(idx[:, :, None] == jnp.arange(N_EXPERTS, dtype=I32), pad_start, 0), axis=-1)
    dest = start_of + rank
    return gate, dest, block_info, nb * MOE_ROWS


def _windows(dest, w, multiple, fill):
    kk, n = dest.shape
    nwin = n // w
    per = -(-nwin // SC_WORKERS)
    per = -(-per // multiple) * multiple
    idx = dest.reshape(kk, nwin, w).transpose(1, 0, 2)
    pad = jnp.broadcast_to(fill[None, None, :], (SC_WORKERS * per - nwin, kk, w)).astype(I32)
    return jnp.concatenate([idx, pad], axis=0)


def _sc_scatter_rows(x, idx, p_rows):
    n, c = x.shape
    nwin, kk, w = idx.shape
    per = nwin // SC_WORKERS
    nreal = n // w
    mesh = plsc.VectorSubcoreMesh(core_axis_name="c", subcore_axis_name="s")

    @functools.partial(
        pl.kernel, mesh=mesh,
        out_type=jax.ShapeDtypeStruct((p_rows, c), x.dtype),
        scratch_types=[pltpu.VMEM((per, kk, w), I32), pltpu.VMEM((2, w, c), x.dtype),
                       pltpu.SemaphoreType.DMA((2,)), pltpu.SemaphoreType.DMA((2,))],
        name="sc_scatter_rows",
    )
    def k(x_hbm, idx_hbm, out_hbm, idx_v, rows_v, lsem, ssem):
        wid = lax.axis_index("s") * 2 + lax.axis_index("c")
        g0 = wid * per
        pltpu.sync_copy(idx_hbm.at[pl.ds(g0, per)], idx_v)

        def load(j, b):
            g = jnp.minimum(g0 + j, nreal - 1)
            return pltpu.make_async_copy(x_hbm.at[pl.ds(g * w, w)], rows_v.at[b], lsem.at[b])

        def scat(j, b, q):
            return pltpu.make_async_copy(rows_v.at[b], out_hbm.at[idx_v.at[j, q]], ssem.at[b])

        @pl.loop(0, per + 2, step=2)
        def _(j0):
            for b in range(2):
                j = j0 + b

                @pl.when(jnp.logical_and(j >= 2, j < per + 2))
                def _():
                    for q in range(kk):
                        scat(j - 2, b, q).wait()

                @pl.when(j < per)
                def _():
                    load(j, b).start()

                @pl.when(jnp.logical_and(j >= 1, j < per + 1))
                def _():
                    load(j - 1, 1 - b).wait()
                    for q in range(kk):
                        scat(j - 1, 1 - b, q).start()

    return k(x, idx)


def _sc_gather_rows(y, idx):
    p, c = y.shape
    nwin, kk, w = idx.shape
    nb = SC_GATHER_BUFS
    per = nwin // SC_WORKERS
    group = math.lcm(nb, kk)
    wins = group // kk
    n_items = per * kk
    mesh = plsc.VectorSubcoreMesh(core_axis_name="c", subcore_axis_name="s")

    @functools.partial(
        pl.kernel, mesh=mesh,
        out_type=jax.ShapeDtypeStruct((kk, nwin * w, c), y.dtype),
        scratch_types=[pltpu.VMEM((2, wins, kk, w), I32), pltpu.VMEM((nb, w, c), y.dtype),
                       pltpu.SemaphoreType.DMA((nb,)), pltpu.SemaphoreType.DMA((nb,))],
        name="sc_gather_rows",
    )
    def k(y_hbm, idx_hbm, out_hbm, idx_v, rows_v, gsem, wsem):
        wid = lax.axis_index("s") * 2 + lax.axis_index("c")
        g0 = wid * per

        def gath(slot, jj, q, b):
            return pltpu.make_async_copy(y_hbm.at[idx_v.at[slot, jj, q]], rows_v.at[b], gsem.at[b])

        def wr(i, b):
            return pltpu.make_async_copy(rows_v.at[b], out_hbm.at[i % kk, pl.ds((g0 + i // kk) * w, w)], wsem.at[b])

        n_groups = -(-(n_items + nb) // group)

        @pl.loop(0, n_groups)
        def _(gi):
            slot = gi % 2

            @pl.when(gi * wins < per)
            def _():
                pltpu.sync_copy(idx_hbm.at[pl.ds(g0 + gi * wins, wins)], idx_v.at[slot])

            for l in range(group):
                i = gi * group + l
                b = l % nb
                bw = (l - (nb - 1)) % nb

                @pl.when(jnp.logical_and(i >= nb, i < n_items + nb))
                def _():
                    wr(i - nb, b).wait()

                @pl.when(i < n_items)
                def _():
                    gath(slot, l // kk, l % kk, b).start()

                @pl.when(jnp.logical_and(i >= nb - 1, i < n_items + nb - 1))
                def _():
                    gath(slot, 0, 0, bw).wait()
                    wr(i - (nb - 1), bw).start()

    return k(y, idx)


def _moe_kernel(layer, info_ref, x_ref, w1_hbm, b1_ref, w2_hbm, b2_ref, o_ref,
                w1f_ref, w2f_ref, w1b_ref, w2b_ref, sem):
    i = pl.program_id(0)
    nb = pl.num_programs(0)
    expert = info_ref[i]
    nxt = info_ref[nb + i]
    slot = info_ref[2 * nb + i]
    first = jnp.logical_or(i == 0, expert != info_ref[jnp.maximum(i - 1, 0)])

    def fetch(e, s):
        return (pltpu.make_async_copy(w1_hbm.at[layer, e], w1f_ref.at[s], sem.at[0, s]),
                pltpu.make_async_copy(w2_hbm.at[layer, e], w2f_ref.at[s], sem.at[1, s]))

    @pl.when(i == 0)
    def _():
        for cp in fetch(expert, slot):
            cp.start()

    @pl.when(first)
    def _():
        for cp in fetch(expert, slot):
            cp.wait()

        @pl.when(nxt >= 0)
        def _():
            for cp in fetch(nxt, 1 - slot):
                cp.start()

        w1b_ref[...] = w1f_ref[slot].astype(BF16)
        w2b_ref[...] = w2f_ref[slot].astype(BF16)

    @pl.when(i < info_ref[3 * nb])
    def _():
        _expert_block(x_ref, w1b_ref, b1_ref, w2b_ref, b2_ref, o_ref)


def _expert_block(x_ref, w1b_ref, b1_ref, w2b_ref, b2_ref, o_ref):
    lo, hi = _unpack_pairs(x_ref[...])
    x = jnp.concatenate([lo.astype(BF16), hi.astype(BF16)], axis=-1)
    f = w2b_ref.shape[0]
    fc = 512
    acc = None
    for c in range(f // fc):
        gt = jnp.dot(x, w1b_ref[:, c * fc:(c + 1) * fc], preferred_element_type=F32) + b1_ref[0, 0, :, c * fc:(c + 1) * fc]
        up = jnp.dot(x, w1b_ref[:, f + c * fc:f + (c + 1) * fc], preferred_element_type=F32) + b1_ref[0, 0, :, f + c * fc:f + (c + 1) * fc]
        gt = jnp.minimum(gt, SWIGLU_LIMIT)
        up = jnp.clip(up, -SWIGLU_LIMIT, SWIGLU_LIMIT)
        act = (gt * jax.nn.sigmoid(SWIGLU_ALPHA * gt) * (up + 1.0)).astype(BF16)
        part = jnp.dot(act, w2b_ref[c * fc:(c + 1) * fc, :], preferred_element_type=F32)
        acc = part if acc is None else acc + part
    o_ref[...] = _pack_pairs(acc + b2_ref[0, 0])


def _moe_experts(layer, block_expert, xs, p_rows, w1, b1, w2, b2):
    _, e, d, f2 = w1.shape
    f = f2 // 2
    nb = p_rows // MOE_ROWS
    grid_spec = pltpu.PrefetchScalarGridSpec(
        num_scalar_prefetch=1,
        grid=(nb,),
        in_specs=[pl.BlockSpec((MOE_ROWS, d // 2), lambda i, be: (i, 0)),
                  pl.BlockSpec(memory_space=pl.ANY),
                  pl.BlockSpec((1, 1, 1, f2), lambda i, be: (layer, be[i], 0, 0)),
                  pl.BlockSpec(memory_space=pl.ANY),
                  pl.BlockSpec((1, 1, 1, d), lambda i, be: (layer, be[i], 0, 0))],
        out_specs=pl.BlockSpec((MOE_ROWS, d // 2), lambda i, be: (i, 0)),
        scratch_shapes=[pltpu.VMEM((2, d, f2), F32), pltpu.VMEM((2, f, d), F32),
                        pltpu.VMEM((d, f2), BF16), pltpu.VMEM((f, d), BF16),
                        pltpu.SemaphoreType.DMA((2, 2))],
    )
    return pl.pallas_call(
        functools.partial(_moe_kernel, layer),
        grid_spec=grid_spec,
        out_shape=jax.ShapeDtypeStruct((p_rows, d // 2), U32),
        compiler_params=_cparams("arbitrary"),
        name="moe_experts",
    )(block_expert, xs, w1, b1.reshape(b1.shape[0], e, 1, f2), w2, b2.reshape(b2.shape[0], e, 1, d))


def _moe_dispatch(h2_words, lg_t):
    gate, dest, block_info, p_rows = _route(lg_t)
    spare = p_rows + jnp.arange(SC_SCATTER_WIN, dtype=I32)
    xs = _sc_scatter_rows(h2_words, _windows(dest, SC_SCATTER_WIN, 2, spare), p_rows + SC_SCATTER_WIN)
    return xs, gate.T, dest, block_info, p_rows


def _moe_compute(layer, dispatched, w1, b1, w2, b2):
    xs, gate, dest, block_info, p_rows = dispatched
    ys = _moe_experts(layer, block_info, xs, p_rows, w1, b1, w2, b2)
    group_wins = math.lcm(SC_GATHER_BUFS, TOP_K) // TOP_K
    yg = _sc_gather_rows(ys, _windows(dest, SC_GATHER_WIN, group_wins, jnp.zeros((SC_GATHER_WIN,), I32)))
    return yg, gate


def _final_kernel(x_ref, yg_ref, gate_ref, g2_ref, nw_ref, *rest):
    o_ref = rest[-1]
    x = x_ref[0] + g2_ref[0] * _combine(yg_ref, gate_ref)
    ms = jnp.mean(x * x, axis=-1, keepdims=True)
    o_ref[0] = x * lax.rsqrt(ms + EPS) * nw_ref[...]


def _final(x, res, nw, ts, out_buf, ob0, b_total):
    b, s, d = x.shape
    nj = s // ts
    yg, gate, g2, first = res
    blk0 = first // ts
    row = pl.BlockSpec((1, ts, d), lambda i, j: (i, j, 0))
    in_specs = [row, pl.BlockSpec((TOP_K, ts, d // 2), lambda i, j: (0, blk0 + i * nj + j, 0)),
                pl.BlockSpec((1, ts, TOP_K), lambda i, j: (i, j, 0)),
                pl.BlockSpec((1, 1, d), lambda i, j: (i, 0, 0)),
                pl.BlockSpec((1, d), lambda i, j: (0, 0))]
    args = [x, yg, gate, g2, nw.reshape(1, d)]
    aliases = {}
    if out_buf is not None:
        in_specs.append(pl.BlockSpec(memory_space=pl.ANY))
        args.append(out_buf)
        aliases = {len(args) - 1: 0}
    return pl.pallas_call(
        _final_kernel,
        grid=(b, nj),
        in_specs=in_specs,
        out_specs=pl.BlockSpec((1, ts, d), lambda i, j: (i + ob0, j, 0)),
        out_shape=jax.ShapeDtypeStruct((b_total, s, d), F32),
        input_output_aliases=aliases,
        compiler_params=_cparams("parallel", "parallel"),
        name="final_norm",
    )(*args)


def kernel(x_prompt, x_sample, c_prompt, c_sample, cache_k, cache_v, state_hgrn, norm_mix, norm_ffn, norm_final, w_ada, b_ada, w_in, rel_bias, hgrn_lb_logits, g_attn_out, g_hgrn_out, w_out, w_router, b_router, w_e1, b_e1, w_e2, b_e2):
    depth = w_in.shape[0]
    bp, sp, d = x_prompt.shape
    bs, ls, _ = x_sample.shape
    wc = cache_k.shape[2]
    n_p, n_s = bp * sp, bs * ls

    lbs = jax.nn.softmax(hgrn_lb_logits.astype(F32), axis=0)
    lbs = jnp.cumsum(lbs, axis=0) - lbs[0]

    mods = _modulation(jnp.concatenate([c_prompt, c_sample], axis=0), w_ada, b_ada)
    lk_s = -(-(wc + ls) // LANES) * LANES
    tail = min(LEFT, sp)
    rec_chunk = min(REC_CHUNK, sp)

    n_groups = PROMPT_GROUPS if bp % PROMPT_GROUPS == 0 else 1
    gb = bp // n_groups
    n_g = gb * sp
    x_g = [(x_prompt, g * gb) for g in range(n_groups)]
    x_s = (x_sample, 0)
    res_g = [None] * n_groups
    res_s = None
    kp_l, vp_l, sp_l, ks_l, vs_l, ss_l = [], [], [], [], [], []
    for layer in range(depth):
        m6 = mods[layer].reshape(bp + bs, 1, N_MOD, d)
        mod_s = [m6[bp:, :, n, :] for n in range(N_MOD)]
        w_in_bf = w_in[layer].astype(BF16)
        w_out_bf = w_out[layer].astype(BF16)
        wr_hi, wr_mid, _ = _split3(w_router[layer].T)
        wr2 = jnp.concatenate([wr_hi, wr_mid], axis=0)
        table = rel_bias[layer]
        bias_p = _prompt_bias(table)

        def mixer_prompt(g):
            mod = [m6[g * gb:(g + 1) * gb, :, n, :] for n in range(N_MOD)]
            x_new, h2, lg, kt, vt, s_fin = _mixer_prompt(
                x_g[g][0], x_g[g][1], res_g[g], norm_mix[layer], mod[1], mod[0], w_in_bf, lbs[layer], bias_p,
                g_attn_out[layer], g_hgrn_out[layer], w_out_bf, mod[2], norm_ffn[layer], mod[4], mod[3],
                wr2, b_router[layer], tail, rec_chunk)
            x_g[g] = (x_new, 0)
            return h2.reshape(n_g, d // 2), lg, kt, vt, s_fin, mod[5]

        def mixer_sample():
            x, qa, ka, va, kt, vt, qh, kh, lf, vh, gh = _inproj(
                x_s[0], x_s[1], res_s, norm_mix[layer], mod_s[1], mod_s[0], w_in_bf, lbs[layer], ls, ls)
            zpad = jnp.zeros((bs, lk_s - wc - ls, ATT_W), BF16)
            keys = jnp.concatenate([cache_k[layer].reshape(bs, wc, ATT_W).astype(BF16), ka, zpad], axis=1)
            vals = jnp.concatenate([cache_v[layer].reshape(bs, wc, ATT_W).astype(BF16), va, zpad], axis=1)
            o_att = _attn_sample(qa, keys, vals, _sample_bias(table, ls, wc, lk_s))
            s0 = jnp.swapaxes(state_hgrn[layer].astype(F32), -1, -2)
            o_rec, s_new = _hgrn(qh, kh, lf, vh, s0, ls, ls)
            x_new, h2, lg = _merge(o_att, o_rec, gh, x[0], x[1], g_attn_out[layer], g_hgrn_out[layer], w_out_bf,
                                   mod_s[2], norm_ffn[layer], mod_s[4], mod_s[3], wr2, b_router[layer], ls)
            return (x_new, 0), h2.reshape(n_s, d // 2), jnp.swapaxes(lg, 0, 1).reshape(N_EXPERTS, n_s), kt, vt, s_new

        x_s, h2_s, lg_s, ka_s, va_s, s_new = mixer_sample()
        ks_l.append(ka_s.reshape(bs, ls, ATT_HEADS, ATT_DIM))
        vs_l.append(va_s.reshape(bs, ls, ATT_HEADS, ATT_DIM))
        ss_l.append(jnp.swapaxes(s_new, -1, -2))

        dispatched, g2s, k_t, v_t, s_t = [], [], [], [], []
        for g in range(n_groups):
            h2, lg, kt, vt, s_fin, g2 = mixer_prompt(g)
            if g == n_groups - 1:
                h2 = jnp.concatenate([h2, h2_s], axis=0)
                lg = jnp.concatenate([lg, lg_s], axis=1)
            dispatched.append(_moe_dispatch(h2, lg))
            g2s.append(g2)
            k_t.append(kt)
            v_t.append(vt)
            s_t.append(s_fin)
        kp_l.append(jnp.concatenate(k_t, axis=0).reshape(bp, tail, ATT_HEADS, ATT_DIM))
        vp_l.append(jnp.concatenate(v_t, axis=0).reshape(bp, tail, ATT_HEADS, ATT_DIM))
        sp_l.append(jnp.swapaxes(jnp.concatenate(s_t, axis=0), -1, -2))

        for g in range(n_groups):
            yg, gate = _moe_compute(layer, dispatched[g], w_e1, b_e1, w_e2, b_e2)
            res_g[g] = (yg, gate[:n_g].reshape(gb, sp, TOP_K), g2s[g], 0)
            if g == n_groups - 1:
                res_s = (yg, gate[n_g:].reshape(bs, ls, TOP_K), mod_s[5], n_g)

    y_prompt = None
    for g in range(n_groups):
        y_prompt = _final(x_g[g][0], res_g[g], norm_final, SEQ_TILE, y_prompt, g * gb, bp)
    y_sample = _final(x_s[0], res_s, norm_final, ls, None, 0, bs)
    return (y_prompt, y_sample, jnp.stack(kp_l), jnp.stack(vp_l), jnp.stack(sp_l),
            jnp.stack(ks_l), jnp.stack(vs_l), jnp.stack(ss_l))
```

```python
import functools

import jax
import jax.numpy as jnp
import numpy as np
from jax import lax
from jax.experimental import pallas as pl
from jax.experimental.pallas import tpu as pltpu
from jax.experimental.pallas import tpu_sc as plsc

F32 = jnp.float32
BF16 = jnp.bfloat16
I32 = jnp.int32
U32 = jnp.uint32

EPS = 1e-6
CHUNK = 64
LEFT_CHUNKS = 8
LEFT = LEFT_CHUNKS * CHUNK
MAX_REL = 2 * CHUNK
ATT_HEADS = 8
ATT_DIM = 64
ATT_W = ATT_HEADS * ATT_DIM
REC_HEADS = 4
REC_DIM = 128
REC_W = REC_HEADS * REC_DIM
N_EXPERTS = 32
TOP_K = 4
SWIGLU_ALPHA = 1.702
SWIGLU_LIMIT = 7.0
N_MOD = 6
NEG = -1e30
LOG2E = 1.4426950408889634

LANES = 128
ATT_QSUB = 128
ATT_WIN = LEFT + ATT_QSUB
SEQ_TILE = 512
REC_CHUNK = 128
MOE_ROWS = 512
ROUTE_TILE = 256
PROMPT_GROUPS = 2
GATE_ROWS = 8
VMEM_LIMIT = 56 * 1024 * 1024
SC_WORKERS = 32
SC_SCATTER_WIN = 64
SC_GATHER_WIN = 32
SC_GATHER_BUFS = 4


def _cparams(*sem):
    return pltpu.CompilerParams(dimension_semantics=sem, vmem_limit_bytes=VMEM_LIMIT)


def _split3(x):
    hi = x.astype(BF16)
    r1 = x - hi.astype(F32)
    mid = r1.astype(BF16)
    lo = (r1 - mid.astype(F32)).astype(BF16)
    return hi, mid, lo


def _bits(x):
    return lax.bitcast_convert_type(x, U32)


def _pack_pairs(x):
    r = _bits(x.astype(BF16).astype(F32))
    half = x.shape[1] // 2
    return jnp.bitwise_or(jnp.bitwise_and(r[:, half:], jnp.uint32(0xFFFF0000)),
                          jnp.right_shift(r[:, :half], jnp.uint32(16)))


def _unpack_pairs(word):
    lo = lax.bitcast_convert_type(jnp.left_shift(word, jnp.uint32(16)), F32)
    hi = lax.bitcast_convert_type(jnp.bitwise_and(word, jnp.uint32(0xFFFF0000)), F32)
    return lo, hi


def _mod_kernel(c_ref, w_ref, b_ref, o_ref):
    c = c_ref[...]
    act = (c * jax.nn.sigmoid(c)).astype(BF16)
    o_ref[0] = jnp.dot(act, w_ref[0].astype(BF16), preferred_element_type=F32) + b_ref[0]


def _modulation(c_all, w_ada, b_ada):
    depth, d, n6 = w_ada.shape
    rows = c_all.shape[0]
    tn = 1536
    return pl.pallas_call(
        _mod_kernel,
        grid=(depth, n6 // tn),
        in_specs=[pl.BlockSpec((rows, d), lambda l, j: (0, 0)),
                  pl.BlockSpec((1, d, tn), lambda l, j: (l, 0, j)),
                  pl.BlockSpec((1, 1, tn), lambda l, j: (l, 0, j))],
        out_specs=pl.BlockSpec((1, rows, tn), lambda l, j: (l, 0, j)),
        out_shape=jax.ShapeDtypeStruct((depth, rows, n6), F32),
        compiler_params=_cparams("parallel", "parallel"),
        name="adaln_modulation",
    )(c_all, w_ada, b_ada.reshape(depth, 1, n6))


def _combine(yg_ref, gate_ref):
    g = gate_ref[0] if len(gate_ref.shape) == 3 else gate_ref[...].T
    lo = hi = None
    for k in range(TOP_K):
        l, h = _unpack_pairs(yg_ref[k])
        gk = g[:, k:k + 1]
        lo = l * gk if lo is None else lo + l * gk
        hi = h * gk if hi is None else hi + h * gk
    return jnp.concatenate([lo, hi], axis=-1)


def _inproj_kernel(has_res, *refs):
    if has_res:
        (x_ref, yg_ref, gate_ref, g2_ref, nw_ref, sc_ref, sh_ref, w_ref, lb_ref,
         xo_ref, qa_ref, ka_ref, va_ref, kt_ref, vt_ref, qh_ref, kh_ref, lf_ref, vh_ref, gh_ref) = refs
        x = x_ref[0] + g2_ref[0] * _combine(yg_ref, gate_ref)
        xo_ref[0] = x
    else:
        (x_ref, nw_ref, sc_ref, sh_ref, w_ref, lb_ref,
         qa_ref, ka_ref, va_ref, kt_ref, vt_ref, qh_ref, kh_ref, lf_ref, vh_ref, gh_ref) = refs
        x = x_ref[0]
    _project_rows(x, nw_ref, sc_ref, sh_ref, w_ref, lb_ref, qa_ref.at[0], ka_ref.at[0], va_ref.at[0],
                  kt_ref.at[0], vt_ref.at[0], qh_ref.at[0], kh_ref.at[0], lf_ref.at[0], vh_ref.at[0], gh_ref.at[0])


def _project_rows(x, nw_ref, sc_ref, sh_ref, w_ref, lb_ref,
                  qa_ref, ka_ref, va_ref, kt_ref, vt_ref, qh_ref, kh_ref, lf_ref, vh_ref, gh_ref):
    ms = jnp.mean(x * x, axis=-1, keepdims=True)
    h = (x * lax.rsqrt(ms + EPS) * nw_ref[...]) * (1.0 + sc_ref[0]) + sh_ref[0]
    hb = h.astype(BF16)

    def proj(g):
        return jnp.dot(hb, w_ref[:, g * ATT_W:(g + 1) * ATT_W], preferred_element_type=F32)

    qa_ref[...] = (proj(0) * (ATT_DIM ** -0.5)).astype(BF16)
    k = proj(1)
    ka_ref[...] = k.astype(BF16)
    kt_ref[...] = k
    v = proj(2)
    va_ref[...] = v.astype(BF16)
    vt_ref[...] = v
    qh = proj(3)
    qh_ref[...] = qh * jax.nn.sigmoid(qh)
    f = proj(4)
    lb = lb_ref[...]
    lf_ref[...] = jnp.log(lb + (1.0 - lb) * jax.nn.sigmoid(f))
    kh_ref[...] = (1.0 - lb) * jax.nn.sigmoid(-f)
    vh_ref[...] = proj(5).astype(BF16)
    gh_ref[...] = proj(6)


def _inproj(x, xb0, res, nw, sc, sh, w_bf, lb, ts, tail):
    _, s, d = x.shape
    b = sc.shape[0]
    pw = w_bf.shape[1]
    nj = s // ts
    n_tail = tail // ts
    has_res = res is not None
    tail_spec = pl.BlockSpec((1, ts, ATT_W), lambda i, j: (i, jnp.maximum(j - (nj - n_tail), 0), 0))
    ht = jax.ShapeDtypeStruct((b, tail, ATT_W), F32)
    row = pl.BlockSpec((1, ts, d), lambda i, j: (i, j, 0))
    per_b = pl.BlockSpec((1, 1, d), lambda i, j: (i, 0, 0))
    half = pl.BlockSpec((1, ts, ATT_W), lambda i, j: (i, j, 0))
    in_specs, args = [pl.BlockSpec((1, ts, d), lambda i, j: (i + xb0, j, 0))], [x]
    if has_res:
        yg, gate, g2, first = res
        blk0 = first // ts
        in_specs += [pl.BlockSpec((TOP_K, ts, d // 2), lambda i, j: (0, blk0 + i * nj + j, 0)),
                     pl.BlockSpec((1, ts, TOP_K), lambda i, j: (i, j, 0)), per_b]
        args += [yg, gate, g2]
    in_specs += [pl.BlockSpec((1, d), lambda i, j: (0, 0)), per_b, per_b,
                 pl.BlockSpec((d, pw), lambda i, j: (0, 0)),
                 pl.BlockSpec((1, REC_W), lambda i, j: (0, 0))]
    args += [nw.reshape(1, d), sc, sh, w_bf, lb.reshape(1, REC_W)]
    hs = jax.ShapeDtypeStruct((b, s, ATT_W), F32)
    hb = jax.ShapeDtypeStruct((b, s, ATT_W), BF16)
    out_shape = ([jax.ShapeDtypeStruct((b, s, d), F32)] if has_res else []) + [hb, hb, hb, ht, ht, hs, hs, hs, hb, hs]
    out_specs = ([row] if has_res else []) + [half] * 3 + [tail_spec] * 2 + [half] * 5
    outs = pl.pallas_call(
        functools.partial(_inproj_kernel, has_res),
        grid=(b, nj),
        in_specs=in_specs, out_specs=out_specs, out_shape=out_shape,
        compiler_params=_cparams("parallel", "arbitrary"),
        name="norm_inproj",
    )(*args)
    if has_res:
        return [(outs[0], 0)] + list(outs[1:])
    return [(x, xb0)] + list(outs)


def _attend(q, kw, vw, bias_ref, s_ref, col_thr):
    lq, lk = q.shape[0], kw.shape[0]
    lane = lax.broadcasted_iota(I32, (lq, LANES), 1)
    lo = lane < ATT_DIM
    nt = (((1,), (1,)), ((), ()))
    if col_thr is not None:
        keep = lax.broadcasted_iota(I32, (2 * lq, lk), 1) >= col_thr
    ms = []
    for p in range(ATT_HEADS // 2):
        sl = slice(p * LANES, (p + 1) * LANES)
        qp = q[:, sl]
        z = jnp.zeros_like(qp)
        lhs = jnp.concatenate([jnp.where(lo, qp, z), jnp.where(lo, z, qp)], axis=0)
        s = lax.dot_general(lhs, kw[:, sl], nt, preferred_element_type=F32) + bias_ref[p]
        if col_thr is not None:
            s = jnp.where(keep, s, NEG)
        ms.append(jnp.max(s, axis=-1, keepdims=True))
        s_ref[p] = s
    outs = []
    for p in range(ATT_HEADS // 2):
        sl = slice(p * LANES, (p + 1) * LANES)
        e = jnp.exp(s_ref[p] - ms[p])
        l = jnp.sum(e, axis=-1, keepdims=True)
        o = jnp.dot(e.astype(BF16), vw[:, sl], preferred_element_type=F32) / l
        outs.append(jnp.where(lo, o[:lq], o[lq:]))
    return jnp.concatenate(outs, axis=-1)


def _band_attention(q_ref, kw_ref, vw_ref, bias_ref, s_ref, o_ref, first_tile):
    ts = q_ref.shape[0]

    def body(masked, r, carry):
        r0 = pl.multiple_of(r * ATT_QSUB, ATT_QSUB)
        q = q_ref[pl.ds(r0, ATT_QSUB), :]
        kw = kw_ref[pl.ds(r0, ATT_WIN), :]
        vw = vw_ref[pl.ds(r0, ATT_WIN), :]
        o_ref[pl.ds(r0, ATT_QSUB), :] = _attend(q, kw, vw, bias_ref, s_ref, (ts - r0) if masked else None)
        return carry

    @pl.when(first_tile)
    def _():
        lax.fori_loop(0, ts // ATT_QSUB, functools.partial(body, True), 0)

    @pl.when(jnp.logical_not(first_tile))
    def _():
        lax.fori_loop(0, ts // ATT_QSUB, functools.partial(body, False), 0)


def _attn_sample_kernel(q_ref, k_ref, v_ref, bias_ref, o_ref, s_ref):
    o_ref[0] = _attend(q_ref[0], k_ref[0], v_ref[0], bias_ref, s_ref, None)


def _attn_sample(qa, keys, vals, bias):
    b, l, w = qa.shape
    lk = keys.shape[1]
    return pl.pallas_call(
        _attn_sample_kernel,
        grid=(b,),
        in_specs=[pl.BlockSpec((1, l, w), lambda i: (i, 0, 0)),
                  pl.BlockSpec((1, lk, w), lambda i: (i, 0, 0)),
                  pl.BlockSpec((1, lk, w), lambda i: (i, 0, 0)),
                  pl.BlockSpec(bias.shape, lambda i: (0, 0, 0))],
        out_specs=pl.BlockSpec((1, l, w), lambda i: (i, 0, 0)),
        out_shape=jax.ShapeDtypeStruct((b, l, w), F32),
        scratch_shapes=[pltpu.VMEM((ATT_HEADS // 2, 2 * l, lk), F32)],
        compiler_params=_cparams("parallel"),
        name="cache_attention",
    )(qa, keys, vals, bias)


def _toeplitz_bias(table, lq, lk, offset, valid):
    m = lq + lk - 1
    k = np.arange(m)
    diff = np.where(k < lk, k, k - m)
    g = table[:, np.clip(offset - diff, -MAX_REL, MAX_REL) + MAX_REL]
    skew = jnp.tile(g, (1, lq))[:, :lq * (m - 1)].reshape(-1, lq, m - 1)[:, :, :lk]
    bias = jnp.where(jnp.asarray(valid)[None], skew, NEG).astype(F32)
    return bias.reshape(ATT_HEADS // 2, 2 * lq, lk)


def _prompt_bias(table):
    t = np.arange(ATT_QSUB)[:, None]
    j = np.arange(ATT_WIN)[None, :]
    start = (t // CHUNK) * CHUNK
    return _toeplitz_bias(table, ATT_QSUB, ATT_WIN, LEFT, (j >= start) & (j < start + LEFT + CHUNK))


def _sample_bias(table, l, w, lk):
    valid = np.broadcast_to(np.arange(lk)[None, :] < w + l, (l, lk))
    return _toeplitz_bias(table, l, lk, w, valid)


def _neg_abs(x):
    return lax.bitcast_convert_type(jnp.bitwise_or(_bits(x), jnp.uint32(0x80000000)), F32)


def _hgrn_kernel(chunk, q_ref, k_ref, lf_ref, v_ref, s0_ref, o_ref, sf_ref, st_ref):
    i = pl.program_id(1)

    @pl.when(i == 0)
    def _():
        st_ref[...] = s0_ref[0]

    _hgrn_rows(chunk, q_ref.at[0], k_ref.at[0], lf_ref.at[0], v_ref.at[0], o_ref.at[0], st_ref)

    @pl.when(i == pl.num_programs(1) - 1)
    def _():
        sf_ref[0] = st_ref[...]


def _hgrn_rows(chunk, q_ref, k_ref, lf_ref, v_ref, o_ref, st_ref):
    ts = q_ref.shape[0]
    n_chunks = ts // chunk

    row = lax.broadcasted_iota(I32, (chunk, chunk), 0)
    colm = lax.broadcasted_iota(I32, (chunk, chunk), 1)
    tri = (colm <= row).astype(BF16)
    xor = jnp.bitwise_xor(row, colm)
    lower = colm < row
    rid = lax.broadcasted_iota(I32, (chunk, REC_W), 0)

    levels = []
    b = chunk // 2
    while b >= 1:
        levels.append(b)
        b //= 2
    pair_mask = [xor == 0] + [lower & (xor >= b) & (xor < 2 * b) for b in levels]
    upper = [jnp.bitwise_and(rid, b) != 0 for b in levels]
    ph4 = jnp.bitwise_and(rid, 3)
    odd = jnp.bitwise_and(rid, 1) == 1
    nt = (((1,), (1,)), ((), ()))
    tn = (((0,), (0,)), ((), ()))
    sls = [slice(h * REC_DIM, (h + 1) * REC_DIM) for h in range(REC_HEADS)]

    def boundary(cum, b):
        if 2 * b >= 8:
            pieces = [jnp.broadcast_to(cum[m * 2 * b + b - 1:m * 2 * b + b, :], (2 * b, REC_W))
                      for m in range(chunk // (2 * b))]
            return pieces[0] if len(pieces) == 1 else jnp.concatenate(pieces, axis=0)
        if b == 2:
            up1 = pltpu.roll(cum, chunk - 1, 0)
            dn1 = pltpu.roll(cum, 1, 0)
            dn2 = pltpu.roll(cum, 2, 0)
            return jnp.where(ph4 == 0, up1, jnp.where(ph4 == 1, cum, jnp.where(ph4 == 2, dn1, dn2)))
        return jnp.where(odd, pltpu.roll(cum, 1, 0), cum)

    def one_chunk(c, carry):
        r0 = pl.multiple_of(c * chunk, chunk)
        rows = pl.ds(r0, chunk)
        q = q_ref[rows, :]
        k = k_ref[rows, :]
        v = v_ref[rows, :]
        lf = lf_ref[rows, :]
        cum = sum(jnp.dot(tri, part, preferred_element_type=F32) for part in _split3(lf)) * LOG2E
        last = cum[chunk - 1:chunk, :]
        q_in = (q * jnp.exp2(cum)).astype(BF16)
        k_out = (k * jnp.exp2(last - cum)).astype(BF16)
        decay = jnp.exp2(last)
        qb = q.astype(BF16)
        kb = k.astype(BF16)
        zs = []
        for n, b in enumerate(levels):
            w = jnp.exp2(_neg_abs(cum - boundary(cum, b)))
            zs.append((jnp.where(upper[n], q, k) * w).astype(BF16))
        sts = [st_ref[h] for h in range(REC_HEADS)]
        d0 = [lax.dot_general(qb[:, sl], kb[:, sl], nt, preferred_element_type=F32) for sl in sls]
        dl = [[lax.dot_general(z[:, sl], z[:, sl], nt, preferred_element_type=F32) for z in zs] for sl in sls]
        oi = [lax.dot_general(q_in[:, sl], st.astype(BF16), nt, preferred_element_type=F32)
              for sl, st in zip(sls, sts)]
        upd = [lax.dot_general(v[:, sl], k_out[:, sl], tn, preferred_element_type=F32) for sl in sls]
        outs = []
        for h, sl in enumerate(sls):
            a = jnp.where(pair_mask[0], d0[h], 0.0)
            for n in range(len(levels)):
                a = jnp.where(pair_mask[n + 1], dl[h][n], a)
            outs.append(jnp.dot(a.astype(BF16), v[:, sl], preferred_element_type=F32) + oi[h])
        o_ref[rows, :] = jnp.concatenate(outs, axis=-1)
        for h, sl in enumerate(sls):
            st_ref[h] = decay[:, sl] * sts[h] + upd[h]
        return carry

    lax.fori_loop(0, n_chunks, one_chunk, 0, unroll=2 if n_chunks % 2 == 0 else 1)


def _hgrn(q, k, lf, v, s0_t, ts, chunk):
    b, s, w = q.shape
    blk = pl.BlockSpec((1, ts, w), lambda i, j: (i, j, 0))
    st = pl.BlockSpec((1, REC_HEADS, REC_DIM, REC_DIM), lambda i, j: (i, 0, 0, 0))
    return pl.pallas_call(
        functools.partial(_hgrn_kernel, chunk),
        grid=(b, s // ts),
        in_specs=[blk, blk, blk, blk, st],
        out_specs=[blk, st],
        out_shape=[jax.ShapeDtypeStruct((b, s, w), F32),
                   jax.ShapeDtypeStruct((b, REC_HEADS, REC_DIM, REC_DIM), F32)],
        scratch_shapes=[pltpu.VMEM((REC_HEADS, REC_DIM, REC_DIM), F32)],
        compiler_params=_cparams("parallel", "arbitrary"),
        name="hgrn_recurrence",
    )(q, k, lf, v, s0_t)


def _head_rms(x, width):
    lane = lax.broadcasted_iota(I32, (x.shape[0], LANES), 1)
    lo = lane < ATT_DIM
    outs = []
    for p in range(x.shape[1] // LANES):
        xp = x[:, p * LANES:(p + 1) * LANES]
        sq = xp * xp
        tot = jnp.sum(sq, axis=-1, keepdims=True)
        if width == LANES:
            ms = tot * (1.0 / LANES)
        else:
            s_lo = jnp.sum(jnp.where(lo, sq, 0.0), axis=-1, keepdims=True)
            ms = jnp.where(lo, s_lo, tot - s_lo) * (1.0 / ATT_DIM)
        outs.append(xp * lax.rsqrt(ms + EPS))
    return jnp.concatenate(outs, axis=-1)


def _merge_rows(oa, orr, gh, x, ga_ref, gr_ref, wo_ref, g1_ref, nf_ref, sc_ref, sh_ref, wr_ref, br_ref):
    a = _head_rms(oa, ATT_DIM) * ga_ref[...]
    r = _head_rms(orr, REC_DIM) * gr_ref[...] * (gh * jax.nn.sigmoid(gh))
    mix = jnp.dot(a.astype(BF16), wo_ref[0:ATT_W, :], preferred_element_type=F32)
    mix += jnp.dot(r.astype(BF16), wo_ref[ATT_W:ATT_W + REC_W, :], preferred_element_type=F32)
    x = x + g1_ref[0] * mix
    ms = jnp.mean(x * x, axis=-1, keepdims=True)
    h2 = (x * lax.rsqrt(ms + EPS) * nf_ref[...]) * (1.0 + sc_ref[0]) + sh_ref[0]
    hh = h2.astype(BF16)
    words = _pack_pairs(h2)
    hm = (h2 - hh.astype(F32)).astype(BF16)
    nt = (((1,), (1,)), ((), ()))
    l1 = lax.dot_general(wr_ref[...], hh, nt, preferred_element_type=F32)
    l2 = lax.dot_general(wr_ref[0:N_EXPERTS, :], hm, nt, preferred_element_type=F32)
    lg = l1[0:N_EXPERTS] + l1[N_EXPERTS:2 * N_EXPERTS] + l2 + br_ref[...]
    return x, words, lg


def _merge_kernel(oa_ref, or_ref, gh_ref, x_ref, ga_ref, gr_ref, wo_ref, g1_ref, nf_ref, sc_ref, sh_ref,
                  wr_ref, br_ref, xo_ref, h2_ref, lg_ref):
    x, words, lg = _merge_rows(oa_ref[0], or_ref[0], gh_ref[0], x_ref[0], ga_ref, gr_ref, wo_ref, g1_ref,
                               nf_ref, sc_ref, sh_ref, wr_ref, br_ref)
    xo_ref[0] = x
    h2_ref[0] = words
    if len(lg_ref.shape) == 3:
        lg_ref[0] = lg
    else:
        lg_ref[...] = lg


def _merge(o_att, o_rec, gh, x, xb0, g_att, g_rec, wo_bf, g1, nf, sc, sh, wr2, br, ts):
    _, s, d = x.shape
    b = o_att.shape[0]
    nj = s // ts
    row = pl.BlockSpec((1, ts, d), lambda i, j: (i, j, 0))
    xrow = pl.BlockSpec((1, ts, d), lambda i, j: (i + xb0, j, 0))
    half = pl.BlockSpec((1, ts, ATT_W), lambda i, j: (i, j, 0))
    per_b = pl.BlockSpec((1, 1, d), lambda i, j: (i, 0, 0))
    const2 = lambda shp: pl.BlockSpec(shp, lambda i, j: (0,) * len(shp))
    if ts % LANES == 0:
        lg_spec = pl.BlockSpec((N_EXPERTS, ts), lambda i, j: (0, i * nj + j))
        lg_shape = jax.ShapeDtypeStruct((N_EXPERTS, b * s), F32)
    else:
        lg_spec = pl.BlockSpec((1, N_EXPERTS, ts), lambda i, j: (i, 0, j))
        lg_shape = jax.ShapeDtypeStruct((b, N_EXPERTS, s), F32)
    return pl.pallas_call(
        _merge_kernel,
        grid=(b, nj),
        in_specs=[half, half, half, xrow, const2((1, ATT_W)), const2((1, REC_W)), const2(wo_bf.shape),
                  per_b, const2((1, d)), per_b, per_b, const2(wr2.shape), const2((N_EXPERTS, 1))],
        out_specs=[row, pl.BlockSpec((1, ts, d // 2), lambda i, j: (i, j, 0)), lg_spec],
        out_shape=[jax.ShapeDtypeStruct((b, s, d), F32), jax.ShapeDtypeStruct((b, s, d // 2), U32), lg_shape],
        compiler_params=_cparams("parallel", "parallel"),
        name="merge_outproj_router",
    )(o_att, o_rec, gh, x, g_att.reshape(1, ATT_W), g_rec.reshape(1, REC_W), wo_bf, g1, nf.reshape(1, d),
      sc, sh, wr2, br.reshape(N_EXPERTS, 1))


def _mixer_kernel(has_res, chunk, *refs):
    n_in = 4 if has_res else 1
    x_ref = refs[0]
    (nw_ref, sc1_ref, sh1_ref, lb_ref, ga_ref, gr_ref, g1_ref, nf_ref, sc2_ref, sh2_ref, wr_ref, br_ref,
     win_hbm, wout_hbm, bias_hbm,
     xo_ref, h2_ref, lg_ref, kt_ref, vt_ref, sf_ref,
     win_v, wout_v, bias_v, q_s, kw_s, vw_s, s_s, qh_s, kh_s, lf_s, vh_s, gh_s, oa_s, or_s, st_s, sem) = refs[n_in:]
    j = pl.program_id(1)
    ts = x_ref.shape[1]

    @pl.when(j == 0)
    def _():
        copies = [pltpu.make_async_copy(src, dst, sem.at[n])
                  for n, (src, dst) in enumerate(((win_hbm, win_v), (wout_hbm, wout_v), (bias_hbm, bias_v)))]
        for cp in copies:
            cp.start()
        for cp in copies:
            cp.wait()
        st_s[...] = jnp.zeros_like(st_s)
        kw_s[0:ts, :] = jnp.zeros((ts, ATT_W), BF16)
        vw_s[0:ts, :] = jnp.zeros((ts, ATT_W), BF16)

    if has_res:
        yg_ref, gate_ref, g2_ref = refs[1:4]
        x = x_ref[0] + g2_ref[0] * _combine(yg_ref, gate_ref)
        xo_ref[0] = x
    else:
        x = x_ref[0]
    cur = pl.ds(ts, ts)
    _project_rows(x, nw_ref, sc1_ref, sh1_ref, win_v, lb_ref, q_s, kw_s.at[cur], vw_s.at[cur],
                  kt_ref.at[0], vt_ref.at[0], qh_s, kh_s, lf_s, vh_s, gh_s)
    _band_attention(q_s, kw_s, vw_s, bias_v, s_s, oa_s, j == 0)
    _hgrn_rows(chunk, qh_s, kh_s, lf_s, vh_s, or_s, st_s)
    x_in = xo_ref[0] if has_res else x_ref[0]
    x_new, words, lg = _merge_rows(oa_s[...], or_s[...], gh_s[...], x_in, ga_ref, gr_ref, wout_v, g1_ref,
                                   nf_ref, sc2_ref, sh2_ref, wr_ref, br_ref)
    xo_ref[0] = x_new
    h2_ref[0] = words
    lg_ref[...] = lg
    kw_s[0:ts, :] = kw_s[ts:2 * ts, :]
    vw_s[0:ts, :] = vw_s[ts:2 * ts, :]

    @pl.when(j == pl.num_programs(1) - 1)
    def _():
        sf_ref[0] = st_s[...]


def _mixer_prompt(x, xb0, res, nw, sc1, sh1, w_in_bf, lb, bias, g_att, g_rec, w_out_bf, g1, nf, sc2, sh2,
                  wr2, br, tail, chunk):
    _, s, d = x.shape
    b = sc1.shape[0]
    ts = LEFT
    nj = s // ts
    n_tail = tail // ts
    has_res = res is not None
    row = pl.BlockSpec((1, ts, d), lambda i, j: (i, j, 0))
    per_b = pl.BlockSpec((1, 1, d), lambda i, j: (i, 0, 0))
    const = lambda shp: pl.BlockSpec(shp, lambda i, j: (0,) * len(shp))
    hbm = pl.BlockSpec(memory_space=pl.ANY)
    in_specs, args = [pl.BlockSpec((1, ts, d), lambda i, j: (i + xb0, j, 0))], [x]
    if has_res:
        yg, gate, g2, first = res
        blk0 = first // ts
        in_specs += [pl.BlockSpec((TOP_K, ts, d // 2), lambda i, j: (0, blk0 + i * nj + j, 0)),
                     pl.BlockSpec((GATE_ROWS, ts), lambda i, j: (0, blk0 + i * nj + j)), per_b]
        args += [yg, gate, g2]
    in_specs += [const((1, d)), per_b, per_b, const((1, REC_W)), const((1, ATT_W)), const((1, REC_W)),
                 per_b, const((1, d)), per_b, per_b, const(wr2.shape), const((N_EXPERTS, 1)), hbm, hbm, hbm]
    args += [nw.reshape(1, d), sc1, sh1, lb.reshape(1, REC_W), g_att.reshape(1, ATT_W), g_rec.reshape(1, REC_W),
             g1, nf.reshape(1, d), sc2, sh2, wr2, br.reshape(N_EXPERTS, 1), w_in_bf, w_out_bf, bias]
    tail_spec = pl.BlockSpec((1, ts, ATT_W), lambda i, j: (i, jnp.maximum(j - (nj - n_tail), 0), 0))
    state_spec = pl.BlockSpec((1, REC_HEADS, REC_DIM, REC_DIM), lambda i, j: (i, 0, 0, 0))
    tile = lambda dt: pltpu.VMEM((ts, ATT_W), dt)
    return pl.pallas_call(
        functools.partial(_mixer_kernel, has_res, chunk),
        grid=(b, nj),
        in_specs=in_specs,
        out_specs=[row, pl.BlockSpec((1, ts, d // 2), lambda i, j: (i, j, 0)),
                   pl.BlockSpec((N_EXPERTS, ts), lambda i, j: (0, i * nj + j)), tail_spec, tail_spec, state_spec],
        out_shape=[jax.ShapeDtypeStruct((b, s, d), F32), jax.ShapeDtypeStruct((b, s, d // 2), U32),
                   jax.ShapeDtypeStruct((N_EXPERTS, b * s), F32),
                   jax.ShapeDtypeStruct((b, tail, ATT_W), F32), jax.ShapeDtypeStruct((b, tail, ATT_W), F32),
                   jax.ShapeDtypeStruct((b, REC_HEADS, REC_DIM, REC_DIM), F32)],
        scratch_shapes=[pltpu.VMEM(w_in_bf.shape, BF16), pltpu.VMEM(w_out_bf.shape, BF16), pltpu.VMEM(bias.shape, F32),
                        tile(BF16), pltpu.VMEM((2 * ts, ATT_W), BF16), pltpu.VMEM((2 * ts, ATT_W), BF16),
                        pltpu.VMEM((ATT_HEADS // 2, 2 * ATT_QSUB, ATT_WIN), F32),
                        tile(F32), tile(F32), tile(F32), tile(BF16), tile(F32), tile(F32), tile(F32),
                        pltpu.VMEM((REC_HEADS, REC_DIM, REC_DIM), F32), pltpu.SemaphoreType.DMA((3,))],
        compiler_params=_cparams("parallel", "arbitrary"),
        name="prompt_mixer",
    )(*args)


def _route_kernel(lg_ref, idx_ref, gate_ref, rank_ref, cnt_ref, run_ref):
    i = pl.program_id(0)
    t = lg_ref.shape[1]

    @pl.when(i == 0)
    def _():
        run_ref[...] = jnp.zeros_like(run_ref)

    x = lg_ref[...]
    row = lax.broadcasted_iota(I32, x.shape, 0)
    vals, idxs, hits = [], [], []
    for _ in range(TOP_K):
        m = jnp.max(x, axis=0, keepdims=True)
        ik = jnp.min(jnp.where(x == m, row, N_EXPERTS), axis=0, keepdims=True)
        hit = row == ik
        x = jnp.where(hit, -jnp.inf, x)
        vals.append(m)
        idxs.append(ik)
        hits.append(hit)
    es = [jnp.exp(v - vals[0]) for v in vals]
    tot = es[0] + es[1] + es[2] + es[3]
    gate_ref[...] = jnp.concatenate([e / tot for e in es] + [jnp.zeros((GATE_ROWS - TOP_K, t), F32)], axis=0)
    idx_ref[...] = jnp.concatenate(idxs, axis=0)
    chosen = (hits[0] | hits[1] | hits[2] | hits[3])
    onehot = jnp.where(chosen, 1.0, 0.0)
    earlier = (lax.broadcasted_iota(I32, (t, t), 0) < lax.broadcasted_iota(I32, (t, t), 1)).astype(BF16)
    run = run_ref[:, 0:1]
    before = jnp.dot(onehot.astype(BF16), earlier, preferred_element_type=F32) + run
    rank_ref[...] = jnp.concatenate(
        [jnp.sum(jnp.where(h, before, 0.0), axis=0, keepdims=True) for h in hits], axis=0).astype(I32)
    run = run + jnp.sum(onehot, axis=1, keepdims=True)
    run_ref[...] = jnp.broadcast_to(run, run_ref.shape)
    cnt_ref[...] = jnp.broadcast_to(run, cnt_ref.shape)


def _route(lg_t):
    e, n = lg_t.shape
    t = ROUTE_TILE
    tok = pl.BlockSpec((TOP_K, t), lambda i: (0, i))
    idx, gate, rank, cnt = pl.pallas_call(
        _route_kernel,
        grid=(n // t,),
        in_specs=[pl.BlockSpec((e, t), lambda i: (0, i))],
        out_specs=[tok, pl.BlockSpec((GATE_ROWS, t), lambda i: (0, i)), tok, pl.BlockSpec((e, LANES), lambda i: (0, 0))],
        out_shape=[jax.ShapeDtypeStruct((TOP_K, n), I32), jax.ShapeDtypeStruct((GATE_ROWS, n), F32),
                   jax.ShapeDtypeStruct((TOP_K, n), I32), jax.ShapeDtypeStruct((e, LANES), F32)],
        scratch_shapes=[pltpu.VMEM((e, LANES), F32)],
        compiler_params=_cparams("arbitrary"),
        name="moe_route",
    )(lg_t)
    counts = cnt[:, 0].astype(I32)
    padded = (counts + MOE_ROWS - 1) // MOE_ROWS * MOE_ROWS
    pad_end = jnp.cumsum(padded)
    pad_start = pad_end - padded
    m = n * TOP_K
    nb = (m + N_EXPERTS * (MOE_ROWS - 1) + MOE_ROWS - 1) // MOE_ROWS
    block_start = jnp.arange(nb, dtype=I32) * MOE_ROWS
    block_expert = jnp.minimum(jnp.sum((pad_end[None, :] <= block_start[:, None]).astype(I32), axis=1), N_EXPERTS - 1)
    experts = jnp.arange(N_EXPERTS, dtype=I32)
    owns = jnp.any(block_expert[:, None] == experts[None, :], axis=0)
    later = owns[None, :] & (experts[None, :] > block_expert[:, None])
    nxt = jnp.min(jnp.where(later, experts[None, :], N_EXPERTS), axis=1)
    nxt = jnp.where(nxt == N_EXPERTS, -1, nxt)
    run = jnp.sum((owns[None, :] & (experts[None, :] < block_expert[:, None])).astype(I32), axis=1)
    block_info = jnp.concatenate([block_expert, nxt, run % 2, pad_end[-1:] // MOE_ROWS]).astype(I32)
    start_of = sum(jnp.where(idx == e, pad_start[e], 0) for e in range(N_EXPERTS))
    dest = start_of + rank
    return gate, dest, block_info, nb * MOE_ROWS


def _windows(dest, w, multiple, fill):
    kk, n = dest.shape
    nwin = n // w
    per = -(-nwin // SC_WORKERS)
    per = -(-per // multiple) * multiple
    idx = dest.reshape(kk, nwin, w).transpose(1, 0, 2)
    pad = jnp.broadcast_to(fill[None, None, :], (SC_WORKERS * per - nwin, kk, w)).astype(I32)
    return jnp.concatenate([idx, pad], axis=0)


def _sc_scatter_rows(x, idx, p_rows):
    n, c = x.shape
    nwin, kk, w = idx.shape
    per = nwin // SC_WORKERS
    nreal = n // w
    mesh = plsc.VectorSubcoreMesh(core_axis_name="c", subcore_axis_name="s")

    @functools.partial(
        pl.kernel, mesh=mesh,
        out_type=jax.ShapeDtypeStruct((p_rows, c), x.dtype),
        scratch_types=[pltpu.VMEM((per, kk, w), I32), pltpu.VMEM((2, w, c), x.dtype),
                       pltpu.SemaphoreType.DMA((2,)), pltpu.SemaphoreType.DMA((2,))],
        name="sc_scatter_rows",
    )
    def k(x_hbm, idx_hbm, out_hbm, idx_v, rows_v, lsem, ssem):
        wid = lax.axis_index("s") * 2 + lax.axis_index("c")
        g0 = wid * per
        pltpu.sync_copy(idx_hbm.at[pl.ds(g0, per)], idx_v)

        def load(j, b):
            g = jnp.minimum(g0 + j, nreal - 1)
            return pltpu.make_async_copy(x_hbm.at[pl.ds(g * w, w)], rows_v.at[b], lsem.at[b])

        def scat(j, b, q):
            return pltpu.make_async_copy(rows_v.at[b], out_hbm.at[idx_v.at[j, q]], ssem.at[b])

        @pl.loop(0, per + 2, step=2)
        def _(j0):
            for b in range(2):
                j = j0 + b

                @pl.when(jnp.logical_and(j >= 2, j < per + 2))
                def _():
                    for q in range(kk):
                        scat(j - 2, b, q).wait()

                @pl.when(j < per)
                def _():
                    load(j, b).start()

                @pl.when(jnp.logical_and(j >= 1, j < per + 1))
                def _():
                    load(j - 1, 1 - b).wait()
                    for q in range(kk):
                        scat(j - 1, 1 - b, q).start()

    return k(x, idx)


def _sc_gather_rows(y, idx):
    p, c = y.shape
    kk, n_pad = idx.shape
    w = SC_GATHER_WIN
    nb = SC_GATHER_BUFS
    per = n_pad // (SC_WORKERS * w)
    wins = LANES // w
    group = wins * kk
    assert group % nb == 0 and per % wins == 0 and n_pad == SC_WORKERS * per * w
    n_items = per * kk
    mesh = plsc.VectorSubcoreMesh(core_axis_name="c", subcore_axis_name="s")

    @functools.partial(
        pl.kernel, mesh=mesh,
        out_type=jax.ShapeDtypeStruct((kk, n_pad, c), y.dtype),
        scratch_types=[pltpu.VMEM((2, kk, LANES), I32), pltpu.VMEM((nb, w, c), y.dtype),
                       pltpu.SemaphoreType.DMA((nb,)), pltpu.SemaphoreType.DMA((nb,))],
        name="sc_gather_rows",
    )
    def k(y_hbm, idx_hbm, out_hbm, idx_v, rows_v, gsem, wsem):
        wid = lax.axis_index("s") * 2 + lax.axis_index("c")
        t0 = wid * (per * w)

        def gath(slot, jj, q, b):
            return pltpu.make_async_copy(y_hbm.at[idx_v.at[slot, q, pl.ds(jj * w, w)]], rows_v.at[b], gsem.at[b])

        def wr(i, b):
            return pltpu.make_async_copy(rows_v.at[b], out_hbm.at[i % kk, pl.ds(t0 + (i // kk) * w, w)], wsem.at[b])

        n_groups = -(-(n_items + nb) // group)

        @pl.loop(0, n_groups)
        def _(gi):
            slot = gi % 2

            @pl.when(gi * wins < per)
            def _():
                pltpu.sync_copy(idx_hbm.at[:, pl.ds(t0 + gi * LANES, LANES)], idx_v.at[slot])

            for l in range(group):
                i = gi * group + l
                b = l % nb
                bw = (l - (nb - 1)) % nb

                @pl.when(jnp.logical_and(i >= nb, i < n_items + nb))
                def _():
                    wr(i - nb, b).wait()

                @pl.when(i < n_items)
                def _():
                    gath(slot, l // kk, l % kk, b).start()

                @pl.when(jnp.logical_and(i >= nb - 1, i < n_items + nb - 1))
                def _():
                    gath(slot, 0, 0, bw).wait()
                    wr(i - (nb - 1), bw).start()

    return k(y, idx)


def _moe_kernel(layer, info_ref, x_ref, w1_hbm, b1_ref, w2_hbm, b2_ref, o_ref,
                w1f_ref, w2f_ref, w1b_ref, w2b_ref, sem):
    i = pl.program_id(0)
    nb = pl.num_programs(0)
    expert = info_ref[i]
    nxt = info_ref[nb + i]
    slot = info_ref[2 * nb + i]
    first = jnp.logical_or(i == 0, expert != info_ref[jnp.maximum(i - 1, 0)])

    def fetch(e, s):
        return (pltpu.make_async_copy(w1_hbm.at[layer, e], w1f_ref.at[s], sem.at[0, s]),
                pltpu.make_async_copy(w2_hbm.at[layer, e], w2f_ref.at[s], sem.at[1, s]))

    @pl.when(i == 0)
    def _():
        for cp in fetch(expert, slot):
            cp.start()

    @pl.when(first)
    def _():
        for cp in fetch(expert, slot):
            cp.wait()

        @pl.when(nxt >= 0)
        def _():
            for cp in fetch(nxt, 1 - slot):
                cp.start()

        w1b_ref[...] = w1f_ref[slot].astype(BF16)
        w2b_ref[...] = w2f_ref[slot].astype(BF16)

    @pl.when(i < info_ref[3 * nb])
    def _():
        _expert_block(x_ref, w1b_ref, b1_ref, w2b_ref, b2_ref, o_ref)


def _expert_block(x_ref, w1b_ref, b1_ref, w2b_ref, b2_ref, o_ref):
    lo, hi = _unpack_pairs(x_ref[...])
    x = jnp.concatenate([lo.astype(BF16), hi.astype(BF16)], axis=-1)
    f = w2b_ref.shape[0]
    fc = 512
    acc = None
    for c in range(f // fc):
        gt = jnp.dot(x, w1b_ref[:, c * fc:(c + 1) * fc], preferred_element_type=F32) + b1_ref[0, 0, :, c * fc:(c + 1) * fc]
        up = jnp.dot(x, w1b_ref[:, f + c * fc:f + (c + 1) * fc], preferred_element_type=F32) + b1_ref[0, 0, :, f + c * fc:f + (c + 1) * fc]
        gt = jnp.minimum(gt, SWIGLU_LIMIT)
        up = jnp.clip(up, -SWIGLU_LIMIT, SWIGLU_LIMIT)
        act = (gt * jax.nn.sigmoid(SWIGLU_ALPHA * gt) * (up + 1.0)).astype(BF16)
        part = jnp.dot(act, w2b_ref[c * fc:(c + 1) * fc, :], preferred_element_type=F32)
        acc = part if acc is None else acc + part
    o_ref[...] = _pack_pairs(acc + b2_ref[0, 0])


def _moe_experts(layer, block_expert, xs, p_rows, w1, b1, w2, b2):
    _, e, d, f2 = w1.shape
    f = f2 // 2
    nb = p_rows // MOE_ROWS
    grid_spec = pltpu.PrefetchScalarGridSpec(
        num_scalar_prefetch=1,
        grid=(nb,),
        in_specs=[pl.BlockSpec((MOE_ROWS, d // 2), lambda i, be: (i, 0)),
                  pl.BlockSpec(memory_space=pl.ANY),
                  pl.BlockSpec((1, 1, 1, f2), lambda i, be: (layer, be[i], 0, 0)),
                  pl.BlockSpec(memory_space=pl.ANY),
                  pl.BlockSpec((1, 1, 1, d), lambda i, be: (layer, be[i], 0, 0))],
        out_specs=pl.BlockSpec((MOE_ROWS, d // 2), lambda i, be: (i, 0)),
        scratch_shapes=[pltpu.VMEM((2, d, f2), F32), pltpu.VMEM((2, f, d), F32),
                        pltpu.VMEM((d, f2), BF16), pltpu.VMEM((f, d), BF16),
                        pltpu.SemaphoreType.DMA((2, 2))],
    )
    return pl.pallas_call(
        functools.partial(_moe_kernel, layer),
        grid_spec=grid_spec,
        out_shape=jax.ShapeDtypeStruct((p_rows, d // 2), U32),
        compiler_params=_cparams("arbitrary"),
        name="moe_experts",
    )(block_expert, xs, w1, b1.reshape(b1.shape[0], e, 1, f2), w2, b2.reshape(b2.shape[0], e, 1, d))


def _moe_dispatch(h2_words, lg_t):
    gate, dest, block_info, p_rows = _route(lg_t)
    spare = p_rows + jnp.arange(SC_SCATTER_WIN, dtype=I32)
    xs = _sc_scatter_rows(h2_words, _windows(dest, SC_SCATTER_WIN, 2, spare), p_rows + SC_SCATTER_WIN)
    n = dest.shape[1]
    step = SC_WORKERS * LANES
    back = jnp.pad(dest, ((0, 0), (0, -(-n // step) * step - n)))
    return xs, gate, back, block_info, p_rows


def _moe_compute(layer, dispatched, w1, b1, w2, b2):
    xs, gate, back, block_info, p_rows = dispatched
    ys = _moe_experts(layer, block_info, xs, p_rows, w1, b1, w2, b2)
    return _sc_gather_rows(ys, back), gate


def _final_kernel(x_ref, yg_ref, gate_ref, g2_ref, nw_ref, *rest):
    o_ref = rest[-1]
    x = x_ref[0] + g2_ref[0] * _combine(yg_ref, gate_ref)
    ms = jnp.mean(x * x, axis=-1, keepdims=True)
    o_ref[0] = x * lax.rsqrt(ms + EPS) * nw_ref[...]


def _final(x, res, nw, ts, out_buf, ob0, b_total):
    b, s, d = x.shape
    nj = s // ts
    yg, gate, g2, first = res
    blk0 = first // ts
    row = pl.BlockSpec((1, ts, d), lambda i, j: (i, j, 0))
    if gate.ndim == 2:
        gate_spec = pl.BlockSpec((GATE_ROWS, ts), lambda i, j: (0, blk0 + i * nj + j))
    else:
        gate_spec = pl.BlockSpec((1, ts, TOP_K), lambda i, j: (i, j, 0))
    in_specs = [row, pl.BlockSpec((TOP_K, ts, d // 2), lambda i, j: (0, blk0 + i * nj + j, 0)),
                gate_spec,
                pl.BlockSpec((1, 1, d), lambda i, j: (i, 0, 0)),
                pl.BlockSpec((1, d), lambda i, j: (0, 0))]
    args = [x, yg, gate, g2, nw.reshape(1, d)]
    aliases = {}
    if out_buf is not None:
        in_specs.append(pl.BlockSpec(memory_space=pl.ANY))
        args.append(out_buf)
        aliases = {len(args) - 1: 0}
    return pl.pallas_call(
        _final_kernel,
        grid=(b, nj),
        in_specs=in_specs,
        out_specs=pl.BlockSpec((1, ts, d), lambda i, j: (i + ob0, j, 0)),
        out_shape=jax.ShapeDtypeStruct((b_total, s, d), F32),
        input_output_aliases=aliases,
        compiler_params=_cparams("parallel", "parallel"),
        name="final_norm",
    )(*args)


def kernel(x_prompt, x_sample, c_prompt, c_sample, cache_k, cache_v, state_hgrn, norm_mix, norm_ffn, norm_final, w_ada, b_ada, w_in, rel_bias, hgrn_lb_logits, g_attn_out, g_hgrn_out, w_out, w_router, b_router, w_e1, b_e1, w_e2, b_e2):
    depth = w_in.shape[0]
    bp, sp, d = x_prompt.shape
    bs, ls, _ = x_sample.shape
    wc = cache_k.shape[2]
    n_p, n_s = bp * sp, bs * ls

    lbs = jax.nn.softmax(hgrn_lb_logits.astype(F32), axis=0)
    lbs = jnp.cumsum(lbs, axis=0) - lbs[0]

    mods = _modulation(jnp.concatenate([c_prompt, c_sample], axis=0), w_ada, b_ada)
    lk_s = -(-(wc + ls) // LANES) * LANES
    tail = min(LEFT, sp)
    rec_chunk = min(REC_CHUNK, sp)

    n_groups = PROMPT_GROUPS if bp % PROMPT_GROUPS == 0 else 1
    gb = bp // n_groups
    n_g = gb * sp
    x_g = [(x_prompt, g * gb) for g in range(n_groups)]
    x_s = (x_sample, 0)
    res_g = [None] * n_groups
    res_s = None
    kp_l, vp_l, sp_l, ks_l, vs_l, ss_l = [], [], [], [], [], []
    for layer in range(depth):
        m6 = mods[layer].reshape(bp + bs, 1, N_MOD, d)
        mod_s = [m6[bp:, :, n, :] for n in range(N_MOD)]
        w_in_bf = w_in[layer].astype(BF16)
        w_out_bf = w_out[layer].astype(BF16)
        wr_hi, wr_mid, _ = _split3(w_router[layer].T)
        wr2 = jnp.concatenate([wr_hi, wr_mid], axis=0)
        table = rel_bias[layer]
        bias_p = _prompt_bias(table)

        def mixer_prompt(g):
            mod = [m6[g * gb:(g + 1) * gb, :, n, :] for n in range(N_MOD)]
            x_new, h2, lg, kt, vt, s_fin = _mixer_prompt(
                x_g[g][0], x_g[g][1], res_g[g], norm_mix[layer], mod[1], mod[0], w_in_bf, lbs[layer], bias_p,
                g_attn_out[layer], g_hgrn_out[layer], w_out_bf, mod[2], norm_ffn[layer], mod[4], mod[3],
                wr2, b_router[layer], tail, rec_chunk)
            x_g[g] = (x_new, 0)
            return h2.reshape(n_g, d // 2), lg, kt, vt, s_fin, mod[5]

        def mixer_sample():
            x, qa, ka, va, kt, vt, qh, kh, lf, vh, gh = _inproj(
                x_s[0], x_s[1], res_s, norm_mix[layer], mod_s[1], mod_s[0], w_in_bf, lbs[layer], ls, ls)
            zpad = jnp.zeros((bs, lk_s - wc - ls, ATT_W), BF16)
            keys = jnp.concatenate([cache_k[layer].reshape(bs, wc, ATT_W).astype(BF16), ka, zpad], axis=1)
            vals = jnp.concatenate([cache_v[layer].reshape(bs, wc, ATT_W).astype(BF16), va, zpad], axis=1)
            o_att = _attn_sample(qa, keys, vals, _sample_bias(table, ls, wc, lk_s))
            s0 = jnp.swapaxes(state_hgrn[layer].astype(F32), -1, -2)
            o_rec, s_new = _hgrn(qh, kh, lf, vh, s0, ls, ls)
            x_new, h2, lg = _merge(o_att, o_rec, gh, x[0], x[1], g_attn_out[layer], g_hgrn_out[layer], w_out_bf,
                                   mod_s[2], norm_ffn[layer], mod_s[4], mod_s[3], wr2, b_router[layer], ls)
            return (x_new, 0), h2.reshape(n_s, d // 2), jnp.swapaxes(lg, 0, 1).reshape(N_EXPERTS, n_s), kt, vt, s_new

        x_s, h2_s, lg_s, ka_s, va_s, s_new = mixer_sample()
        ks_l.append(ka_s.reshape(bs, ls, ATT_HEADS, ATT_DIM))
        vs_l.append(va_s.reshape(bs, ls, ATT_HEADS, ATT_DIM))
        ss_l.append(jnp.swapaxes(s_new, -1, -2))

        dispatched, g2s, k_t, v_t, s_t = [], [], [], [], []
        for g in range(n_groups):
            h2, lg, kt, vt, s_fin, g2 = mixer_prompt(g)
            if g == n_groups - 1:
                h2 = jnp.concatenate([h2, h2_s], axis=0)
                lg = jnp.concatenate([lg, lg_s], axis=1)
            dispatched.append(_moe_dispatch(h2, lg))
            g2s.append(g2)
            k_t.append(kt)
            v_t.append(vt)
            s_t.append(s_fin)
        kp_l.append(jnp.concatenate(k_t, axis=0).reshape(bp, tail, ATT_HEADS, ATT_DIM))
        vp_l.append(jnp.concatenate(v_t, axis=0).reshape(bp, tail, ATT_HEADS, ATT_DIM))
        sp_l.append(jnp.swapaxes(jnp.concatenate(s_t, axis=0), -1, -2))

        for g in range(n_groups):
            yg, gate = _moe_compute(layer, dispatched[g], w_e1, b_e1, w_e2, b_e2)
            res_g[g] = (yg, gate, g2s[g], 0)
            if g == n_groups - 1:
                res_s = (yg, gate[:TOP_K, n_g:].T.reshape(bs, ls, TOP_K), mod_s[5], n_g)

    y_prompt = None
    for g in range(n_groups):
        y_prompt = _final(x_g[g][0], res_g[g], norm_final, SEQ_TILE, y_prompt, g * gb, bp)
    y_sample = _final(x_s[0], res_s, norm_final, ls, None, 0, bs)
    return (y_prompt, y_sample, jnp.stack(kp_l), jnp.stack(vp_l), jnp.stack(sp_l),
            jnp.stack(ks_l), jnp.stack(vs_l), jnp.stack(ss_l))
```

```python
import functools

import jax
import jax.numpy as jnp
import numpy as np
from jax import lax
from jax.experimental import pallas as pl
from jax.experimental.pallas import tpu as pltpu
from jax.experimental.pallas import tpu_sc as plsc

F32 = jnp.float32
BF16 = jnp.bfloat16
I32 = jnp.int32
U32 = jnp.uint32

EPS = 1e-6
CHUNK = 64
LEFT_CHUNKS = 8
LEFT = LEFT_CHUNKS * CHUNK
MAX_REL = 2 * CHUNK
ATT_HEADS = 8
ATT_DIM = 64
ATT_W = ATT_HEADS * ATT_DIM
REC_HEADS = 4
REC_DIM = 128
REC_W = REC_HEADS * REC_DIM
N_EXPERTS = 32
TOP_K = 4
SWIGLU_ALPHA = 1.702
SWIGLU_LIMIT = 7.0
N_MOD = 6
NEG = -1e30
LOG2E = 1.4426950408889634

LANES = 128
ATT_QSUB = 128
ATT_WIN = LEFT + ATT_QSUB
SEQ_TILE = 512
REC_CHUNK = 128
MOE_ROWS = 512
ROUTE_TILE = 256
PROMPT_GROUPS = 2
GATE_ROWS = 8
VMEM_LIMIT = 56 * 1024 * 1024
SC_WORKERS = 32
SC_SCATTER_WIN = 64
SC_GATHER_WIN = 32
SC_GATHER_BUFS = 4


def _cparams(*sem):
    return pltpu.CompilerParams(dimension_semantics=sem, vmem_limit_bytes=VMEM_LIMIT)


def _split3(x):
    hi = x.astype(BF16)
    r1 = x - hi.astype(F32)
    mid = r1.astype(BF16)
    lo = (r1 - mid.astype(F32)).astype(BF16)
    return hi, mid, lo


def _bits(x):
    return lax.bitcast_convert_type(x, U32)


def _pack_pairs(x):
    r = _bits(x.astype(BF16).astype(F32))
    half = x.shape[1] // 2
    return jnp.bitwise_or(jnp.bitwise_and(r[:, half:], jnp.uint32(0xFFFF0000)),
                          jnp.right_shift(r[:, :half], jnp.uint32(16)))


def _unpack_pairs(word):
    lo = lax.bitcast_convert_type(jnp.left_shift(word, jnp.uint32(16)), F32)
    hi = lax.bitcast_convert_type(jnp.bitwise_and(word, jnp.uint32(0xFFFF0000)), F32)
    return lo, hi


def _mod_kernel(c_ref, w_ref, b_ref, o_ref):
    c = c_ref[...]
    act = (c * jax.nn.sigmoid(c)).astype(BF16)
    o_ref[0] = jnp.dot(act, w_ref[0].astype(BF16), preferred_element_type=F32) + b_ref[0]


def _modulation(c_all, w_ada, b_ada):
    depth, d, n6 = w_ada.shape
    rows = c_all.shape[0]
    tn = 1536
    return pl.pallas_call(
        _mod_kernel,
        grid=(depth, n6 // tn),
        in_specs=[pl.BlockSpec((rows, d), lambda l, j: (0, 0)),
                  pl.BlockSpec((1, d, tn), lambda l, j: (l, 0, j)),
                  pl.BlockSpec((1, 1, tn), lambda l, j: (l, 0, j))],
        out_specs=pl.BlockSpec((1, rows, tn), lambda l, j: (l, 0, j)),
        out_shape=jax.ShapeDtypeStruct((depth, rows, n6), F32),
        compiler_params=_cparams("parallel", "parallel"),
        name="adaln_modulation",
    )(c_all, w_ada, b_ada.reshape(depth, 1, n6))


def _combine(yg_ref, gate_ref):
    g = gate_ref[0] if len(gate_ref.shape) == 3 else gate_ref[...].T
    lo = hi = None
    for k in range(TOP_K):
        l, h = _unpack_pairs(yg_ref[k])
        gk = g[:, k:k + 1]
        lo = l * gk if lo is None else lo + l * gk
        hi = h * gk if hi is None else hi + h * gk
    return jnp.concatenate([lo, hi], axis=-1)


def _inproj_kernel(has_res, *refs):
    if has_res:
        (x_ref, yg_ref, gate_ref, g2_ref, nw_ref, sc_ref, sh_ref, w_ref, lb_ref,
         xo_ref, qa_ref, ka_ref, va_ref, kt_ref, vt_ref, qh_ref, kh_ref, lf_ref, vh_ref, gh_ref) = refs
        x = x_ref[0] + g2_ref[0] * _combine(yg_ref, gate_ref)
        xo_ref[0] = x
    else:
        (x_ref, nw_ref, sc_ref, sh_ref, w_ref, lb_ref,
         qa_ref, ka_ref, va_ref, kt_ref, vt_ref, qh_ref, kh_ref, lf_ref, vh_ref, gh_ref) = refs
        x = x_ref[0]
    _project_rows(x, nw_ref, sc_ref, sh_ref, w_ref, lb_ref, qa_ref.at[0], ka_ref.at[0], va_ref.at[0],
                  kt_ref.at[0], vt_ref.at[0], qh_ref.at[0], kh_ref.at[0], lf_ref.at[0], vh_ref.at[0], gh_ref.at[0])


def _project_rows(x, nw_ref, sc_ref, sh_ref, w_ref, lb_ref,
                  qa_ref, ka_ref, va_ref, kt_ref, vt_ref, qh_ref, kh_ref, lf_ref, vh_ref, gh_ref):
    ms = jnp.mean(x * x, axis=-1, keepdims=True)
    h = (x * lax.rsqrt(ms + EPS) * nw_ref[...]) * (1.0 + sc_ref[0]) + sh_ref[0]
    hb = h.astype(BF16)

    def proj(g):
        return jnp.dot(hb, w_ref[:, g * ATT_W:(g + 1) * ATT_W], preferred_element_type=F32)

    qa_ref[...] = (proj(0) * (ATT_DIM ** -0.5)).astype(BF16)
    k = proj(1)
    ka_ref[...] = k.astype(BF16)
    kt_ref[...] = k
    v = proj(2)
    va_ref[...] = v.astype(BF16)
    vt_ref[...] = v
    qh = proj(3)
    qh_ref[...] = qh * jax.nn.sigmoid(qh)
    f = proj(4)
    lb = lb_ref[...]
    lf_ref[...] = jnp.log(lb + (1.0 - lb) * jax.nn.sigmoid(f))
    kh_ref[...] = (1.0 - lb) * jax.nn.sigmoid(-f)
    vh_ref[...] = proj(5).astype(BF16)
    gh_ref[...] = proj(6)


def _inproj(x, xb0, res, nw, sc, sh, w_bf, lb, ts, tail):
    _, s, d = x.shape
    b = sc.shape[0]
    pw = w_bf.shape[1]
    nj = s // ts
    n_tail = tail // ts
    has_res = res is not None
    tail_spec = pl.BlockSpec((1, ts, ATT_W), lambda i, j: (i, jnp.maximum(j - (nj - n_tail), 0), 0))
    ht = jax.ShapeDtypeStruct((b, tail, ATT_W), F32)
    row = pl.BlockSpec((1, ts, d), lambda i, j: (i, j, 0))
    per_b = pl.BlockSpec((1, 1, d), lambda i, j: (i, 0, 0))
    half = pl.BlockSpec((1, ts, ATT_W), lambda i, j: (i, j, 0))
    in_specs, args = [pl.BlockSpec((1, ts, d), lambda i, j: (i + xb0, j, 0))], [x]
    if has_res:
        yg, gate, g2, first = res
        blk0 = first // ts
        in_specs += [pl.BlockSpec((TOP_K, ts, d // 2), lambda i, j: (0, blk0 + i * nj + j, 0)),
                     pl.BlockSpec((1, ts, TOP_K), lambda i, j: (i, j, 0)), per_b]
        args += [yg, gate, g2]
    in_specs += [pl.BlockSpec((1, d), lambda i, j: (0, 0)), per_b, per_b,
                 pl.BlockSpec((d, pw), lambda i, j: (0, 0)),
                 pl.BlockSpec((1, REC_W), lambda i, j: (0, 0))]
    args += [nw.reshape(1, d), sc, sh, w_bf, lb.reshape(1, REC_W)]
    hs = jax.ShapeDtypeStruct((b, s, ATT_W), F32)
    hb = jax.ShapeDtypeStruct((b, s, ATT_W), BF16)
    out_shape = ([jax.ShapeDtypeStruct((b, s, d), F32)] if has_res else []) + [hb, hb, hb, ht, ht, hs, hs, hs, hb, hs]
    out_specs = ([row] if has_res else []) + [half] * 3 + [tail_spec] * 2 + [half] * 5
    outs = pl.pallas_call(
        functools.partial(_inproj_kernel, has_res),
        grid=(b, nj),
        in_specs=in_specs, out_specs=out_specs, out_shape=out_shape,
        compiler_params=_cparams("parallel", "arbitrary"),
        name="norm_inproj",
    )(*args)
    if has_res:
        return [(outs[0], 0)] + list(outs[1:])
    return [(x, xb0)] + list(outs)


def _attend(q, kw, vw, bias_ref, s_ref, col_thr):
    lq, lk = q.shape[0], kw.shape[0]
    lane = lax.broadcasted_iota(I32, (lq, LANES), 1)
    lo = lane < ATT_DIM
    nt = (((1,), (1,)), ((), ()))
    if col_thr is not None:
        keep = lax.broadcasted_iota(I32, (2 * lq, lk), 1) >= col_thr
    ms = []
    for p in range(ATT_HEADS // 2):
        sl = slice(p * LANES, (p + 1) * LANES)
        qp = q[:, sl]
        z = jnp.zeros_like(qp)
        lhs = jnp.concatenate([jnp.where(lo, qp, z), jnp.where(lo, z, qp)], axis=0)
        s = lax.dot_general(lhs, kw[:, sl], nt, preferred_element_type=F32) + bias_ref[p]
        if col_thr is not None:
            s = jnp.where(keep, s, NEG)
        ms.append(jnp.max(s, axis=-1, keepdims=True))
        s_ref[p] = s
    outs = []
    for p in range(ATT_HEADS // 2):
        sl = slice(p * LANES, (p + 1) * LANES)
        e = jnp.exp(s_ref[p] - ms[p])
        l = jnp.sum(e, axis=-1, keepdims=True)
        o = jnp.dot(e.astype(BF16), vw[:, sl], preferred_element_type=F32) / l
        outs.append(jnp.where(lo, o[:lq], o[lq:]))
    return jnp.concatenate(outs, axis=-1)


def _band_attention(q_ref, kw_ref, vw_ref, bias_ref, s_ref, o_ref, first_tile):
    ts = q_ref.shape[0]

    def body(masked, r, carry):
        r0 = pl.multiple_of(r * ATT_QSUB, ATT_QSUB)
        q = q_ref[pl.ds(r0, ATT_QSUB), :]
        kw = kw_ref[pl.ds(r0, ATT_WIN), :]
        vw = vw_ref[pl.ds(r0, ATT_WIN), :]
        o_ref[pl.ds(r0, ATT_QSUB), :] = _attend(q, kw, vw, bias_ref, s_ref, (ts - r0) if masked else None)
        return carry

    @pl.when(first_tile)
    def _():
        lax.fori_loop(0, ts // ATT_QSUB, functools.partial(body, True), 0)

    @pl.when(jnp.logical_not(first_tile))
    def _():
        lax.fori_loop(0, ts // ATT_QSUB, functools.partial(body, False), 0)


def _attn_sample_kernel(q_ref, k_ref, v_ref, bias_ref, o_ref, s_ref):
    o_ref[0] = _attend(q_ref[0], k_ref[0], v_ref[0], bias_ref, s_ref, None)


def _attn_sample(qa, keys, vals, bias):
    b, l, w = qa.shape
    lk = keys.shape[1]
    return pl.pallas_call(
        _attn_sample_kernel,
        grid=(b,),
        in_specs=[pl.BlockSpec((1, l, w), lambda i: (i, 0, 0)),
                  pl.BlockSpec((1, lk, w), lambda i: (i, 0, 0)),
                  pl.BlockSpec((1, lk, w), lambda i: (i, 0, 0)),
                  pl.BlockSpec(bias.shape, lambda i: (0, 0, 0))],
        out_specs=pl.BlockSpec((1, l, w), lambda i: (i, 0, 0)),
        out_shape=jax.ShapeDtypeStruct((b, l, w), F32),
        scratch_shapes=[pltpu.VMEM((ATT_HEADS // 2, 2 * l, lk), F32)],
        compiler_params=_cparams("parallel"),
        name="cache_attention",
    )(qa, keys, vals, bias)


def _toeplitz_bias(table, lq, lk, offset, valid):
    m = lq + lk - 1
    k = np.arange(m)
    diff = np.where(k < lk, k, k - m)
    g = table[:, np.clip(offset - diff, -MAX_REL, MAX_REL) + MAX_REL]
    skew = jnp.tile(g, (1, lq))[:, :lq * (m - 1)].reshape(-1, lq, m - 1)[:, :, :lk]
    bias = jnp.where(jnp.asarray(valid)[None], skew, NEG).astype(F32)
    return bias.reshape(ATT_HEADS // 2, 2 * lq, lk)


def _prompt_bias(table):
    t = np.arange(ATT_QSUB)[:, None]
    j = np.arange(ATT_WIN)[None, :]
    start = (t // CHUNK) * CHUNK
    return _toeplitz_bias(table, ATT_QSUB, ATT_WIN, LEFT, (j >= start) & (j < start + LEFT + CHUNK))


def _sample_bias(table, l, w, lk):
    valid = np.broadcast_to(np.arange(lk)[None, :] < w + l, (l, lk))
    return _toeplitz_bias(table, l, lk, w, valid)


def _neg_abs(x):
    return lax.bitcast_convert_type(jnp.bitwise_or(_bits(x), jnp.uint32(0x80000000)), F32)


def _hgrn_kernel(chunk, q_ref, k_ref, lf_ref, v_ref, s0_ref, o_ref, sf_ref, st_ref):
    i = pl.program_id(1)

    @pl.when(i == 0)
    def _():
        st_ref[...] = s0_ref[0]

    _hgrn_rows(chunk, q_ref.at[0], k_ref.at[0], lf_ref.at[0], v_ref.at[0], o_ref.at[0], st_ref)

    @pl.when(i == pl.num_programs(1) - 1)
    def _():
        sf_ref[0] = st_ref[...]


def _hgrn_rows(chunk, q_ref, k_ref, lf_ref, v_ref, o_ref, st_ref):
    ts = q_ref.shape[0]
    n_chunks = ts // chunk

    row = lax.broadcasted_iota(I32, (chunk, chunk), 0)
    colm = lax.broadcasted_iota(I32, (chunk, chunk), 1)
    tri = (colm <= row).astype(BF16)
    xor = jnp.bitwise_xor(row, colm)
    lower = colm < row
    rid = lax.broadcasted_iota(I32, (chunk, REC_W), 0)

    levels = []
    b = chunk // 2
    while b >= 1:
        levels.append(b)
        b //= 2
    pair_mask = [xor == 0] + [lower & (xor >= b) & (xor < 2 * b) for b in levels]
    upper = [jnp.bitwise_and(rid, b) != 0 for b in levels]
    ph4 = jnp.bitwise_and(rid, 3)
    odd = jnp.bitwise_and(rid, 1) == 1
    nt = (((1,), (1,)), ((), ()))
    tn = (((0,), (0,)), ((), ()))
    sls = [slice(h * REC_DIM, (h + 1) * REC_DIM) for h in range(REC_HEADS)]

    def boundary(cum, b):
        if 2 * b >= 8:
            pieces = [jnp.broadcast_to(cum[m * 2 * b + b - 1:m * 2 * b + b, :], (2 * b, REC_W))
                      for m in range(chunk // (2 * b))]
            return pieces[0] if len(pieces) == 1 else jnp.concatenate(pieces, axis=0)
        if b == 2:
            up1 = pltpu.roll(cum, chunk - 1, 0)
            dn1 = pltpu.roll(cum, 1, 0)
            dn2 = pltpu.roll(cum, 2, 0)
            return jnp.where(ph4 == 0, up1, jnp.where(ph4 == 1, cum, jnp.where(ph4 == 2, dn1, dn2)))
        return jnp.where(odd, pltpu.roll(cum, 1, 0), cum)

    def one_chunk(c, carry):
        r0 = pl.multiple_of(c * chunk, chunk)
        rows = pl.ds(r0, chunk)
        q = q_ref[rows, :]
        k = k_ref[rows, :]
        v = v_ref[rows, :]
        lf = lf_ref[rows, :]
        cum = sum(jnp.dot(tri, part, preferred_element_type=F32) for part in _split3(lf)) * LOG2E
        last = cum[chunk - 1:chunk, :]
        q_in = (q * jnp.exp2(cum)).astype(BF16)
        k_out = (k * jnp.exp2(last - cum)).astype(BF16)
        decay = jnp.exp2(last)
        qb = q.astype(BF16)
        kb = k.astype(BF16)
        zs = []
        for n, b in enumerate(levels):
            w = jnp.exp2(_neg_abs(cum - boundary(cum, b)))
            zs.append((jnp.where(upper[n], q, k) * w).astype(BF16))
        sts = [st_ref[h] for h in range(REC_HEADS)]
        d0 = [lax.dot_general(qb[:, sl], kb[:, sl], nt, preferred_element_type=F32) for sl in sls]
        dl = [[lax.dot_general(z[:, sl], z[:, sl], nt, preferred_element_type=F32) for z in zs] for sl in sls]
        oi = [lax.dot_general(q_in[:, sl], st.astype(BF16), nt, preferred_element_type=F32)
              for sl, st in zip(sls, sts)]
        upd = [lax.dot_general(v[:, sl], k_out[:, sl], tn, preferred_element_type=F32) for sl in sls]
        outs = []
        for h, sl in enumerate(sls):
            a = jnp.where(pair_mask[0], d0[h], 0.0)
            for n in range(len(levels)):
                a = jnp.where(pair_mask[n + 1], dl[h][n], a)
            outs.append(jnp.dot(a.astype(BF16), v[:, sl], preferred_element_type=F32) + oi[h])
        o_ref[rows, :] = jnp.concatenate(outs, axis=-1)
        for h, sl in enumerate(sls):
            st_ref[h] = decay[:, sl] * sts[h] + upd[h]
        return carry

    lax.fori_loop(0, n_chunks, one_chunk, 0, unroll=2 if n_chunks % 2 == 0 else 1)


def _hgrn(q, k, lf, v, s0_t, ts, chunk):
    b, s, w = q.shape
    blk = pl.BlockSpec((1, ts, w), lambda i, j: (i, j, 0))
    st = pl.BlockSpec((1, REC_HEADS, REC_DIM, REC_DIM), lambda i, j: (i, 0, 0, 0))
    return pl.pallas_call(
        functools.partial(_hgrn_kernel, chunk),
        grid=(b, s // ts),
        in_specs=[blk, blk, blk, blk, st],
        out_specs=[blk, st],
        out_shape=[jax.ShapeDtypeStruct((b, s, w), F32),
                   jax.ShapeDtypeStruct((b, REC_HEADS, REC_DIM, REC_DIM), F32)],
        scratch_shapes=[pltpu.VMEM((REC_HEADS, REC_DIM, REC_DIM), F32)],
        compiler_params=_cparams("parallel", "arbitrary"),
        name="hgrn_recurrence",
    )(q, k, lf, v, s0_t)


def _head_rms(x, width):
    lane = lax.broadcasted_iota(I32, (x.shape[0], LANES), 1)
    lo = lane < ATT_DIM
    outs = []
    for p in range(x.shape[1] // LANES):
        xp = x[:, p * LANES:(p + 1) * LANES]
        sq = xp * xp
        tot = jnp.sum(sq, axis=-1, keepdims=True)
        if width == LANES:
            ms = tot * (1.0 / LANES)
        else:
            s_lo = jnp.sum(jnp.where(lo, sq, 0.0), axis=-1, keepdims=True)
            ms = jnp.where(lo, s_lo, tot - s_lo) * (1.0 / ATT_DIM)
        outs.append(xp * lax.rsqrt(ms + EPS))
    return jnp.concatenate(outs, axis=-1)


def _merge_rows(oa, orr, gh, x, ga_ref, gr_ref, wo_ref, g1_ref, nf_ref, sc_ref, sh_ref, wr_ref, br_ref):
    a = _head_rms(oa, ATT_DIM) * ga_ref[...]
    r = _head_rms(orr, REC_DIM) * gr_ref[...] * (gh * jax.nn.sigmoid(gh))
    mix = jnp.dot(a.astype(BF16), wo_ref[0:ATT_W, :], preferred_element_type=F32)
    mix += jnp.dot(r.astype(BF16), wo_ref[ATT_W:ATT_W + REC_W, :], preferred_element_type=F32)
    x = x + g1_ref[0] * mix
    ms = jnp.mean(x * x, axis=-1, keepdims=True)
    h2 = (x * lax.rsqrt(ms + EPS) * nf_ref[...]) * (1.0 + sc_ref[0]) + sh_ref[0]
    hh = h2.astype(BF16)
    words = _pack_pairs(h2)
    hm = (h2 - hh.astype(F32)).astype(BF16)
    nt = (((1,), (1,)), ((), ()))
    l1 = lax.dot_general(wr_ref[...], hh, nt, preferred_element_type=F32)
    l2 = lax.dot_general(wr_ref[0:N_EXPERTS, :], hm, nt, preferred_element_type=F32)
    lg = l1[0:N_EXPERTS] + l1[N_EXPERTS:2 * N_EXPERTS] + l2 + br_ref[...]
    return x, words, lg


def _merge_kernel(oa_ref, or_ref, gh_ref, x_ref, ga_ref, gr_ref, wo_ref, g1_ref, nf_ref, sc_ref, sh_ref,
                  wr_ref, br_ref, xo_ref, h2_ref, lg_ref):
    x, words, lg = _merge_rows(oa_ref[0], or_ref[0], gh_ref[0], x_ref[0], ga_ref, gr_ref, wo_ref, g1_ref,
                               nf_ref, sc_ref, sh_ref, wr_ref, br_ref)
    xo_ref[0] = x
    h2_ref[0] = words
    if len(lg_ref.shape) == 3:
        lg_ref[0] = lg
    else:
        lg_ref[...] = lg


def _merge(o_att, o_rec, gh, x, xb0, g_att, g_rec, wo_bf, g1, nf, sc, sh, wr2, br, ts):
    _, s, d = x.shape
    b = o_att.shape[0]
    nj = s // ts
    row = pl.BlockSpec((1, ts, d), lambda i, j: (i, j, 0))
    xrow = pl.BlockSpec((1, ts, d), lambda i, j: (i + xb0, j, 0))
    half = pl.BlockSpec((1, ts, ATT_W), lambda i, j: (i, j, 0))
    per_b = pl.BlockSpec((1, 1, d), lambda i, j: (i, 0, 0))
    const2 = lambda shp: pl.BlockSpec(shp, lambda i, j: (0,) * len(shp))
    if ts % LANES == 0:
        lg_spec = pl.BlockSpec((N_EXPERTS, ts), lambda i, j: (0, i * nj + j))
        lg_shape = jax.ShapeDtypeStruct((N_EXPERTS, b * s), F32)
    else:
        lg_spec = pl.BlockSpec((1, N_EXPERTS, ts), lambda i, j: (i, 0, j))
        lg_shape = jax.ShapeDtypeStruct((b, N_EXPERTS, s), F32)
    return pl.pallas_call(
        _merge_kernel,
        grid=(b, nj),
        in_specs=[half, half, half, xrow, const2((1, ATT_W)), const2((1, REC_W)), const2(wo_bf.shape),
                  per_b, const2((1, d)), per_b, per_b, const2(wr2.shape), const2((N_EXPERTS, 1))],
        out_specs=[row, pl.BlockSpec((1, ts, d // 2), lambda i, j: (i, j, 0)), lg_spec],
        out_shape=[jax.ShapeDtypeStruct((b, s, d), F32), jax.ShapeDtypeStruct((b, s, d // 2), U32), lg_shape],
        compiler_params=_cparams("parallel", "parallel"),
        name="merge_outproj_router",
    )(o_att, o_rec, gh, x, g_att.reshape(1, ATT_W), g_rec.reshape(1, REC_W), wo_bf, g1, nf.reshape(1, d),
      sc, sh, wr2, br.reshape(N_EXPERTS, 1))


def _mixer_kernel(has_res, chunk, *refs):
    n_in = 4 if has_res else 1
    x_ref = refs[0]
    (nw_ref, sc1_ref, sh1_ref, lb_ref, ga_ref, gr_ref, g1_ref, nf_ref, sc2_ref, sh2_ref, wr_ref, br_ref,
     win_hbm, wout_hbm, bias_hbm,
     xo_ref, h2_ref, lg_ref, kt_ref, vt_ref, sf_ref,
     win_v, wout_v, bias_v, q_s, kw_s, vw_s, s_s, qh_s, kh_s, lf_s, vh_s, gh_s, oa_s, or_s, st_s, sem) = refs[n_in:]
    j = pl.program_id(1)
    ts = x_ref.shape[1]

    @pl.when(j == 0)
    def _():
        copies = [pltpu.make_async_copy(src, dst, sem.at[n])
                  for n, (src, dst) in enumerate(((win_hbm, win_v), (wout_hbm, wout_v), (bias_hbm, bias_v)))]
        for cp in copies:
            cp.start()
        for cp in copies:
            cp.wait()
        st_s[...] = jnp.zeros_like(st_s)
        kw_s[0:ts, :] = jnp.zeros((ts, ATT_W), BF16)
        vw_s[0:ts, :] = jnp.zeros((ts, ATT_W), BF16)

    if has_res:
        yg_ref, gate_ref, g2_ref = refs[1:4]
        x = x_ref[0] + g2_ref[0] * _combine(yg_ref, gate_ref)
        xo_ref[0] = x
    else:
        x = x_ref[0]
    cur = pl.ds(ts, ts)
    _project_rows(x, nw_ref, sc1_ref, sh1_ref, win_v, lb_ref, q_s, kw_s.at[cur], vw_s.at[cur],
                  kt_ref.at[0], vt_ref.at[0], qh_s, kh_s, lf_s, vh_s, gh_s)
    _band_attention(q_s, kw_s, vw_s, bias_v, s_s, oa_s, j == 0)
    _hgrn_rows(chunk, qh_s, kh_s, lf_s, vh_s, or_s, st_s)
    x_in = xo_ref[0] if has_res else x_ref[0]
    x_new, words, lg = _merge_rows(oa_s[...], or_s[...], gh_s[...], x_in, ga_ref, gr_ref, wout_v, g1_ref,
                                   nf_ref, sc2_ref, sh2_ref, wr_ref, br_ref)
    xo_ref[0] = x_new
    h2_ref[0] = words
    lg_ref[...] = lg
    kw_s[0:ts, :] = kw_s[ts:2 * ts, :]
    vw_s[0:ts, :] = vw_s[ts:2 * ts, :]

    @pl.when(j == pl.num_programs(1) - 1)
    def _():
        sf_ref[0] = st_s[...]


def _mixer_prompt(x, xb0, res, nw, sc1, sh1, w_in_bf, lb, bias, g_att, g_rec, w_out_bf, g1, nf, sc2, sh2,
                  wr2, br, tail, chunk):
    _, s, d = x.shape
    b = sc1.shape[0]
    ts = LEFT
    nj = s // ts
    n_tail = tail // ts
    has_res = res is not None
    row = pl.BlockSpec((1, ts, d), lambda i, j: (i, j, 0))
    per_b = pl.BlockSpec((1, 1, d), lambda i, j: (i, 0, 0))
    const = lambda shp: pl.BlockSpec(shp, lambda i, j: (0,) * len(shp))
    hbm = pl.BlockSpec(memory_space=pl.ANY)
    in_specs, args = [pl.BlockSpec((1, ts, d), lambda i, j: (i + xb0, j, 0))], [x]
    if has_res:
        yg, gate, g2, first = res
        blk0 = first // ts
        in_specs += [pl.BlockSpec((TOP_K, ts, d // 2), lambda i, j: (0, blk0 + i * nj + j, 0)),
                     pl.BlockSpec((GATE_ROWS, ts), lambda i, j: (0, blk0 + i * nj + j)), per_b]
        args += [yg, gate, g2]
    in_specs += [const((1, d)), per_b, per_b, const((1, REC_W)), const((1, ATT_W)), const((1, REC_W)),
                 per_b, const((1, d)), per_b, per_b, const(wr2.shape), const((N_EXPERTS, 1)), hbm, hbm, hbm]
    args += [nw.reshape(1, d), sc1, sh1, lb.reshape(1, REC_W), g_att.reshape(1, ATT_W), g_rec.reshape(1, REC_W),
             g1, nf.reshape(1, d), sc2, sh2, wr2, br.reshape(N_EXPERTS, 1), w_in_bf, w_out_bf, bias]
    tail_spec = pl.BlockSpec((1, ts, ATT_W), lambda i, j: (i, jnp.maximum(j - (nj - n_tail), 0), 0))
    state_spec = pl.BlockSpec((1, REC_HEADS, REC_DIM, REC_DIM), lambda i, j: (i, 0, 0, 0))
    tile = lambda dt: pltpu.VMEM((ts, ATT_W), dt)
    return pl.pallas_call(
        functools.partial(_mixer_kernel, has_res, chunk),
        grid=(b, nj),
        in_specs=in_specs,
        out_specs=[row, pl.BlockSpec((1, ts, d // 2), lambda i, j: (i, j, 0)),
                   pl.BlockSpec((N_EXPERTS, ts), lambda i, j: (0, i * nj + j)), tail_spec, tail_spec, state_spec],
        out_shape=[jax.ShapeDtypeStruct((b, s, d), F32), jax.ShapeDtypeStruct((b, s, d // 2), U32),
                   jax.ShapeDtypeStruct((N_EXPERTS, b * s), F32),
                   jax.ShapeDtypeStruct((b, tail, ATT_W), F32), jax.ShapeDtypeStruct((b, tail, ATT_W), F32),
                   jax.ShapeDtypeStruct((b, REC_HEADS, REC_DIM, REC_DIM), F32)],
        scratch_shapes=[pltpu.VMEM(w_in_bf.shape, BF16), pltpu.VMEM(w_out_bf.shape, BF16), pltpu.VMEM(bias.shape, F32),
                        tile(BF16), pltpu.VMEM((2 * ts, ATT_W), BF16), pltpu.VMEM((2 * ts, ATT_W), BF16),
                        pltpu.VMEM((ATT_HEADS // 2, 2 * ATT_QSUB, ATT_WIN), F32),
                        tile(F32), tile(F32), tile(F32), tile(BF16), tile(F32), tile(F32), tile(F32),
                        pltpu.VMEM((REC_HEADS, REC_DIM, REC_DIM), F32), pltpu.SemaphoreType.DMA((3,))],
        compiler_params=_cparams("parallel", "arbitrary"),
        name="prompt_mixer",
    )(*args)


def _route_kernel(lg_ref, idx_ref, gate_ref, rank_ref, cnt_ref, run_ref):
    i = pl.program_id(0)
    t = lg_ref.shape[1]

    @pl.when(i == 0)
    def _():
        run_ref[...] = jnp.zeros_like(run_ref)

    x = lg_ref[...]
    row = lax.broadcasted_iota(I32, x.shape, 0)
    vals, idxs, hits = [], [], []
    for _ in range(TOP_K):
        m = jnp.max(x, axis=0, keepdims=True)
        ik = jnp.min(jnp.where(x == m, row, N_EXPERTS), axis=0, keepdims=True)
        hit = row == ik
        x = jnp.where(hit, -jnp.inf, x)
        vals.append(m)
        idxs.append(ik)
        hits.append(hit)
    es = [jnp.exp(v - vals[0]) for v in vals]
    tot = es[0] + es[1] + es[2] + es[3]
    gate_ref[...] = jnp.concatenate([e / tot for e in es] + [jnp.zeros((GATE_ROWS - TOP_K, t), F32)], axis=0)
    idx_ref[...] = jnp.concatenate(idxs, axis=0)
    chosen = (hits[0] | hits[1] | hits[2] | hits[3])
    onehot = jnp.where(chosen, 1.0, 0.0)
    earlier = (lax.broadcasted_iota(I32, (t, t), 0) < lax.broadcasted_iota(I32, (t, t), 1)).astype(BF16)
    run = run_ref[:, 0:1]
    before = jnp.dot(onehot.astype(BF16), earlier, preferred_element_type=F32) + run
    rank_ref[...] = jnp.concatenate(
        [jnp.sum(jnp.where(h, before, 0.0), axis=0, keepdims=True) for h in hits], axis=0).astype(I32)
    run = run + jnp.sum(onehot, axis=1, keepdims=True)
    run_ref[...] = jnp.broadcast_to(run, run_ref.shape)
    cnt_ref[...] = jnp.broadcast_to(run, cnt_ref.shape)


def _route(lg_t):
    e, n = lg_t.shape
    t = ROUTE_TILE
    tok = pl.BlockSpec((TOP_K, t), lambda i: (0, i))
    idx, gate, rank, cnt = pl.pallas_call(
        _route_kernel,
        grid=(n // t,),
        in_specs=[pl.BlockSpec((e, t), lambda i: (0, i))],
        out_specs=[tok, pl.BlockSpec((GATE_ROWS, t), lambda i: (0, i)), tok, pl.BlockSpec((e, LANES), lambda i: (0, 0))],
        out_shape=[jax.ShapeDtypeStruct((TOP_K, n), I32), jax.ShapeDtypeStruct((GATE_ROWS, n), F32),
                   jax.ShapeDtypeStruct((TOP_K, n), I32), jax.ShapeDtypeStruct((e, LANES), F32)],
        scratch_shapes=[pltpu.VMEM((e, LANES), F32)],
        compiler_params=_cparams("arbitrary"),
        name="moe_route",
    )(lg_t)
    counts = cnt[:, 0].astype(I32)
    padded = (counts + MOE_ROWS - 1) // MOE_ROWS * MOE_ROWS
    pad_end = jnp.cumsum(padded)
    pad_start = pad_end - padded
    m = n * TOP_K
    nb = (m + N_EXPERTS * (MOE_ROWS - 1) + MOE_ROWS - 1) // MOE_ROWS
    block_start = jnp.arange(nb, dtype=I32) * MOE_ROWS
    block_expert = jnp.minimum(jnp.sum((pad_end[None, :] <= block_start[:, None]).astype(I32), axis=1), N_EXPERTS - 1)
    experts = jnp.arange(N_EXPERTS, dtype=I32)
    owns = jnp.any(block_expert[:, None] == experts[None, :], axis=0)
    later = owns[None, :] & (experts[None, :] > block_expert[:, None])
    nxt = jnp.min(jnp.where(later, experts[None, :], N_EXPERTS), axis=1)
    nxt = jnp.where(nxt == N_EXPERTS, -1, nxt)
    run = jnp.sum((owns[None, :] & (experts[None, :] < block_expert[:, None])).astype(I32), axis=1)
    block_info = jnp.concatenate([block_expert, nxt, run % 2, pad_end[-1:] // MOE_ROWS]).astype(I32)
    start_of = sum(jnp.where(idx == e, pad_start[e], 0) for e in range(N_EXPERTS))
    dest = start_of + rank
    return gate, dest, block_info, nb * MOE_ROWS


def _windows(dest, w, multiple, fill):
    kk, n = dest.shape
    nwin = n // w
    per = -(-nwin // SC_WORKERS)
    per = -(-per // multiple) * multiple
    idx = dest.reshape(kk, nwin, w).transpose(1, 0, 2)
    pad = jnp.broadcast_to(fill[None, None, :], (SC_WORKERS * per - nwin, kk, w)).astype(I32)
    return jnp.concatenate([idx, pad], axis=0)


def _sc_scatter_rows(x, idx, p_rows):
    n, c = x.shape
    nwin, kk, w = idx.shape
    per = nwin // SC_WORKERS
    nreal = n // w
    mesh = plsc.VectorSubcoreMesh(core_axis_name="c", subcore_axis_name="s")

    @functools.partial(
        pl.kernel, mesh=mesh,
        out_type=jax.ShapeDtypeStruct((p_rows, c), x.dtype),
        scratch_types=[pltpu.VMEM((per, kk, w), I32), pltpu.VMEM((2, w, c), x.dtype),
                       pltpu.SemaphoreType.DMA((2,)), pltpu.SemaphoreType.DMA((2,))],
        name="sc_scatter_rows",
    )
    def k(x_hbm, idx_hbm, out_hbm, idx_v, rows_v, lsem, ssem):
        wid = lax.axis_index("s") * 2 + lax.axis_index("c")
        g0 = wid * per
        pltpu.sync_copy(idx_hbm.at[pl.ds(g0, per)], idx_v)

        def load(j, b):
            g = jnp.minimum(g0 + j, nreal - 1)
            return pltpu.make_async_copy(x_hbm.at[pl.ds(g * w, w)], rows_v.at[b], lsem.at[b])

        def scat(j, b, q):
            return pltpu.make_async_copy(rows_v.at[b], out_hbm.at[idx_v.at[j, q]], ssem.at[b])

        @pl.loop(0, per + 2, step=2)
        def _(j0):
            for b in range(2):
                j = j0 + b

                @pl.when(jnp.logical_and(j >= 2, j < per + 2))
                def _():
                    for q in range(kk):
                        scat(j - 2, b, q).wait()

                @pl.when(j < per)
                def _():
                    load(j, b).start()

                @pl.when(jnp.logical_and(j >= 1, j < per + 1))
                def _():
                    load(j - 1, 1 - b).wait()
                    for q in range(kk):
                        scat(j - 1, 1 - b, q).start()

    return k(x, idx)


def _sc_gather_rows(y, idx):
    p, c = y.shape
    kk, n = idx.shape
    w = SC_GATHER_WIN
    nb = SC_GATHER_BUFS
    wins = LANES // w
    group = wins * kk
    assert group % nb == 0 and group >= nb and n % LANES == 0
    n_units = n // LANES
    max_units = -(-n_units // SC_WORKERS)
    mesh = plsc.VectorSubcoreMesh(core_axis_name="c", subcore_axis_name="s")

    @functools.partial(
        pl.kernel, mesh=mesh,
        out_type=jax.ShapeDtypeStruct((kk, n, c), y.dtype),
        scratch_types=[pltpu.VMEM((2, kk, LANES), I32), pltpu.VMEM((nb, w, c), y.dtype),
                       pltpu.SemaphoreType.DMA((nb,)), pltpu.SemaphoreType.DMA((nb,))],
        name="sc_gather_rows",
    )
    def k(y_hbm, idx_hbm, out_hbm, idx_v, rows_v, gsem, wsem):
        wid = lax.axis_index("s") * 2 + lax.axis_index("c")
        my_units = (n_units - wid + SC_WORKERS - 1) // SC_WORKERS
        n_items = my_units * group

        def gath(slot, jj, q, b):
            return pltpu.make_async_copy(y_hbm.at[idx_v.at[slot, q, pl.ds(jj * w, w)]], rows_v.at[b], gsem.at[b])

        def wr(i, b):
            row = (wid + (i // group) * SC_WORKERS) * LANES + ((i % group) // kk) * w
            return pltpu.make_async_copy(rows_v.at[b], out_hbm.at[i % kk, pl.ds(row, w)], wsem.at[b])

        @pl.loop(0, max_units + 1)
        def _(gi):
            slot = gi % 2

            @pl.when(gi < my_units)
            def _():
                pltpu.sync_copy(idx_hbm.at[:, pl.ds((wid + gi * SC_WORKERS) * LANES, LANES)], idx_v.at[slot])

            for l in range(group):
                i = gi * group + l
                b = l % nb
                bw = (l - (nb - 1)) % nb

                @pl.when(jnp.logical_and(i >= nb, i < n_items + nb))
                def _():
                    wr(i - nb, b).wait()

                @pl.when(i < n_items)
                def _():
                    gath(slot, l // kk, l % kk, b).start()

                @pl.when(jnp.logical_and(i >= nb - 1, i < n_items + nb - 1))
                def _():
                    gath(slot, 0, 0, bw).wait()
                    wr(i - (nb - 1), bw).start()

    return k(y, idx)


def _moe_kernel(layer, info_ref, x_ref, w1_hbm, b1_ref, w2_hbm, b2_ref, o_ref,
                w1f_ref, w2f_ref, w1b_ref, w2b_ref, sem):
    i = pl.program_id(0)
    nb = pl.num_programs(0)
    expert = info_ref[i]
    nxt = info_ref[nb + i]
    slot = info_ref[2 * nb + i]
    first = jnp.logical_or(i == 0, expert != info_ref[jnp.maximum(i - 1, 0)])

    def fetch(e, s):
        return (pltpu.make_async_copy(w1_hbm.at[layer, e], w1f_ref.at[s], sem.at[0, s]),
                pltpu.make_async_copy(w2_hbm.at[layer, e], w2f_ref.at[s], sem.at[1, s]))

    @pl.when(i == 0)
    def _():
        for cp in fetch(expert, slot):
            cp.start()

    @pl.when(first)
    def _():
        for cp in fetch(expert, slot):
            cp.wait()

        @pl.when(nxt >= 0)
        def _():
            for cp in fetch(nxt, 1 - slot):
                cp.start()

        w1b_ref[...] = w1f_ref[slot].astype(BF16)
        w2b_ref[...] = w2f_ref[slot].astype(BF16)

    @pl.when(i < info_ref[3 * nb])
    def _():
        _expert_block(x_ref, w1b_ref, b1_ref, w2b_ref, b2_ref, o_ref)


def _expert_block(x_ref, w1b_ref, b1_ref, w2b_ref, b2_ref, o_ref):
    lo, hi = _unpack_pairs(x_ref[...])
    x = jnp.concatenate([lo.astype(BF16), hi.astype(BF16)], axis=-1)
    f = w2b_ref.shape[0]
    fc = 512
    acc = None
    for c in range(f // fc):
        gt = jnp.dot(x, w1b_ref[:, c * fc:(c + 1) * fc], preferred_element_type=F32) + b1_ref[0, 0, :, c * fc:(c + 1) * fc]
        up = jnp.dot(x, w1b_ref[:, f + c * fc:f + (c + 1) * fc], preferred_element_type=F32) + b1_ref[0, 0, :, f + c * fc:f + (c + 1) * fc]
        gt = jnp.minimum(gt, SWIGLU_LIMIT)
        up = jnp.clip(up, -SWIGLU_LIMIT, SWIGLU_LIMIT)
        act = (gt * jax.nn.sigmoid(SWIGLU_ALPHA * gt) * (up + 1.0)).astype(BF16)
        part = jnp.dot(act, w2b_ref[c * fc:(c + 1) * fc, :], preferred_element_type=F32)
        acc = part if acc is None else acc + part
    o_ref[...] = _pack_pairs(acc + b2_ref[0, 0])


def _moe_experts(layer, block_expert, xs, p_rows, w1, b1, w2, b2):
    _, e, d, f2 = w1.shape
    f = f2 // 2
    nb = p_rows // MOE_ROWS
    grid_spec = pltpu.PrefetchScalarGridSpec(
        num_scalar_prefetch=1,
        grid=(nb,),
        in_specs=[pl.BlockSpec((MOE_ROWS, d // 2), lambda i, be: (i, 0)),
                  pl.BlockSpec(memory_space=pl.ANY),
                  pl.BlockSpec((1, 1, 1, f2), lambda i, be: (layer, be[i], 0, 0)),
                  pl.BlockSpec(memory_space=pl.ANY),
                  pl.BlockSpec((1, 1, 1, d), lambda i, be: (layer, be[i], 0, 0))],
        out_specs=pl.BlockSpec((MOE_ROWS, d // 2), lambda i, be: (i, 0)),
        scratch_shapes=[pltpu.VMEM((2, d, f2), F32), pltpu.VMEM((2, f, d), F32),
                        pltpu.VMEM((d, f2), BF16), pltpu.VMEM((f, d), BF16),
                        pltpu.SemaphoreType.DMA((2, 2))],
    )
    return pl.pallas_call(
        functools.partial(_moe_kernel, layer),
        grid_spec=grid_spec,
        out_shape=jax.ShapeDtypeStruct((p_rows, d // 2), U32),
        compiler_params=_cparams("arbitrary"),
        name="moe_experts",
    )(block_expert, xs, w1, b1.reshape(b1.shape[0], e, 1, f2), w2, b2.reshape(b2.shape[0], e, 1, d))


def _moe_dispatch(h2_words, lg_t):
    gate, dest, block_info, p_rows = _route(lg_t)
    spare = p_rows + jnp.arange(SC_SCATTER_WIN, dtype=I32)
    xs = _sc_scatter_rows(h2_words, _windows(dest, SC_SCATTER_WIN, 2, spare), p_rows + SC_SCATTER_WIN)
    return xs, gate, dest, block_info, p_rows


def _moe_compute(layer, dispatched, w1, b1, w2, b2):
    xs, gate, back, block_info, p_rows = dispatched
    ys = _moe_experts(layer, block_info, xs, p_rows, w1, b1, w2, b2)
    return _sc_gather_rows(ys, back), gate


def _final_kernel(x_ref, yg_ref, gate_ref, g2_ref, nw_ref, *rest):
    o_ref = rest[-1]
    x = x_ref[0] + g2_ref[0] * _combine(yg_ref, gate_ref)
    ms = jnp.mean(x * x, axis=-1, keepdims=True)
    o_ref[0] = x * lax.rsqrt(ms + EPS) * nw_ref[...]


def _final(x, res, nw, ts, out_buf, ob0, b_total):
    b, s, d = x.shape
    nj = s // ts
    yg, gate, g2, first = res
    blk0 = first // ts
    row = pl.BlockSpec((1, ts, d), lambda i, j: (i, j, 0))
    if gate.ndim == 2:
        gate_spec = pl.BlockSpec((GATE_ROWS, ts), lambda i, j: (0, blk0 + i * nj + j))
    else:
        gate_spec = pl.BlockSpec((1, ts, TOP_K), lambda i, j: (i, j, 0))
    in_specs = [row, pl.BlockSpec((TOP_K, ts, d // 2), lambda i, j: (0, blk0 + i * nj + j, 0)),
                gate_spec,
                pl.BlockSpec((1, 1, d), lambda i, j: (i, 0, 0)),
                pl.BlockSpec((1, d), lambda i, j: (0, 0))]
    args = [x, yg, gate, g2, nw.reshape(1, d)]
    aliases = {}
    if out_buf is not None:
        in_specs.append(pl.BlockSpec(memory_space=pl.ANY))
        args.append(out_buf)
        aliases = {len(args) - 1: 0}
    return pl.pallas_call(
        _final_kernel,
        grid=(b, nj),
        in_specs=in_specs,
        out_specs=pl.BlockSpec((1, ts, d), lambda i, j: (i + ob0, j, 0)),
        out_shape=jax.ShapeDtypeStruct((b_total, s, d), F32),
        input_output_aliases=aliases,
        compiler_params=_cparams("parallel", "parallel"),
        name="final_norm",
    )(*args)


def kernel(x_prompt, x_sample, c_prompt, c_sample, cache_k, cache_v, state_hgrn, norm_mix, norm_ffn, norm_final, w_ada, b_ada, w_in, rel_bias, hgrn_lb_logits, g_attn_out, g_hgrn_out, w_out, w_router, b_router, w_e1, b_e1, w_e2, b_e2):
    depth = w_in.shape[0]
    bp, sp, d = x_prompt.shape
    bs, ls, _ = x_sample.shape
    wc = cache_k.shape[2]
    n_p, n_s = bp * sp, bs * ls

    lbs = jax.nn.softmax(hgrn_lb_logits.astype(F32), axis=0)
    lbs = jnp.cumsum(lbs, axis=0) - lbs[0]

    mods = _modulation(jnp.concatenate([c_prompt, c_sample], axis=0), w_ada, b_ada)
    lk_s = -(-(wc + ls) // LANES) * LANES
    tail = min(LEFT, sp)
    rec_chunk = min(REC_CHUNK, sp)

    n_groups = PROMPT_GROUPS if bp % PROMPT_GROUPS == 0 else 1
    gb = bp // n_groups
    n_g = gb * sp
    x_g = [(x_prompt, g * gb) for g in range(n_groups)]
    x_s = (x_sample, 0)
    res_g = [None] * n_groups
    res_s = None
    kp_l, vp_l, sp_l, ks_l, vs_l, ss_l = [], [], [], [], [], []
    for layer in range(depth):
        m6 = mods[layer].reshape(bp + bs, 1, N_MOD, d)
        mod_s = [m6[bp:, :, n, :] for n in range(N_MOD)]
        w_in_bf = w_in[layer].astype(BF16)
        w_out_bf = w_out[layer].astype(BF16)
        wr_hi, wr_mid, _ = _split3(w_router[layer].T)
        wr2 = jnp.concatenate([wr_hi, wr_mid], axis=0)
        table = rel_bias[layer]
        bias_p = _prompt_bias(table)

        def mixer_prompt(g):
            mod = [m6[g * gb:(g + 1) * gb, :, n, :] for n in range(N_MOD)]
            x_new, h2, lg, kt, vt, s_fin = _mixer_prompt(
                x_g[g][0], x_g[g][1], res_g[g], norm_mix[layer], mod[1], mod[0], w_in_bf, lbs[layer], bias_p,
                g_attn_out[layer], g_hgrn_out[layer], w_out_bf, mod[2], norm_ffn[layer], mod[4], mod[3],
                wr2, b_router[layer], tail, rec_chunk)
            x_g[g] = (x_new, 0)
            return h2.reshape(n_g, d // 2), lg, kt, vt, s_fin, mod[5]

        def mixer_sample():
            x, qa, ka, va, kt, vt, qh, kh, lf, vh, gh = _inproj(
                x_s[0], x_s[1], res_s, norm_mix[layer], mod_s[1], mod_s[0], w_in_bf, lbs[layer], ls, ls)
            zpad = jnp.zeros((bs, lk_s - wc - ls, ATT_W), BF16)
            keys = jnp.concatenate([cache_k[layer].reshape(bs, wc, ATT_W).astype(BF16), ka, zpad], axis=1)
            vals = jnp.concatenate([cache_v[layer].reshape(bs, wc, ATT_W).astype(BF16), va, zpad], axis=1)
            o_att = _attn_sample(qa, keys, vals, _sample_bias(table, ls, wc, lk_s))
            s0 = jnp.swapaxes(state_hgrn[layer].astype(F32), -1, -2)
            o_rec, s_new = _hgrn(qh, kh, lf, vh, s0, ls, ls)
            x_new, h2, lg = _merge(o_att, o_rec, gh, x[0], x[1], g_attn_out[layer], g_hgrn_out[layer], w_out_bf,
                                   mod_s[2], norm_ffn[layer], mod_s[4], mod_s[3], wr2, b_router[layer], ls)
            return (x_new, 0), h2.reshape(n_s, d // 2), jnp.swapaxes(lg, 0, 1).reshape(N_EXPERTS, n_s), kt, vt, s_new

        x_s, h2_s, lg_s, ka_s, va_s, s_new = mixer_sample()
        ks_l.append(ka_s.reshape(bs, ls, ATT_HEADS, ATT_DIM))
        vs_l.append(va_s.reshape(bs, ls, ATT_HEADS, ATT_DIM))
        ss_l.append(jnp.swapaxes(s_new, -1, -2))

        dispatched, g2s, k_t, v_t, s_t = [], [], [], [], []
        for g in range(n_groups):
            h2, lg, kt, vt, s_fin, g2 = mixer_prompt(g)
            if g == n_groups - 1:
                h2 = jnp.concatenate([h2, h2_s], axis=0)
                lg = jnp.concatenate([lg, lg_s], axis=1)
            dispatched.append(_moe_dispatch(h2, lg))
            g2s.append(g2)
            k_t.append(kt)
            v_t.append(vt)
            s_t.append(s_fin)
        kp_l.append(jnp.concatenate(k_t, axis=0).reshape(bp, tail, ATT_HEADS, ATT_DIM))
        vp_l.append(jnp.concatenate(v_t, axis=0).reshape(bp, tail, ATT_HEADS, ATT_DIM))
        sp_l.append(jnp.swapaxes(jnp.concatenate(s_t, axis=0), -1, -2))

        for g in range(n_groups):
            yg, gate = _moe_compute(layer, dispatched[g], w_e1, b_e1, w_e2, b_e2)
            res_g[g] = (yg, gate, g2s[g], 0)
            if g == n_groups - 1:
                res_s = (yg, gate[:TOP_K, n_g:].T.reshape(bs, ls, TOP_K), mod_s[5], n_g)

    y_prompt = None
    for g in range(n_groups):
        y_prompt = _final(x_g[g][0], res_g[g], norm_final, SEQ_TILE, y_prompt, g * gb, bp)
    y_sample = _final(x_s[0], res_s, norm_final, ls, None, 0, bs)
    return (y_prompt, y_sample, jnp.stack(kp_l), jnp.stack(vp_l), jnp.stack(sp_l),
            jnp.stack(ks_l), jnp.stack(vs_l), jnp.stack(ss_l))
```

```python
import functools

import jax
import jax.numpy as jnp
import numpy as np
from jax import lax
from jax.experimental import pallas as pl
from jax.experimental.pallas import tpu as pltpu
from jax.experimental.pallas import tpu_sc as plsc

F32 = jnp.float32
BF16 = jnp.bfloat16
I32 = jnp.int32
U32 = jnp.uint32

EPS = 1e-6
CHUNK = 64
LEFT_CHUNKS = 8
LEFT = LEFT_CHUNKS * CHUNK
MAX_REL = 2 * CHUNK
ATT_HEADS = 8
ATT_DIM = 64
ATT_W = ATT_HEADS * ATT_DIM
REC_HEADS = 4
REC_DIM = 128
REC_W = REC_HEADS * REC_DIM
N_EXPERTS = 32
TOP_K = 4
SWIGLU_ALPHA = 1.702
SWIGLU_LIMIT = 7.0
N_MOD = 6
NEG = -1e30
LOG2E = 1.4426950408889634

LANES = 128
ATT_QSUB = 128
ATT_WIN = LEFT + ATT_QSUB
SEQ_TILE = 512
REC_CHUNK = 128
MOE_ROWS = 512
ROUTE_TILE = 256
PROMPT_GROUPS = 2
GATE_ROWS = 8
VMEM_LIMIT = 56 * 1024 * 1024
SC_WORKERS = 32
SC_SCATTER_WIN = 64
SC_GATHER_WIN = 32
SC_GATHER_BUFS = 4


def _cparams(*sem):
    return pltpu.CompilerParams(dimension_semantics=sem, vmem_limit_bytes=VMEM_LIMIT)


def _split3(x):
    hi = x.astype(BF16)
    r1 = x - hi.astype(F32)
    mid = r1.astype(BF16)
    lo = (r1 - mid.astype(F32)).astype(BF16)
    return hi, mid, lo


def _bits(x):
    return lax.bitcast_convert_type(x, U32)


def _pack_pairs(x):
    r = _bits(x.astype(BF16).astype(F32))
    half = x.shape[1] // 2
    return jnp.bitwise_or(jnp.bitwise_and(r[:, half:], jnp.uint32(0xFFFF0000)),
                          jnp.right_shift(r[:, :half], jnp.uint32(16)))


def _unpack_pairs(word):
    lo = lax.bitcast_convert_type(jnp.left_shift(word, jnp.uint32(16)), F32)
    hi = lax.bitcast_convert_type(jnp.bitwise_and(word, jnp.uint32(0xFFFF0000)), F32)
    return lo, hi


def _mod_kernel(c_ref, w_ref, b_ref, o_ref):
    c = c_ref[...]
    act = (c * jax.nn.sigmoid(c)).astype(BF16)
    o_ref[0] = jnp.dot(act, w_ref[0].astype(BF16), preferred_element_type=F32) + b_ref[0]


def _modulation(c_all, w_ada, b_ada):
    depth, d, n6 = w_ada.shape
    rows = c_all.shape[0]
    tn = 1536
    return pl.pallas_call(
        _mod_kernel,
        grid=(depth, n6 // tn),
        in_specs=[pl.BlockSpec((rows, d), lambda l, j: (0, 0)),
                  pl.BlockSpec((1, d, tn), lambda l, j: (l, 0, j)),
                  pl.BlockSpec((1, 1, tn), lambda l, j: (l, 0, j))],
        out_specs=pl.BlockSpec((1, rows, tn), lambda l, j: (l, 0, j)),
        out_shape=jax.ShapeDtypeStruct((depth, rows, n6), F32),
        compiler_params=_cparams("parallel", "parallel"),
        name="adaln_modulation",
    )(c_all, w_ada, b_ada.reshape(depth, 1, n6))


def _combine(yg_ref, gate_ref):
    g = gate_ref[0] if len(gate_ref.shape) == 3 else gate_ref[...].T
    lo = hi = None
    for k in range(TOP_K):
        l, h = _unpack_pairs(yg_ref[k])
        gk = g[:, k:k + 1]
        lo = l * gk if lo is None else lo + l * gk
        hi = h * gk if hi is None else hi + h * gk
    return jnp.concatenate([lo, hi], axis=-1)


def _inproj_kernel(has_res, *refs):
    if has_res:
        (x_ref, yg_ref, gate_ref, g2_ref, nw_ref, sc_ref, sh_ref, w_ref, lb_ref,
         xo_ref, qa_ref, ka_ref, va_ref, kt_ref, vt_ref, qh_ref, kh_ref, lf_ref, vh_ref, gh_ref) = refs
        x = x_ref[0] + g2_ref[0] * _combine(yg_ref, gate_ref)
        xo_ref[0] = x
    else:
        (x_ref, nw_ref, sc_ref, sh_ref, w_ref, lb_ref,
         qa_ref, ka_ref, va_ref, kt_ref, vt_ref, qh_ref, kh_ref, lf_ref, vh_ref, gh_ref) = refs
        x = x_ref[0]
    _project_rows(x, nw_ref, sc_ref, sh_ref, w_ref, lb_ref, qa_ref.at[0], ka_ref.at[0], va_ref.at[0],
                  kt_ref.at[0], vt_ref.at[0], qh_ref.at[0], kh_ref.at[0], lf_ref.at[0], vh_ref.at[0], gh_ref.at[0])


def _project_rows(x, nw_ref, sc_ref, sh_ref, w_ref, lb_ref,
                  qa_ref, ka_ref, va_ref, kt_ref, vt_ref, qh_ref, kh_ref, lf_ref, vh_ref, gh_ref):
    ms = jnp.mean(x * x, axis=-1, keepdims=True)
    h = (x * lax.rsqrt(ms + EPS) * nw_ref[...]) * (1.0 + sc_ref[0]) + sh_ref[0]
    hb = h.astype(BF16)

    def proj(g):
        return jnp.dot(hb, w_ref[:, g * ATT_W:(g + 1) * ATT_W], preferred_element_type=F32)

    qa_ref[...] = (proj(0) * (ATT_DIM ** -0.5)).astype(BF16)
    k = proj(1)
    ka_ref[...] = k.astype(BF16)
    kt_ref[...] = k
    v = proj(2)
    va_ref[...] = v.astype(BF16)
    vt_ref[...] = v
    qh = proj(3)
    qh_ref[...] = qh * jax.nn.sigmoid(qh)
    f = proj(4)
    lb = lb_ref[...]
    lf_ref[...] = jnp.log(lb + (1.0 - lb) * jax.nn.sigmoid(f))
    kh_ref[...] = (1.0 - lb) * jax.nn.sigmoid(-f)
    vh_ref[...] = proj(5).astype(BF16)
    gh_ref[...] = proj(6)


def _inproj(x, xb0, res, nw, sc, sh, w_bf, lb, ts, tail):
    _, s, d = x.shape
    b = sc.shape[0]
    pw = w_bf.shape[1]
    nj = s // ts
    n_tail = tail // ts
    has_res = res is not None
    tail_spec = pl.BlockSpec((1, ts, ATT_W), lambda i, j: (i, jnp.maximum(j - (nj - n_tail), 0), 0))
    ht = jax.ShapeDtypeStruct((b, tail, ATT_W), F32)
    row = pl.BlockSpec((1, ts, d), lambda i, j: (i, j, 0))
    per_b = pl.BlockSpec((1, 1, d), lambda i, j: (i, 0, 0))
    half = pl.BlockSpec((1, ts, ATT_W), lambda i, j: (i, j, 0))
    in_specs, args = [pl.BlockSpec((1, ts, d), lambda i, j: (i + xb0, j, 0))], [x]
    if has_res:
        yg, gate, g2, first = res
        blk0 = first // ts
        in_specs += [pl.BlockSpec((TOP_K, ts, d // 2), lambda i, j: (0, blk0 + i * nj + j, 0)),
                     pl.BlockSpec((1, ts, TOP_K), lambda i, j: (i, j, 0)), per_b]
        args += [yg, gate, g2]
    in_specs += [pl.BlockSpec((1, d), lambda i, j: (0, 0)), per_b, per_b,
                 pl.BlockSpec((d, pw), lambda i, j: (0, 0)),
                 pl.BlockSpec((1, REC_W), lambda i, j: (0, 0))]
    args += [nw.reshape(1, d), sc, sh, w_bf, lb.reshape(1, REC_W)]
    hs = jax.ShapeDtypeStruct((b, s, ATT_W), F32)
    hb = jax.ShapeDtypeStruct((b, s, ATT_W), BF16)
    out_shape = ([jax.ShapeDtypeStruct((b, s, d), F32)] if has_res else []) + [hb, hb, hb, ht, ht, hs, hs, hs, hb, hs]
    out_specs = ([row] if has_res else []) + [half] * 3 + [tail_spec] * 2 + [half] * 5
    outs = pl.pallas_call(
        functools.partial(_inproj_kernel, has_res),
        grid=(b, nj),
        in_specs=in_specs, out_specs=out_specs, out_shape=out_shape,
        compiler_params=_cparams("parallel", "arbitrary"),
        name="norm_inproj",
    )(*args)
    if has_res:
        return [(outs[0], 0)] + list(outs[1:])
    return [(x, xb0)] + list(outs)


def _attend(q, kw, vw, bias_ref, s_ref, col_thr):
    lq, lk = q.shape[0], kw.shape[0]
    lane = lax.broadcasted_iota(I32, (lq, LANES), 1)
    lo = lane < ATT_DIM
    nt = (((1,), (1,)), ((), ()))
    if col_thr is not None:
        keep = lax.broadcasted_iota(I32, (2 * lq, lk), 1) >= col_thr
    ms = []
    for p in range(ATT_HEADS // 2):
        sl = slice(p * LANES, (p + 1) * LANES)
        qp = q[:, sl]
        z = jnp.zeros_like(qp)
        lhs = jnp.concatenate([jnp.where(lo, qp, z), jnp.where(lo, z, qp)], axis=0)
        s = lax.dot_general(lhs, kw[:, sl], nt, preferred_element_type=F32) + bias_ref[p]
        if col_thr is not None:
            s = jnp.where(keep, s, NEG)
        ms.append(jnp.max(s, axis=-1, keepdims=True))
        s_ref[p] = s
    outs = []
    for p in range(ATT_HEADS // 2):
        sl = slice(p * LANES, (p + 1) * LANES)
        e = jnp.exp(s_ref[p] - ms[p])
        l = jnp.sum(e, axis=-1, keepdims=True)
        o = jnp.dot(e.astype(BF16), vw[:, sl], preferred_element_type=F32) / l
        outs.append(jnp.where(lo, o[:lq], o[lq:]))
    return jnp.concatenate(outs, axis=-1)


def _band_attention(q_ref, kw_ref, vw_ref, bias_ref, s_ref, o_ref, first_tile):
    ts = q_ref.shape[0]

    def body(masked, r, carry):
        r0 = pl.multiple_of(r * ATT_QSUB, ATT_QSUB)
        q = q_ref[pl.ds(r0, ATT_QSUB), :]
        kw = kw_ref[pl.ds(r0, ATT_WIN), :]
        vw = vw_ref[pl.ds(r0, ATT_WIN), :]
        o_ref[pl.ds(r0, ATT_QSUB), :] = _attend(q, kw, vw, bias_ref, s_ref, (ts - r0) if masked else None)
        return carry

    @pl.when(first_tile)
    def _():
        lax.fori_loop(0, ts // ATT_QSUB, functools.partial(body, True), 0)

    @pl.when(jnp.logical_not(first_tile))
    def _():
        lax.fori_loop(0, ts // ATT_QSUB, functools.partial(body, False), 0)


def _attn_sample_kernel(q_ref, ck_ref, cv_ref, k_ref, v_ref, bias_ref, o_ref, kw_ref, vw_ref, s_ref):
    wc, l, lk = ck_ref.shape[1], k_ref.shape[1], kw_ref.shape[0]
    for dst, cache, new in ((kw_ref, ck_ref, k_ref), (vw_ref, cv_ref, v_ref)):
        dst[0:wc, :] = cache[0].astype(BF16)
        dst[wc:wc + l, :] = new[0]
        dst[wc + l:lk, :] = jnp.zeros((lk - wc - l, ATT_W), BF16)
    o_ref[0] = _attend(q_ref[0], kw_ref[...], vw_ref[...], bias_ref, s_ref, None)


def _attn_sample(qa, cache_k, cache_v, ka, va, bias):
    b, l, w = qa.shape
    wc = cache_k.shape[1]
    lk = bias.shape[2]
    new = pl.BlockSpec((1, l, w), lambda i: (i, 0, 0))
    old = pl.BlockSpec((1, wc, w), lambda i: (i, 0, 0))
    return pl.pallas_call(
        _attn_sample_kernel,
        grid=(b,),
        in_specs=[new, old, old, new, new, pl.BlockSpec(bias.shape, lambda i: (0, 0, 0))],
        out_specs=new,
        out_shape=jax.ShapeDtypeStruct((b, l, w), F32),
        scratch_shapes=[pltpu.VMEM((lk, w), BF16), pltpu.VMEM((lk, w), BF16),
                        pltpu.VMEM((ATT_HEADS // 2, 2 * l, lk), F32)],
        compiler_params=_cparams("parallel"),
        name="cache_attention",
    )(qa, cache_k, cache_v, ka, va, bias)


def _toeplitz_bias(table, lq, lk, offset, valid):
    m = lq + lk - 1
    k = np.arange(m)
    diff = np.where(k < lk, k, k - m)
    g = table[:, np.clip(offset - diff, -MAX_REL, MAX_REL) + MAX_REL]
    skew = jnp.tile(g, (1, lq))[:, :lq * (m - 1)].reshape(-1, lq, m - 1)[:, :, :lk]
    bias = jnp.where(jnp.asarray(valid)[None], skew, NEG).astype(F32)
    return bias.reshape(ATT_HEADS // 2, 2 * lq, lk)


def _prompt_bias(table):
    t = np.arange(ATT_QSUB)[:, None]
    j = np.arange(ATT_WIN)[None, :]
    start = (t // CHUNK) * CHUNK
    return _toeplitz_bias(table, ATT_QSUB, ATT_WIN, LEFT, (j >= start) & (j < start + LEFT + CHUNK))


def _sample_bias(table, l, w, lk):
    valid = np.broadcast_to(np.arange(lk)[None, :] < w + l, (l, lk))
    return _toeplitz_bias(table, l, lk, w, valid)


def _neg_abs(x):
    return lax.bitcast_convert_type(jnp.bitwise_or(_bits(x), jnp.uint32(0x80000000)), F32)


def _hgrn_kernel(chunk, q_ref, k_ref, lf_ref, v_ref, s0_ref, o_ref, sf_ref, st_ref):
    i = pl.program_id(1)

    @pl.when(i == 0)
    def _():
        st_ref[...] = s0_ref[0]

    _hgrn_rows(chunk, q_ref.at[0], k_ref.at[0], lf_ref.at[0], v_ref.at[0], o_ref.at[0], st_ref)

    @pl.when(i == pl.num_programs(1) - 1)
    def _():
        sf_ref[0] = st_ref[...]


def _hgrn_rows(chunk, q_ref, k_ref, lf_ref, v_ref, o_ref, st_ref):
    ts = q_ref.shape[0]
    n_chunks = ts // chunk

    row = lax.broadcasted_iota(I32, (chunk, chunk), 0)
    colm = lax.broadcasted_iota(I32, (chunk, chunk), 1)
    tri = (colm <= row).astype(BF16)
    xor = jnp.bitwise_xor(row, colm)
    lower = colm < row
    rid = lax.broadcasted_iota(I32, (chunk, REC_W), 0)

    levels = []
    b = chunk // 2
    while b >= 1:
        levels.append(b)
        b //= 2
    pair_mask = [xor == 0] + [lower & (xor >= b) & (xor < 2 * b) for b in levels]
    upper = [jnp.bitwise_and(rid, b) != 0 for b in levels]
    ph4 = jnp.bitwise_and(rid, 3)
    odd = jnp.bitwise_and(rid, 1) == 1
    nt = (((1,), (1,)), ((), ()))
    tn = (((0,), (0,)), ((), ()))
    sls = [slice(h * REC_DIM, (h + 1) * REC_DIM) for h in range(REC_HEADS)]

    def boundary(cum, b):
        if 2 * b >= 8:
            pieces = [jnp.broadcast_to(cum[m * 2 * b + b - 1:m * 2 * b + b, :], (2 * b, REC_W))
                      for m in range(chunk // (2 * b))]
            return pieces[0] if len(pieces) == 1 else jnp.concatenate(pieces, axis=0)
        if b == 2:
            up1 = pltpu.roll(cum, chunk - 1, 0)
            dn1 = pltpu.roll(cum, 1, 0)
            dn2 = pltpu.roll(cum, 2, 0)
            return jnp.where(ph4 == 0, up1, jnp.where(ph4 == 1, cum, jnp.where(ph4 == 2, dn1, dn2)))
        return jnp.where(odd, pltpu.roll(cum, 1, 0), cum)

    def one_chunk(c, carry):
        r0 = pl.multiple_of(c * chunk, chunk)
        rows = pl.ds(r0, chunk)
        q = q_ref[rows, :]
        k = k_ref[rows, :]
        v = v_ref[rows, :]
        lf = lf_ref[rows, :]
        cum = sum(jnp.dot(tri, part, preferred_element_type=F32) for part in _split3(lf)) * LOG2E
        last = cum[chunk - 1:chunk, :]
        q_in = (q * jnp.exp2(cum)).astype(BF16)
        k_out = (k * jnp.exp2(last - cum)).astype(BF16)
        decay = jnp.exp2(last)
        qb = q.astype(BF16)
        kb = k.astype(BF16)
        zs = []
        for n, b in enumerate(levels):
            w = jnp.exp2(_neg_abs(cum - boundary(cum, b)))
            zs.append((jnp.where(upper[n], q, k) * w).astype(BF16))
        sts = [st_ref[h] for h in range(REC_HEADS)]
        d0 = [lax.dot_general(qb[:, sl], kb[:, sl], nt, preferred_element_type=F32) for sl in sls]
        dl = [[lax.dot_general(z[:, sl], z[:, sl], nt, preferred_element_type=F32) for z in zs] for sl in sls]
        oi = [lax.dot_general(q_in[:, sl], st.astype(BF16), nt, preferred_element_type=F32)
              for sl, st in zip(sls, sts)]
        upd = [lax.dot_general(v[:, sl], k_out[:, sl], tn, preferred_element_type=F32) for sl in sls]
        outs = []
        for h, sl in enumerate(sls):
            a = jnp.where(pair_mask[0], d0[h], 0.0)
            for n in range(len(levels)):
                a = jnp.where(pair_mask[n + 1], dl[h][n], a)
            outs.append(jnp.dot(a.astype(BF16), v[:, sl], preferred_element_type=F32) + oi[h])
        o_ref[rows, :] = jnp.concatenate(outs, axis=-1)
        for h, sl in enumerate(sls):
            st_ref[h] = decay[:, sl] * sts[h] + upd[h]
        return carry

    lax.fori_loop(0, n_chunks, one_chunk, 0, unroll=2 if n_chunks % 2 == 0 else 1)


def _hgrn(q, k, lf, v, s0_t, ts, chunk):
    b, s, w = q.shape
    blk = pl.BlockSpec((1, ts, w), lambda i, j: (i, j, 0))
    st = pl.BlockSpec((1, REC_HEADS, REC_DIM, REC_DIM), lambda i, j: (i, 0, 0, 0))
    return pl.pallas_call(
        functools.partial(_hgrn_kernel, chunk),
        grid=(b, s // ts),
        in_specs=[blk, blk, blk, blk, st],
        out_specs=[blk, st],
        out_shape=[jax.ShapeDtypeStruct((b, s, w), F32),
                   jax.ShapeDtypeStruct((b, REC_HEADS, REC_DIM, REC_DIM), F32)],
        scratch_shapes=[pltpu.VMEM((REC_HEADS, REC_DIM, REC_DIM), F32)],
        compiler_params=_cparams("parallel", "arbitrary"),
        name="hgrn_recurrence",
    )(q, k, lf, v, s0_t)


def _head_rms(x, width):
    lane = lax.broadcasted_iota(I32, (x.shape[0], LANES), 1)
    lo = lane < ATT_DIM
    outs = []
    for p in range(x.shape[1] // LANES):
        xp = x[:, p * LANES:(p + 1) * LANES]
        sq = xp * xp
        tot = jnp.sum(sq, axis=-1, keepdims=True)
        if width == LANES:
            scale = lax.rsqrt(tot * (1.0 / LANES) + EPS)
        else:
            s_lo = jnp.sum(jnp.where(lo, sq, 0.0), axis=-1, keepdims=True)
            scale = jnp.where(lo, lax.rsqrt(s_lo * (1.0 / ATT_DIM) + EPS),
                              lax.rsqrt((tot - s_lo) * (1.0 / ATT_DIM) + EPS))
        outs.append(xp * scale)
    return jnp.concatenate(outs, axis=-1)


def _merge_rows(oa, orr, gh, x, ga_ref, gr_ref, wo_ref, g1_ref, nf_ref, sc_ref, sh_ref, wr_ref, br_ref):
    a = _head_rms(oa, ATT_DIM) * ga_ref[...]
    r = _head_rms(orr, REC_DIM) * gr_ref[...] * (gh * jax.nn.sigmoid(gh))
    mix = jnp.dot(a.astype(BF16), wo_ref[0:ATT_W, :], preferred_element_type=F32)
    mix += jnp.dot(r.astype(BF16), wo_ref[ATT_W:ATT_W + REC_W, :], preferred_element_type=F32)
    x = x + g1_ref[0] * mix
    ms = jnp.mean(x * x, axis=-1, keepdims=True)
    h2 = (x * lax.rsqrt(ms + EPS) * nf_ref[...]) * (1.0 + sc_ref[0]) + sh_ref[0]
    hh = h2.astype(BF16)
    words = _pack_pairs(h2)
    hm = (h2 - hh.astype(F32)).astype(BF16)
    nt = (((1,), (1,)), ((), ()))
    l1 = lax.dot_general(wr_ref[...], hh, nt, preferred_element_type=F32)
    l2 = lax.dot_general(wr_ref[0:N_EXPERTS, :], hm, nt, preferred_element_type=F32)
    lg = l1[0:N_EXPERTS] + l1[N_EXPERTS:2 * N_EXPERTS] + l2 + br_ref[...]
    return x, words, lg


def _merge_kernel(oa_ref, or_ref, gh_ref, x_ref, ga_ref, gr_ref, wo_ref, g1_ref, nf_ref, sc_ref, sh_ref,
                  wr_ref, br_ref, rows_hbm, xo_ref, h2_ref, lg_ref):
    del rows_hbm
    x, words, lg = _merge_rows(oa_ref[0], or_ref[0], gh_ref[0], x_ref[0], ga_ref, gr_ref, wo_ref, g1_ref,
                               nf_ref, sc_ref, sh_ref, wr_ref, br_ref)
    xo_ref[0] = x
    h2_ref[...] = words
    lg_ref[0] = lg


def _merge(o_att, o_rec, gh, x, xb0, g_att, g_rec, wo_bf, g1, nf, sc, sh, wr2, br, rows, row0):
    _, ts, d = x.shape
    b = o_att.shape[0]
    row = pl.BlockSpec((1, ts, d), lambda i: (i, 0, 0))
    xrow = pl.BlockSpec((1, ts, d), lambda i: (i + xb0, 0, 0))
    half = pl.BlockSpec((1, ts, ATT_W), lambda i: (i, 0, 0))
    per_b = pl.BlockSpec((1, 1, d), lambda i: (i, 0, 0))
    const2 = lambda shp: pl.BlockSpec(shp, lambda i: (0,) * len(shp))
    blk0 = row0 // ts
    return pl.pallas_call(
        _merge_kernel,
        grid=(b,),
        in_specs=[half, half, half, xrow, const2((1, ATT_W)), const2((1, REC_W)), const2(wo_bf.shape),
                  per_b, const2((1, d)), per_b, per_b, const2(wr2.shape), const2((N_EXPERTS, 1)),
                  pl.BlockSpec(memory_space=pl.ANY)],
        out_specs=[row, pl.BlockSpec((ts, d // 2), lambda i: (blk0 + i, 0)),
                   pl.BlockSpec((1, N_EXPERTS, ts), lambda i: (i, 0, 0))],
        out_shape=[jax.ShapeDtypeStruct((b, ts, d), F32), jax.ShapeDtypeStruct(rows.shape, U32),
                   jax.ShapeDtypeStruct((b, N_EXPERTS, ts), F32)],
        input_output_aliases={13: 1},
        compiler_params=_cparams("parallel"),
        name="merge_outproj_router",
    )(o_att, o_rec, gh, x, g_att.reshape(1, ATT_W), g_rec.reshape(1, REC_W), wo_bf, g1, nf.reshape(1, d),
      sc, sh, wr2, br.reshape(N_EXPERTS, 1), rows)


def _mixer_kernel(has_res, chunk, *refs):
    n_in = 4 if has_res else 1
    x_ref = refs[0]
    (nw_ref, sc1_ref, sh1_ref, lb_ref, ga_ref, gr_ref, g1_ref, nf_ref, sc2_ref, sh2_ref, wr_ref, br_ref,
     win_hbm, wout_hbm, bias_hbm,
     xo_ref, h2_ref, lg_ref, kt_ref, vt_ref, sf_ref,
     win_v, wout_v, bias_v, q_s, kw_s, vw_s, s_s, qh_s, kh_s, lf_s, vh_s, gh_s, oa_s, or_s, st_s, sem) = refs[n_in:]
    j = pl.program_id(1)
    ts = x_ref.shape[1]

    @pl.when(j == 0)
    def _():
        copies = [pltpu.make_async_copy(src, dst, sem.at[n])
                  for n, (src, dst) in enumerate(((win_hbm, win_v), (wout_hbm, wout_v), (bias_hbm, bias_v)))]
        for cp in copies:
            cp.start()
        for cp in copies:
            cp.wait()
        st_s[...] = jnp.zeros_like(st_s)
        kw_s[0:ts, :] = jnp.zeros((ts, ATT_W), BF16)
        vw_s[0:ts, :] = jnp.zeros((ts, ATT_W), BF16)

    if has_res:
        yg_ref, gate_ref, g2_ref = refs[1:4]
        x = x_ref[0] + g2_ref[0] * _combine(yg_ref, gate_ref)
        xo_ref[0] = x
    else:
        x = x_ref[0]
    cur = pl.ds(ts, ts)
    _project_rows(x, nw_ref, sc1_ref, sh1_ref, win_v, lb_ref, q_s, kw_s.at[cur], vw_s.at[cur],
                  kt_ref.at[0], vt_ref.at[0], qh_s, kh_s, lf_s, vh_s, gh_s)
    _band_attention(q_s, kw_s, vw_s, bias_v, s_s, oa_s, j == 0)
    _hgrn_rows(chunk, qh_s, kh_s, lf_s, vh_s, or_s, st_s)
    x_in = xo_ref[0] if has_res else x_ref[0]
    x_new, words, lg = _merge_rows(oa_s[...], or_s[...], gh_s[...], x_in, ga_ref, gr_ref, wout_v, g1_ref,
                                   nf_ref, sc2_ref, sh2_ref, wr_ref, br_ref)
    xo_ref[0] = x_new
    h2_ref[...] = words
    lg_ref[...] = lg
    kw_s[0:ts, :] = kw_s[ts:2 * ts, :]
    vw_s[0:ts, :] = vw_s[ts:2 * ts, :]

    @pl.when(j == pl.num_programs(1) - 1)
    def _():
        sf_ref[0] = st_s[...]


def _mixer_prompt(x, xb0, res, nw, sc1, sh1, w_in_bf, lb, bias, g_att, g_rec, w_out_bf, g1, nf, sc2, sh2,
                  wr2, br, tail, chunk, extra_rows):
    _, s, d = x.shape
    b = sc1.shape[0]
    ts = LEFT
    nj = s // ts
    n_tail = tail // ts
    has_res = res is not None
    row = pl.BlockSpec((1, ts, d), lambda i, j: (i, j, 0))
    per_b = pl.BlockSpec((1, 1, d), lambda i, j: (i, 0, 0))
    const = lambda shp: pl.BlockSpec(shp, lambda i, j: (0,) * len(shp))
    hbm = pl.BlockSpec(memory_space=pl.ANY)
    in_specs, args = [pl.BlockSpec((1, ts, d), lambda i, j: (i + xb0, j, 0))], [x]
    if has_res:
        yg, gate, g2, first = res
        blk0 = first // ts
        in_specs += [pl.BlockSpec((TOP_K, ts, d // 2), lambda i, j: (0, blk0 + i * nj + j, 0)),
                     pl.BlockSpec((GATE_ROWS, ts), lambda i, j: (0, blk0 + i * nj + j)), per_b]
        args += [yg, gate, g2]
    in_specs += [const((1, d)), per_b, per_b, const((1, REC_W)), const((1, ATT_W)), const((1, REC_W)),
                 per_b, const((1, d)), per_b, per_b, const(wr2.shape), const((N_EXPERTS, 1)), hbm, hbm, hbm]
    args += [nw.reshape(1, d), sc1, sh1, lb.reshape(1, REC_W), g_att.reshape(1, ATT_W), g_rec.reshape(1, REC_W),
             g1, nf.reshape(1, d), sc2, sh2, wr2, br.reshape(N_EXPERTS, 1), w_in_bf, w_out_bf, bias]
    tail_spec = pl.BlockSpec((1, ts, ATT_W), lambda i, j: (i, jnp.maximum(j - (nj - n_tail), 0), 0))
    state_spec = pl.BlockSpec((1, REC_HEADS, REC_DIM, REC_DIM), lambda i, j: (i, 0, 0, 0))
    tile = lambda dt: pltpu.VMEM((ts, ATT_W), dt)
    return pl.pallas_call(
        functools.partial(_mixer_kernel, has_res, chunk),
        grid=(b, nj),
        in_specs=in_specs,
        out_specs=[row, pl.BlockSpec((ts, d // 2), lambda i, j: (i * nj + j, 0)),
                   pl.BlockSpec((N_EXPERTS, ts), lambda i, j: (0, i * nj + j)), tail_spec, tail_spec, state_spec],
        out_shape=[jax.ShapeDtypeStruct((b, s, d), F32), jax.ShapeDtypeStruct((b * s + extra_rows, d // 2), U32),
                   jax.ShapeDtypeStruct((N_EXPERTS, b * s), F32),
                   jax.ShapeDtypeStruct((b, tail, ATT_W), F32), jax.ShapeDtypeStruct((b, tail, ATT_W), F32),
                   jax.ShapeDtypeStruct((b, REC_HEADS, REC_DIM, REC_DIM), F32)],
        scratch_shapes=[pltpu.VMEM(w_in_bf.shape, BF16), pltpu.VMEM(w_out_bf.shape, BF16), pltpu.VMEM(bias.shape, F32),
                        tile(BF16), pltpu.VMEM((2 * ts, ATT_W), BF16), pltpu.VMEM((2 * ts, ATT_W), BF16),
                        pltpu.VMEM((ATT_HEADS // 2, 2 * ATT_QSUB, ATT_WIN), F32),
                        tile(F32), tile(F32), tile(F32), tile(BF16), tile(F32), tile(F32), tile(F32),
                        pltpu.VMEM((REC_HEADS, REC_DIM, REC_DIM), F32), pltpu.SemaphoreType.DMA((3,))],
        compiler_params=_cparams("parallel", "arbitrary"),
        name="prompt_mixer",
    )(*args)


def _route_kernel(lg_ref, idx_ref, gate_ref, rank_ref, cnt_ref, run_ref):
    i = pl.program_id(0)
    t = lg_ref.shape[1]

    @pl.when(i == 0)
    def _():
        run_ref[...] = jnp.zeros_like(run_ref)

    x = lg_ref[...]
    row = lax.broadcasted_iota(I32, x.shape, 0)
    vals, idxs, hits = [], [], []
    for _ in range(TOP_K):
        m = jnp.max(x, axis=0, keepdims=True)
        ik = jnp.min(jnp.where(x == m, row, N_EXPERTS), axis=0, keepdims=True)
        hit = row == ik
        x = jnp.where(hit, -jnp.inf, x)
        vals.append(m)
        idxs.append(ik)
        hits.append(hit)
    es = [jnp.exp(v - vals[0]) for v in vals]
    tot = es[0] + es[1] + es[2] + es[3]
    gate_ref[...] = jnp.concatenate([e / tot for e in es] + [jnp.zeros((GATE_ROWS - TOP_K, t), F32)], axis=0)
    idx_ref[...] = jnp.concatenate(idxs, axis=0)
    chosen = (hits[0] | hits[1] | hits[2] | hits[3])
    onehot = jnp.where(chosen, 1.0, 0.0)
    earlier = (lax.broadcasted_iota(I32, (t, t), 0) < lax.broadcasted_iota(I32, (t, t), 1)).astype(BF16)
    run = run_ref[:, 0:1]
    before = jnp.dot(onehot.astype(BF16), earlier, preferred_element_type=F32) + run
    rank_ref[...] = jnp.concatenate(
        [jnp.sum(jnp.where(h, before, 0.0), axis=0, keepdims=True) for h in hits], axis=0).astype(I32)
    run = run + jnp.sum(onehot, axis=1, keepdims=True)
    run_ref[...] = jnp.broadcast_to(run, run_ref.shape)
    cnt_ref[...] = jnp.broadcast_to(run, cnt_ref.shape)


def _route(lg_t):
    e, n = lg_t.shape
    t = ROUTE_TILE
    tok = pl.BlockSpec((TOP_K, t), lambda i: (0, i))
    idx, gate, rank, cnt = pl.pallas_call(
        _route_kernel,
        grid=(n // t,),
        in_specs=[pl.BlockSpec((e, t), lambda i: (0, i))],
        out_specs=[tok, pl.BlockSpec((GATE_ROWS, t), lambda i: (0, i)), tok, pl.BlockSpec((e, LANES), lambda i: (0, 0))],
        out_shape=[jax.ShapeDtypeStruct((TOP_K, n), I32), jax.ShapeDtypeStruct((GATE_ROWS, n), F32),
                   jax.ShapeDtypeStruct((TOP_K, n), I32), jax.ShapeDtypeStruct((e, LANES), F32)],
        scratch_shapes=[pltpu.VMEM((e, LANES), F32)],
        compiler_params=_cparams("arbitrary"),
        name="moe_route",
    )(lg_t)
    counts = cnt[:, 0].astype(I32)
    padded = (counts + MOE_ROWS - 1) // MOE_ROWS * MOE_ROWS
    pad_end = jnp.cumsum(padded)
    pad_start = pad_end - padded
    m = n * TOP_K
    nb = (m + N_EXPERTS * (MOE_ROWS - 1) + MOE_ROWS - 1) // MOE_ROWS
    block_start = jnp.arange(nb, dtype=I32) * MOE_ROWS
    block_expert = jnp.minimum(jnp.sum((pad_end[None, :] <= block_start[:, None]).astype(I32), axis=1), N_EXPERTS - 1)
    experts = jnp.arange(N_EXPERTS, dtype=I32)
    owns = jnp.any(block_expert[:, None] == experts[None, :], axis=0)
    later = owns[None, :] & (experts[None, :] > block_expert[:, None])
    nxt = jnp.min(jnp.where(later, experts[None, :], N_EXPERTS), axis=1)
    nxt = jnp.where(nxt == N_EXPERTS, -1, nxt)
    run = jnp.sum((owns[None, :] & (experts[None, :] < block_expert[:, None])).astype(I32), axis=1)
    block_info = jnp.concatenate([block_expert, nxt, run % 2, pad_end[-1:] // MOE_ROWS]).astype(I32)
    start_of = sum(jnp.where(idx == e, pad_start[e], 0) for e in range(N_EXPERTS))
    dest = start_of + rank
    return gate, dest, block_info, nb * MOE_ROWS


def _windows(dest, w, multiple, fill):
    kk, n = dest.shape
    nwin = n // w
    per = -(-nwin // SC_WORKERS)
    per = -(-per // multiple) * multiple
    idx = dest.reshape(kk, nwin, w).transpose(1, 0, 2)
    pad = jnp.broadcast_to(fill[None, None, :], (SC_WORKERS * per - nwin, kk, w)).astype(I32)
    return jnp.concatenate([idx, pad], axis=0)


def _sc_scatter_rows(x, idx, p_rows):
    n, c = x.shape
    nwin, kk, w = idx.shape
    per = nwin // SC_WORKERS
    nreal = n // w
    mesh = plsc.VectorSubcoreMesh(core_axis_name="c", subcore_axis_name="s")

    @functools.partial(
        pl.kernel, mesh=mesh,
        out_type=jax.ShapeDtypeStruct((p_rows, c), x.dtype),
        scratch_types=[pltpu.VMEM((per, kk, w), I32), pltpu.VMEM((2, w, c), x.dtype),
                       pltpu.SemaphoreType.DMA((2,)), pltpu.SemaphoreType.DMA((2,))],
        name="sc_scatter_rows",
    )
    def k(x_hbm, idx_hbm, out_hbm, idx_v, rows_v, lsem, ssem):
        wid = lax.axis_index("s") * 2 + lax.axis_index("c")
        g0 = wid * per
        pltpu.sync_copy(idx_hbm.at[pl.ds(g0, per)], idx_v)

        def load(j, b):
            g = jnp.minimum(g0 + j, nreal - 1)
            return pltpu.make_async_copy(x_hbm.at[pl.ds(g * w, w)], rows_v.at[b], lsem.at[b])

        def scat(j, b, q):
            return pltpu.make_async_copy(rows_v.at[b], out_hbm.at[idx_v.at[j, q]], ssem.at[b])

        @pl.loop(0, per + 2, step=2)
        def _(j0):
            for b in range(2):
                j = j0 + b

                @pl.when(jnp.logical_and(j >= 2, j < per + 2))
                def _():
                    for q in range(kk):
                        scat(j - 2, b, q).wait()

                @pl.when(j < per)
                def _():
                    load(j, b).start()

                @pl.when(jnp.logical_and(j >= 1, j < per + 1))
                def _():
                    load(j - 1, 1 - b).wait()
                    for q in range(kk):
                        scat(j - 1, 1 - b, q).start()

    return k(x, idx)


def _sc_gather_rows(y, idx):
    p, c = y.shape
    kk, n = idx.shape
    w = SC_GATHER_WIN
    nb = SC_GATHER_BUFS
    wins = LANES // w
    group = wins * kk
    assert group % nb == 0 and group >= nb and n % LANES == 0
    n_units = n // LANES
    max_units = -(-n_units // SC_WORKERS)
    mesh = plsc.VectorSubcoreMesh(core_axis_name="c", subcore_axis_name="s")

    @functools.partial(
        pl.kernel, mesh=mesh,
        out_type=jax.ShapeDtypeStruct((kk, n, c), y.dtype),
        scratch_types=[pltpu.VMEM((2, kk, LANES), I32), pltpu.VMEM((nb, w, c), y.dtype),
                       pltpu.SemaphoreType.DMA((nb,)), pltpu.SemaphoreType.DMA((nb,))],
        name="sc_gather_rows",
    )
    def k(y_hbm, idx_hbm, out_hbm, idx_v, rows_v, gsem, wsem):
        wid = lax.axis_index("s") * 2 + lax.axis_index("c")
        my_units = (n_units - wid + SC_WORKERS - 1) // SC_WORKERS
        n_items = my_units * group

        def gath(slot, jj, q, b):
            return pltpu.make_async_copy(y_hbm.at[idx_v.at[slot, q, pl.ds(jj * w, w)]], rows_v.at[b], gsem.at[b])

        def wr(i, b):
            row = (wid + (i // group) * SC_WORKERS) * LANES + ((i % group) // kk) * w
            return pltpu.make_async_copy(rows_v.at[b], out_hbm.at[i % kk, pl.ds(row, w)], wsem.at[b])

        @pl.loop(0, max_units + 1)
        def _(gi):
            slot = gi % 2

            @pl.when(gi < my_units)
            def _():
                pltpu.sync_copy(idx_hbm.at[:, pl.ds((wid + gi * SC_WORKERS) * LANES, LANES)], idx_v.at[slot])

            for l in range(group):
                i = gi * group + l
                b = l % nb
                bw = (l - (nb - 1)) % nb

                @pl.when(jnp.logical_and(i >= nb, i < n_items + nb))
                def _():
                    wr(i - nb, b).wait()

                @pl.when(i < n_items)
                def _():
                    gath(slot, l // kk, l % kk, b).start()

                @pl.when(jnp.logical_and(i >= nb - 1, i < n_items + nb - 1))
                def _():
                    gath(slot, 0, 0, bw).wait()
                    wr(i - (nb - 1), bw).start()

    return k(y, idx)


def _moe_kernel(layer, info_ref, x_ref, w1_hbm, b1_ref, w2_hbm, b2_ref, o_ref,
                w1f_ref, w2f_ref, w1b_ref, w2b_ref, sem):
    i = pl.program_id(0)
    nb = pl.num_programs(0)
    expert = info_ref[i]
    nxt = info_ref[nb + i]
    slot = info_ref[2 * nb + i]
    first = jnp.logical_or(i == 0, expert != info_ref[jnp.maximum(i - 1, 0)])

    def fetch(e, s):
        return (pltpu.make_async_copy(w1_hbm.at[layer, e], w1f_ref.at[s], sem.at[0, s]),
                pltpu.make_async_copy(w2_hbm.at[layer, e], w2f_ref.at[s], sem.at[1, s]))

    @pl.when(i == 0)
    def _():
        for cp in fetch(expert, slot):
            cp.start()

    @pl.when(first)
    def _():
        for cp in fetch(expert, slot):
            cp.wait()

        @pl.when(nxt >= 0)
        def _():
            for cp in fetch(nxt, 1 - slot):
                cp.start()

        w1b_ref[...] = w1f_ref[slot].astype(BF16)
        w2b_ref[...] = w2f_ref[slot].astype(BF16)

    @pl.when(i < info_ref[3 * nb])
    def _():
        _expert_block(x_ref, w1b_ref, b1_ref, w2b_ref, b2_ref, o_ref)


def _expert_block(x_ref, w1b_ref, b1_ref, w2b_ref, b2_ref, o_ref):
    lo, hi = _unpack_pairs(x_ref[...])
    x = jnp.concatenate([lo.astype(BF16), hi.astype(BF16)], axis=-1)
    f = w2b_ref.shape[0]
    fc = 512
    acc = None
    for c in range(f // fc):
        gt = jnp.dot(x, w1b_ref[:, c * fc:(c + 1) * fc], preferred_element_type=F32) + b1_ref[0, 0, :, c * fc:(c + 1) * fc]
        up = jnp.dot(x, w1b_ref[:, f + c * fc:f + (c + 1) * fc], preferred_element_type=F32) + b1_ref[0, 0, :, f + c * fc:f + (c + 1) * fc]
        gt = jnp.minimum(gt, SWIGLU_LIMIT)
        up = jnp.clip(up, -SWIGLU_LIMIT, SWIGLU_LIMIT)
        act = (gt * jax.nn.sigmoid(SWIGLU_ALPHA * gt) * (up + 1.0)).astype(BF16)
        part = jnp.dot(act, w2b_ref[c * fc:(c + 1) * fc, :], preferred_element_type=F32)
        acc = part if acc is None else acc + part
    o_ref[...] = _pack_pairs(acc + b2_ref[0, 0])


def _moe_experts(layer, block_expert, xs, p_rows, w1, b1, w2, b2):
    _, e, d, f2 = w1.shape
    f = f2 // 2
    nb = p_rows // MOE_ROWS
    grid_spec = pltpu.PrefetchScalarGridSpec(
        num_scalar_prefetch=1,
        grid=(nb,),
        in_specs=[pl.BlockSpec((MOE_ROWS, d // 2), lambda i, be: (i, 0)),
                  pl.BlockSpec(memory_space=pl.ANY),
                  pl.BlockSpec((1, 1, 1, f2), lambda i, be: (layer, be[i], 0, 0)),
                  pl.BlockSpec(memory_space=pl.ANY),
                  pl.BlockSpec((1, 1, 1, d), lambda i, be: (layer, be[i], 0, 0))],
        out_specs=pl.BlockSpec((MOE_ROWS, d // 2), lambda i, be: (i, 0)),
        scratch_shapes=[pltpu.VMEM((2, d, f2), F32), pltpu.VMEM((2, f, d), F32),
                        pltpu.VMEM((d, f2), BF16), pltpu.VMEM((f, d), BF16),
                        pltpu.SemaphoreType.DMA((2, 2))],
    )
    return pl.pallas_call(
        functools.partial(_moe_kernel, layer),
        grid_spec=grid_spec,
        out_shape=jax.ShapeDtypeStruct((p_rows, d // 2), U32),
        compiler_params=_cparams("arbitrary"),
        name="moe_experts",
    )(block_expert, xs, w1, b1.reshape(b1.shape[0], e, 1, f2), w2, b2.reshape(b2.shape[0], e, 1, d))


def _moe_dispatch(h2_words, lg_t):
    gate, dest, block_info, p_rows = _route(lg_t)
    spare = p_rows + jnp.arange(SC_SCATTER_WIN, dtype=I32)
    xs = _sc_scatter_rows(h2_words, _windows(dest, SC_SCATTER_WIN, 2, spare), p_rows + SC_SCATTER_WIN)
    return xs, gate, dest, block_info, p_rows


def _moe_compute(layer, dispatched, w1, b1, w2, b2):
    xs, gate, back, block_info, p_rows = dispatched
    ys = _moe_experts(layer, block_info, xs, p_rows, w1, b1, w2, b2)
    return _sc_gather_rows(ys, back), gate


def _final_kernel(x_ref, yg_ref, gate_ref, g2_ref, nw_ref, *rest):
    o_ref = rest[-1]
    x = x_ref[0] + g2_ref[0] * _combine(yg_ref, gate_ref)
    ms = jnp.mean(x * x, axis=-1, keepdims=True)
    o_ref[0] = x * lax.rsqrt(ms + EPS) * nw_ref[...]


def _final(x, res, nw, ts, out_buf, ob0, b_total):
    b, s, d = x.shape
    nj = s // ts
    yg, gate, g2, first = res
    blk0 = first // ts
    row = pl.BlockSpec((1, ts, d), lambda i, j: (i, j, 0))
    if gate.ndim == 2:
        gate_spec = pl.BlockSpec((GATE_ROWS, ts), lambda i, j: (0, blk0 + i * nj + j))
    else:
        gate_spec = pl.BlockSpec((1, ts, TOP_K), lambda i, j: (i, j, 0))
    in_specs = [row, pl.BlockSpec((TOP_K, ts, d // 2), lambda i, j: (0, blk0 + i * nj + j, 0)),
                gate_spec,
                pl.BlockSpec((1, 1, d), lambda i, j: (i, 0, 0)),
                pl.BlockSpec((1, d), lambda i, j: (0, 0))]
    args = [x, yg, gate, g2, nw.reshape(1, d)]
    aliases = {}
    if out_buf is not None:
        in_specs.append(pl.BlockSpec(memory_space=pl.ANY))
        args.append(out_buf)
        aliases = {len(args) - 1: 0}
    return pl.pallas_call(
        _final_kernel,
        grid=(b, nj),
        in_specs=in_specs,
        out_specs=pl.BlockSpec((1, ts, d), lambda i, j: (i + ob0, j, 0)),
        out_shape=jax.ShapeDtypeStruct((b_total, s, d), F32),
        input_output_aliases=aliases,
        compiler_params=_cparams("parallel", "parallel"),
        name="final_norm",
    )(*args)


def kernel(x_prompt, x_sample, c_prompt, c_sample, cache_k, cache_v, state_hgrn, norm_mix, norm_ffn, norm_final, w_ada, b_ada, w_in, rel_bias, hgrn_lb_logits, g_attn_out, g_hgrn_out, w_out, w_router, b_router, w_e1, b_e1, w_e2, b_e2):
    depth = w_in.shape[0]
    bp, sp, d = x_prompt.shape
    bs, ls, _ = x_sample.shape
    wc = cache_k.shape[2]
    n_p, n_s = bp * sp, bs * ls

    lbs = jax.nn.softmax(hgrn_lb_logits.astype(F32), axis=0)
    lbs = jnp.cumsum(lbs, axis=0) - lbs[0]

    mods = _modulation(jnp.concatenate([c_prompt, c_sample], axis=0), w_ada, b_ada)
    lk_s = -(-(wc + ls) // LANES) * LANES
    tail = min(LEFT, sp)
    rec_chunk = min(REC_CHUNK, sp)

    n_groups = PROMPT_GROUPS if bp % PROMPT_GROUPS == 0 else 1
    gb = bp // n_groups
    n_g = gb * sp
    x_g = [(x_prompt, g * gb) for g in range(n_groups)]
    x_s = (x_sample, 0)
    res_g = [None] * n_groups
    res_s = None
    kp_l, vp_l, sp_l, ks_l, vs_l, ss_l = [], [], [], [], [], []
    for layer in range(depth):
        m6 = mods[layer].reshape(bp + bs, 1, N_MOD, d)
        mod_s = [m6[bp:, :, n, :] for n in range(N_MOD)]
        w_in_bf = w_in[layer].astype(BF16)
        w_out_bf = w_out[layer].astype(BF16)
        wr_hi, wr_mid, _ = _split3(w_router[layer].T)
        wr2 = jnp.concatenate([wr_hi, wr_mid], axis=0)
        table = rel_bias[layer]
        bias_p = _prompt_bias(table)

        def mixer_prompt(g):
            mod = [m6[g * gb:(g + 1) * gb, :, n, :] for n in range(N_MOD)]
            x_new, h2, lg, kt, vt, s_fin = _mixer_prompt(
                x_g[g][0], x_g[g][1], res_g[g], norm_mix[layer], mod[1], mod[0], w_in_bf, lbs[layer], bias_p,
                g_attn_out[layer], g_hgrn_out[layer], w_out_bf, mod[2], norm_ffn[layer], mod[4], mod[3],
                wr2, b_router[layer], tail, rec_chunk, n_s if g == n_groups - 1 else 0)
            x_g[g] = (x_new, 0)
            return h2, lg, kt, vt, s_fin, mod[5]

        def mixer_sample(rows):
            x, qa, ka, va, kt, vt, qh, kh, lf, vh, gh = _inproj(
                x_s[0], x_s[1], res_s, norm_mix[layer], mod_s[1], mod_s[0], w_in_bf, lbs[layer], ls, ls)
            o_att = _attn_sample(qa, cache_k[layer].reshape(bs, wc, ATT_W), cache_v[layer].reshape(bs, wc, ATT_W),
                                 ka, va, _sample_bias(table, ls, wc, lk_s))
            s0 = jnp.swapaxes(state_hgrn[layer].astype(F32), -1, -2)
            o_rec, s_new = _hgrn(qh, kh, lf, vh, s0, ls, ls)
            x_new, rows, lg = _merge(o_att, o_rec, gh, x[0], x[1], g_attn_out[layer], g_hgrn_out[layer], w_out_bf,
                                     mod_s[2], norm_ffn[layer], mod_s[4], mod_s[3], wr2, b_router[layer], rows, n_g)
            return (x_new, 0), rows, jnp.swapaxes(lg, 0, 1).reshape(N_EXPERTS, n_s), kt, vt, s_new

        dispatched, g2s, k_t, v_t, s_t = [], [], [], [], []
        for g in range(n_groups):
            h2, lg, kt, vt, s_fin, g2 = mixer_prompt(g)
            if g == n_groups - 1:
                x_s, h2, lg_s, ka_s, va_s, s_new = mixer_sample(h2)
                ks_l.append(ka_s.reshape(bs, ls, ATT_HEADS, ATT_DIM))
                vs_l.append(va_s.reshape(bs, ls, ATT_HEADS, ATT_DIM))
                ss_l.append(jnp.swapaxes(s_new, -1, -2))
                lg = jnp.concatenate([lg, lg_s], axis=1)
            dispatched.append(_moe_dispatch(h2, lg))
            g2s.append(g2)
            k_t.append(kt)
            v_t.append(vt)
            s_t.append(s_fin)
        kp_l.append(jnp.concatenate(k_t, axis=0).reshape(bp, tail, ATT_HEADS, ATT_DIM))
        vp_l.append(jnp.concatenate(v_t, axis=0).reshape(bp, tail, ATT_HEADS, ATT_DIM))
        sp_l.append(jnp.swapaxes(jnp.concatenate(s_t, axis=0), -1, -2))

        for g in range(n_groups):
            yg, gate = _moe_compute(layer, dispatched[g], w_e1, b_e1, w_e2, b_e2)
            res_g[g] = (yg, gate, g2s[g], 0)
            if g == n_groups - 1:
                res_s = (yg, gate[:TOP_K, n_g:].T.reshape(bs, ls, TOP_K), mod_s[5], n_g)

    y_prompt = None
    for g in range(n_groups):
        y_prompt = _final(x_g[g][0], res_g[g], norm_final, SEQ_TILE, y_prompt, g * gb, bp)
    y_sample = _final(x_s[0], res_s, norm_final, ls, None, 0, bs)
    return (y_prompt, y_sample, jnp.stack(kp_l), jnp.stack(vp_l), jnp.stack(sp_l),
            jnp.stack(ks_l), jnp.stack(vs_l), jnp.stack(ss_l))
```

```python
import functools

import jax
import jax.numpy as jnp
import numpy as np
from jax import lax
from jax.experimental import pallas as pl
from jax.experimental.pallas import tpu as pltpu
from jax.experimental.pallas import tpu_sc as plsc

F32 = jnp.float32
BF16 = jnp.bfloat16
I32 = jnp.int32
U32 = jnp.uint32

EPS = 1e-6
CHUNK = 64
LEFT_CHUNKS = 8
LEFT = LEFT_CHUNKS * CHUNK
MAX_REL = 2 * CHUNK
ATT_HEADS = 8
ATT_DIM = 64
ATT_W = ATT_HEADS * ATT_DIM
REC_HEADS = 4
REC_DIM = 128
REC_W = REC_HEADS * REC_DIM
N_EXPERTS = 32
TOP_K = 4
SWIGLU_ALPHA = 1.702
SWIGLU_LIMIT = 7.0
N_MOD = 6
NEG = -1e30
LOG2E = 1.4426950408889634

LANES = 128
ATT_QSUB = 128
ATT_WIN = LEFT + ATT_QSUB
SEQ_TILE = 512
REC_CHUNK = 128
MOE_ROWS = 512
ROUTE_TILE = 256
ROUTE_MAX_SUB = 4
PROMPT_GROUPS = 2
GATE_ROWS = 8
VMEM_LIMIT = 56 * 1024 * 1024
SC_WORKERS = 32
SC_SCATTER_WIN = 64
SC_GATHER_WIN = 32
SC_GATHER_BUFS = 4


def _cparams(*sem):
    return pltpu.CompilerParams(dimension_semantics=sem, vmem_limit_bytes=VMEM_LIMIT)


def _split3(x):
    hi = x.astype(BF16)
    r1 = x - hi.astype(F32)
    mid = r1.astype(BF16)
    lo = (r1 - mid.astype(F32)).astype(BF16)
    return hi, mid, lo


def _bits(x):
    return lax.bitcast_convert_type(x, U32)


def _pack_pairs(x):
    r = _bits(x.astype(BF16).astype(F32))
    half = x.shape[1] // 2
    return jnp.bitwise_or(jnp.bitwise_and(r[:, half:], jnp.uint32(0xFFFF0000)),
                          jnp.right_shift(r[:, :half], jnp.uint32(16)))


def _unpack_pairs(word):
    lo = lax.bitcast_convert_type(jnp.left_shift(word, jnp.uint32(16)), F32)
    hi = lax.bitcast_convert_type(jnp.bitwise_and(word, jnp.uint32(0xFFFF0000)), F32)
    return lo, hi


def _mod_kernel(c_ref, w_ref, b_ref, o_ref):
    c = c_ref[...]
    act = (c * jax.nn.sigmoid(c)).astype(BF16)
    o_ref[0] = jnp.dot(act, w_ref[0].astype(BF16), preferred_element_type=F32) + b_ref[0]


def _modulation(c_all, w_ada, b_ada):
    depth, d, n6 = w_ada.shape
    rows = c_all.shape[0]
    tn = 1536
    return pl.pallas_call(
        _mod_kernel,
        grid=(depth, n6 // tn),
        in_specs=[pl.BlockSpec((rows, d), lambda l, j: (0, 0)),
                  pl.BlockSpec((1, d, tn), lambda l, j: (l, 0, j)),
                  pl.BlockSpec((1, 1, tn), lambda l, j: (l, 0, j))],
        out_specs=pl.BlockSpec((1, rows, tn), lambda l, j: (l, 0, j)),
        out_shape=jax.ShapeDtypeStruct((depth, rows, n6), F32),
        compiler_params=_cparams("parallel", "parallel"),
        name="adaln_modulation",
    )(c_all, w_ada, b_ada.reshape(depth, 1, n6))


def _combine(yg_ref, gate_ref):
    g = gate_ref[0] if len(gate_ref.shape) == 3 else gate_ref[...].T
    lo = hi = None
    for k in range(TOP_K):
        l, h = _unpack_pairs(yg_ref[k])
        gk = g[:, k:k + 1]
        lo = l * gk if lo is None else lo + l * gk
        hi = h * gk if hi is None else hi + h * gk
    return jnp.concatenate([lo, hi], axis=-1)


def _inproj_kernel(has_res, *refs):
    if has_res:
        (x_ref, yg_ref, gate_ref, g2_ref, nw_ref, sc_ref, sh_ref, w_ref, lb_ref,
         xo_ref, qa_ref, ka_ref, va_ref, kt_ref, vt_ref, qh_ref, kh_ref, lf_ref, vh_ref, gh_ref) = refs
        x = x_ref[0] + g2_ref[0] * _combine(yg_ref, gate_ref)
        xo_ref[0] = x
    else:
        (x_ref, nw_ref, sc_ref, sh_ref, w_ref, lb_ref,
         qa_ref, ka_ref, va_ref, kt_ref, vt_ref, qh_ref, kh_ref, lf_ref, vh_ref, gh_ref) = refs
        x = x_ref[0]
    _project_rows(x, nw_ref, sc_ref, sh_ref, w_ref, lb_ref, qa_ref.at[0], ka_ref.at[0], va_ref.at[0],
                  kt_ref.at[0], vt_ref.at[0], qh_ref.at[0], kh_ref.at[0], lf_ref.at[0], vh_ref.at[0], gh_ref.at[0])


def _project_rows(x, nw_ref, sc_ref, sh_ref, w_ref, lb_ref,
                  qa_ref, ka_ref, va_ref, kt_ref, vt_ref, qh_ref, kh_ref, lf_ref, vh_ref, gh_ref):
    ms = jnp.mean(x * x, axis=-1, keepdims=True)
    h = (x * lax.rsqrt(ms + EPS) * nw_ref[...]) * (1.0 + sc_ref[0]) + sh_ref[0]
    hb = h.astype(BF16)

    def proj(g):
        return jnp.dot(hb, w_ref[:, g * ATT_W:(g + 1) * ATT_W], preferred_element_type=F32)

    qa_ref[...] = (proj(0) * (ATT_DIM ** -0.5)).astype(BF16)
    k = proj(1)
    ka_ref[...] = k.astype(BF16)
    kt_ref[...] = k
    v = proj(2)
    va_ref[...] = v.astype(BF16)
    vt_ref[...] = v
    qh = proj(3)
    qh_ref[...] = qh * jax.nn.sigmoid(qh)
    f = proj(4)
    lb = lb_ref[...]
    lf_ref[...] = jnp.log(lb + (1.0 - lb) * jax.nn.sigmoid(f))
    kh_ref[...] = (1.0 - lb) * jax.nn.sigmoid(-f)
    vh_ref[...] = proj(5).astype(BF16)
    gh_ref[...] = proj(6)


def _inproj(x, xb0, res, nw, sc, sh, w_bf, lb, ts, tail):
    _, s, d = x.shape
    b = sc.shape[0]
    pw = w_bf.shape[1]
    nj = s // ts
    n_tail = tail // ts
    has_res = res is not None
    tail_spec = pl.BlockSpec((1, ts, ATT_W), lambda i, j: (i, jnp.maximum(j - (nj - n_tail), 0), 0))
    ht = jax.ShapeDtypeStruct((b, tail, ATT_W), F32)
    row = pl.BlockSpec((1, ts, d), lambda i, j: (i, j, 0))
    per_b = pl.BlockSpec((1, 1, d), lambda i, j: (i, 0, 0))
    half = pl.BlockSpec((1, ts, ATT_W), lambda i, j: (i, j, 0))
    in_specs, args = [pl.BlockSpec((1, ts, d), lambda i, j: (i + xb0, j, 0))], [x]
    if has_res:
        yg, gate, g2, first = res
        blk0 = first // ts
        in_specs += [pl.BlockSpec((TOP_K, ts, d // 2), lambda i, j: (0, blk0 + i * nj + j, 0)),
                     pl.BlockSpec((1, ts, TOP_K), lambda i, j: (i, j, 0)), per_b]
        args += [yg, gate, g2]
    in_specs += [pl.BlockSpec((1, d), lambda i, j: (0, 0)), per_b, per_b,
                 pl.BlockSpec((d, pw), lambda i, j: (0, 0)),
                 pl.BlockSpec((1, REC_W), lambda i, j: (0, 0))]
    args += [nw.reshape(1, d), sc, sh, w_bf, lb.reshape(1, REC_W)]
    hs = jax.ShapeDtypeStruct((b, s, ATT_W), F32)
    hb = jax.ShapeDtypeStruct((b, s, ATT_W), BF16)
    out_shape = ([jax.ShapeDtypeStruct((b, s, d), F32)] if has_res else []) + [hb, hb, hb, ht, ht, hs, hs, hs, hb, hs]
    out_specs = ([row] if has_res else []) + [half] * 3 + [tail_spec] * 2 + [half] * 5
    outs = pl.pallas_call(
        functools.partial(_inproj_kernel, has_res),
        grid=(b, nj),
        in_specs=in_specs, out_specs=out_specs, out_shape=out_shape,
        compiler_params=_cparams("parallel", "arbitrary"),
        name="norm_inproj",
    )(*args)
    if has_res:
        return [(outs[0], 0)] + list(outs[1:])
    return [(x, xb0)] + list(outs)


def _attend(q, kw, vw, bias_ref, s_ref, col_thr):
    lq, lk = q.shape[0], kw.shape[0]
    lane = lax.broadcasted_iota(I32, (lq, LANES), 1)
    lo = lane < ATT_DIM
    nt = (((1,), (1,)), ((), ()))
    if col_thr is not None:
        keep = lax.broadcasted_iota(I32, (2 * lq, lk), 1) >= col_thr
    ms = []
    for p in range(ATT_HEADS // 2):
        sl = slice(p * LANES, (p + 1) * LANES)
        qp = q[:, sl]
        z = jnp.zeros_like(qp)
        lhs = jnp.concatenate([jnp.where(lo, qp, z), jnp.where(lo, z, qp)], axis=0)
        s = lax.dot_general(lhs, kw[:, sl], nt, preferred_element_type=F32) + bias_ref[p]
        if col_thr is not None:
            s = jnp.where(keep, s, NEG)
        ms.append(jnp.max(s, axis=-1, keepdims=True))
        s_ref[p] = s
    outs = []
    for p in range(ATT_HEADS // 2):
        sl = slice(p * LANES, (p + 1) * LANES)
        e = jnp.exp(s_ref[p] - ms[p])
        l = jnp.sum(e, axis=-1, keepdims=True)
        o = jnp.dot(e.astype(BF16), vw[:, sl], preferred_element_type=F32) / l
        outs.append(jnp.where(lo, o[:lq], o[lq:]))
    return jnp.concatenate(outs, axis=-1)


def _band_attention(q_ref, kw_ref, vw_ref, bias_ref, s_ref, o_ref, first_tile):
    ts = q_ref.shape[0]

    def body(masked, r, carry):
        r0 = pl.multiple_of(r * ATT_QSUB, ATT_QSUB)
        q = q_ref[pl.ds(r0, ATT_QSUB), :]
        kw = kw_ref[pl.ds(r0, ATT_WIN), :]
        vw = vw_ref[pl.ds(r0, ATT_WIN), :]
        o_ref[pl.ds(r0, ATT_QSUB), :] = _attend(q, kw, vw, bias_ref, s_ref, (ts - r0) if masked else None)
        return carry

    @pl.when(first_tile)
    def _():
        lax.fori_loop(0, ts // ATT_QSUB, functools.partial(body, True), 0)

    @pl.when(jnp.logical_not(first_tile))
    def _():
        lax.fori_loop(0, ts // ATT_QSUB, functools.partial(body, False), 0)


def _attn_sample_kernel(q_ref, ck_ref, cv_ref, k_ref, v_ref, bias_ref, o_ref, kw_ref, vw_ref, s_ref):
    wc, l, lk = ck_ref.shape[1], k_ref.shape[1], kw_ref.shape[0]
    for dst, cache, new in ((kw_ref, ck_ref, k_ref), (vw_ref, cv_ref, v_ref)):
        dst[0:wc, :] = cache[0].astype(BF16)
        dst[wc:wc + l, :] = new[0]
        dst[wc + l:lk, :] = jnp.zeros((lk - wc - l, ATT_W), BF16)
    o_ref[0] = _attend(q_ref[0], kw_ref[...], vw_ref[...], bias_ref, s_ref, None)


def _attn_sample(qa, cache_k, cache_v, ka, va, bias):
    b, l, w = qa.shape
    wc = cache_k.shape[1]
    lk = bias.shape[2]
    new = pl.BlockSpec((1, l, w), lambda i: (i, 0, 0))
    old = pl.BlockSpec((1, wc, w), lambda i: (i, 0, 0))
    return pl.pallas_call(
        _attn_sample_kernel,
        grid=(b,),
        in_specs=[new, old, old, new, new, pl.BlockSpec(bias.shape, lambda i: (0, 0, 0))],
        out_specs=new,
        out_shape=jax.ShapeDtypeStruct((b, l, w), F32),
        scratch_shapes=[pltpu.VMEM((lk, w), BF16), pltpu.VMEM((lk, w), BF16),
                        pltpu.VMEM((ATT_HEADS // 2, 2 * l, lk), F32)],
        compiler_params=_cparams("parallel"),
        name="cache_attention",
    )(qa, cache_k, cache_v, ka, va, bias)


def _toeplitz_bias(table, lq, lk, offset, valid):
    m = lq + lk - 1
    k = np.arange(m)
    diff = np.where(k < lk, k, k - m)
    g = table[:, np.clip(offset - diff, -MAX_REL, MAX_REL) + MAX_REL]
    skew = jnp.tile(g, (1, lq))[:, :lq * (m - 1)].reshape(-1, lq, m - 1)[:, :, :lk]
    bias = jnp.where(jnp.asarray(valid)[None], skew, NEG).astype(F32)
    return bias.reshape(ATT_HEADS // 2, 2 * lq, lk)


def _prompt_bias(table):
    t = np.arange(ATT_QSUB)[:, None]
    j = np.arange(ATT_WIN)[None, :]
    start = (t // CHUNK) * CHUNK
    return _toeplitz_bias(table, ATT_QSUB, ATT_WIN, LEFT, (j >= start) & (j < start + LEFT + CHUNK))


def _sample_bias(table, l, w, lk):
    valid = np.broadcast_to(np.arange(lk)[None, :] < w + l, (l, lk))
    return _toeplitz_bias(table, l, lk, w, valid)


def _neg_abs(x):
    return lax.bitcast_convert_type(jnp.bitwise_or(_bits(x), jnp.uint32(0x80000000)), F32)


def _hgrn_kernel(chunk, q_ref, k_ref, lf_ref, v_ref, s0_ref, o_ref, sf_ref, st_ref):
    i = pl.program_id(1)

    @pl.when(i == 0)
    def _():
        st_ref[...] = s0_ref[0]

    _hgrn_rows(chunk, q_ref.at[0], k_ref.at[0], lf_ref.at[0], v_ref.at[0], o_ref.at[0], st_ref)

    @pl.when(i == pl.num_programs(1) - 1)
    def _():
        sf_ref[0] = st_ref[...]


def _hgrn_rows(chunk, q_ref, k_ref, lf_ref, v_ref, o_ref, st_ref):
    ts = q_ref.shape[0]
    n_chunks = ts // chunk

    row = lax.broadcasted_iota(I32, (chunk, chunk), 0)
    colm = lax.broadcasted_iota(I32, (chunk, chunk), 1)
    tri = (colm <= row).astype(BF16)
    xor = jnp.bitwise_xor(row, colm)
    lower = colm < row
    rid = lax.broadcasted_iota(I32, (chunk, REC_W), 0)

    levels = []
    b = chunk // 2
    while b >= 1:
        levels.append(b)
        b //= 2
    pair_mask = [xor == 0] + [lower & (xor >= b) & (xor < 2 * b) for b in levels]
    upper = [jnp.bitwise_and(rid, b) != 0 for b in levels]
    ph4 = jnp.bitwise_and(rid, 3)
    odd = jnp.bitwise_and(rid, 1) == 1
    nt = (((1,), (1,)), ((), ()))
    tn = (((0,), (0,)), ((), ()))
    sls = [slice(h * REC_DIM, (h + 1) * REC_DIM) for h in range(REC_HEADS)]

    def boundary(cum, b):
        if 2 * b >= 8:
            pieces = [jnp.broadcast_to(cum[m * 2 * b + b - 1:m * 2 * b + b, :], (2 * b, REC_W))
                      for m in range(chunk // (2 * b))]
            return pieces[0] if len(pieces) == 1 else jnp.concatenate(pieces, axis=0)
        if b == 2:
            up1 = pltpu.roll(cum, chunk - 1, 0)
            dn1 = pltpu.roll(cum, 1, 0)
            dn2 = pltpu.roll(cum, 2, 0)
            return jnp.where(ph4 == 0, up1, jnp.where(ph4 == 1, cum, jnp.where(ph4 == 2, dn1, dn2)))
        return jnp.where(odd, pltpu.roll(cum, 1, 0), cum)

    def one_chunk(c, carry):
        r0 = pl.multiple_of(c * chunk, chunk)
        rows = pl.ds(r0, chunk)
        q = q_ref[rows, :]
        k = k_ref[rows, :]
        v = v_ref[rows, :]
        lf = lf_ref[rows, :]
        cum = sum(jnp.dot(tri, part, preferred_element_type=F32) for part in _split3(lf)) * LOG2E
        last = cum[chunk - 1:chunk, :]
        q_in = (q * jnp.exp2(cum)).astype(BF16)
        k_out = (k * jnp.exp2(last - cum)).astype(BF16)
        decay = jnp.exp2(last)
        qb = q.astype(BF16)
        kb = k.astype(BF16)
        zs = []
        for n, b in enumerate(levels):
            w = jnp.exp2(_neg_abs(cum - boundary(cum, b)))
            zs.append((jnp.where(upper[n], q, k) * w).astype(BF16))
        sts = [st_ref[h] for h in range(REC_HEADS)]
        d0 = [lax.dot_general(qb[:, sl], kb[:, sl], nt, preferred_element_type=F32) for sl in sls]
        dl = [[lax.dot_general(z[:, sl], z[:, sl], nt, preferred_element_type=F32) for z in zs] for sl in sls]
        oi = [lax.dot_general(q_in[:, sl], st.astype(BF16), nt, preferred_element_type=F32)
              for sl, st in zip(sls, sts)]
        upd = [lax.dot_general(v[:, sl], k_out[:, sl], tn, preferred_element_type=F32) for sl in sls]
        outs = []
        for h, sl in enumerate(sls):
            a = jnp.where(pair_mask[0], d0[h], 0.0)
            for n in range(len(levels)):
                a = jnp.where(pair_mask[n + 1], dl[h][n], a)
            outs.append(jnp.dot(a.astype(BF16), v[:, sl], preferred_element_type=F32) + oi[h])
        o_ref[rows, :] = jnp.concatenate(outs, axis=-1)
        for h, sl in enumerate(sls):
            st_ref[h] = decay[:, sl] * sts[h] + upd[h]
        return carry

    lax.fori_loop(0, n_chunks, one_chunk, 0, unroll=2 if n_chunks % 2 == 0 else 1)


def _hgrn(q, k, lf, v, s0_t, ts, chunk):
    b, s, w = q.shape
    blk = pl.BlockSpec((1, ts, w), lambda i, j: (i, j, 0))
    st = pl.BlockSpec((1, REC_HEADS, REC_DIM, REC_DIM), lambda i, j: (i, 0, 0, 0))
    return pl.pallas_call(
        functools.partial(_hgrn_kernel, chunk),
        grid=(b, s // ts),
        in_specs=[blk, blk, blk, blk, st],
        out_specs=[blk, st],
        out_shape=[jax.ShapeDtypeStruct((b, s, w), F32),
                   jax.ShapeDtypeStruct((b, REC_HEADS, REC_DIM, REC_DIM), F32)],
        scratch_shapes=[pltpu.VMEM((REC_HEADS, REC_DIM, REC_DIM), F32)],
        compiler_params=_cparams("parallel", "arbitrary"),
        name="hgrn_recurrence",
    )(q, k, lf, v, s0_t)


def _head_rms(x, width):
    lane = lax.broadcasted_iota(I32, (x.shape[0], LANES), 1)
    lo = lane < ATT_DIM
    outs = []
    for p in range(x.shape[1] // LANES):
        xp = x[:, p * LANES:(p + 1) * LANES]
        sq = xp * xp
        tot = jnp.sum(sq, axis=-1, keepdims=True)
        if width == LANES:
            scale = lax.rsqrt(tot * (1.0 / LANES) + EPS)
        else:
            s_lo = jnp.sum(jnp.where(lo, sq, 0.0), axis=-1, keepdims=True)
            scale = jnp.where(lo, lax.rsqrt(s_lo * (1.0 / ATT_DIM) + EPS),
                              lax.rsqrt((tot - s_lo) * (1.0 / ATT_DIM) + EPS))
        outs.append(xp * scale)
    return jnp.concatenate(outs, axis=-1)


def _merge_rows(oa, orr, gh, x, ga_ref, gr_ref, wo_ref, g1_ref, nf_ref, sc_ref, sh_ref, wr_ref, br_ref):
    a = _head_rms(oa, ATT_DIM) * ga_ref[...]
    r = _head_rms(orr, REC_DIM) * gr_ref[...] * (gh * jax.nn.sigmoid(gh))
    mix = jnp.dot(a.astype(BF16), wo_ref[0:ATT_W, :], preferred_element_type=F32)
    mix += jnp.dot(r.astype(BF16), wo_ref[ATT_W:ATT_W + REC_W, :], preferred_element_type=F32)
    x = x + g1_ref[0] * mix
    ms = jnp.mean(x * x, axis=-1, keepdims=True)
    h2 = (x * lax.rsqrt(ms + EPS) * nf_ref[...]) * (1.0 + sc_ref[0]) + sh_ref[0]
    hh = h2.astype(BF16)
    words = _pack_pairs(h2)
    hm = (h2 - hh.astype(F32)).astype(BF16)
    nt = (((1,), (1,)), ((), ()))
    l1 = lax.dot_general(wr_ref[...], hh, nt, preferred_element_type=F32)
    l2 = lax.dot_general(wr_ref[0:N_EXPERTS, :], hm, nt, preferred_element_type=F32)
    lg = l1[0:N_EXPERTS] + l1[N_EXPERTS:2 * N_EXPERTS] + l2 + br_ref[...]
    return x, words, lg


def _merge_kernel(oa_ref, or_ref, gh_ref, x_ref, ga_ref, gr_ref, wo_ref, g1_ref, nf_ref, sc_ref, sh_ref,
                  wr_ref, br_ref, rows_hbm, xo_ref, h2_ref, lg_ref):
    del rows_hbm
    x, words, lg = _merge_rows(oa_ref[0], or_ref[0], gh_ref[0], x_ref[0], ga_ref, gr_ref, wo_ref, g1_ref,
                               nf_ref, sc_ref, sh_ref, wr_ref, br_ref)
    xo_ref[0] = x
    h2_ref[...] = words
    lg_ref[0] = lg


def _merge(o_att, o_rec, gh, x, xb0, g_att, g_rec, wo_bf, g1, nf, sc, sh, wr2, br, rows, row0):
    _, ts, d = x.shape
    b = o_att.shape[0]
    row = pl.BlockSpec((1, ts, d), lambda i: (i, 0, 0))
    xrow = pl.BlockSpec((1, ts, d), lambda i: (i + xb0, 0, 0))
    half = pl.BlockSpec((1, ts, ATT_W), lambda i: (i, 0, 0))
    per_b = pl.BlockSpec((1, 1, d), lambda i: (i, 0, 0))
    const2 = lambda shp: pl.BlockSpec(shp, lambda i: (0,) * len(shp))
    blk0 = row0 // ts
    return pl.pallas_call(
        _merge_kernel,
        grid=(b,),
        in_specs=[half, half, half, xrow, const2((1, ATT_W)), const2((1, REC_W)), const2(wo_bf.shape),
                  per_b, const2((1, d)), per_b, per_b, const2(wr2.shape), const2((N_EXPERTS, 1)),
                  pl.BlockSpec(memory_space=pl.ANY)],
        out_specs=[row, pl.BlockSpec((ts, d // 2), lambda i: (blk0 + i, 0)),
                   pl.BlockSpec((1, N_EXPERTS, ts), lambda i: (i, 0, 0))],
        out_shape=[jax.ShapeDtypeStruct((b, ts, d), F32), jax.ShapeDtypeStruct(rows.shape, U32),
                   jax.ShapeDtypeStruct((b, N_EXPERTS, ts), F32)],
        input_output_aliases={13: 1},
        compiler_params=_cparams("parallel"),
        name="merge_outproj_router",
    )(o_att, o_rec, gh, x, g_att.reshape(1, ATT_W), g_rec.reshape(1, REC_W), wo_bf, g1, nf.reshape(1, d),
      sc, sh, wr2, br.reshape(N_EXPERTS, 1), rows)


def _mixer_kernel(has_res, chunk, *refs):
    n_in = 4 if has_res else 1
    x_ref = refs[0]
    (nw_ref, sc1_ref, sh1_ref, lb_ref, ga_ref, gr_ref, g1_ref, nf_ref, sc2_ref, sh2_ref, wr_ref, br_ref,
     win_hbm, wout_hbm, bias_hbm,
     xo_ref, h2_ref, lg_ref, kt_ref, vt_ref, sf_ref,
     win_v, wout_v, bias_v, q_s, kw_s, vw_s, s_s, qh_s, kh_s, lf_s, vh_s, gh_s, oa_s, or_s, st_s, sem) = refs[n_in:]
    j = pl.program_id(1)
    ts = x_ref.shape[1]

    @pl.when(j == 0)
    def _():
        copies = [pltpu.make_async_copy(src, dst, sem.at[n])
                  for n, (src, dst) in enumerate(((win_hbm, win_v), (wout_hbm, wout_v), (bias_hbm, bias_v)))]
        for cp in copies:
            cp.start()
        for cp in copies:
            cp.wait()
        st_s[...] = jnp.zeros_like(st_s)
        kw_s[0:ts, :] = jnp.zeros((ts, ATT_W), BF16)
        vw_s[0:ts, :] = jnp.zeros((ts, ATT_W), BF16)

    if has_res:
        yg_ref, gate_ref, g2_ref = refs[1:4]
        x = x_ref[0] + g2_ref[0] * _combine(yg_ref, gate_ref)
        xo_ref[0] = x
    else:
        x = x_ref[0]
    cur = pl.ds(ts, ts)
    _project_rows(x, nw_ref, sc1_ref, sh1_ref, win_v, lb_ref, q_s, kw_s.at[cur], vw_s.at[cur],
                  kt_ref.at[0], vt_ref.at[0], qh_s, kh_s, lf_s, vh_s, gh_s)
    _band_attention(q_s, kw_s, vw_s, bias_v, s_s, oa_s, j == 0)
    _hgrn_rows(chunk, qh_s, kh_s, lf_s, vh_s, or_s, st_s)
    x_in = xo_ref[0] if has_res else x_ref[0]
    x_new, words, lg = _merge_rows(oa_s[...], or_s[...], gh_s[...], x_in, ga_ref, gr_ref, wout_v, g1_ref,
                                   nf_ref, sc2_ref, sh2_ref, wr_ref, br_ref)
    xo_ref[0] = x_new
    h2_ref[...] = words
    lg_ref[...] = lg
    kw_s[0:ts, :] = kw_s[ts:2 * ts, :]
    vw_s[0:ts, :] = vw_s[ts:2 * ts, :]

    @pl.when(j == pl.num_programs(1) - 1)
    def _():
        sf_ref[0] = st_s[...]


def _mixer_prompt(x, xb0, res, nw, sc1, sh1, w_in_bf, lb, bias, g_att, g_rec, w_out_bf, g1, nf, sc2, sh2,
                  wr2, br, tail, chunk, extra_rows):
    _, s, d = x.shape
    b = sc1.shape[0]
    ts = LEFT
    nj = s // ts
    n_tail = tail // ts
    has_res = res is not None
    row = pl.BlockSpec((1, ts, d), lambda i, j: (i, j, 0))
    per_b = pl.BlockSpec((1, 1, d), lambda i, j: (i, 0, 0))
    const = lambda shp: pl.BlockSpec(shp, lambda i, j: (0,) * len(shp))
    hbm = pl.BlockSpec(memory_space=pl.ANY)
    in_specs, args = [pl.BlockSpec((1, ts, d), lambda i, j: (i + xb0, j, 0))], [x]
    if has_res:
        yg, gate, g2, first = res
        blk0 = first // ts
        in_specs += [pl.BlockSpec((TOP_K, ts, d // 2), lambda i, j: (0, blk0 + i * nj + j, 0)),
                     pl.BlockSpec((GATE_ROWS, ts), lambda i, j: (0, blk0 + i * nj + j)), per_b]
        args += [yg, gate, g2]
    in_specs += [const((1, d)), per_b, per_b, const((1, REC_W)), const((1, ATT_W)), const((1, REC_W)),
                 per_b, const((1, d)), per_b, per_b, const(wr2.shape), const((N_EXPERTS, 1)), hbm, hbm, hbm]
    args += [nw.reshape(1, d), sc1, sh1, lb.reshape(1, REC_W), g_att.reshape(1, ATT_W), g_rec.reshape(1, REC_W),
             g1, nf.reshape(1, d), sc2, sh2, wr2, br.reshape(N_EXPERTS, 1), w_in_bf, w_out_bf, bias]
    tail_spec = pl.BlockSpec((1, ts, ATT_W), lambda i, j: (i, jnp.maximum(j - (nj - n_tail), 0), 0))
    state_spec = pl.BlockSpec((1, REC_HEADS, REC_DIM, REC_DIM), lambda i, j: (i, 0, 0, 0))
    tile = lambda dt: pltpu.VMEM((ts, ATT_W), dt)
    return pl.pallas_call(
        functools.partial(_mixer_kernel, has_res, chunk),
        grid=(b, nj),
        in_specs=in_specs,
        out_specs=[row, pl.BlockSpec((ts, d // 2), lambda i, j: (i * nj + j, 0)),
                   pl.BlockSpec((N_EXPERTS, ts), lambda i, j: (0, i * nj + j)), tail_spec, tail_spec, state_spec],
        out_shape=[jax.ShapeDtypeStruct((b, s, d), F32), jax.ShapeDtypeStruct((b * s + extra_rows, d // 2), U32),
                   jax.ShapeDtypeStruct((N_EXPERTS, b * s), F32),
                   jax.ShapeDtypeStruct((b, tail, ATT_W), F32), jax.ShapeDtypeStruct((b, tail, ATT_W), F32),
                   jax.ShapeDtypeStruct((b, REC_HEADS, REC_DIM, REC_DIM), F32)],
        scratch_shapes=[pltpu.VMEM(w_in_bf.shape, BF16), pltpu.VMEM(w_out_bf.shape, BF16), pltpu.VMEM(bias.shape, F32),
                        tile(BF16), pltpu.VMEM((2 * ts, ATT_W), BF16), pltpu.VMEM((2 * ts, ATT_W), BF16),
                        pltpu.VMEM((ATT_HEADS // 2, 2 * ATT_QSUB, ATT_WIN), F32),
                        tile(F32), tile(F32), tile(F32), tile(BF16), tile(F32), tile(F32), tile(F32),
                        pltpu.VMEM((REC_HEADS, REC_DIM, REC_DIM), F32), pltpu.SemaphoreType.DMA((3,))],
        compiler_params=_cparams("parallel", "arbitrary"),
        name="prompt_mixer",
    )(*args)


def _route_kernel(lg_ref, idx_ref, gate_ref, rank_ref, cnt_ref, run_ref):
    i = pl.program_id(0)
    t = ROUTE_TILE

    @pl.when(i == 0)
    def _():
        run_ref[...] = jnp.zeros_like(run_ref)

    row = lax.broadcasted_iota(I32, (N_EXPERTS, t), 0)
    earlier = (lax.broadcasted_iota(I32, (t, t), 0) < lax.broadcasted_iota(I32, (t, t), 1)).astype(BF16)
    run = run_ref[:, 0:1]
    for sb in range(lg_ref.shape[1] // t):
        cols = slice(sb * t, (sb + 1) * t)
        x = lg_ref[:, cols]
        vals, idxs, hits = [], [], []
        for _ in range(TOP_K):
            m = jnp.max(x, axis=0, keepdims=True)
            ik = jnp.min(jnp.where(x == m, row, N_EXPERTS), axis=0, keepdims=True)
            hit = row == ik
            x = jnp.where(hit, -jnp.inf, x)
            vals.append(m)
            idxs.append(ik)
            hits.append(hit)
        es = [jnp.exp(v - vals[0]) for v in vals]
        tot = es[0] + es[1] + es[2] + es[3]
        gate_ref[:, cols] = jnp.concatenate([e / tot for e in es] + [jnp.zeros((GATE_ROWS - TOP_K, t), F32)], axis=0)
        idx_ref[:, cols] = jnp.concatenate(idxs, axis=0)
        chosen = (hits[0] | hits[1] | hits[2] | hits[3])
        onehot = jnp.where(chosen, 1.0, 0.0)
        before = jnp.dot(onehot.astype(BF16), earlier, preferred_element_type=F32) + run
        rank_ref[:, cols] = jnp.concatenate(
            [jnp.sum(jnp.where(h, before, 0.0), axis=0, keepdims=True) for h in hits], axis=0).astype(I32)
        run = run + jnp.sum(onehot, axis=1, keepdims=True)
    run_ref[...] = jnp.broadcast_to(run, run_ref.shape)
    cnt_ref[...] = jnp.broadcast_to(run, cnt_ref.shape)


def _route(lg_t):
    e, n = lg_t.shape
    t = max(m * ROUTE_TILE for m in range(1, ROUTE_MAX_SUB + 1) if n % (m * ROUTE_TILE) == 0)
    tok = pl.BlockSpec((TOP_K, t), lambda i: (0, i))
    idx, gate, rank, cnt = pl.pallas_call(
        _route_kernel,
        grid=(n // t,),
        in_specs=[pl.BlockSpec((e, t), lambda i: (0, i))],
        out_specs=[tok, pl.BlockSpec((GATE_ROWS, t), lambda i: (0, i)), tok, pl.BlockSpec((e, LANES), lambda i: (0, 0))],
        out_shape=[jax.ShapeDtypeStruct((TOP_K, n), I32), jax.ShapeDtypeStruct((GATE_ROWS, n), F32),
                   jax.ShapeDtypeStruct((TOP_K, n), I32), jax.ShapeDtypeStruct((e, LANES), F32)],
        scratch_shapes=[pltpu.VMEM((e, LANES), F32)],
        compiler_params=_cparams("arbitrary"),
        name="moe_route",
    )(lg_t)
    counts = cnt[:, 0].astype(I32)
    padded = (counts + MOE_ROWS - 1) // MOE_ROWS * MOE_ROWS
    pad_end = jnp.cumsum(padded)
    pad_start = pad_end - padded
    m = n * TOP_K
    nb = (m + N_EXPERTS * (MOE_ROWS - 1) + MOE_ROWS - 1) // MOE_ROWS
    block_start = jnp.arange(nb, dtype=I32) * MOE_ROWS
    block_expert = jnp.minimum(jnp.sum((pad_end[None, :] <= block_start[:, None]).astype(I32), axis=1), N_EXPERTS - 1)
    experts = jnp.arange(N_EXPERTS, dtype=I32)
    in_block = block_start[:, None] - pad_start[None, :]
    mine = block_expert[:, None] == experts[None, :]
    filled = jnp.sum(jnp.where(mine, jnp.clip(counts[None, :] - in_block, 0, MOE_ROWS), 0), axis=1)
    owns = jnp.any(block_expert[:, None] == experts[None, :], axis=0)
    later = owns[None, :] & (experts[None, :] > block_expert[:, None])
    nxt = jnp.min(jnp.where(later, experts[None, :], N_EXPERTS), axis=1)
    nxt = jnp.where(nxt == N_EXPERTS, -1, nxt)
    run = jnp.sum((owns[None, :] & (experts[None, :] < block_expert[:, None])).astype(I32), axis=1)
    block_info = jnp.concatenate([block_expert, nxt, run % 2, filled]).astype(I32)
    start_of = sum(jnp.where(idx == e, pad_start[e], 0) for e in range(N_EXPERTS))
    dest = start_of + rank
    return gate, dest, block_info, nb * MOE_ROWS


def _windows(dest, w, multiple, fill):
    kk, n = dest.shape
    nwin = n // w
    per = -(-nwin // SC_WORKERS)
    per = -(-per // multiple) * multiple
    idx = dest.reshape(kk, nwin, w).transpose(1, 0, 2)
    pad = jnp.broadcast_to(fill[None, None, :], (SC_WORKERS * per - nwin, kk, w)).astype(I32)
    return jnp.concatenate([idx, pad], axis=0)


def _sc_scatter_rows(x, idx, p_rows):
    n, c = x.shape
    nwin, kk, w = idx.shape
    per = nwin // SC_WORKERS
    nreal = n // w
    mesh = plsc.VectorSubcoreMesh(core_axis_name="c", subcore_axis_name="s")

    @functools.partial(
        pl.kernel, mesh=mesh,
        out_type=jax.ShapeDtypeStruct((p_rows, c), x.dtype),
        scratch_types=[pltpu.VMEM((per, kk, w), I32), pltpu.VMEM((2, w, c), x.dtype),
                       pltpu.SemaphoreType.DMA((2,)), pltpu.SemaphoreType.DMA((2,))],
        name="sc_scatter_rows",
    )
    def k(x_hbm, idx_hbm, out_hbm, idx_v, rows_v, lsem, ssem):
        wid = lax.axis_index("s") * 2 + lax.axis_index("c")
        g0 = wid * per
        pltpu.sync_copy(idx_hbm.at[pl.ds(g0, per)], idx_v)

        def load(j, b):
            g = jnp.minimum(g0 + j, nreal - 1)
            return pltpu.make_async_copy(x_hbm.at[pl.ds(g * w, w)], rows_v.at[b], lsem.at[b])

        def scat(j, b, q):
            return pltpu.make_async_copy(rows_v.at[b], out_hbm.at[idx_v.at[j, q]], ssem.at[b])

        @pl.loop(0, per + 2, step=2)
        def _(j0):
            for b in range(2):
                j = j0 + b

                @pl.when(jnp.logical_and(j >= 2, j < per + 2))
                def _():
                    for q in range(kk):
                        scat(j - 2, b, q).wait()

                @pl.when(j < per)
                def _():
                    load(j, b).start()

                @pl.when(jnp.logical_and(j >= 1, j < per + 1))
                def _():
                    load(j - 1, 1 - b).wait()
                    for q in range(kk):
                        scat(j - 1, 1 - b, q).start()

    return k(x, idx)


def _sc_gather_rows(y, idx):
    p, c = y.shape
    kk, n = idx.shape
    w = SC_GATHER_WIN
    nb = SC_GATHER_BUFS
    wins = LANES // w
    group = wins * kk
    assert group % nb == 0 and group >= nb and n % LANES == 0
    n_units = n // LANES
    max_units = -(-n_units // SC_WORKERS)
    mesh = plsc.VectorSubcoreMesh(core_axis_name="c", subcore_axis_name="s")

    @functools.partial(
        pl.kernel, mesh=mesh,
        out_type=jax.ShapeDtypeStruct((kk, n, c), y.dtype),
        scratch_types=[pltpu.VMEM((2, kk, LANES), I32), pltpu.VMEM((nb, w, c), y.dtype),
                       pltpu.SemaphoreType.DMA((nb,)), pltpu.SemaphoreType.DMA((nb,))],
        name="sc_gather_rows",
    )
    def k(y_hbm, idx_hbm, out_hbm, idx_v, rows_v, gsem, wsem):
        wid = lax.axis_index("s") * 2 + lax.axis_index("c")
        my_units = (n_units - wid + SC_WORKERS - 1) // SC_WORKERS
        n_items = my_units * group

        def gath(slot, jj, q, b):
            return pltpu.make_async_copy(y_hbm.at[idx_v.at[slot, q, pl.ds(jj * w, w)]], rows_v.at[b], gsem.at[b])

        def wr(i, b):
            row = (wid + (i // group) * SC_WORKERS) * LANES + ((i % group) // kk) * w
            return pltpu.make_async_copy(rows_v.at[b], out_hbm.at[i % kk, pl.ds(row, w)], wsem.at[b])

        @pl.loop(0, max_units + 1)
        def _(gi):
            slot = gi % 2

            @pl.when(gi < my_units)
            def _():
                pltpu.sync_copy(idx_hbm.at[:, pl.ds((wid + gi * SC_WORKERS) * LANES, LANES)], idx_v.at[slot])

            for l in range(group):
                i = gi * group + l
                b = l % nb
                bw = (l - (nb - 1)) % nb

                @pl.when(jnp.logical_and(i >= nb, i < n_items + nb))
                def _():
                    wr(i - nb, b).wait()

                @pl.when(i < n_items)
                def _():
                    gath(slot, l // kk, l % kk, b).start()

                @pl.when(jnp.logical_and(i >= nb - 1, i < n_items + nb - 1))
                def _():
                    gath(slot, 0, 0, bw).wait()
                    wr(i - (nb - 1), bw).start()

    return k(y, idx)


def _moe_kernel(layer, info_ref, x_ref, w1_hbm, b1_ref, w2_hbm, b2_ref, o_ref,
                w1f_ref, w2f_ref, w1b_ref, w2b_ref, sem):
    i = pl.program_id(0)
    nb = pl.num_programs(0)
    expert = info_ref[i]
    nxt = info_ref[nb + i]
    slot = info_ref[2 * nb + i]
    first = jnp.logical_or(i == 0, expert != info_ref[jnp.maximum(i - 1, 0)])

    def fetch(e, s):
        return (pltpu.make_async_copy(w1_hbm.at[layer, e], w1f_ref.at[s], sem.at[0, s]),
                pltpu.make_async_copy(w2_hbm.at[layer, e], w2f_ref.at[s], sem.at[1, s]))

    @pl.when(i == 0)
    def _():
        for cp in fetch(expert, slot):
            cp.start()

    @pl.when(first)
    def _():
        for cp in fetch(expert, slot):
            cp.wait()

        @pl.when(nxt >= 0)
        def _():
            for cp in fetch(nxt, 1 - slot):
                cp.start()

        w1b_ref[...] = w1f_ref[slot].astype(BF16)
        w2b_ref[...] = w2f_ref[slot].astype(BF16)

    filled = info_ref[3 * nb + i]
    half = x_ref.shape[0] // 2

    @pl.when(filled > half)
    def _():
        _expert_block(x_ref, w1b_ref, b1_ref, w2b_ref, b2_ref, o_ref)

    @pl.when(jnp.logical_and(filled > 0, filled <= half))
    def _():
        _expert_block(x_ref.at[0:half], w1b_ref, b1_ref, w2b_ref, b2_ref, o_ref.at[0:half])


def _expert_block(x_ref, w1b_ref, b1_ref, w2b_ref, b2_ref, o_ref):
    lo, hi = _unpack_pairs(x_ref[...])
    x = jnp.concatenate([lo.astype(BF16), hi.astype(BF16)], axis=-1)
    f = w2b_ref.shape[0]
    fc = 512
    acc = None
    for c in range(f // fc):
        gt = jnp.dot(x, w1b_ref[:, c * fc:(c + 1) * fc], preferred_element_type=F32) + b1_ref[0, 0, :, c * fc:(c + 1) * fc]
        up = jnp.dot(x, w1b_ref[:, f + c * fc:f + (c + 1) * fc], preferred_element_type=F32) + b1_ref[0, 0, :, f + c * fc:f + (c + 1) * fc]
        gt = jnp.minimum(gt, SWIGLU_LIMIT)
        up = jnp.clip(up, -SWIGLU_LIMIT, SWIGLU_LIMIT)
        act = (gt * jax.nn.sigmoid(SWIGLU_ALPHA * gt) * (up + 1.0)).astype(BF16)
        part = jnp.dot(act, w2b_ref[c * fc:(c + 1) * fc, :], preferred_element_type=F32)
        acc = part if acc is None else acc + part
    o_ref[...] = _pack_pairs(acc + b2_ref[0, 0])


def _moe_experts(layer, block_expert, xs, p_rows, w1, b1, w2, b2):
    _, e, d, f2 = w1.shape
    f = f2 // 2
    nb = p_rows // MOE_ROWS
    grid_spec = pltpu.PrefetchScalarGridSpec(
        num_scalar_prefetch=1,
        grid=(nb,),
        in_specs=[pl.BlockSpec((MOE_ROWS, d // 2), lambda i, be: (i, 0)),
                  pl.BlockSpec(memory_space=pl.ANY),
                  pl.BlockSpec((1, 1, 1, f2), lambda i, be: (layer, be[i], 0, 0)),
                  pl.BlockSpec(memory_space=pl.ANY),
                  pl.BlockSpec((1, 1, 1, d), lambda i, be: (layer, be[i], 0, 0))],
        out_specs=pl.BlockSpec((MOE_ROWS, d // 2), lambda i, be: (i, 0)),
        scratch_shapes=[pltpu.VMEM((2, d, f2), F32), pltpu.VMEM((2, f, d), F32),
                        pltpu.VMEM((d, f2), BF16), pltpu.VMEM((f, d), BF16),
                        pltpu.SemaphoreType.DMA((2, 2))],
    )
    return pl.pallas_call(
        functools.partial(_moe_kernel, layer),
        grid_spec=grid_spec,
        out_shape=jax.ShapeDtypeStruct((p_rows, d // 2), U32),
        compiler_params=_cparams("arbitrary"),
        name="moe_experts",
    )(block_expert, xs, w1, b1.reshape(b1.shape[0], e, 1, f2), w2, b2.reshape(b2.shape[0], e, 1, d))


def _moe_dispatch(h2_words, lg_t):
    gate, dest, block_info, p_rows = _route(lg_t)
    spare = p_rows + jnp.arange(SC_SCATTER_WIN, dtype=I32)
    xs = _sc_scatter_rows(h2_words, _windows(dest, SC_SCATTER_WIN, 2, spare), p_rows + SC_SCATTER_WIN)
    return xs, gate, dest, block_info, p_rows


def _moe_compute(layer, dispatched, w1, b1, w2, b2):
    xs, gate, back, block_info, p_rows = dispatched
    ys = _moe_experts(layer, block_info, xs, p_rows, w1, b1, w2, b2)
    return _sc_gather_rows(ys, back), gate


def _final_kernel(x_ref, yg_ref, gate_ref, g2_ref, nw_ref, *rest):
    o_ref = rest[-1]
    x = x_ref[0] + g2_ref[0] * _combine(yg_ref, gate_ref)
    ms = jnp.mean(x * x, axis=-1, keepdims=True)
    o_ref[0] = x * lax.rsqrt(ms + EPS) * nw_ref[...]


def _final(x, res, nw, ts, out_buf, ob0, b_total):
    b, s, d = x.shape
    nj = s // ts
    yg, gate, g2, first = res
    blk0 = first // ts
    row = pl.BlockSpec((1, ts, d), lambda i, j: (i, j, 0))
    if gate.ndim == 2:
        gate_spec = pl.BlockSpec((GATE_ROWS, ts), lambda i, j: (0, blk0 + i * nj + j))
    else:
        gate_spec = pl.BlockSpec((1, ts, TOP_K), lambda i, j: (i, j, 0))
    in_specs = [row, pl.BlockSpec((TOP_K, ts, d // 2), lambda i, j: (0, blk0 + i * nj + j, 0)),
                gate_spec,
                pl.BlockSpec((1, 1, d), lambda i, j: (i, 0, 0)),
                pl.BlockSpec((1, d), lambda i, j: (0, 0))]
    args = [x, yg, gate, g2, nw.reshape(1, d)]
    aliases = {}
    if out_buf is not None:
        in_specs.append(pl.BlockSpec(memory_space=pl.ANY))
        args.append(out_buf)
        aliases = {len(args) - 1: 0}
    return pl.pallas_call(
        _final_kernel,
        grid=(b, nj),
        in_specs=in_specs,
        out_specs=pl.BlockSpec((1, ts, d), lambda i, j: (i + ob0, j, 0)),
        out_shape=jax.ShapeDtypeStruct((b_total, s, d), F32),
        input_output_aliases=aliases,
        compiler_params=_cparams("parallel", "parallel"),
        name="final_norm",
    )(*args)


def kernel(x_prompt, x_sample, c_prompt, c_sample, cache_k, cache_v, state_hgrn, norm_mix, norm_ffn, norm_final, w_ada, b_ada, w_in, rel_bias, hgrn_lb_logits, g_attn_out, g_hgrn_out, w_out, w_router, b_router, w_e1, b_e1, w_e2, b_e2):
    depth = w_in.shape[0]
    bp, sp, d = x_prompt.shape
    bs, ls, _ = x_sample.shape
    wc = cache_k.shape[2]
    n_p, n_s = bp * sp, bs * ls

    lbs = jax.nn.softmax(hgrn_lb_logits.astype(F32), axis=0)
    lbs = jnp.cumsum(lbs, axis=0) - lbs[0]

    mods = _modulation(jnp.concatenate([c_prompt, c_sample], axis=0), w_ada, b_ada)
    lk_s = -(-(wc + ls) // LANES) * LANES
    tail = min(LEFT, sp)
    rec_chunk = min(REC_CHUNK, sp)

    n_groups = PROMPT_GROUPS if bp % PROMPT_GROUPS == 0 else 1
    gb = bp // n_groups
    n_g = gb * sp
    x_g = [(x_prompt, g * gb) for g in range(n_groups)]
    x_s = (x_sample, 0)
    res_g = [None] * n_groups
    res_s = None
    kp_l, vp_l, sp_l, ks_l, vs_l, ss_l = [], [], [], [], [], []
    for layer in range(depth):
        m6 = mods[layer].reshape(bp + bs, 1, N_MOD, d)
        mod_s = [m6[bp:, :, n, :] for n in range(N_MOD)]
        w_in_bf = w_in[layer].astype(BF16)
        w_out_bf = w_out[layer].astype(BF16)
        wr_hi, wr_mid, _ = _split3(w_router[layer].T)
        wr2 = jnp.concatenate([wr_hi, wr_mid], axis=0)
        table = rel_bias[layer]
        bias_p = _prompt_bias(table)

        def mixer_prompt(g):
            mod = [m6[g * gb:(g + 1) * gb, :, n, :] for n in range(N_MOD)]
            x_new, h2, lg, kt, vt, s_fin = _mixer_prompt(
                x_g[g][0], x_g[g][1], res_g[g], norm_mix[layer], mod[1], mod[0], w_in_bf, lbs[layer], bias_p,
                g_attn_out[layer], g_hgrn_out[layer], w_out_bf, mod[2], norm_ffn[layer], mod[4], mod[3],
                wr2, b_router[layer], tail, rec_chunk, n_s if g == n_groups - 1 else 0)
            x_g[g] = (x_new, 0)
            return h2, lg, kt, vt, s_fin, mod[5]

        def mixer_sample(rows):
            x, qa, ka, va, kt, vt, qh, kh, lf, vh, gh = _inproj(
                x_s[0], x_s[1], res_s, norm_mix[layer], mod_s[1], mod_s[0], w_in_bf, lbs[layer], ls, ls)
            o_att = _attn_sample(qa, cache_k[layer].reshape(bs, wc, ATT_W), cache_v[layer].reshape(bs, wc, ATT_W),
                                 ka, va, _sample_bias(table, ls, wc, lk_s))
            s0 = jnp.swapaxes(state_hgrn[layer].astype(F32), -1, -2)
            o_rec, s_new = _hgrn(qh, kh, lf, vh, s0, ls, ls)
            x_new, rows, lg = _merge(o_att, o_rec, gh, x[0], x[1], g_attn_out[layer], g_hgrn_out[layer], w_out_bf,
                                     mod_s[2], norm_ffn[layer], mod_s[4], mod_s[3], wr2, b_router[layer], rows, n_g)
            return (x_new, 0), rows, jnp.swapaxes(lg, 0, 1).reshape(N_EXPERTS, n_s), kt, vt, s_new

        dispatched, g2s, k_t, v_t, s_t = [], [], [], [], []
        for g in range(n_groups):
            h2, lg, kt, vt, s_fin, g2 = mixer_prompt(g)
            if g == n_groups - 1:
                x_s, h2, lg_s, ka_s, va_s, s_new = mixer_sample(h2)
                ks_l.append(ka_s.reshape(bs, ls, ATT_HEADS, ATT_DIM))
                vs_l.append(va_s.reshape(bs, ls, ATT_HEADS, ATT_DIM))
                ss_l.append(jnp.swapaxes(s_new, -1, -2))
                lg = jnp.concatenate([lg, lg_s], axis=1)
            dispatched.append(_moe_dispatch(h2, lg))
            g2s.append(g2)
            k_t.append(kt)
            v_t.append(vt)
            s_t.append(s_fin)
        kp_l.append(jnp.concatenate(k_t, axis=0).reshape(bp, tail, ATT_HEADS, ATT_DIM))
        vp_l.append(jnp.concatenate(v_t, axis=0).reshape(bp, tail, ATT_HEADS, ATT_DIM))
        sp_l.append(jnp.swapaxes(jnp.concatenate(s_t, axis=0), -1, -2))

        for g in range(n_groups):
            yg, gate = _moe_compute(layer, dispatched[g], w_e1, b_e1, w_e2, b_e2)
            res_g[g] = (yg, gate, g2s[g], 0)
            if g == n_groups - 1:
                res_s = (yg, gate[:TOP_K, n_g:].T.reshape(bs, ls, TOP_K), mod_s[5], n_g)

    y_prompt = None
    for g in range(n_groups):
        y_prompt = _final(x_g[g][0], res_g[g], norm_final, SEQ_TILE, y_prompt, g * gb, bp)
    y_sample = _final(x_s[0], res_s, norm_final, ls, None, 0, bs)
    return (y_prompt, y_sample, jnp.stack(kp_l), jnp.stack(vp_l), jnp.stack(sp_l),
            jnp.stack(ks_l), jnp.stack(vs_l), jnp.stack(ss_l))
```

```python
import functools

import jax
import jax.numpy as jnp
import numpy as np
from jax import lax
from jax.experimental import pallas as pl
from jax.experimental.pallas import tpu as pltpu
from jax.experimental.pallas import tpu_sc as plsc

F32 = jnp.float32
BF16 = jnp.bfloat16
I32 = jnp.int32
U32 = jnp.uint32

EPS = 1e-6
CHUNK = 64
LEFT_CHUNKS = 8
LEFT = LEFT_CHUNKS * CHUNK
MAX_REL = 2 * CHUNK
ATT_HEADS = 8
ATT_DIM = 64
ATT_W = ATT_HEADS * ATT_DIM
REC_HEADS = 4
REC_DIM = 128
REC_W = REC_HEADS * REC_DIM
N_EXPERTS = 32
TOP_K = 4
SWIGLU_ALPHA = 1.702
SWIGLU_LIMIT = 7.0
N_MOD = 6
NEG = -1e30
LOG2E = 1.4426950408889634

LANES = 128
ATT_QSUB = 128
ATT_WIN = LEFT + ATT_QSUB
SEQ_TILE = 512
REC_CHUNK = 128
MOE_ROWS = 1024
MOE_UNIT = 256
ROUTE_TILE = 256
ROUTE_MAX_SUB = 4
PROMPT_GROUPS = 2
GATE_ROWS = 8
VMEM_LIMIT = 56 * 1024 * 1024
SC_WORKERS = 32
SC_SCATTER_WIN = 64
SC_GATHER_WIN = 32
SC_GATHER_BUFS = 4


def _cparams(*sem):
    return pltpu.CompilerParams(dimension_semantics=sem, vmem_limit_bytes=VMEM_LIMIT)


def _split3(x):
    hi = x.astype(BF16)
    r1 = x - hi.astype(F32)
    mid = r1.astype(BF16)
    lo = (r1 - mid.astype(F32)).astype(BF16)
    return hi, mid, lo


def _bits(x):
    return lax.bitcast_convert_type(x, U32)


def _pack_pairs(x):
    r = _bits(x.astype(BF16).astype(F32))
    half = x.shape[1] // 2
    return jnp.bitwise_or(jnp.bitwise_and(r[:, half:], jnp.uint32(0xFFFF0000)),
                          jnp.right_shift(r[:, :half], jnp.uint32(16)))


def _unpack_pairs(word):
    lo = lax.bitcast_convert_type(jnp.left_shift(word, jnp.uint32(16)), F32)
    hi = lax.bitcast_convert_type(jnp.bitwise_and(word, jnp.uint32(0xFFFF0000)), F32)
    return lo, hi


def _mod_kernel(c_ref, w_ref, b_ref, o_ref):
    c = c_ref[...]
    act = (c * jax.nn.sigmoid(c)).astype(BF16)
    o_ref[0] = jnp.dot(act, w_ref[0].astype(BF16), preferred_element_type=F32) + b_ref[0]


def _modulation(c_all, w_ada, b_ada):
    depth, d, n6 = w_ada.shape
    rows = c_all.shape[0]
    tn = 1536
    return pl.pallas_call(
        _mod_kernel,
        grid=(depth, n6 // tn),
        in_specs=[pl.BlockSpec((rows, d), lambda l, j: (0, 0)),
                  pl.BlockSpec((1, d, tn), lambda l, j: (l, 0, j)),
                  pl.BlockSpec((1, 1, tn), lambda l, j: (l, 0, j))],
        out_specs=pl.BlockSpec((1, rows, tn), lambda l, j: (l, 0, j)),
        out_shape=jax.ShapeDtypeStruct((depth, rows, n6), F32),
        compiler_params=_cparams("parallel", "parallel"),
        name="adaln_modulation",
    )(c_all, w_ada, b_ada.reshape(depth, 1, n6))


def _combine(yg_ref, gate_ref):
    g = gate_ref[0] if len(gate_ref.shape) == 3 else gate_ref[...].T
    lo = hi = None
    for k in range(TOP_K):
        l, h = _unpack_pairs(yg_ref[k])
        gk = g[:, k:k + 1]
        lo = l * gk if lo is None else lo + l * gk
        hi = h * gk if hi is None else hi + h * gk
    return jnp.concatenate([lo, hi], axis=-1)


def _inproj_kernel(has_res, *refs):
    if has_res:
        (x_ref, yg_ref, gate_ref, g2_ref, nw_ref, sc_ref, sh_ref, w_ref, lb_ref,
         xo_ref, qa_ref, ka_ref, va_ref, kt_ref, vt_ref, qh_ref, kh_ref, lf_ref, vh_ref, gh_ref) = refs
        x = x_ref[0] + g2_ref[0] * _combine(yg_ref, gate_ref)
        xo_ref[0] = x
    else:
        (x_ref, nw_ref, sc_ref, sh_ref, w_ref, lb_ref,
         qa_ref, ka_ref, va_ref, kt_ref, vt_ref, qh_ref, kh_ref, lf_ref, vh_ref, gh_ref) = refs
        x = x_ref[0]
    _project_rows(x, nw_ref, sc_ref, sh_ref, w_ref, lb_ref, qa_ref.at[0], ka_ref.at[0], va_ref.at[0],
                  kt_ref.at[0], vt_ref.at[0], qh_ref.at[0], kh_ref.at[0], lf_ref.at[0], vh_ref.at[0], gh_ref.at[0])


def _project_rows(x, nw_ref, sc_ref, sh_ref, w_ref, lb_ref,
                  qa_ref, ka_ref, va_ref, kt_ref, vt_ref, qh_ref, kh_ref, lf_ref, vh_ref, gh_ref):
    ms = jnp.mean(x * x, axis=-1, keepdims=True)
    h = (x * lax.rsqrt(ms + EPS) * nw_ref[...]) * (1.0 + sc_ref[0]) + sh_ref[0]
    hb = h.astype(BF16)

    def proj(g):
        return jnp.dot(hb, w_ref[:, g * ATT_W:(g + 1) * ATT_W], preferred_element_type=F32)

    qa_ref[...] = (proj(0) * (ATT_DIM ** -0.5)).astype(BF16)
    k = proj(1)
    ka_ref[...] = k.astype(BF16)
    kt_ref[...] = k
    v = proj(2)
    va_ref[...] = v.astype(BF16)
    vt_ref[...] = v
    qh = proj(3)
    qh_ref[...] = qh * jax.nn.sigmoid(qh)
    f = proj(4)
    lb = lb_ref[...]
    lf_ref[...] = jnp.log(lb + (1.0 - lb) * jax.nn.sigmoid(f))
    kh_ref[...] = (1.0 - lb) * jax.nn.sigmoid(-f)
    vh_ref[...] = proj(5).astype(BF16)
    gh_ref[...] = proj(6)


def _inproj(x, xb0, res, nw, sc, sh, w_bf, lb, ts, tail):
    _, s, d = x.shape
    b = sc.shape[0]
    pw = w_bf.shape[1]
    nj = s // ts
    n_tail = tail // ts
    has_res = res is not None
    tail_spec = pl.BlockSpec((1, ts, ATT_W), lambda i, j: (i, jnp.maximum(j - (nj - n_tail), 0), 0))
    ht = jax.ShapeDtypeStruct((b, tail, ATT_W), F32)
    row = pl.BlockSpec((1, ts, d), lambda i, j: (i, j, 0))
    per_b = pl.BlockSpec((1, 1, d), lambda i, j: (i, 0, 0))
    half = pl.BlockSpec((1, ts, ATT_W), lambda i, j: (i, j, 0))
    in_specs, args = [pl.BlockSpec((1, ts, d), lambda i, j: (i + xb0, j, 0))], [x]
    if has_res:
        yg, gate, g2, first = res
        blk0 = first // ts
        in_specs += [pl.BlockSpec((TOP_K, ts, d // 2), lambda i, j: (0, blk0 + i * nj + j, 0)),
                     pl.BlockSpec((1, ts, TOP_K), lambda i, j: (i, j, 0)), per_b]
        args += [yg, gate, g2]
    in_specs += [pl.BlockSpec((1, d), lambda i, j: (0, 0)), per_b, per_b,
                 pl.BlockSpec((d, pw), lambda i, j: (0, 0)),
                 pl.BlockSpec((1, REC_W), lambda i, j: (0, 0))]
    args += [nw.reshape(1, d), sc, sh, w_bf, lb.reshape(1, REC_W)]
    hs = jax.ShapeDtypeStruct((b, s, ATT_W), F32)
    hb = jax.ShapeDtypeStruct((b, s, ATT_W), BF16)
    out_shape = ([jax.ShapeDtypeStruct((b, s, d), F32)] if has_res else []) + [hb, hb, hb, ht, ht, hs, hs, hs, hb, hs]
    out_specs = ([row] if has_res else []) + [half] * 3 + [tail_spec] * 2 + [half] * 5
    outs = pl.pallas_call(
        functools.partial(_inproj_kernel, has_res),
        grid=(b, nj),
        in_specs=in_specs, out_specs=out_specs, out_shape=out_shape,
        compiler_params=_cparams("parallel", "arbitrary"),
        name="norm_inproj",
    )(*args)
    if has_res:
        return [(outs[0], 0)] + list(outs[1:])
    return [(x, xb0)] + list(outs)


def _attend(q, kw, vw, bias_ref, s_ref, col_thr):
    lq, lk = q.shape[0], kw.shape[0]
    lane = lax.broadcasted_iota(I32, (lq, LANES), 1)
    lo = lane < ATT_DIM
    nt = (((1,), (1,)), ((), ()))
    if col_thr is not None:
        keep = lax.broadcasted_iota(I32, (2 * lq, lk), 1) >= col_thr
    ms = []
    for p in range(ATT_HEADS // 2):
        sl = slice(p * LANES, (p + 1) * LANES)
        qp = q[:, sl]
        z = jnp.zeros_like(qp)
        lhs = jnp.concatenate([jnp.where(lo, qp, z), jnp.where(lo, z, qp)], axis=0)
        s = lax.dot_general(lhs, kw[:, sl], nt, preferred_element_type=F32) + bias_ref[p]
        if col_thr is not None:
            s = jnp.where(keep, s, NEG)
        ms.append(jnp.max(s, axis=-1, keepdims=True))
        s_ref[p] = s
    outs = []
    for p in range(ATT_HEADS // 2):
        sl = slice(p * LANES, (p + 1) * LANES)
        e = jnp.exp(s_ref[p] - ms[p])
        l = jnp.sum(e, axis=-1, keepdims=True)
        o = jnp.dot(e.astype(BF16), vw[:, sl], preferred_element_type=F32) / l
        outs.append(jnp.where(lo, o[:lq], o[lq:]))
    return jnp.concatenate(outs, axis=-1)


def _band_attention(q_ref, kw_ref, vw_ref, bias_ref, s_ref, o_ref, first_tile):
    ts = q_ref.shape[0]

    def body(masked, r, carry):
        r0 = pl.multiple_of(r * ATT_QSUB, ATT_QSUB)
        q = q_ref[pl.ds(r0, ATT_QSUB), :]
        kw = kw_ref[pl.ds(r0, ATT_WIN), :]
        vw = vw_ref[pl.ds(r0, ATT_WIN), :]
        o_ref[pl.ds(r0, ATT_QSUB), :] = _attend(q, kw, vw, bias_ref, s_ref, (ts - r0) if masked else None)
        return carry

    @pl.when(first_tile)
    def _():
        lax.fori_loop(0, ts // ATT_QSUB, functools.partial(body, True), 0)

    @pl.when(jnp.logical_not(first_tile))
    def _():
        lax.fori_loop(0, ts // ATT_QSUB, functools.partial(body, False), 0)


def _attn_sample_kernel(q_ref, ck_ref, cv_ref, k_ref, v_ref, bias_ref, o_ref, kw_ref, vw_ref, s_ref):
    wc, l, lk = ck_ref.shape[1], k_ref.shape[1], kw_ref.shape[0]
    for dst, cache, new in ((kw_ref, ck_ref, k_ref), (vw_ref, cv_ref, v_ref)):
        dst[0:wc, :] = cache[0].astype(BF16)
        dst[wc:wc + l, :] = new[0]
        dst[wc + l:lk, :] = jnp.zeros((lk - wc - l, ATT_W), BF16)
    o_ref[0] = _attend(q_ref[0], kw_ref[...], vw_ref[...], bias_ref, s_ref, None)


def _attn_sample(qa, cache_k, cache_v, ka, va, bias):
    b, l, w = qa.shape
    wc = cache_k.shape[1]
    lk = bias.shape[2]
    new = pl.BlockSpec((1, l, w), lambda i: (i, 0, 0))
    old = pl.BlockSpec((1, wc, w), lambda i: (i, 0, 0))
    return pl.pallas_call(
        _attn_sample_kernel,
        grid=(b,),
        in_specs=[new, old, old, new, new, pl.BlockSpec(bias.shape, lambda i: (0, 0, 0))],
        out_specs=new,
        out_shape=jax.ShapeDtypeStruct((b, l, w), F32),
        scratch_shapes=[pltpu.VMEM((lk, w), BF16), pltpu.VMEM((lk, w), BF16),
                        pltpu.VMEM((ATT_HEADS // 2, 2 * l, lk), F32)],
        compiler_params=_cparams("parallel"),
        name="cache_attention",
    )(qa, cache_k, cache_v, ka, va, bias)


def _toeplitz_bias(table, lq, lk, offset, valid):
    m = lq + lk - 1
    k = np.arange(m)
    diff = np.where(k < lk, k, k - m)
    g = table[:, np.clip(offset - diff, -MAX_REL, MAX_REL) + MAX_REL]
    skew = jnp.tile(g, (1, lq))[:, :lq * (m - 1)].reshape(-1, lq, m - 1)[:, :, :lk]
    bias = jnp.where(jnp.asarray(valid)[None], skew, NEG).astype(F32)
    return bias.reshape(ATT_HEADS // 2, 2 * lq, lk)


def _prompt_bias(table):
    t = np.arange(ATT_QSUB)[:, None]
    j = np.arange(ATT_WIN)[None, :]
    start = (t // CHUNK) * CHUNK
    return _toeplitz_bias(table, ATT_QSUB, ATT_WIN, LEFT, (j >= start) & (j < start + LEFT + CHUNK))


def _sample_bias(table, l, w, lk):
    valid = np.broadcast_to(np.arange(lk)[None, :] < w + l, (l, lk))
    return _toeplitz_bias(table, l, lk, w, valid)


def _neg_abs(x):
    return lax.bitcast_convert_type(jnp.bitwise_or(_bits(x), jnp.uint32(0x80000000)), F32)


def _hgrn_kernel(chunk, q_ref, k_ref, lf_ref, v_ref, s0_ref, o_ref, sf_ref, st_ref):
    i = pl.program_id(1)

    @pl.when(i == 0)
    def _():
        st_ref[...] = s0_ref[0]

    _hgrn_rows(chunk, q_ref.at[0], k_ref.at[0], lf_ref.at[0], v_ref.at[0], o_ref.at[0], st_ref)

    @pl.when(i == pl.num_programs(1) - 1)
    def _():
        sf_ref[0] = st_ref[...]


def _hgrn_rows(chunk, q_ref, k_ref, lf_ref, v_ref, o_ref, st_ref):
    ts = q_ref.shape[0]
    n_chunks = ts // chunk

    row = lax.broadcasted_iota(I32, (chunk, chunk), 0)
    colm = lax.broadcasted_iota(I32, (chunk, chunk), 1)
    tri = (colm <= row).astype(BF16)
    xor = jnp.bitwise_xor(row, colm)
    lower = colm < row
    rid = lax.broadcasted_iota(I32, (chunk, REC_W), 0)

    levels = []
    b = chunk // 2
    while b >= 1:
        levels.append(b)
        b //= 2
    pair_mask = [xor == 0] + [lower & (xor >= b) & (xor < 2 * b) for b in levels]
    upper = [jnp.bitwise_and(rid, b) != 0 for b in levels]
    ph4 = jnp.bitwise_and(rid, 3)
    odd = jnp.bitwise_and(rid, 1) == 1
    nt = (((1,), (1,)), ((), ()))
    tn = (((0,), (0,)), ((), ()))
    sls = [slice(h * REC_DIM, (h + 1) * REC_DIM) for h in range(REC_HEADS)]

    def boundary(cum, b):
        if 2 * b >= 8:
            pieces = [jnp.broadcast_to(cum[m * 2 * b + b - 1:m * 2 * b + b, :], (2 * b, REC_W))
                      for m in range(chunk // (2 * b))]
            return pieces[0] if len(pieces) == 1 else jnp.concatenate(pieces, axis=0)
        if b == 2:
            up1 = pltpu.roll(cum, chunk - 1, 0)
            dn1 = pltpu.roll(cum, 1, 0)
            dn2 = pltpu.roll(cum, 2, 0)
            return jnp.where(ph4 == 0, up1, jnp.where(ph4 == 1, cum, jnp.where(ph4 == 2, dn1, dn2)))
        return jnp.where(odd, pltpu.roll(cum, 1, 0), cum)

    def one_chunk(c, carry):
        r0 = pl.multiple_of(c * chunk, chunk)
        rows = pl.ds(r0, chunk)
        q = q_ref[rows, :]
        k = k_ref[rows, :]
        v = v_ref[rows, :]
        lf = lf_ref[rows, :]
        cum = sum(jnp.dot(tri, part, preferred_element_type=F32) for part in _split3(lf)) * LOG2E
        last = cum[chunk - 1:chunk, :]
        q_in = (q * jnp.exp2(cum)).astype(BF16)
        k_out = (k * jnp.exp2(last - cum)).astype(BF16)
        decay = jnp.exp2(last)
        qb = q.astype(BF16)
        kb = k.astype(BF16)
        zs = []
        for n, b in enumerate(levels):
            w = jnp.exp2(_neg_abs(cum - boundary(cum, b)))
            zs.append((jnp.where(upper[n], q, k) * w).astype(BF16))
        sts = [st_ref[h] for h in range(REC_HEADS)]
        d0 = [lax.dot_general(qb[:, sl], kb[:, sl], nt, preferred_element_type=F32) for sl in sls]
        dl = [[lax.dot_general(z[:, sl], z[:, sl], nt, preferred_element_type=F32) for z in zs] for sl in sls]
        oi = [lax.dot_general(q_in[:, sl], st.astype(BF16), nt, preferred_element_type=F32)
              for sl, st in zip(sls, sts)]
        upd = [lax.dot_general(v[:, sl], k_out[:, sl], tn, preferred_element_type=F32) for sl in sls]
        outs = []
        for h, sl in enumerate(sls):
            a = jnp.where(pair_mask[0], d0[h], 0.0)
            for n in range(len(levels)):
                a = jnp.where(pair_mask[n + 1], dl[h][n], a)
            outs.append(jnp.dot(a.astype(BF16), v[:, sl], preferred_element_type=F32) + oi[h])
        o_ref[rows, :] = jnp.concatenate(outs, axis=-1)
        for h, sl in enumerate(sls):
            st_ref[h] = decay[:, sl] * sts[h] + upd[h]
        return carry

    lax.fori_loop(0, n_chunks, one_chunk, 0, unroll=2 if n_chunks % 2 == 0 else 1)


def _hgrn(q, k, lf, v, s0_t, ts, chunk):
    b, s, w = q.shape
    blk = pl.BlockSpec((1, ts, w), lambda i, j: (i, j, 0))
    st = pl.BlockSpec((1, REC_HEADS, REC_DIM, REC_DIM), lambda i, j: (i, 0, 0, 0))
    return pl.pallas_call(
        functools.partial(_hgrn_kernel, chunk),
        grid=(b, s // ts),
        in_specs=[blk, blk, blk, blk, st],
        out_specs=[blk, st],
        out_shape=[jax.ShapeDtypeStruct((b, s, w), F32),
                   jax.ShapeDtypeStruct((b, REC_HEADS, REC_DIM, REC_DIM), F32)],
        scratch_shapes=[pltpu.VMEM((REC_HEADS, REC_DIM, REC_DIM), F32)],
        compiler_params=_cparams("parallel", "arbitrary"),
        name="hgrn_recurrence",
    )(q, k, lf, v, s0_t)


def _head_rms(x, width):
    lane = lax.broadcasted_iota(I32, (x.shape[0], LANES), 1)
    lo = lane < ATT_DIM
    outs = []
    for p in range(x.shape[1] // LANES):
        xp = x[:, p * LANES:(p + 1) * LANES]
        sq = xp * xp
        tot = jnp.sum(sq, axis=-1, keepdims=True)
        if width == LANES:
            scale = lax.rsqrt(tot * (1.0 / LANES) + EPS)
        else:
            s_lo = jnp.sum(jnp.where(lo, sq, 0.0), axis=-1, keepdims=True)
            scale = jnp.where(lo, lax.rsqrt(s_lo * (1.0 / ATT_DIM) + EPS),
                              lax.rsqrt((tot - s_lo) * (1.0 / ATT_DIM) + EPS))
        outs.append(xp * scale)
    return jnp.concatenate(outs, axis=-1)


def _merge_rows(oa, orr, gh, x, ga_ref, gr_ref, wo_ref, g1_ref, nf_ref, sc_ref, sh_ref, wr_ref, br_ref):
    a = _head_rms(oa, ATT_DIM) * ga_ref[...]
    r = _head_rms(orr, REC_DIM) * gr_ref[...] * (gh * jax.nn.sigmoid(gh))
    mix = jnp.dot(a.astype(BF16), wo_ref[0:ATT_W, :], preferred_element_type=F32)
    mix += jnp.dot(r.astype(BF16), wo_ref[ATT_W:ATT_W + REC_W, :], preferred_element_type=F32)
    x = x + g1_ref[0] * mix
    ms = jnp.mean(x * x, axis=-1, keepdims=True)
    h2 = (x * lax.rsqrt(ms + EPS) * nf_ref[...]) * (1.0 + sc_ref[0]) + sh_ref[0]
    hh = h2.astype(BF16)
    words = _pack_pairs(h2)
    hm = (h2 - hh.astype(F32)).astype(BF16)
    nt = (((1,), (1,)), ((), ()))
    l1 = lax.dot_general(wr_ref[...], hh, nt, preferred_element_type=F32)
    l2 = lax.dot_general(wr_ref[0:N_EXPERTS, :], hm, nt, preferred_element_type=F32)
    lg = l1[0:N_EXPERTS] + l1[N_EXPERTS:2 * N_EXPERTS] + l2 + br_ref[...]
    return x, words, lg


def _merge_kernel(oa_ref, or_ref, gh_ref, x_ref, ga_ref, gr_ref, wo_ref, g1_ref, nf_ref, sc_ref, sh_ref,
                  wr_ref, br_ref, rows_hbm, xo_ref, h2_ref, lg_ref):
    del rows_hbm
    x, words, lg = _merge_rows(oa_ref[0], or_ref[0], gh_ref[0], x_ref[0], ga_ref, gr_ref, wo_ref, g1_ref,
                               nf_ref, sc_ref, sh_ref, wr_ref, br_ref)
    xo_ref[0] = x
    h2_ref[...] = words
    lg_ref[0] = lg


def _merge(o_att, o_rec, gh, x, xb0, g_att, g_rec, wo_bf, g1, nf, sc, sh, wr2, br, rows, row0):
    _, ts, d = x.shape
    b = o_att.shape[0]
    row = pl.BlockSpec((1, ts, d), lambda i: (i, 0, 0))
    xrow = pl.BlockSpec((1, ts, d), lambda i: (i + xb0, 0, 0))
    half = pl.BlockSpec((1, ts, ATT_W), lambda i: (i, 0, 0))
    per_b = pl.BlockSpec((1, 1, d), lambda i: (i, 0, 0))
    const2 = lambda shp: pl.BlockSpec(shp, lambda i: (0,) * len(shp))
    blk0 = row0 // ts
    return pl.pallas_call(
        _merge_kernel,
        grid=(b,),
        in_specs=[half, half, half, xrow, const2((1, ATT_W)), const2((1, REC_W)), const2(wo_bf.shape),
                  per_b, const2((1, d)), per_b, per_b, const2(wr2.shape), const2((N_EXPERTS, 1)),
                  pl.BlockSpec(memory_space=pl.ANY)],
        out_specs=[row, pl.BlockSpec((ts, d // 2), lambda i: (blk0 + i, 0)),
                   pl.BlockSpec((1, N_EXPERTS, ts), lambda i: (i, 0, 0))],
        out_shape=[jax.ShapeDtypeStruct((b, ts, d), F32), jax.ShapeDtypeStruct(rows.shape, U32),
                   jax.ShapeDtypeStruct((b, N_EXPERTS, ts), F32)],
        input_output_aliases={13: 1},
        compiler_params=_cparams("parallel"),
        name="merge_outproj_router",
    )(o_att, o_rec, gh, x, g_att.reshape(1, ATT_W), g_rec.reshape(1, REC_W), wo_bf, g1, nf.reshape(1, d),
      sc, sh, wr2, br.reshape(N_EXPERTS, 1), rows)


def _mixer_kernel(has_res, chunk, *refs):
    n_in = 4 if has_res else 1
    x_ref = refs[0]
    (nw_ref, sc1_ref, sh1_ref, lb_ref, ga_ref, gr_ref, g1_ref, nf_ref, sc2_ref, sh2_ref, wr_ref, br_ref,
     win_hbm, wout_hbm, bias_hbm,
     xo_ref, h2_ref, lg_ref, kt_ref, vt_ref, sf_ref,
     win_v, wout_v, bias_v, q_s, kw_s, vw_s, s_s, qh_s, kh_s, lf_s, vh_s, gh_s, oa_s, or_s, st_s, sem) = refs[n_in:]
    j = pl.program_id(1)
    ts = x_ref.shape[1]

    @pl.when(j == 0)
    def _():
        copies = [pltpu.make_async_copy(src, dst, sem.at[n])
                  for n, (src, dst) in enumerate(((win_hbm, win_v), (wout_hbm, wout_v), (bias_hbm, bias_v)))]
        for cp in copies:
            cp.start()
        for cp in copies:
            cp.wait()
        st_s[...] = jnp.zeros_like(st_s)
        kw_s[0:ts, :] = jnp.zeros((ts, ATT_W), BF16)
        vw_s[0:ts, :] = jnp.zeros((ts, ATT_W), BF16)

    if has_res:
        yg_ref, gate_ref, g2_ref = refs[1:4]
        x = x_ref[0] + g2_ref[0] * _combine(yg_ref, gate_ref)
        xo_ref[0] = x
    else:
        x = x_ref[0]
    cur = pl.ds(ts, ts)
    _project_rows(x, nw_ref, sc1_ref, sh1_ref, win_v, lb_ref, q_s, kw_s.at[cur], vw_s.at[cur],
                  kt_ref.at[0], vt_ref.at[0], qh_s, kh_s, lf_s, vh_s, gh_s)
    _band_attention(q_s, kw_s, vw_s, bias_v, s_s, oa_s, j == 0)
    _hgrn_rows(chunk, qh_s, kh_s, lf_s, vh_s, or_s, st_s)
    x_in = xo_ref[0] if has_res else x_ref[0]
    x_new, words, lg = _merge_rows(oa_s[...], or_s[...], gh_s[...], x_in, ga_ref, gr_ref, wout_v, g1_ref,
                                   nf_ref, sc2_ref, sh2_ref, wr_ref, br_ref)
    xo_ref[0] = x_new
    h2_ref[...] = words
    lg_ref[...] = lg
    kw_s[0:ts, :] = kw_s[ts:2 * ts, :]
    vw_s[0:ts, :] = vw_s[ts:2 * ts, :]

    @pl.when(j == pl.num_programs(1) - 1)
    def _():
        sf_ref[0] = st_s[...]


def _mixer_prompt(x, xb0, res, nw, sc1, sh1, w_in_bf, lb, bias, g_att, g_rec, w_out_bf, g1, nf, sc2, sh2,
                  wr2, br, tail, chunk, extra_rows):
    _, s, d = x.shape
    b = sc1.shape[0]
    ts = LEFT
    nj = s // ts
    n_tail = tail // ts
    has_res = res is not None
    row = pl.BlockSpec((1, ts, d), lambda i, j: (i, j, 0))
    per_b = pl.BlockSpec((1, 1, d), lambda i, j: (i, 0, 0))
    const = lambda shp: pl.BlockSpec(shp, lambda i, j: (0,) * len(shp))
    hbm = pl.BlockSpec(memory_space=pl.ANY)
    in_specs, args = [pl.BlockSpec((1, ts, d), lambda i, j: (i + xb0, j, 0))], [x]
    if has_res:
        yg, gate, g2, first = res
        blk0 = first // ts
        in_specs += [pl.BlockSpec((TOP_K, ts, d // 2), lambda i, j: (0, blk0 + i * nj + j, 0)),
                     pl.BlockSpec((GATE_ROWS, ts), lambda i, j: (0, blk0 + i * nj + j)), per_b]
        args += [yg, gate, g2]
    in_specs += [const((1, d)), per_b, per_b, const((1, REC_W)), const((1, ATT_W)), const((1, REC_W)),
                 per_b, const((1, d)), per_b, per_b, const(wr2.shape), const((N_EXPERTS, 1)), hbm, hbm, hbm]
    args += [nw.reshape(1, d), sc1, sh1, lb.reshape(1, REC_W), g_att.reshape(1, ATT_W), g_rec.reshape(1, REC_W),
             g1, nf.reshape(1, d), sc2, sh2, wr2, br.reshape(N_EXPERTS, 1), w_in_bf, w_out_bf, bias]
    tail_spec = pl.BlockSpec((1, ts, ATT_W), lambda i, j: (i, jnp.maximum(j - (nj - n_tail), 0), 0))
    state_spec = pl.BlockSpec((1, REC_HEADS, REC_DIM, REC_DIM), lambda i, j: (i, 0, 0, 0))
    tile = lambda dt: pltpu.VMEM((ts, ATT_W), dt)
    return pl.pallas_call(
        functools.partial(_mixer_kernel, has_res, chunk),
        grid=(b, nj),
        in_specs=in_specs,
        out_specs=[row, pl.BlockSpec((ts, d // 2), lambda i, j: (i * nj + j, 0)),
                   pl.BlockSpec((N_EXPERTS, ts), lambda i, j: (0, i * nj + j)), tail_spec, tail_spec, state_spec],
        out_shape=[jax.ShapeDtypeStruct((b, s, d), F32), jax.ShapeDtypeStruct((b * s + extra_rows, d // 2), U32),
                   jax.ShapeDtypeStruct((N_EXPERTS, b * s), F32),
                   jax.ShapeDtypeStruct((b, tail, ATT_W), F32), jax.ShapeDtypeStruct((b, tail, ATT_W), F32),
                   jax.ShapeDtypeStruct((b, REC_HEADS, REC_DIM, REC_DIM), F32)],
        scratch_shapes=[pltpu.VMEM(w_in_bf.shape, BF16), pltpu.VMEM(w_out_bf.shape, BF16), pltpu.VMEM(bias.shape, F32),
                        tile(BF16), pltpu.VMEM((2 * ts, ATT_W), BF16), pltpu.VMEM((2 * ts, ATT_W), BF16),
                        pltpu.VMEM((ATT_HEADS // 2, 2 * ATT_QSUB, ATT_WIN), F32),
                        tile(F32), tile(F32), tile(F32), tile(BF16), tile(F32), tile(F32), tile(F32),
                        pltpu.VMEM((REC_HEADS, REC_DIM, REC_DIM), F32), pltpu.SemaphoreType.DMA((3,))],
        compiler_params=_cparams("parallel", "arbitrary"),
        name="prompt_mixer",
    )(*args)


def _route_kernel(lg_ref, idx_ref, gate_ref, rank_ref, cnt_ref, run_ref):
    i = pl.program_id(0)
    t = ROUTE_TILE

    @pl.when(i == 0)
    def _():
        run_ref[...] = jnp.zeros_like(run_ref)

    row = lax.broadcasted_iota(I32, (N_EXPERTS, t), 0)
    earlier = (lax.broadcasted_iota(I32, (t, t), 0) < lax.broadcasted_iota(I32, (t, t), 1)).astype(BF16)
    run = run_ref[:, 0:1]
    for sb in range(lg_ref.shape[1] // t):
        cols = slice(sb * t, (sb + 1) * t)
        x = lg_ref[:, cols]
        vals, idxs, hits = [], [], []
        for _ in range(TOP_K):
            m = jnp.max(x, axis=0, keepdims=True)
            ik = jnp.min(jnp.where(x == m, row, N_EXPERTS), axis=0, keepdims=True)
            hit = row == ik
            x = jnp.where(hit, -jnp.inf, x)
            vals.append(m)
            idxs.append(ik)
            hits.append(hit)
        es = [jnp.exp(v - vals[0]) for v in vals]
        tot = es[0] + es[1] + es[2] + es[3]
        gate_ref[:, cols] = jnp.concatenate([e / tot for e in es] + [jnp.zeros((GATE_ROWS - TOP_K, t), F32)], axis=0)
        idx_ref[:, cols] = jnp.concatenate(idxs, axis=0)
        chosen = (hits[0] | hits[1] | hits[2] | hits[3])
        onehot = jnp.where(chosen, 1.0, 0.0)
        before = jnp.dot(onehot.astype(BF16), earlier, preferred_element_type=F32) + run
        rank_ref[:, cols] = jnp.concatenate(
            [jnp.sum(jnp.where(h, before, 0.0), axis=0, keepdims=True) for h in hits], axis=0).astype(I32)
        run = run + jnp.sum(onehot, axis=1, keepdims=True)
    run_ref[...] = jnp.broadcast_to(run, run_ref.shape)
    cnt_ref[...] = jnp.broadcast_to(run, cnt_ref.shape)


def _route(lg_t):
    e, n = lg_t.shape
    t = max(m * ROUTE_TILE for m in range(1, ROUTE_MAX_SUB + 1) if n % (m * ROUTE_TILE) == 0)
    tok = pl.BlockSpec((TOP_K, t), lambda i: (0, i))
    idx, gate, rank, cnt = pl.pallas_call(
        _route_kernel,
        grid=(n // t,),
        in_specs=[pl.BlockSpec((e, t), lambda i: (0, i))],
        out_specs=[tok, pl.BlockSpec((GATE_ROWS, t), lambda i: (0, i)), tok, pl.BlockSpec((e, LANES), lambda i: (0, 0))],
        out_shape=[jax.ShapeDtypeStruct((TOP_K, n), I32), jax.ShapeDtypeStruct((GATE_ROWS, n), F32),
                   jax.ShapeDtypeStruct((TOP_K, n), I32), jax.ShapeDtypeStruct((e, LANES), F32)],
        scratch_shapes=[pltpu.VMEM((e, LANES), F32)],
        compiler_params=_cparams("arbitrary"),
        name="moe_route",
    )(lg_t)
    counts = cnt[:, 0].astype(I32)
    padded = (counts + MOE_ROWS - 1) // MOE_ROWS * MOE_ROWS
    pad_end = jnp.cumsum(padded)
    pad_start = pad_end - padded
    m = n * TOP_K
    nb = (m + N_EXPERTS * (MOE_ROWS - 1) + MOE_ROWS - 1) // MOE_ROWS
    block_start = jnp.arange(nb, dtype=I32) * MOE_ROWS
    block_expert = jnp.minimum(jnp.sum((pad_end[None, :] <= block_start[:, None]).astype(I32), axis=1), N_EXPERTS - 1)
    experts = jnp.arange(N_EXPERTS, dtype=I32)
    in_block = block_start[:, None] - pad_start[None, :]
    mine = block_expert[:, None] == experts[None, :]
    filled = jnp.sum(jnp.where(mine, jnp.clip(counts[None, :] - in_block, 0, MOE_ROWS), 0), axis=1)
    owns = jnp.any(block_expert[:, None] == experts[None, :], axis=0)
    later = owns[None, :] & (experts[None, :] > block_expert[:, None])
    nxt = jnp.min(jnp.where(later, experts[None, :], N_EXPERTS), axis=1)
    nxt = jnp.where(nxt == N_EXPERTS, -1, nxt)
    run = jnp.sum((owns[None, :] & (experts[None, :] < block_expert[:, None])).astype(I32), axis=1)
    block_info = jnp.concatenate([block_expert, nxt, run % 2, filled]).astype(I32)
    start_of = sum(jnp.where(idx == e, pad_start[e], 0) for e in range(N_EXPERTS))
    dest = start_of + rank
    return gate, dest, block_info, nb * MOE_ROWS


def _windows(dest, w, multiple, fill):
    kk, n = dest.shape
    nwin = n // w
    per = -(-nwin // SC_WORKERS)
    per = -(-per // multiple) * multiple
    idx = dest.reshape(kk, nwin, w).transpose(1, 0, 2)
    pad = jnp.broadcast_to(fill[None, None, :], (SC_WORKERS * per - nwin, kk, w)).astype(I32)
    return jnp.concatenate([idx, pad], axis=0)


def _sc_scatter_rows(x, idx, p_rows):
    n, c = x.shape
    nwin, kk, w = idx.shape
    per = nwin // SC_WORKERS
    nreal = n // w
    mesh = plsc.VectorSubcoreMesh(core_axis_name="c", subcore_axis_name="s")

    @functools.partial(
        pl.kernel, mesh=mesh,
        out_type=jax.ShapeDtypeStruct((p_rows, c), x.dtype),
        scratch_types=[pltpu.VMEM((per, kk, w), I32), pltpu.VMEM((2, w, c), x.dtype),
                       pltpu.SemaphoreType.DMA((2,)), pltpu.SemaphoreType.DMA((2,))],
        name="sc_scatter_rows",
    )
    def k(x_hbm, idx_hbm, out_hbm, idx_v, rows_v, lsem, ssem):
        wid = lax.axis_index("s") * 2 + lax.axis_index("c")
        g0 = wid * per
        pltpu.sync_copy(idx_hbm.at[pl.ds(g0, per)], idx_v)

        def load(j, b):
            g = jnp.minimum(g0 + j, nreal - 1)
            return pltpu.make_async_copy(x_hbm.at[pl.ds(g * w, w)], rows_v.at[b], lsem.at[b])

        def scat(j, b, q):
            return pltpu.make_async_copy(rows_v.at[b], out_hbm.at[idx_v.at[j, q]], ssem.at[b])

        @pl.loop(0, per + 2, step=2)
        def _(j0):
            for b in range(2):
                j = j0 + b

                @pl.when(jnp.logical_and(j >= 2, j < per + 2))
                def _():
                    for q in range(kk):
                        scat(j - 2, b, q).wait()

                @pl.when(j < per)
                def _():
                    load(j, b).start()

                @pl.when(jnp.logical_and(j >= 1, j < per + 1))
                def _():
                    load(j - 1, 1 - b).wait()
                    for q in range(kk):
                        scat(j - 1, 1 - b, q).start()

    return k(x, idx)


def _sc_gather_rows(y, idx):
    p, c = y.shape
    kk, n = idx.shape
    w = SC_GATHER_WIN
    nb = SC_GATHER_BUFS
    wins = LANES // w
    group = wins * kk
    assert group % nb == 0 and group >= nb and n % LANES == 0
    n_units = n // LANES
    max_units = -(-n_units // SC_WORKERS)
    mesh = plsc.VectorSubcoreMesh(core_axis_name="c", subcore_axis_name="s")

    @functools.partial(
        pl.kernel, mesh=mesh,
        out_type=jax.ShapeDtypeStruct((kk, n, c), y.dtype),
        scratch_types=[pltpu.VMEM((2, kk, LANES), I32), pltpu.VMEM((nb, w, c), y.dtype),
                       pltpu.SemaphoreType.DMA((nb,)), pltpu.SemaphoreType.DMA((nb,))],
        name="sc_gather_rows",
    )
    def k(y_hbm, idx_hbm, out_hbm, idx_v, rows_v, gsem, wsem):
        wid = lax.axis_index("s") * 2 + lax.axis_index("c")
        my_units = (n_units - wid + SC_WORKERS - 1) // SC_WORKERS
        n_items = my_units * group

        def gath(slot, jj, q, b):
            return pltpu.make_async_copy(y_hbm.at[idx_v.at[slot, q, pl.ds(jj * w, w)]], rows_v.at[b], gsem.at[b])

        def wr(i, b):
            row = (wid + (i // group) * SC_WORKERS) * LANES + ((i % group) // kk) * w
            return pltpu.make_async_copy(rows_v.at[b], out_hbm.at[i % kk, pl.ds(row, w)], wsem.at[b])

        @pl.loop(0, max_units + 1)
        def _(gi):
            slot = gi % 2

            @pl.when(gi < my_units)
            def _():
                pltpu.sync_copy(idx_hbm.at[:, pl.ds((wid + gi * SC_WORKERS) * LANES, LANES)], idx_v.at[slot])

            for l in range(group):
                i = gi * group + l
                b = l % nb
                bw = (l - (nb - 1)) % nb

                @pl.when(jnp.logical_and(i >= nb, i < n_items + nb))
                def _():
                    wr(i - nb, b).wait()

                @pl.when(i < n_items)
                def _():
                    gath(slot, l // kk, l % kk, b).start()

                @pl.when(jnp.logical_and(i >= nb - 1, i < n_items + nb - 1))
                def _():
                    gath(slot, 0, 0, bw).wait()
                    wr(i - (nb - 1), bw).start()

    return k(y, idx)


def _moe_kernel(layer, info_ref, x_ref, w1_hbm, b1_ref, w2_hbm, b2_ref, o_ref,
                w1f_ref, w2f_ref, w1b_ref, w2b_ref, sem):
    i = pl.program_id(0)
    nb = pl.num_programs(0)
    expert = info_ref[i]
    nxt = info_ref[nb + i]
    slot = info_ref[2 * nb + i]
    first = jnp.logical_or(i == 0, expert != info_ref[jnp.maximum(i - 1, 0)])

    def fetch(e, s):
        return (pltpu.make_async_copy(w1_hbm.at[layer, e], w1f_ref.at[s], sem.at[0, s]),
                pltpu.make_async_copy(w2_hbm.at[layer, e], w2f_ref.at[s], sem.at[1, s]))

    @pl.when(i == 0)
    def _():
        for cp in fetch(expert, slot):
            cp.start()

    @pl.when(first)
    def _():
        for cp in fetch(expert, slot):
            cp.wait()

        @pl.when(nxt >= 0)
        def _():
            for cp in fetch(nxt, 1 - slot):
                cp.start()

        w1b_ref[...] = w1f_ref[slot].astype(BF16)
        w2b_ref[...] = w2f_ref[slot].astype(BF16)

    units = (info_ref[3 * nb + i] + MOE_UNIT - 1) // MOE_UNIT
    for u in range(1, x_ref.shape[0] // MOE_UNIT + 1):
        @pl.when(units == u)
        def _():
            rows = slice(0, u * MOE_UNIT)
            _expert_block(x_ref.at[rows], w1b_ref, b1_ref, w2b_ref, b2_ref, o_ref.at[rows])


def _expert_block(x_ref, w1b_ref, b1_ref, w2b_ref, b2_ref, o_ref):
    lo, hi = _unpack_pairs(x_ref[...])
    x = jnp.concatenate([lo.astype(BF16), hi.astype(BF16)], axis=-1)
    f = w2b_ref.shape[0]
    fc = 512
    acc = None
    for c in range(f // fc):
        gt = jnp.dot(x, w1b_ref[:, c * fc:(c + 1) * fc], preferred_element_type=F32) + b1_ref[0, 0, :, c * fc:(c + 1) * fc]
        up = jnp.dot(x, w1b_ref[:, f + c * fc:f + (c + 1) * fc], preferred_element_type=F32) + b1_ref[0, 0, :, f + c * fc:f + (c + 1) * fc]
        gt = jnp.minimum(gt, SWIGLU_LIMIT)
        up = jnp.clip(up, -SWIGLU_LIMIT, SWIGLU_LIMIT)
        act = (gt * jax.nn.sigmoid(SWIGLU_ALPHA * gt) * (up + 1.0)).astype(BF16)
        part = jnp.dot(act, w2b_ref[c * fc:(c + 1) * fc, :], preferred_element_type=F32)
        acc = part if acc is None else acc + part
    o_ref[...] = _pack_pairs(acc + b2_ref[0, 0])


def _moe_experts(layer, block_expert, xs, p_rows, w1, b1, w2, b2):
    _, e, d, f2 = w1.shape
    f = f2 // 2
    nb = p_rows // MOE_ROWS
    grid_spec = pltpu.PrefetchScalarGridSpec(
        num_scalar_prefetch=1,
        grid=(nb,),
        in_specs=[pl.BlockSpec((MOE_ROWS, d // 2), lambda i, be: (i, 0)),
                  pl.BlockSpec(memory_space=pl.ANY),
                  pl.BlockSpec((1, 1, 1, f2), lambda i, be: (layer, be[i], 0, 0)),
                  pl.BlockSpec(memory_space=pl.ANY),
                  pl.BlockSpec((1, 1, 1, d), lambda i, be: (layer, be[i], 0, 0))],
        out_specs=pl.BlockSpec((MOE_ROWS, d // 2), lambda i, be: (i, 0)),
        scratch_shapes=[pltpu.VMEM((2, d, f2), F32), pltpu.VMEM((2, f, d), F32),
                        pltpu.VMEM((d, f2), BF16), pltpu.VMEM((f, d), BF16),
                        pltpu.SemaphoreType.DMA((2, 2))],
    )
    return pl.pallas_call(
        functools.partial(_moe_kernel, layer),
        grid_spec=grid_spec,
        out_shape=jax.ShapeDtypeStruct((p_rows, d // 2), U32),
        compiler_params=_cparams("arbitrary"),
        name="moe_experts",
    )(block_expert, xs, w1, b1.reshape(b1.shape[0], e, 1, f2), w2, b2.reshape(b2.shape[0], e, 1, d))


def _moe_dispatch(h2_words, lg_t):
    gate, dest, block_info, p_rows = _route(lg_t)
    spare = p_rows + jnp.arange(SC_SCATTER_WIN, dtype=I32)
    xs = _sc_scatter_rows(h2_words, _windows(dest, SC_SCATTER_WIN, 2, spare), p_rows + SC_SCATTER_WIN)
    return xs, gate, dest, block_info, p_rows


def _moe_compute(layer, dispatched, w1, b1, w2, b2):
    xs, gate, back, block_info, p_rows = dispatched
    ys = _moe_experts(layer, block_info, xs, p_rows, w1, b1, w2, b2)
    return _sc_gather_rows(ys, back), gate


def _final_kernel(x_ref, yg_ref, gate_ref, g2_ref, nw_ref, *rest):
    o_ref = rest[-1]
    x = x_ref[0] + g2_ref[0] * _combine(yg_ref, gate_ref)
    ms = jnp.mean(x * x, axis=-1, keepdims=True)
    o_ref[0] = x * lax.rsqrt(ms + EPS) * nw_ref[...]


def _final(x, res, nw, ts, out_buf, ob0, b_total):
    b, s, d = x.shape
    nj = s // ts
    yg, gate, g2, first = res
    blk0 = first // ts
    row = pl.BlockSpec((1, ts, d), lambda i, j: (i, j, 0))
    if gate.ndim == 2:
        gate_spec = pl.BlockSpec((GATE_ROWS, ts), lambda i, j: (0, blk0 + i * nj + j))
    else:
        gate_spec = pl.BlockSpec((1, ts, TOP_K), lambda i, j: (i, j, 0))
    in_specs = [row, pl.BlockSpec((TOP_K, ts, d // 2), lambda i, j: (0, blk0 + i * nj + j, 0)),
                gate_spec,
                pl.BlockSpec((1, 1, d), lambda i, j: (i, 0, 0)),
                pl.BlockSpec((1, d), lambda i, j: (0, 0))]
    args = [x, yg, gate, g2, nw.reshape(1, d)]
    aliases = {}
    if out_buf is not None:
        in_specs.append(pl.BlockSpec(memory_space=pl.ANY))
        args.append(out_buf)
        aliases = {len(args) - 1: 0}
    return pl.pallas_call(
        _final_kernel,
        grid=(b, nj),
        in_specs=in_specs,
        out_specs=pl.BlockSpec((1, ts, d), lambda i, j: (i + ob0, j, 0)),
        out_shape=jax.ShapeDtypeStruct((b_total, s, d), F32),
        input_output_aliases=aliases,
        compiler_params=_cparams("parallel", "parallel"),
        name="final_norm",
    )(*args)


def kernel(x_prompt, x_sample, c_prompt, c_sample, cache_k, cache_v, state_hgrn, norm_mix, norm_ffn, norm_final, w_ada, b_ada, w_in, rel_bias, hgrn_lb_logits, g_attn_out, g_hgrn_out, w_out, w_router, b_router, w_e1, b_e1, w_e2, b_e2):
    depth = w_in.shape[0]
    bp, sp, d = x_prompt.shape
    bs, ls, _ = x_sample.shape
    wc = cache_k.shape[2]
    n_p, n_s = bp * sp, bs * ls

    lbs = jax.nn.softmax(hgrn_lb_logits.astype(F32), axis=0)
    lbs = jnp.cumsum(lbs, axis=0) - lbs[0]

    mods = _modulation(jnp.concatenate([c_prompt, c_sample], axis=0), w_ada, b_ada)
    lk_s = -(-(wc + ls) // LANES) * LANES
    tail = min(LEFT, sp)
    rec_chunk = min(REC_CHUNK, sp)

    n_groups = PROMPT_GROUPS if bp % PROMPT_GROUPS == 0 else 1
    gb = bp // n_groups
    n_g = gb * sp
    x_g = [(x_prompt, g * gb) for g in range(n_groups)]
    x_s = (x_sample, 0)
    res_g = [None] * n_groups
    res_s = None
    kp_l, vp_l, sp_l, ks_l, vs_l, ss_l = [], [], [], [], [], []
    for layer in range(depth):
        m6 = mods[layer].reshape(bp + bs, 1, N_MOD, d)
        mod_s = [m6[bp:, :, n, :] for n in range(N_MOD)]
        w_in_bf = w_in[layer].astype(BF16)
        w_out_bf = w_out[layer].astype(BF16)
        wr_hi, wr_mid, _ = _split3(w_router[layer].T)
        wr2 = jnp.concatenate([wr_hi, wr_mid], axis=0)
        table = rel_bias[layer]
        bias_p = _prompt_bias(table)

        def mixer_prompt(g):
            mod = [m6[g * gb:(g + 1) * gb, :, n, :] for n in range(N_MOD)]
            x_new, h2, lg, kt, vt, s_fin = _mixer_prompt(
                x_g[g][0], x_g[g][1], res_g[g], norm_mix[layer], mod[1], mod[0], w_in_bf, lbs[layer], bias_p,
                g_attn_out[layer], g_hgrn_out[layer], w_out_bf, mod[2], norm_ffn[layer], mod[4], mod[3],
                wr2, b_router[layer], tail, rec_chunk, n_s if g == n_groups - 1 else 0)
            x_g[g] = (x_new, 0)
            return h2, lg, kt, vt, s_fin, mod[5]

        def mixer_sample(rows):
            x, qa, ka, va, kt, vt, qh, kh, lf, vh, gh = _inproj(
                x_s[0], x_s[1], res_s, norm_mix[layer], mod_s[1], mod_s[0], w_in_bf, lbs[layer], ls, ls)
            o_att = _attn_sample(qa, cache_k[layer].reshape(bs, wc, ATT_W), cache_v[layer].reshape(bs, wc, ATT_W),
                                 ka, va, _sample_bias(table, ls, wc, lk_s))
            s0 = jnp.swapaxes(state_hgrn[layer].astype(F32), -1, -2)
            o_rec, s_new = _hgrn(qh, kh, lf, vh, s0, ls, ls)
            x_new, rows, lg = _merge(o_att, o_rec, gh, x[0], x[1], g_attn_out[layer], g_hgrn_out[layer], w_out_bf,
                                     mod_s[2], norm_ffn[layer], mod_s[4], mod_s[3], wr2, b_router[layer], rows, n_g)
            return (x_new, 0), rows, jnp.swapaxes(lg, 0, 1).reshape(N_EXPERTS, n_s), kt, vt, s_new

        dispatched, g2s, k_t, v_t, s_t = [], [], [], [], []
        for g in range(n_groups):
            h2, lg, kt, vt, s_fin, g2 = mixer_prompt(g)
            if g == n_groups - 1:
                x_s, h2, lg_s, ka_s, va_s, s_new = mixer_sample(h2)
                ks_l.append(ka_s.reshape(bs, ls, ATT_HEADS, ATT_DIM))
                vs_l.append(va_s.reshape(bs, ls, ATT_HEADS, ATT_DIM))
                ss_l.append(jnp.swapaxes(s_new, -1, -2))
                lg = jnp.concatenate([lg, lg_s], axis=1)
            dispatched.append(_moe_dispatch(h2, lg))
            g2s.append(g2)
            k_t.append(kt)
            v_t.append(vt)
            s_t.append(s_fin)
        kp_l.append(jnp.concatenate(k_t, axis=0).reshape(bp, tail, ATT_HEADS, ATT_DIM))
        vp_l.append(jnp.concatenate(v_t, axis=0).reshape(bp, tail, ATT_HEADS, ATT_DIM))
        sp_l.append(jnp.swapaxes(jnp.concatenate(s_t, axis=0), -1, -2))

        for g in range(n_groups):
            yg, gate = _moe_compute(layer, dispatched[g], w_e1, b_e1, w_e2, b_e2)
            res_g[g] = (yg, gate, g2s[g], 0)
            if g == n_groups - 1:
                res_s = (yg, gate[:TOP_K, n_g:].T.reshape(bs, ls, TOP_K), mod_s[5], n_g)

    y_prompt = None
    for g in range(n_groups):
        y_prompt = _final(x_g[g][0], res_g[g], norm_final, SEQ_TILE, y_prompt, g * gb, bp)
    y_sample = _final(x_s[0], res_s, norm_final, ls, None, 0, bs)
    return (y_prompt, y_sample, jnp.stack(kp_l), jnp.stack(vp_l), jnp.stack(sp_l),
            jnp.stack(ks_l), jnp.stack(vs_l), jnp.stack(ss_l))
```

```python
import functools

import jax
import jax.numpy as jnp
import numpy as np
from jax import lax
from jax.experimental import pallas as pl
from jax.experimental.pallas import tpu as pltpu
from jax.experimental.pallas import tpu_sc as plsc

F32 = jnp.float32
BF16 = jnp.bfloat16
I32 = jnp.int32
U32 = jnp.uint32

EPS = 1e-6
CHUNK = 64
LEFT_CHUNKS = 8
LEFT = LEFT_CHUNKS * CHUNK
MAX_REL = 2 * CHUNK
ATT_HEADS = 8
ATT_DIM = 64
ATT_W = ATT_HEADS * ATT_DIM
REC_HEADS = 4
REC_DIM = 128
REC_W = REC_HEADS * REC_DIM
N_EXPERTS = 32
TOP_K = 4
SWIGLU_ALPHA = 1.702
SWIGLU_LIMIT = 7.0
N_MOD = 6
NEG = -1e30
LOG2E = 1.4426950408889634

LANES = 128
ATT_QSUB = 128
ATT_WIN = LEFT + ATT_QSUB
SEQ_TILE = 512
REC_CHUNK = 128
MOE_ROWS = 1024
MOE_UNIT = 256
ROUTE_TILE = 256
ROUTE_MAX_SUB = 4
GROUP_SHARES = ((3, 4), (1, 4))
GATE_ROWS = 8
VMEM_LIMIT = 56 * 1024 * 1024
SC_WORKERS = 32
SC_SCATTER_WIN = 64
SC_GATHER_WIN = 32
SC_GATHER_BUFS = 4


def _cparams(*sem):
    return pltpu.CompilerParams(dimension_semantics=sem, vmem_limit_bytes=VMEM_LIMIT)


def _split3(x):
    hi = x.astype(BF16)
    r1 = x - hi.astype(F32)
    mid = r1.astype(BF16)
    lo = (r1 - mid.astype(F32)).astype(BF16)
    return hi, mid, lo


def _bits(x):
    return lax.bitcast_convert_type(x, U32)


def _pack_pairs(x):
    r = _bits(x.astype(BF16).astype(F32))
    half = x.shape[1] // 2
    return jnp.bitwise_or(jnp.bitwise_and(r[:, half:], jnp.uint32(0xFFFF0000)),
                          jnp.right_shift(r[:, :half], jnp.uint32(16)))


def _unpack_pairs(word):
    lo = lax.bitcast_convert_type(jnp.left_shift(word, jnp.uint32(16)), F32)
    hi = lax.bitcast_convert_type(jnp.bitwise_and(word, jnp.uint32(0xFFFF0000)), F32)
    return lo, hi


def _mod_kernel(c_ref, w_ref, b_ref, o_ref):
    c = c_ref[...]
    act = (c * jax.nn.sigmoid(c)).astype(BF16)
    o_ref[0] = jnp.dot(act, w_ref[0].astype(BF16), preferred_element_type=F32) + b_ref[0]


def _modulation(c_all, w_ada, b_ada):
    depth, d, n6 = w_ada.shape
    rows = c_all.shape[0]
    tn = 1536
    return pl.pallas_call(
        _mod_kernel,
        grid=(depth, n6 // tn),
        in_specs=[pl.BlockSpec((rows, d), lambda l, j: (0, 0)),
                  pl.BlockSpec((1, d, tn), lambda l, j: (l, 0, j)),
                  pl.BlockSpec((1, 1, tn), lambda l, j: (l, 0, j))],
        out_specs=pl.BlockSpec((1, rows, tn), lambda l, j: (l, 0, j)),
        out_shape=jax.ShapeDtypeStruct((depth, rows, n6), F32),
        compiler_params=_cparams("parallel", "parallel"),
        name="adaln_modulation",
    )(c_all, w_ada, b_ada.reshape(depth, 1, n6))


def _combine(yg_ref, gate_ref):
    g = gate_ref[0] if len(gate_ref.shape) == 3 else gate_ref[...].T
    lo = hi = None
    for k in range(TOP_K):
        l, h = _unpack_pairs(yg_ref[k])
        gk = g[:, k:k + 1]
        lo = l * gk if lo is None else lo + l * gk
        hi = h * gk if hi is None else hi + h * gk
    return jnp.concatenate([lo, hi], axis=-1)


def _inproj_kernel(has_res, *refs):
    if has_res:
        (x_ref, yg_ref, gate_ref, g2_ref, nw_ref, sc_ref, sh_ref, w_ref, lb_ref,
         xo_ref, qa_ref, ka_ref, va_ref, kt_ref, vt_ref, qh_ref, kh_ref, lf_ref, vh_ref, gh_ref) = refs
        x = x_ref[0] + g2_ref[0] * _combine(yg_ref, gate_ref)
        xo_ref[0] = x
    else:
        (x_ref, nw_ref, sc_ref, sh_ref, w_ref, lb_ref,
         qa_ref, ka_ref, va_ref, kt_ref, vt_ref, qh_ref, kh_ref, lf_ref, vh_ref, gh_ref) = refs
        x = x_ref[0]
    _project_rows(x, nw_ref, sc_ref, sh_ref, w_ref, lb_ref, qa_ref.at[0], ka_ref.at[0], va_ref.at[0],
                  kt_ref.at[0], vt_ref.at[0], qh_ref.at[0], kh_ref.at[0], lf_ref.at[0], vh_ref.at[0], gh_ref.at[0])


def _project_rows(x, nw_ref, sc_ref, sh_ref, w_ref, lb_ref,
                  qa_ref, ka_ref, va_ref, kt_ref, vt_ref, qh_ref, kh_ref, lf_ref, vh_ref, gh_ref):
    ms = jnp.mean(x * x, axis=-1, keepdims=True)
    h = (x * lax.rsqrt(ms + EPS) * nw_ref[...]) * (1.0 + sc_ref[0]) + sh_ref[0]
    hb = h.astype(BF16)

    def proj(g):
        return jnp.dot(hb, w_ref[:, g * ATT_W:(g + 1) * ATT_W], preferred_element_type=F32)

    qa_ref[...] = (proj(0) * (ATT_DIM ** -0.5)).astype(BF16)
    k = proj(1)
    ka_ref[...] = k.astype(BF16)
    kt_ref[...] = k
    v = proj(2)
    va_ref[...] = v.astype(BF16)
    vt_ref[...] = v
    qh = proj(3)
    qh_ref[...] = qh * jax.nn.sigmoid(qh)
    f = proj(4)
    lb = lb_ref[...]
    lf_ref[...] = jnp.log(lb + (1.0 - lb) * jax.nn.sigmoid(f))
    kh_ref[...] = (1.0 - lb) * jax.nn.sigmoid(-f)
    vh_ref[...] = proj(5).astype(BF16)
    gh_ref[...] = proj(6)


def _inproj(x, xb0, res, nw, sc, sh, w_bf, lb, ts, tail):
    _, s, d = x.shape
    b = sc.shape[0]
    pw = w_bf.shape[1]
    nj = s // ts
    n_tail = tail // ts
    has_res = res is not None
    tail_spec = pl.BlockSpec((1, ts, ATT_W), lambda i, j: (i, jnp.maximum(j - (nj - n_tail), 0), 0))
    ht = jax.ShapeDtypeStruct((b, tail, ATT_W), F32)
    row = pl.BlockSpec((1, ts, d), lambda i, j: (i, j, 0))
    per_b = pl.BlockSpec((1, 1, d), lambda i, j: (i, 0, 0))
    half = pl.BlockSpec((1, ts, ATT_W), lambda i, j: (i, j, 0))
    in_specs, args = [pl.BlockSpec((1, ts, d), lambda i, j: (i + xb0, j, 0))], [x]
    if has_res:
        yg, gate, g2, first = res
        blk0 = first // ts
        in_specs += [pl.BlockSpec((TOP_K, ts, d // 2), lambda i, j: (0, blk0 + i * nj + j, 0)),
                     pl.BlockSpec((1, ts, TOP_K), lambda i, j: (i, j, 0)), per_b]
        args += [yg, gate, g2]
    in_specs += [pl.BlockSpec((1, d), lambda i, j: (0, 0)), per_b, per_b,
                 pl.BlockSpec((d, pw), lambda i, j: (0, 0)),
                 pl.BlockSpec((1, REC_W), lambda i, j: (0, 0))]
    args += [nw.reshape(1, d), sc, sh, w_bf, lb.reshape(1, REC_W)]
    hs = jax.ShapeDtypeStruct((b, s, ATT_W), F32)
    hb = jax.ShapeDtypeStruct((b, s, ATT_W), BF16)
    out_shape = ([jax.ShapeDtypeStruct((b, s, d), F32)] if has_res else []) + [hb, hb, hb, ht, ht, hs, hs, hs, hb, hs]
    out_specs = ([row] if has_res else []) + [half] * 3 + [tail_spec] * 2 + [half] * 5
    outs = pl.pallas_call(
        functools.partial(_inproj_kernel, has_res),
        grid=(b, nj),
        in_specs=in_specs, out_specs=out_specs, out_shape=out_shape,
        compiler_params=_cparams("parallel", "arbitrary"),
        name="norm_inproj",
    )(*args)
    if has_res:
        return [(outs[0], 0)] + list(outs[1:])
    return [(x, xb0)] + list(outs)


def _attend(q, kw, vw, bias_ref, s_ref, col_thr):
    lq, lk = q.shape[0], kw.shape[0]
    lane = lax.broadcasted_iota(I32, (lq, LANES), 1)
    lo = lane < ATT_DIM
    nt = (((1,), (1,)), ((), ()))
    if col_thr is not None:
        keep = lax.broadcasted_iota(I32, (2 * lq, lk), 1) >= col_thr
    ms = []
    for p in range(ATT_HEADS // 2):
        sl = slice(p * LANES, (p + 1) * LANES)
        qp = q[:, sl]
        z = jnp.zeros_like(qp)
        lhs = jnp.concatenate([jnp.where(lo, qp, z), jnp.where(lo, z, qp)], axis=0)
        s = lax.dot_general(lhs, kw[:, sl], nt, preferred_element_type=F32) + bias_ref[p]
        if col_thr is not None:
            s = jnp.where(keep, s, NEG)
        ms.append(jnp.max(s, axis=-1, keepdims=True))
        s_ref[p] = s
    outs = []
    for p in range(ATT_HEADS // 2):
        sl = slice(p * LANES, (p + 1) * LANES)
        e = jnp.exp(s_ref[p] - ms[p])
        l = jnp.sum(e, axis=-1, keepdims=True)
        o = jnp.dot(e.astype(BF16), vw[:, sl], preferred_element_type=F32) / l
        outs.append(jnp.where(lo, o[:lq], o[lq:]))
    return jnp.concatenate(outs, axis=-1)


def _band_attention(q_ref, kw_ref, vw_ref, bias_ref, s_ref, o_ref, first_tile):
    ts = q_ref.shape[0]

    def body(masked, r, carry):
        r0 = pl.multiple_of(r * ATT_QSUB, ATT_QSUB)
        q = q_ref[pl.ds(r0, ATT_QSUB), :]
        kw = kw_ref[pl.ds(r0, ATT_WIN), :]
        vw = vw_ref[pl.ds(r0, ATT_WIN), :]
        o_ref[pl.ds(r0, ATT_QSUB), :] = _attend(q, kw, vw, bias_ref, s_ref, (ts - r0) if masked else None)
        return carry

    @pl.when(first_tile)
    def _():
        lax.fori_loop(0, ts // ATT_QSUB, functools.partial(body, True), 0)

    @pl.when(jnp.logical_not(first_tile))
    def _():
        lax.fori_loop(0, ts // ATT_QSUB, functools.partial(body, False), 0)


def _attn_sample_kernel(q_ref, ck_ref, cv_ref, k_ref, v_ref, bias_ref, o_ref, kw_ref, vw_ref, s_ref):
    wc, l, lk = ck_ref.shape[1], k_ref.shape[1], kw_ref.shape[0]
    for dst, cache, new in ((kw_ref, ck_ref, k_ref), (vw_ref, cv_ref, v_ref)):
        dst[0:wc, :] = cache[0].astype(BF16)
        dst[wc:wc + l, :] = new[0]
        dst[wc + l:lk, :] = jnp.zeros((lk - wc - l, ATT_W), BF16)
    o_ref[0] = _attend(q_ref[0], kw_ref[...], vw_ref[...], bias_ref, s_ref, None)


def _attn_sample(qa, cache_k, cache_v, ka, va, bias):
    b, l, w = qa.shape
    wc = cache_k.shape[1]
    lk = bias.shape[2]
    new = pl.BlockSpec((1, l, w), lambda i: (i, 0, 0))
    old = pl.BlockSpec((1, wc, w), lambda i: (i, 0, 0))
    return pl.pallas_call(
        _attn_sample_kernel,
        grid=(b,),
        in_specs=[new, old, old, new, new, pl.BlockSpec(bias.shape, lambda i: (0, 0, 0))],
        out_specs=new,
        out_shape=jax.ShapeDtypeStruct((b, l, w), F32),
        scratch_shapes=[pltpu.VMEM((lk, w), BF16), pltpu.VMEM((lk, w), BF16),
                        pltpu.VMEM((ATT_HEADS // 2, 2 * l, lk), F32)],
        compiler_params=_cparams("parallel"),
        name="cache_attention",
    )(qa, cache_k, cache_v, ka, va, bias)


def _toeplitz_bias(table, lq, lk, offset, valid):
    m = lq + lk - 1
    k = np.arange(m)
    diff = np.where(k < lk, k, k - m)
    g = table[:, np.clip(offset - diff, -MAX_REL, MAX_REL) + MAX_REL]
    skew = jnp.tile(g, (1, lq))[:, :lq * (m - 1)].reshape(-1, lq, m - 1)[:, :, :lk]
    bias = jnp.where(jnp.asarray(valid)[None], skew, NEG).astype(F32)
    return bias.reshape(ATT_HEADS // 2, 2 * lq, lk)


def _prompt_bias(table):
    t = np.arange(ATT_QSUB)[:, None]
    j = np.arange(ATT_WIN)[None, :]
    start = (t // CHUNK) * CHUNK
    return _toeplitz_bias(table, ATT_QSUB, ATT_WIN, LEFT, (j >= start) & (j < start + LEFT + CHUNK))


def _sample_bias(table, l, w, lk):
    valid = np.broadcast_to(np.arange(lk)[None, :] < w + l, (l, lk))
    return _toeplitz_bias(table, l, lk, w, valid)


def _neg_abs(x):
    return lax.bitcast_convert_type(jnp.bitwise_or(_bits(x), jnp.uint32(0x80000000)), F32)


def _hgrn_kernel(chunk, q_ref, k_ref, lf_ref, v_ref, s0_ref, o_ref, sf_ref, st_ref):
    i = pl.program_id(1)

    @pl.when(i == 0)
    def _():
        st_ref[...] = s0_ref[0]

    _hgrn_rows(chunk, q_ref.at[0], k_ref.at[0], lf_ref.at[0], v_ref.at[0], o_ref.at[0], st_ref)

    @pl.when(i == pl.num_programs(1) - 1)
    def _():
        sf_ref[0] = st_ref[...]


def _hgrn_rows(chunk, q_ref, k_ref, lf_ref, v_ref, o_ref, st_ref):
    ts = q_ref.shape[0]
    n_chunks = ts // chunk

    row = lax.broadcasted_iota(I32, (chunk, chunk), 0)
    colm = lax.broadcasted_iota(I32, (chunk, chunk), 1)
    tri = (colm <= row).astype(BF16)
    xor = jnp.bitwise_xor(row, colm)
    lower = colm < row
    rid = lax.broadcasted_iota(I32, (chunk, REC_W), 0)

    levels = []
    b = chunk // 2
    while b >= 1:
        levels.append(b)
        b //= 2
    pair_mask = [xor == 0] + [lower & (xor >= b) & (xor < 2 * b) for b in levels]
    upper = [jnp.bitwise_and(rid, b) != 0 for b in levels]
    ph4 = jnp.bitwise_and(rid, 3)
    odd = jnp.bitwise_and(rid, 1) == 1
    nt = (((1,), (1,)), ((), ()))
    tn = (((0,), (0,)), ((), ()))
    sls = [slice(h * REC_DIM, (h + 1) * REC_DIM) for h in range(REC_HEADS)]

    def boundary(cum, b):
        if 2 * b >= 8:
            pieces = [jnp.broadcast_to(cum[m * 2 * b + b - 1:m * 2 * b + b, :], (2 * b, REC_W))
                      for m in range(chunk // (2 * b))]
            return pieces[0] if len(pieces) == 1 else jnp.concatenate(pieces, axis=0)
        if b == 2:
            up1 = pltpu.roll(cum, chunk - 1, 0)
            dn1 = pltpu.roll(cum, 1, 0)
            dn2 = pltpu.roll(cum, 2, 0)
            return jnp.where(ph4 == 0, up1, jnp.where(ph4 == 1, cum, jnp.where(ph4 == 2, dn1, dn2)))
        return jnp.where(odd, pltpu.roll(cum, 1, 0), cum)

    def one_chunk(c, carry):
        r0 = pl.multiple_of(c * chunk, chunk)
        rows = pl.ds(r0, chunk)
        q = q_ref[rows, :]
        k = k_ref[rows, :]
        v = v_ref[rows, :]
        lf = lf_ref[rows, :]
        cum = sum(jnp.dot(tri, part, preferred_element_type=F32) for part in _split3(lf)) * LOG2E
        last = cum[chunk - 1:chunk, :]
        q_in = (q * jnp.exp2(cum)).astype(BF16)
        k_out = (k * jnp.exp2(last - cum)).astype(BF16)
        decay = jnp.exp2(last)
        qb = q.astype(BF16)
        kb = k.astype(BF16)
        zs = []
        for n, b in enumerate(levels):
            w = jnp.exp2(_neg_abs(cum - boundary(cum, b)))
            zs.append((jnp.where(upper[n], q, k) * w).astype(BF16))
        sts = [st_ref[h] for h in range(REC_HEADS)]
        d0 = [lax.dot_general(qb[:, sl], kb[:, sl], nt, preferred_element_type=F32) for sl in sls]
        dl = [[lax.dot_general(z[:, sl], z[:, sl], nt, preferred_element_type=F32) for z in zs] for sl in sls]
        oi = [lax.dot_general(q_in[:, sl], st.astype(BF16), nt, preferred_element_type=F32)
              for sl, st in zip(sls, sts)]
        upd = [lax.dot_general(v[:, sl], k_out[:, sl], tn, preferred_element_type=F32) for sl in sls]
        outs = []
        for h, sl in enumerate(sls):
            a = jnp.where(pair_mask[0], d0[h], 0.0)
            for n in range(len(levels)):
                a = jnp.where(pair_mask[n + 1], dl[h][n], a)
            outs.append(jnp.dot(a.astype(BF16), v[:, sl], preferred_element_type=F32) + oi[h])
        o_ref[rows, :] = jnp.concatenate(outs, axis=-1)
        for h, sl in enumerate(sls):
            st_ref[h] = decay[:, sl] * sts[h] + upd[h]
        return carry

    lax.fori_loop(0, n_chunks, one_chunk, 0, unroll=2 if n_chunks % 2 == 0 else 1)


def _hgrn(q, k, lf, v, s0_t, ts, chunk):
    b, s, w = q.shape
    blk = pl.BlockSpec((1, ts, w), lambda i, j: (i, j, 0))
    st = pl.BlockSpec((1, REC_HEADS, REC_DIM, REC_DIM), lambda i, j: (i, 0, 0, 0))
    return pl.pallas_call(
        functools.partial(_hgrn_kernel, chunk),
        grid=(b, s // ts),
        in_specs=[blk, blk, blk, blk, st],
        out_specs=[blk, st],
        out_shape=[jax.ShapeDtypeStruct((b, s, w), F32),
                   jax.ShapeDtypeStruct((b, REC_HEADS, REC_DIM, REC_DIM), F32)],
        scratch_shapes=[pltpu.VMEM((REC_HEADS, REC_DIM, REC_DIM), F32)],
        compiler_params=_cparams("parallel", "arbitrary"),
        name="hgrn_recurrence",
    )(q, k, lf, v, s0_t)


def _head_rms(x, width):
    lane = lax.broadcasted_iota(I32, (x.shape[0], LANES), 1)
    lo = lane < ATT_DIM
    outs = []
    for p in range(x.shape[1] // LANES):
        xp = x[:, p * LANES:(p + 1) * LANES]
        sq = xp * xp
        tot = jnp.sum(sq, axis=-1, keepdims=True)
        if width == LANES:
            scale = lax.rsqrt(tot * (1.0 / LANES) + EPS)
        else:
            s_lo = jnp.sum(jnp.where(lo, sq, 0.0), axis=-1, keepdims=True)
            scale = jnp.where(lo, lax.rsqrt(s_lo * (1.0 / ATT_DIM) + EPS),
                              lax.rsqrt((tot - s_lo) * (1.0 / ATT_DIM) + EPS))
        outs.append(xp * scale)
    return jnp.concatenate(outs, axis=-1)


def _merge_rows(oa, orr, gh, x, ga_ref, gr_ref, wo_ref, g1_ref, nf_ref, sc_ref, sh_ref, wr_ref, br_ref):
    a = _head_rms(oa, ATT_DIM) * ga_ref[...]
    r = _head_rms(orr, REC_DIM) * gr_ref[...] * (gh * jax.nn.sigmoid(gh))
    mix = jnp.dot(a.astype(BF16), wo_ref[0:ATT_W, :], preferred_element_type=F32)
    mix += jnp.dot(r.astype(BF16), wo_ref[ATT_W:ATT_W + REC_W, :], preferred_element_type=F32)
    x = x + g1_ref[0] * mix
    ms = jnp.mean(x * x, axis=-1, keepdims=True)
    h2 = (x * lax.rsqrt(ms + EPS) * nf_ref[...]) * (1.0 + sc_ref[0]) + sh_ref[0]
    hh = h2.astype(BF16)
    words = _pack_pairs(h2)
    hm = (h2 - hh.astype(F32)).astype(BF16)
    nt = (((1,), (1,)), ((), ()))
    l1 = lax.dot_general(wr_ref[...], hh, nt, preferred_element_type=F32)
    l2 = lax.dot_general(wr_ref[0:N_EXPERTS, :], hm, nt, preferred_element_type=F32)
    lg = l1[0:N_EXPERTS] + l1[N_EXPERTS:2 * N_EXPERTS] + l2 + br_ref[...]
    return x, words, lg


def _merge_kernel(oa_ref, or_ref, gh_ref, x_ref, ga_ref, gr_ref, wo_ref, g1_ref, nf_ref, sc_ref, sh_ref,
                  wr_ref, br_ref, rows_hbm, xo_ref, h2_ref, lg_ref):
    del rows_hbm
    x, words, lg = _merge_rows(oa_ref[0], or_ref[0], gh_ref[0], x_ref[0], ga_ref, gr_ref, wo_ref, g1_ref,
                               nf_ref, sc_ref, sh_ref, wr_ref, br_ref)
    xo_ref[0] = x
    h2_ref[...] = words
    lg_ref[0] = lg


def _merge(o_att, o_rec, gh, x, xb0, g_att, g_rec, wo_bf, g1, nf, sc, sh, wr2, br, rows, row0):
    _, ts, d = x.shape
    b = o_att.shape[0]
    row = pl.BlockSpec((1, ts, d), lambda i: (i, 0, 0))
    xrow = pl.BlockSpec((1, ts, d), lambda i: (i + xb0, 0, 0))
    half = pl.BlockSpec((1, ts, ATT_W), lambda i: (i, 0, 0))
    per_b = pl.BlockSpec((1, 1, d), lambda i: (i, 0, 0))
    const2 = lambda shp: pl.BlockSpec(shp, lambda i: (0,) * len(shp))
    blk0 = row0 // ts
    return pl.pallas_call(
        _merge_kernel,
        grid=(b,),
        in_specs=[half, half, half, xrow, const2((1, ATT_W)), const2((1, REC_W)), const2(wo_bf.shape),
                  per_b, const2((1, d)), per_b, per_b, const2(wr2.shape), const2((N_EXPERTS, 1)),
                  pl.BlockSpec(memory_space=pl.ANY)],
        out_specs=[row, pl.BlockSpec((ts, d // 2), lambda i: (blk0 + i, 0)),
                   pl.BlockSpec((1, N_EXPERTS, ts), lambda i: (i, 0, 0))],
        out_shape=[jax.ShapeDtypeStruct((b, ts, d), F32), jax.ShapeDtypeStruct(rows.shape, U32),
                   jax.ShapeDtypeStruct((b, N_EXPERTS, ts), F32)],
        input_output_aliases={13: 1},
        compiler_params=_cparams("parallel"),
        name="merge_outproj_router",
    )(o_att, o_rec, gh, x, g_att.reshape(1, ATT_W), g_rec.reshape(1, REC_W), wo_bf, g1, nf.reshape(1, d),
      sc, sh, wr2, br.reshape(N_EXPERTS, 1), rows)


def _mixer_kernel(has_res, chunk, *refs):
    n_in = 4 if has_res else 1
    x_ref = refs[0]
    (nw_ref, sc1_ref, sh1_ref, lb_ref, ga_ref, gr_ref, g1_ref, nf_ref, sc2_ref, sh2_ref, wr_ref, br_ref,
     win_hbm, wout_hbm, bias_hbm,
     xo_ref, h2_ref, lg_ref, kt_ref, vt_ref, sf_ref,
     win_v, wout_v, bias_v, q_s, kw_s, vw_s, s_s, qh_s, kh_s, lf_s, vh_s, gh_s, oa_s, or_s, st_s, sem) = refs[n_in:]
    j = pl.program_id(1)
    ts = x_ref.shape[1]

    @pl.when(j == 0)
    def _():
        copies = [pltpu.make_async_copy(src, dst, sem.at[n])
                  for n, (src, dst) in enumerate(((win_hbm, win_v), (wout_hbm, wout_v), (bias_hbm, bias_v)))]
        for cp in copies:
            cp.start()
        for cp in copies:
            cp.wait()
        st_s[...] = jnp.zeros_like(st_s)
        kw_s[0:ts, :] = jnp.zeros((ts, ATT_W), BF16)
        vw_s[0:ts, :] = jnp.zeros((ts, ATT_W), BF16)

    if has_res:
        yg_ref, gate_ref, g2_ref = refs[1:4]
        x = x_ref[0] + g2_ref[0] * _combine(yg_ref, gate_ref)
        xo_ref[0] = x
    else:
        x = x_ref[0]
    cur = pl.ds(ts, ts)
    _project_rows(x, nw_ref, sc1_ref, sh1_ref, win_v, lb_ref, q_s, kw_s.at[cur], vw_s.at[cur],
                  kt_ref.at[0], vt_ref.at[0], qh_s, kh_s, lf_s, vh_s, gh_s)
    _band_attention(q_s, kw_s, vw_s, bias_v, s_s, oa_s, j == 0)
    _hgrn_rows(chunk, qh_s, kh_s, lf_s, vh_s, or_s, st_s)
    x_in = xo_ref[0] if has_res else x_ref[0]
    x_new, words, lg = _merge_rows(oa_s[...], or_s[...], gh_s[...], x_in, ga_ref, gr_ref, wout_v, g1_ref,
                                   nf_ref, sc2_ref, sh2_ref, wr_ref, br_ref)
    xo_ref[0] = x_new
    h2_ref[...] = words
    lg_ref[...] = lg
    kw_s[0:ts, :] = kw_s[ts:2 * ts, :]
    vw_s[0:ts, :] = vw_s[ts:2 * ts, :]

    @pl.when(j == pl.num_programs(1) - 1)
    def _():
        sf_ref[0] = st_s[...]


def _mixer_prompt(x, xb0, res, nw, sc1, sh1, w_in_bf, lb, bias, g_att, g_rec, w_out_bf, g1, nf, sc2, sh2,
                  wr2, br, tail, chunk, extra_rows):
    _, s, d = x.shape
    b = sc1.shape[0]
    ts = LEFT
    nj = s // ts
    n_tail = tail // ts
    has_res = res is not None
    row = pl.BlockSpec((1, ts, d), lambda i, j: (i, j, 0))
    per_b = pl.BlockSpec((1, 1, d), lambda i, j: (i, 0, 0))
    const = lambda shp: pl.BlockSpec(shp, lambda i, j: (0,) * len(shp))
    hbm = pl.BlockSpec(memory_space=pl.ANY)
    in_specs, args = [pl.BlockSpec((1, ts, d), lambda i, j: (i + xb0, j, 0))], [x]
    if has_res:
        yg, gate, g2, first = res
        blk0 = first // ts
        in_specs += [pl.BlockSpec((TOP_K, ts, d // 2), lambda i, j: (0, blk0 + i * nj + j, 0)),
                     pl.BlockSpec((GATE_ROWS, ts), lambda i, j: (0, blk0 + i * nj + j)), per_b]
        args += [yg, gate, g2]
    in_specs += [const((1, d)), per_b, per_b, const((1, REC_W)), const((1, ATT_W)), const((1, REC_W)),
                 per_b, const((1, d)), per_b, per_b, const(wr2.shape), const((N_EXPERTS, 1)), hbm, hbm, hbm]
    args += [nw.reshape(1, d), sc1, sh1, lb.reshape(1, REC_W), g_att.reshape(1, ATT_W), g_rec.reshape(1, REC_W),
             g1, nf.reshape(1, d), sc2, sh2, wr2, br.reshape(N_EXPERTS, 1), w_in_bf, w_out_bf, bias]
    tail_spec = pl.BlockSpec((1, ts, ATT_W), lambda i, j: (i, jnp.maximum(j - (nj - n_tail), 0), 0))
    state_spec = pl.BlockSpec((1, REC_HEADS, REC_DIM, REC_DIM), lambda i, j: (i, 0, 0, 0))
    tile = lambda dt: pltpu.VMEM((ts, ATT_W), dt)
    return pl.pallas_call(
        functools.partial(_mixer_kernel, has_res, chunk),
        grid=(b, nj),
        in_specs=in_specs,
        out_specs=[row, pl.BlockSpec((ts, d // 2), lambda i, j: (i * nj + j, 0)),
                   pl.BlockSpec((N_EXPERTS, ts), lambda i, j: (0, i * nj + j)), tail_spec, tail_spec, state_spec],
        out_shape=[jax.ShapeDtypeStruct((b, s, d), F32), jax.ShapeDtypeStruct((b * s + extra_rows, d // 2), U32),
                   jax.ShapeDtypeStruct((N_EXPERTS, b * s), F32),
                   jax.ShapeDtypeStruct((b, tail, ATT_W), F32), jax.ShapeDtypeStruct((b, tail, ATT_W), F32),
                   jax.ShapeDtypeStruct((b, REC_HEADS, REC_DIM, REC_DIM), F32)],
        scratch_shapes=[pltpu.VMEM(w_in_bf.shape, BF16), pltpu.VMEM(w_out_bf.shape, BF16), pltpu.VMEM(bias.shape, F32),
                        tile(BF16), pltpu.VMEM((2 * ts, ATT_W), BF16), pltpu.VMEM((2 * ts, ATT_W), BF16),
                        pltpu.VMEM((ATT_HEADS // 2, 2 * ATT_QSUB, ATT_WIN), F32),
                        tile(F32), tile(F32), tile(F32), tile(BF16), tile(F32), tile(F32), tile(F32),
                        pltpu.VMEM((REC_HEADS, REC_DIM, REC_DIM), F32), pltpu.SemaphoreType.DMA((3,))],
        compiler_params=_cparams("parallel", "arbitrary"),
        name="prompt_mixer",
    )(*args)


def _route_kernel(lg_ref, idx_ref, gate_ref, rank_ref, cnt_ref, run_ref):
    i = pl.program_id(0)
    t = ROUTE_TILE

    @pl.when(i == 0)
    def _():
        run_ref[...] = jnp.zeros_like(run_ref)

    row = lax.broadcasted_iota(I32, (N_EXPERTS, t), 0)
    earlier = (lax.broadcasted_iota(I32, (t, t), 0) < lax.broadcasted_iota(I32, (t, t), 1)).astype(BF16)
    run = run_ref[:, 0:1]
    for sb in range(lg_ref.shape[1] // t):
        cols = slice(sb * t, (sb + 1) * t)
        x = lg_ref[:, cols]
        vals, idxs, hits = [], [], []
        for _ in range(TOP_K):
            m = jnp.max(x, axis=0, keepdims=True)
            ik = jnp.min(jnp.where(x == m, row, N_EXPERTS), axis=0, keepdims=True)
            hit = row == ik
            x = jnp.where(hit, -jnp.inf, x)
            vals.append(m)
            idxs.append(ik)
            hits.append(hit)
        es = [jnp.exp(v - vals[0]) for v in vals]
        tot = es[0] + es[1] + es[2] + es[3]
        gate_ref[:, cols] = jnp.concatenate([e / tot for e in es] + [jnp.zeros((GATE_ROWS - TOP_K, t), F32)], axis=0)
        idx_ref[:, cols] = jnp.concatenate(idxs, axis=0)
        chosen = (hits[0] | hits[1] | hits[2] | hits[3])
        onehot = jnp.where(chosen, 1.0, 0.0)
        before = jnp.dot(onehot.astype(BF16), earlier, preferred_element_type=F32) + run
        rank_ref[:, cols] = jnp.concatenate(
            [jnp.sum(jnp.where(h, before, 0.0), axis=0, keepdims=True) for h in hits], axis=0).astype(I32)
        run = run + jnp.sum(onehot, axis=1, keepdims=True)
    run_ref[...] = jnp.broadcast_to(run, run_ref.shape)
    cnt_ref[...] = jnp.broadcast_to(run, cnt_ref.shape)


def _route(lg_t):
    e, n = lg_t.shape
    t = max(m * ROUTE_TILE for m in range(1, ROUTE_MAX_SUB + 1) if n % (m * ROUTE_TILE) == 0)
    tok = pl.BlockSpec((TOP_K, t), lambda i: (0, i))
    idx, gate, rank, cnt = pl.pallas_call(
        _route_kernel,
        grid=(n // t,),
        in_specs=[pl.BlockSpec((e, t), lambda i: (0, i))],
        out_specs=[tok, pl.BlockSpec((GATE_ROWS, t), lambda i: (0, i)), tok, pl.BlockSpec((e, LANES), lambda i: (0, 0))],
        out_shape=[jax.ShapeDtypeStruct((TOP_K, n), I32), jax.ShapeDtypeStruct((GATE_ROWS, n), F32),
                   jax.ShapeDtypeStruct((TOP_K, n), I32), jax.ShapeDtypeStruct((e, LANES), F32)],
        scratch_shapes=[pltpu.VMEM((e, LANES), F32)],
        compiler_params=_cparams("arbitrary"),
        name="moe_route",
    )(lg_t)
    counts = cnt[:, 0].astype(I32)
    padded = (counts + MOE_ROWS - 1) // MOE_ROWS * MOE_ROWS
    pad_end = jnp.cumsum(padded)
    pad_start = pad_end - padded
    m = n * TOP_K
    nb = (m + N_EXPERTS * (MOE_ROWS - 1) + MOE_ROWS - 1) // MOE_ROWS
    block_start = jnp.arange(nb, dtype=I32) * MOE_ROWS
    block_expert = jnp.minimum(jnp.sum((pad_end[None, :] <= block_start[:, None]).astype(I32), axis=1), N_EXPERTS - 1)
    experts = jnp.arange(N_EXPERTS, dtype=I32)
    in_block = block_start[:, None] - pad_start[None, :]
    mine = block_expert[:, None] == experts[None, :]
    filled = jnp.sum(jnp.where(mine, jnp.clip(counts[None, :] - in_block, 0, MOE_ROWS), 0), axis=1)
    owns = jnp.any(block_expert[:, None] == experts[None, :], axis=0)
    later = owns[None, :] & (experts[None, :] > block_expert[:, None])
    nxt = jnp.min(jnp.where(later, experts[None, :], N_EXPERTS), axis=1)
    nxt = jnp.where(nxt == N_EXPERTS, -1, nxt)
    run = jnp.sum((owns[None, :] & (experts[None, :] < block_expert[:, None])).astype(I32), axis=1)
    block_info = jnp.concatenate([block_expert, nxt, run % 2, filled]).astype(I32)
    start_of = sum(jnp.where(idx == e, pad_start[e], 0) for e in range(N_EXPERTS))
    dest = start_of + rank
    return gate, dest, block_info, nb * MOE_ROWS


def _windows(dest, w, multiple, fill):
    kk, n = dest.shape
    nwin = n // w
    per = -(-nwin // SC_WORKERS)
    per = -(-per // multiple) * multiple
    idx = dest.reshape(kk, nwin, w).transpose(1, 0, 2)
    pad = jnp.broadcast_to(fill[None, None, :], (SC_WORKERS * per - nwin, kk, w)).astype(I32)
    return jnp.concatenate([idx, pad], axis=0)


def _sc_scatter_rows(x, idx, p_rows):
    n, c = x.shape
    nwin, kk, w = idx.shape
    per = nwin // SC_WORKERS
    nreal = n // w
    mesh = plsc.VectorSubcoreMesh(core_axis_name="c", subcore_axis_name="s")

    @functools.partial(
        pl.kernel, mesh=mesh,
        out_type=jax.ShapeDtypeStruct((p_rows, c), x.dtype),
        scratch_types=[pltpu.VMEM((per, kk, w), I32), pltpu.VMEM((2, w, c), x.dtype),
                       pltpu.SemaphoreType.DMA((2,)), pltpu.SemaphoreType.DMA((2,))],
        name="sc_scatter_rows",
    )
    def k(x_hbm, idx_hbm, out_hbm, idx_v, rows_v, lsem, ssem):
        wid = lax.axis_index("s") * 2 + lax.axis_index("c")
        g0 = wid * per
        pltpu.sync_copy(idx_hbm.at[pl.ds(g0, per)], idx_v)

        def load(j, b):
            g = jnp.minimum(g0 + j, nreal - 1)
            return pltpu.make_async_copy(x_hbm.at[pl.ds(g * w, w)], rows_v.at[b], lsem.at[b])

        def scat(j, b, q):
            return pltpu.make_async_copy(rows_v.at[b], out_hbm.at[idx_v.at[j, q]], ssem.at[b])

        @pl.loop(0, per + 2, step=2)
        def _(j0):
            for b in range(2):
                j = j0 + b

                @pl.when(jnp.logical_and(j >= 2, j < per + 2))
                def _():
                    for q in range(kk):
                        scat(j - 2, b, q).wait()

                @pl.when(j < per)
                def _():
                    load(j, b).start()

                @pl.when(jnp.logical_and(j >= 1, j < per + 1))
                def _():
                    load(j - 1, 1 - b).wait()
                    for q in range(kk):
                        scat(j - 1, 1 - b, q).start()

    return k(x, idx)


def _sc_gather_rows(y, idx):
    p, c = y.shape
    kk, n = idx.shape
    w = SC_GATHER_WIN
    nb = SC_GATHER_BUFS
    wins = LANES // w
    group = wins * kk
    assert group % nb == 0 and group >= nb and n % LANES == 0
    n_units = n // LANES
    max_units = -(-n_units // SC_WORKERS)
    mesh = plsc.VectorSubcoreMesh(core_axis_name="c", subcore_axis_name="s")

    @functools.partial(
        pl.kernel, mesh=mesh,
        out_type=jax.ShapeDtypeStruct((kk, n, c), y.dtype),
        scratch_types=[pltpu.VMEM((2, kk, LANES), I32), pltpu.VMEM((nb, w, c), y.dtype),
                       pltpu.SemaphoreType.DMA((nb,)), pltpu.SemaphoreType.DMA((nb,))],
        name="sc_gather_rows",
    )
    def k(y_hbm, idx_hbm, out_hbm, idx_v, rows_v, gsem, wsem):
        wid = lax.axis_index("s") * 2 + lax.axis_index("c")
        my_units = (n_units - wid + SC_WORKERS - 1) // SC_WORKERS
        n_items = my_units * group

        def gath(slot, jj, q, b):
            return pltpu.make_async_copy(y_hbm.at[idx_v.at[slot, q, pl.ds(jj * w, w)]], rows_v.at[b], gsem.at[b])

        def wr(i, b):
            row = (wid + (i // group) * SC_WORKERS) * LANES + ((i % group) // kk) * w
            return pltpu.make_async_copy(rows_v.at[b], out_hbm.at[i % kk, pl.ds(row, w)], wsem.at[b])

        @pl.loop(0, max_units + 1)
        def _(gi):
            slot = gi % 2

            @pl.when(gi < my_units)
            def _():
                pltpu.sync_copy(idx_hbm.at[:, pl.ds((wid + gi * SC_WORKERS) * LANES, LANES)], idx_v.at[slot])

            for l in range(group):
                i = gi * group + l
                b = l % nb
                bw = (l - (nb - 1)) % nb

                @pl.when(jnp.logical_and(i >= nb, i < n_items + nb))
                def _():
                    wr(i - nb, b).wait()

                @pl.when(i < n_items)
                def _():
                    gath(slot, l // kk, l % kk, b).start()

                @pl.when(jnp.logical_and(i >= nb - 1, i < n_items + nb - 1))
                def _():
                    gath(slot, 0, 0, bw).wait()
                    wr(i - (nb - 1), bw).start()

    return k(y, idx)


def _moe_kernel(layer, info_ref, x_ref, w1_hbm, b1_ref, w2_hbm, b2_ref, o_ref,
                w1f_ref, w2f_ref, w1b_ref, w2b_ref, sem):
    i = pl.program_id(0)
    nb = pl.num_programs(0)
    expert = info_ref[i]
    nxt = info_ref[nb + i]
    slot = info_ref[2 * nb + i]
    first = jnp.logical_or(i == 0, expert != info_ref[jnp.maximum(i - 1, 0)])

    def fetch(e, s):
        return (pltpu.make_async_copy(w1_hbm.at[layer, e], w1f_ref.at[s], sem.at[0, s]),
                pltpu.make_async_copy(w2_hbm.at[layer, e], w2f_ref.at[s], sem.at[1, s]))

    @pl.when(i == 0)
    def _():
        for cp in fetch(expert, slot):
            cp.start()

    @pl.when(first)
    def _():
        for cp in fetch(expert, slot):
            cp.wait()

        @pl.when(nxt >= 0)
        def _():
            for cp in fetch(nxt, 1 - slot):
                cp.start()

        w1b_ref[...] = w1f_ref[slot].astype(BF16)
        w2b_ref[...] = w2f_ref[slot].astype(BF16)

    units = (info_ref[3 * nb + i] + MOE_UNIT - 1) // MOE_UNIT
    for u in range(1, x_ref.shape[0] // MOE_UNIT + 1):
        @pl.when(units == u)
        def _():
            rows = slice(0, u * MOE_UNIT)
            _expert_block(x_ref.at[rows], w1b_ref, b1_ref, w2b_ref, b2_ref, o_ref.at[rows])


def _expert_block(x_ref, w1b_ref, b1_ref, w2b_ref, b2_ref, o_ref):
    lo, hi = _unpack_pairs(x_ref[...])
    x = jnp.concatenate([lo.astype(BF16), hi.astype(BF16)], axis=-1)
    f = w2b_ref.shape[0]
    fc = 512
    acc = None
    for c in range(f // fc):
        gt = jnp.dot(x, w1b_ref[:, c * fc:(c + 1) * fc], preferred_element_type=F32) + b1_ref[0, 0, :, c * fc:(c + 1) * fc]
        up = jnp.dot(x, w1b_ref[:, f + c * fc:f + (c + 1) * fc], preferred_element_type=F32) + b1_ref[0, 0, :, f + c * fc:f + (c + 1) * fc]
        gt = jnp.minimum(gt, SWIGLU_LIMIT)
        up = jnp.clip(up, -SWIGLU_LIMIT, SWIGLU_LIMIT)
        act = (gt * jax.nn.sigmoid(SWIGLU_ALPHA * gt) * (up + 1.0)).astype(BF16)
        part = jnp.dot(act, w2b_ref[c * fc:(c + 1) * fc, :], preferred_element_type=F32)
        acc = part if acc is None else acc + part
    o_ref[...] = _pack_pairs(acc + b2_ref[0, 0])


def _moe_experts(layer, block_expert, xs, p_rows, w1, b1, w2, b2):
    _, e, d, f2 = w1.shape
    f = f2 // 2
    nb = p_rows // MOE_ROWS
    grid_spec = pltpu.PrefetchScalarGridSpec(
        num_scalar_prefetch=1,
        grid=(nb,),
        in_specs=[pl.BlockSpec((MOE_ROWS, d // 2), lambda i, be: (i, 0)),
                  pl.BlockSpec(memory_space=pl.ANY),
                  pl.BlockSpec((1, 1, 1, f2), lambda i, be: (layer, be[i], 0, 0)),
                  pl.BlockSpec(memory_space=pl.ANY),
                  pl.BlockSpec((1, 1, 1, d), lambda i, be: (layer, be[i], 0, 0))],
        out_specs=pl.BlockSpec((MOE_ROWS, d // 2), lambda i, be: (i, 0)),
        scratch_shapes=[pltpu.VMEM((2, d, f2), F32), pltpu.VMEM((2, f, d), F32),
                        pltpu.VMEM((d, f2), BF16), pltpu.VMEM((f, d), BF16),
                        pltpu.SemaphoreType.DMA((2, 2))],
    )
    return pl.pallas_call(
        functools.partial(_moe_kernel, layer),
        grid_spec=grid_spec,
        out_shape=jax.ShapeDtypeStruct((p_rows, d // 2), U32),
        compiler_params=_cparams("arbitrary"),
        name="moe_experts",
    )(block_expert, xs, w1, b1.reshape(b1.shape[0], e, 1, f2), w2, b2.reshape(b2.shape[0], e, 1, d))


def _moe_dispatch(h2_words, lg_t):
    gate, dest, block_info, p_rows = _route(lg_t)
    spare = p_rows + jnp.arange(SC_SCATTER_WIN, dtype=I32)
    xs = _sc_scatter_rows(h2_words, _windows(dest, SC_SCATTER_WIN, 2, spare), p_rows + SC_SCATTER_WIN)
    return xs, gate, dest, block_info, p_rows


def _moe_compute(layer, dispatched, w1, b1, w2, b2):
    xs, gate, back, block_info, p_rows = dispatched
    ys = _moe_experts(layer, block_info, xs, p_rows, w1, b1, w2, b2)
    return _sc_gather_rows(ys, back), gate


def _final_kernel(x_ref, yg_ref, gate_ref, g2_ref, nw_ref, *rest):
    o_ref = rest[-1]
    x = x_ref[0] + g2_ref[0] * _combine(yg_ref, gate_ref)
    ms = jnp.mean(x * x, axis=-1, keepdims=True)
    o_ref[0] = x * lax.rsqrt(ms + EPS) * nw_ref[...]


def _final(x, res, nw, ts, out_buf, ob0, b_total):
    b, s, d = x.shape
    nj = s // ts
    yg, gate, g2, first = res
    blk0 = first // ts
    row = pl.BlockSpec((1, ts, d), lambda i, j: (i, j, 0))
    if gate.ndim == 2:
        gate_spec = pl.BlockSpec((GATE_ROWS, ts), lambda i, j: (0, blk0 + i * nj + j))
    else:
        gate_spec = pl.BlockSpec((1, ts, TOP_K), lambda i, j: (i, j, 0))
    in_specs = [row, pl.BlockSpec((TOP_K, ts, d // 2), lambda i, j: (0, blk0 + i * nj + j, 0)),
                gate_spec,
                pl.BlockSpec((1, 1, d), lambda i, j: (i, 0, 0)),
                pl.BlockSpec((1, d), lambda i, j: (0, 0))]
    args = [x, yg, gate, g2, nw.reshape(1, d)]
    aliases = {}
    if out_buf is not None:
        in_specs.append(pl.BlockSpec(memory_space=pl.ANY))
        args.append(out_buf)
        aliases = {len(args) - 1: 0}
    return pl.pallas_call(
        _final_kernel,
        grid=(b, nj),
        in_specs=in_specs,
        out_specs=pl.BlockSpec((1, ts, d), lambda i, j: (i + ob0, j, 0)),
        out_shape=jax.ShapeDtypeStruct((b_total, s, d), F32),
        input_output_aliases=aliases,
        compiler_params=_cparams("parallel", "parallel"),
        name="final_norm",
    )(*args)


def kernel(x_prompt, x_sample, c_prompt, c_sample, cache_k, cache_v, state_hgrn, norm_mix, norm_ffn, norm_final, w_ada, b_ada, w_in, rel_bias, hgrn_lb_logits, g_attn_out, g_hgrn_out, w_out, w_router, b_router, w_e1, b_e1, w_e2, b_e2):
    depth = w_in.shape[0]
    bp, sp, d = x_prompt.shape
    bs, ls, _ = x_sample.shape
    wc = cache_k.shape[2]
    n_p, n_s = bp * sp, bs * ls

    lbs = jax.nn.softmax(hgrn_lb_logits.astype(F32), axis=0)
    lbs = jnp.cumsum(lbs, axis=0) - lbs[0]

    mods = _modulation(jnp.concatenate([c_prompt, c_sample], axis=0), w_ada, b_ada)
    lk_s = -(-(wc + ls) // LANES) * LANES
    tail = min(LEFT, sp)
    rec_chunk = min(REC_CHUNK, sp)

    sizes = [bp * num // den for num, den in GROUP_SHARES] if bp % GROUP_SHARES[0][1] == 0 else [bp]
    n_groups = len(sizes)
    starts = [sum(sizes[:g]) for g in range(n_groups)]
    n_g = sizes[-1] * sp
    x_g = [(x_prompt, starts[g]) for g in range(n_groups)]
    x_s = (x_sample, 0)
    res_g = [None] * n_groups
    res_s = None
    kp_l, vp_l, sp_l, ks_l, vs_l, ss_l = [], [], [], [], [], []
    for layer in range(depth):
        m6 = mods[layer].reshape(bp + bs, 1, N_MOD, d)
        mod_s = [m6[bp:, :, n, :] for n in range(N_MOD)]
        w_in_bf = w_in[layer].astype(BF16)
        w_out_bf = w_out[layer].astype(BF16)
        wr_hi, wr_mid, _ = _split3(w_router[layer].T)
        wr2 = jnp.concatenate([wr_hi, wr_mid], axis=0)
        table = rel_bias[layer]
        bias_p = _prompt_bias(table)

        def mixer_prompt(g):
            mod = [m6[starts[g]:starts[g] + sizes[g], :, n, :] for n in range(N_MOD)]
            x_new, h2, lg, kt, vt, s_fin = _mixer_prompt(
                x_g[g][0], x_g[g][1], res_g[g], norm_mix[layer], mod[1], mod[0], w_in_bf, lbs[layer], bias_p,
                g_attn_out[layer], g_hgrn_out[layer], w_out_bf, mod[2], norm_ffn[layer], mod[4], mod[3],
                wr2, b_router[layer], tail, rec_chunk, n_s if g == n_groups - 1 else 0)
            x_g[g] = (x_new, 0)
            return h2, lg, kt, vt, s_fin, mod[5]

        def mixer_sample(rows):
            x, qa, ka, va, kt, vt, qh, kh, lf, vh, gh = _inproj(
                x_s[0], x_s[1], res_s, norm_mix[layer], mod_s[1], mod_s[0], w_in_bf, lbs[layer], ls, ls)
            o_att = _attn_sample(qa, cache_k[layer].reshape(bs, wc, ATT_W), cache_v[layer].reshape(bs, wc, ATT_W),
                                 ka, va, _sample_bias(table, ls, wc, lk_s))
            s0 = jnp.swapaxes(state_hgrn[layer].astype(F32), -1, -2)
            o_rec, s_new = _hgrn(qh, kh, lf, vh, s0, ls, ls)
            x_new, rows, lg = _merge(o_att, o_rec, gh, x[0], x[1], g_attn_out[layer], g_hgrn_out[layer], w_out_bf,
                                     mod_s[2], norm_ffn[layer], mod_s[4], mod_s[3], wr2, b_router[layer], rows, n_g)
            return (x_new, 0), rows, jnp.swapaxes(lg, 0, 1).reshape(N_EXPERTS, n_s), kt, vt, s_new

        dispatched, g2s, k_t, v_t, s_t = [], [], [], [], []
        for g in range(n_groups):
            h2, lg, kt, vt, s_fin, g2 = mixer_prompt(g)
            if g == n_groups - 1:
                x_s, h2, lg_s, ka_s, va_s, s_new = mixer_sample(h2)
                ks_l.append(ka_s.reshape(bs, ls, ATT_HEADS, ATT_DIM))
                vs_l.append(va_s.reshape(bs, ls, ATT_HEADS, ATT_DIM))
                ss_l.append(jnp.swapaxes(s_new, -1, -2))
                lg = jnp.concatenate([lg, lg_s], axis=1)
            dispatched.append(_moe_dispatch(h2, lg))
            g2s.append(g2)
            k_t.append(kt)
            v_t.append(vt)
            s_t.append(s_fin)
        kp_l.append(jnp.concatenate(k_t, axis=0).reshape(bp, tail, ATT_HEADS, ATT_DIM))
        vp_l.append(jnp.concatenate(v_t, axis=0).reshape(bp, tail, ATT_HEADS, ATT_DIM))
        sp_l.append(jnp.swapaxes(jnp.concatenate(s_t, axis=0), -1, -2))

        for g in range(n_groups):
            yg, gate = _moe_compute(layer, dispatched[g], w_e1, b_e1, w_e2, b_e2)
            res_g[g] = (yg, gate, g2s[g], 0)
            if g == n_groups - 1:
                res_s = (yg, gate[:TOP_K, n_g:].T.reshape(bs, ls, TOP_K), mod_s[5], n_g)

    y_prompt = None
    for g in range(n_groups):
        y_prompt = _final(x_g[g][0], res_g[g], norm_final, SEQ_TILE, y_prompt, starts[g], bp)
    y_sample = _final(x_s[0], res_s, norm_final, ls, None, 0, bs)
    return (y_prompt, y_sample, jnp.stack(kp_l), jnp.stack(vp_l), jnp.stack(sp_l),
            jnp.stack(ks_l), jnp.stack(vs_l), jnp.stack(ss_l))
```

```python
import functools

import jax
import jax.numpy as jnp
import numpy as np
from jax import lax
from jax.experimental import pallas as pl
from jax.experimental.pallas import tpu as pltpu
from jax.experimental.pallas import tpu_sc as plsc

F32 = jnp.float32
BF16 = jnp.bfloat16
I32 = jnp.int32
U32 = jnp.uint32

EPS = 1e-6
CHUNK = 64
LEFT_CHUNKS = 8
LEFT = LEFT_CHUNKS * CHUNK
MAX_REL = 2 * CHUNK
ATT_HEADS = 8
ATT_DIM = 64
ATT_W = ATT_HEADS * ATT_DIM
REC_HEADS = 4
REC_DIM = 128
REC_W = REC_HEADS * REC_DIM
N_EXPERTS = 32
TOP_K = 4
SWIGLU_ALPHA = 1.702
SWIGLU_LIMIT = 7.0
N_MOD = 6
NEG = -1e30
LOG2E = 1.4426950408889634

LANES = 128
ATT_QSUB = 128
ATT_WIN = LEFT + ATT_QSUB
SEQ_TILE = 512
REC_CHUNK = 128
MOE_ROWS = 1024
MOE_UNIT = 256
ROUTE_TILE = 256
ROUTE_MAX_SUB = 4
PROMPT_GROUPS = 2
GATE_ROWS = 8
VMEM_LIMIT = 56 * 1024 * 1024
SC_WORKERS = 32
SC_SCATTER_WIN = 64
SC_GATHER_WIN = 32
SC_GATHER_BUFS = 4


def _cparams(*sem):
    return pltpu.CompilerParams(dimension_semantics=sem, vmem_limit_bytes=VMEM_LIMIT)


def _split3(x):
    hi = x.astype(BF16)
    r1 = x - hi.astype(F32)
    mid = r1.astype(BF16)
    lo = (r1 - mid.astype(F32)).astype(BF16)
    return hi, mid, lo


def _bits(x):
    return lax.bitcast_convert_type(x, U32)


def _pack_pairs(x):
    r = _bits(x.astype(BF16).astype(F32))
    half = x.shape[1] // 2
    return jnp.bitwise_or(jnp.bitwise_and(r[:, half:], jnp.uint32(0xFFFF0000)),
                          jnp.right_shift(r[:, :half], jnp.uint32(16)))


def _unpack_pairs(word):
    lo = lax.bitcast_convert_type(jnp.left_shift(word, jnp.uint32(16)), F32)
    hi = lax.bitcast_convert_type(jnp.bitwise_and(word, jnp.uint32(0xFFFF0000)), F32)
    return lo, hi


def _mod_kernel(c_ref, w_ref, b_ref, o_ref):
    c = c_ref[...]
    act = (c * jax.nn.sigmoid(c)).astype(BF16)
    o_ref[0] = jnp.dot(act, w_ref[0].astype(BF16), preferred_element_type=F32) + b_ref[0]


def _modulation(c_all, w_ada, b_ada):
    depth, d, n6 = w_ada.shape
    rows = c_all.shape[0]
    tn = 1536
    return pl.pallas_call(
        _mod_kernel,
        grid=(depth, n6 // tn),
        in_specs=[pl.BlockSpec((rows, d), lambda l, j: (0, 0)),
                  pl.BlockSpec((1, d, tn), lambda l, j: (l, 0, j)),
                  pl.BlockSpec((1, 1, tn), lambda l, j: (l, 0, j))],
        out_specs=pl.BlockSpec((1, rows, tn), lambda l, j: (l, 0, j)),
        out_shape=jax.ShapeDtypeStruct((depth, rows, n6), F32),
        compiler_params=_cparams("parallel", "parallel"),
        name="adaln_modulation",
    )(c_all, w_ada, b_ada.reshape(depth, 1, n6))


def _combine(yg_ref, gate_ref):
    g = gate_ref[0] if len(gate_ref.shape) == 3 else gate_ref[...].T
    lo = hi = None
    for k in range(TOP_K):
        l, h = _unpack_pairs(yg_ref[k])
        gk = g[:, k:k + 1]
        lo = l * gk if lo is None else lo + l * gk
        hi = h * gk if hi is None else hi + h * gk
    return jnp.concatenate([lo, hi], axis=-1)


def _project_rows(x, nw_ref, sc_ref, sh_ref, w_ref, lb_ref,
                  qa_ref, ka_ref, va_ref, kt_ref, vt_ref, qh_ref, kh_ref, lf_ref, vh_ref, gh_ref):
    ms = jnp.mean(x * x, axis=-1, keepdims=True)
    h = (x * lax.rsqrt(ms + EPS) * nw_ref[...]) * (1.0 + sc_ref[0]) + sh_ref[0]
    hb = h.astype(BF16)

    def proj(g):
        return jnp.dot(hb, w_ref[:, g * ATT_W:(g + 1) * ATT_W], preferred_element_type=F32)

    qa_ref[...] = (proj(0) * (ATT_DIM ** -0.5)).astype(BF16)
    k = proj(1)
    ka_ref[...] = k.astype(BF16)
    kt_ref[...] = k
    v = proj(2)
    va_ref[...] = v.astype(BF16)
    vt_ref[...] = v
    qh = proj(3)
    qh_ref[...] = qh * jax.nn.sigmoid(qh)
    f = proj(4)
    lb = lb_ref[...]
    lf_ref[...] = jnp.log(lb + (1.0 - lb) * jax.nn.sigmoid(f))
    kh_ref[...] = (1.0 - lb) * jax.nn.sigmoid(-f)
    vh_ref[...] = proj(5).astype(BF16)
    gh_ref[...] = proj(6)


def _attend(q, kw, vw, bias_ref, s_ref, col_thr):
    lq, lk = q.shape[0], kw.shape[0]
    lane = lax.broadcasted_iota(I32, (lq, LANES), 1)
    lo = lane < ATT_DIM
    nt = (((1,), (1,)), ((), ()))
    if col_thr is not None:
        keep = lax.broadcasted_iota(I32, (2 * lq, lk), 1) >= col_thr
    ms = []
    for p in range(ATT_HEADS // 2):
        sl = slice(p * LANES, (p + 1) * LANES)
        qp = q[:, sl]
        z = jnp.zeros_like(qp)
        lhs = jnp.concatenate([jnp.where(lo, qp, z), jnp.where(lo, z, qp)], axis=0)
        s = lax.dot_general(lhs, kw[:, sl], nt, preferred_element_type=F32) + bias_ref[p]
        if col_thr is not None:
            s = jnp.where(keep, s, NEG)
        ms.append(jnp.max(s, axis=-1, keepdims=True))
        s_ref[p] = s
    outs = []
    for p in range(ATT_HEADS // 2):
        sl = slice(p * LANES, (p + 1) * LANES)
        e = jnp.exp(s_ref[p] - ms[p])
        l = jnp.sum(e, axis=-1, keepdims=True)
        o = jnp.dot(e.astype(BF16), vw[:, sl], preferred_element_type=F32) / l
        outs.append(jnp.where(lo, o[:lq], o[lq:]))
    return jnp.concatenate(outs, axis=-1)


def _band_attention(q_ref, kw_ref, vw_ref, bias_ref, s_ref, o_ref, first_tile):
    ts = q_ref.shape[0]

    def body(masked, r, carry):
        r0 = pl.multiple_of(r * ATT_QSUB, ATT_QSUB)
        q = q_ref[pl.ds(r0, ATT_QSUB), :]
        kw = kw_ref[pl.ds(r0, ATT_WIN), :]
        vw = vw_ref[pl.ds(r0, ATT_WIN), :]
        o_ref[pl.ds(r0, ATT_QSUB), :] = _attend(q, kw, vw, bias_ref, s_ref, (ts - r0) if masked else None)
        return carry

    @pl.when(first_tile)
    def _():
        lax.fori_loop(0, ts // ATT_QSUB, functools.partial(body, True), 0)

    @pl.when(jnp.logical_not(first_tile))
    def _():
        lax.fori_loop(0, ts // ATT_QSUB, functools.partial(body, False), 0)


def _toeplitz_bias(table, lq, lk, offset, valid):
    m = lq + lk - 1
    k = np.arange(m)
    diff = np.where(k < lk, k, k - m)
    g = table[:, np.clip(offset - diff, -MAX_REL, MAX_REL) + MAX_REL]
    skew = jnp.tile(g, (1, lq))[:, :lq * (m - 1)].reshape(-1, lq, m - 1)[:, :, :lk]
    bias = jnp.where(jnp.asarray(valid)[None], skew, NEG).astype(F32)
    return bias.reshape(ATT_HEADS // 2, 2 * lq, lk)


def _prompt_bias(table):
    t = np.arange(ATT_QSUB)[:, None]
    j = np.arange(ATT_WIN)[None, :]
    start = (t // CHUNK) * CHUNK
    return _toeplitz_bias(table, ATT_QSUB, ATT_WIN, LEFT, (j >= start) & (j < start + LEFT + CHUNK))


def _sample_bias(table, l, w, lk):
    valid = np.broadcast_to(np.arange(lk)[None, :] < w + l, (l, lk))
    return _toeplitz_bias(table, l, lk, w, valid)


def _neg_abs(x):
    return lax.bitcast_convert_type(jnp.bitwise_or(_bits(x), jnp.uint32(0x80000000)), F32)


def _hgrn_rows(chunk, q_ref, k_ref, lf_ref, v_ref, o_ref, st_ref):
    ts = q_ref.shape[0]
    n_chunks = ts // chunk

    row = lax.broadcasted_iota(I32, (chunk, chunk), 0)
    colm = lax.broadcasted_iota(I32, (chunk, chunk), 1)
    tri = (colm <= row).astype(BF16)
    xor = jnp.bitwise_xor(row, colm)
    lower = colm < row
    rid = lax.broadcasted_iota(I32, (chunk, REC_W), 0)

    levels = []
    b = chunk // 2
    while b >= 1:
        levels.append(b)
        b //= 2
    pair_mask = [xor == 0] + [lower & (xor >= b) & (xor < 2 * b) for b in levels]
    upper = [jnp.bitwise_and(rid, b) != 0 for b in levels]
    ph4 = jnp.bitwise_and(rid, 3)
    odd = jnp.bitwise_and(rid, 1) == 1
    nt = (((1,), (1,)), ((), ()))
    tn = (((0,), (0,)), ((), ()))
    sls = [slice(h * REC_DIM, (h + 1) * REC_DIM) for h in range(REC_HEADS)]

    def boundary(cum, b):
        if 2 * b >= 8:
            pieces = [jnp.broadcast_to(cum[m * 2 * b + b - 1:m * 2 * b + b, :], (2 * b, REC_W))
                      for m in range(chunk // (2 * b))]
            return pieces[0] if len(pieces) == 1 else jnp.concatenate(pieces, axis=0)
        if b == 2:
            up1 = pltpu.roll(cum, chunk - 1, 0)
            dn1 = pltpu.roll(cum, 1, 0)
            dn2 = pltpu.roll(cum, 2, 0)
            return jnp.where(ph4 == 0, up1, jnp.where(ph4 == 1, cum, jnp.where(ph4 == 2, dn1, dn2)))
        return jnp.where(odd, pltpu.roll(cum, 1, 0), cum)

    def one_chunk(c, carry):
        r0 = pl.multiple_of(c * chunk, chunk)
        rows = pl.ds(r0, chunk)
        q = q_ref[rows, :]
        k = k_ref[rows, :]
        v = v_ref[rows, :]
        lf = lf_ref[rows, :]
        cum = sum(jnp.dot(tri, part, preferred_element_type=F32) for part in _split3(lf)) * LOG2E
        last = cum[chunk - 1:chunk, :]
        q_in = (q * jnp.exp2(cum)).astype(BF16)
        k_out = (k * jnp.exp2(last - cum)).astype(BF16)
        decay = jnp.exp2(last)
        qb = q.astype(BF16)
        kb = k.astype(BF16)
        zs = []
        for n, b in enumerate(levels):
            w = jnp.exp2(_neg_abs(cum - boundary(cum, b)))
            zs.append((jnp.where(upper[n], q, k) * w).astype(BF16))
        sts = [st_ref[h] for h in range(REC_HEADS)]
        d0 = [lax.dot_general(qb[:, sl], kb[:, sl], nt, preferred_element_type=F32) for sl in sls]
        dl = [[lax.dot_general(z[:, sl], z[:, sl], nt, preferred_element_type=F32) for z in zs] for sl in sls]
        oi = [lax.dot_general(q_in[:, sl], st.astype(BF16), nt, preferred_element_type=F32)
              for sl, st in zip(sls, sts)]
        upd = [lax.dot_general(v[:, sl], k_out[:, sl], tn, preferred_element_type=F32) for sl in sls]
        outs = []
        for h, sl in enumerate(sls):
            a = jnp.where(pair_mask[0], d0[h], 0.0)
            for n in range(len(levels)):
                a = jnp.where(pair_mask[n + 1], dl[h][n], a)
            outs.append(jnp.dot(a.astype(BF16), v[:, sl], preferred_element_type=F32) + oi[h])
        o_ref[rows, :] = jnp.concatenate(outs, axis=-1)
        for h, sl in enumerate(sls):
            st_ref[h] = decay[:, sl] * sts[h] + upd[h]
        return carry

    lax.fori_loop(0, n_chunks, one_chunk, 0, unroll=2 if n_chunks % 2 == 0 else 1)


def _head_rms(x, width):
    lane = lax.broadcasted_iota(I32, (x.shape[0], LANES), 1)
    lo = lane < ATT_DIM
    outs = []
    for p in range(x.shape[1] // LANES):
        xp = x[:, p * LANES:(p + 1) * LANES]
        sq = xp * xp
        tot = jnp.sum(sq, axis=-1, keepdims=True)
        if width == LANES:
            scale = lax.rsqrt(tot * (1.0 / LANES) + EPS)
        else:
            s_lo = jnp.sum(jnp.where(lo, sq, 0.0), axis=-1, keepdims=True)
            scale = jnp.where(lo, lax.rsqrt(s_lo * (1.0 / ATT_DIM) + EPS),
                              lax.rsqrt((tot - s_lo) * (1.0 / ATT_DIM) + EPS))
        outs.append(xp * scale)
    return jnp.concatenate(outs, axis=-1)


def _merge_rows(oa, orr, gh, x, ga_ref, gr_ref, wo_ref, g1_ref, nf_ref, sc_ref, sh_ref, wr_ref, br_ref):
    a = _head_rms(oa, ATT_DIM) * ga_ref[...]
    r = _head_rms(orr, REC_DIM) * gr_ref[...] * (gh * jax.nn.sigmoid(gh))
    mix = jnp.dot(a.astype(BF16), wo_ref[0:ATT_W, :], preferred_element_type=F32)
    mix += jnp.dot(r.astype(BF16), wo_ref[ATT_W:ATT_W + REC_W, :], preferred_element_type=F32)
    x = x + g1_ref[0] * mix
    ms = jnp.mean(x * x, axis=-1, keepdims=True)
    h2 = (x * lax.rsqrt(ms + EPS) * nf_ref[...]) * (1.0 + sc_ref[0]) + sh_ref[0]
    hh = h2.astype(BF16)
    words = _pack_pairs(h2)
    hm = (h2 - hh.astype(F32)).astype(BF16)
    nt = (((1,), (1,)), ((), ()))
    l1 = lax.dot_general(wr_ref[...], hh, nt, preferred_element_type=F32)
    l2 = lax.dot_general(wr_ref[0:N_EXPERTS, :], hm, nt, preferred_element_type=F32)
    lg = l1[0:N_EXPERTS] + l1[N_EXPERTS:2 * N_EXPERTS] + l2 + br_ref[...]
    return x, words, lg


def _mixer_kernel(has_res, chunk, *refs):
    n_in = 4 if has_res else 1
    x_ref = refs[0]
    (nw_ref, sc1_ref, sh1_ref, lb_ref, ga_ref, gr_ref, g1_ref, nf_ref, sc2_ref, sh2_ref, wr_ref, br_ref,
     win_hbm, wout_hbm, bias_hbm,
     xo_ref, h2_ref, lg_ref, kt_ref, vt_ref, sf_ref,
     win_v, wout_v, bias_v, q_s, kw_s, vw_s, s_s, qh_s, kh_s, lf_s, vh_s, gh_s, oa_s, or_s, st_s, sem) = refs[n_in:]
    j = pl.program_id(1)
    ts = x_ref.shape[1]

    @pl.when(j == 0)
    def _():
        copies = [pltpu.make_async_copy(src, dst, sem.at[n])
                  for n, (src, dst) in enumerate(((win_hbm, win_v), (wout_hbm, wout_v), (bias_hbm, bias_v)))]
        for cp in copies:
            cp.start()
        for cp in copies:
            cp.wait()
        st_s[...] = jnp.zeros_like(st_s)
        kw_s[0:ts, :] = jnp.zeros((ts, ATT_W), BF16)
        vw_s[0:ts, :] = jnp.zeros((ts, ATT_W), BF16)

    if has_res:
        yg_ref, gate_ref, g2_ref = refs[1:4]
        x = x_ref[0] + g2_ref[0] * _combine(yg_ref, gate_ref)
        xo_ref[0] = x
    else:
        x = x_ref[0]
    cur = pl.ds(ts, ts)
    _project_rows(x, nw_ref, sc1_ref, sh1_ref, win_v, lb_ref, q_s, kw_s.at[cur], vw_s.at[cur],
                  kt_ref.at[0], vt_ref.at[0], qh_s, kh_s, lf_s, vh_s, gh_s)
    _band_attention(q_s, kw_s, vw_s, bias_v, s_s, oa_s, j == 0)
    _hgrn_rows(chunk, qh_s, kh_s, lf_s, vh_s, or_s, st_s)
    x_in = xo_ref[0] if has_res else x_ref[0]
    x_new, words, lg = _merge_rows(oa_s[...], or_s[...], gh_s[...], x_in, ga_ref, gr_ref, wout_v, g1_ref,
                                   nf_ref, sc2_ref, sh2_ref, wr_ref, br_ref)
    xo_ref[0] = x_new
    h2_ref[...] = words
    lg_ref[...] = lg
    kw_s[0:ts, :] = kw_s[ts:2 * ts, :]
    vw_s[0:ts, :] = vw_s[ts:2 * ts, :]

    @pl.when(j == pl.num_programs(1) - 1)
    def _():
        sf_ref[0] = st_s[...]


def _mixer_prompt(x, xb0, res, nw, sc1, sh1, w_in_bf, lb, bias, g_att, g_rec, w_out_bf, g1, nf, sc2, sh2,
                  wr2, br, tail, chunk, extra_rows):
    _, s, d = x.shape
    b = sc1.shape[0]
    ts = LEFT
    nj = s // ts
    n_tail = tail // ts
    has_res = res is not None
    row = pl.BlockSpec((1, ts, d), lambda i, j: (i, j, 0))
    per_b = pl.BlockSpec((1, 1, d), lambda i, j: (i, 0, 0))
    const = lambda shp: pl.BlockSpec(shp, lambda i, j: (0,) * len(shp))
    hbm = pl.BlockSpec(memory_space=pl.ANY)
    in_specs, args = [pl.BlockSpec((1, ts, d), lambda i, j: (i + xb0, j, 0))], [x]
    if has_res:
        yg, gate, g2, first = res
        blk0 = first // ts
        in_specs += [pl.BlockSpec((TOP_K, ts, d // 2), lambda i, j: (0, blk0 + i * nj + j, 0)),
                     pl.BlockSpec((GATE_ROWS, ts), lambda i, j: (0, blk0 + i * nj + j)), per_b]
        args += [yg, gate, g2]
    in_specs += [const((1, d)), per_b, per_b, const((1, REC_W)), const((1, ATT_W)), const((1, REC_W)),
                 per_b, const((1, d)), per_b, per_b, const(wr2.shape), const((N_EXPERTS, 1)), hbm, hbm, hbm]
    args += [nw.reshape(1, d), sc1, sh1, lb.reshape(1, REC_W), g_att.reshape(1, ATT_W), g_rec.reshape(1, REC_W),
             g1, nf.reshape(1, d), sc2, sh2, wr2, br.reshape(N_EXPERTS, 1), w_in_bf, w_out_bf, bias]
    tail_spec = pl.BlockSpec((1, ts, ATT_W), lambda i, j: (i, jnp.maximum(j - (nj - n_tail), 0), 0))
    state_spec = pl.BlockSpec((1, REC_HEADS, REC_DIM, REC_DIM), lambda i, j: (i, 0, 0, 0))
    tile = lambda dt: pltpu.VMEM((ts, ATT_W), dt)
    return pl.pallas_call(
        functools.partial(_mixer_kernel, has_res, chunk),
        grid=(b, nj),
        in_specs=in_specs,
        out_specs=[row, pl.BlockSpec((ts, d // 2), lambda i, j: (i * nj + j, 0)),
                   pl.BlockSpec((N_EXPERTS, ts), lambda i, j: (0, i * nj + j)), tail_spec, tail_spec, state_spec],
        out_shape=[jax.ShapeDtypeStruct((b, s, d), F32), jax.ShapeDtypeStruct((b * s + extra_rows, d // 2), U32),
                   jax.ShapeDtypeStruct((N_EXPERTS, b * s), F32),
                   jax.ShapeDtypeStruct((b, tail, ATT_W), F32), jax.ShapeDtypeStruct((b, tail, ATT_W), F32),
                   jax.ShapeDtypeStruct((b, REC_HEADS, REC_DIM, REC_DIM), F32)],
        scratch_shapes=[pltpu.VMEM(w_in_bf.shape, BF16), pltpu.VMEM(w_out_bf.shape, BF16), pltpu.VMEM(bias.shape, F32),
                        tile(BF16), pltpu.VMEM((2 * ts, ATT_W), BF16), pltpu.VMEM((2 * ts, ATT_W), BF16),
                        pltpu.VMEM((ATT_HEADS // 2, 2 * ATT_QSUB, ATT_WIN), F32),
                        tile(F32), tile(F32), tile(F32), tile(BF16), tile(F32), tile(F32), tile(F32),
                        pltpu.VMEM((REC_HEADS, REC_DIM, REC_DIM), F32), pltpu.SemaphoreType.DMA((3,))],
        compiler_params=_cparams("parallel", "arbitrary"),
        name="prompt_mixer",
    )(*args)


def _sample_mixer_kernel(has_res, *refs):
    n_in = 4 if has_res else 1
    x_ref = refs[0]
    (nw_ref, sc1_ref, sh1_ref, w_ref, lb_ref, ck_ref, cv_ref, bias_ref, s0_ref, ga_ref, gr_ref, wo_ref,
     g1_ref, nf_ref, sc2_ref, sh2_ref, wr_ref, br_ref, rows_hbm,
     xo_ref, h2_ref, lg_ref, kt_ref, vt_ref, sf_ref,
     q_s, kw_s, vw_s, s_s, qh_s, kh_s, lf_s, vh_s, gh_s, or_s, st_s) = refs[n_in:]
    del rows_hbm
    if has_res:
        yg_ref, gate_ref, g2_ref = refs[1:4]
        x = x_ref[0] + g2_ref[0] * _combine(yg_ref, gate_ref)
    else:
        x = x_ref[0]
    wc, l, lk = ck_ref.shape[1], x_ref.shape[1], kw_s.shape[0]
    new = pl.ds(wc, l)
    _project_rows(x, nw_ref, sc1_ref, sh1_ref, w_ref, lb_ref, q_s, kw_s.at[new], vw_s.at[new],
                  kt_ref.at[0], vt_ref.at[0], qh_s, kh_s, lf_s, vh_s, gh_s)
    for dst, cache in ((kw_s, ck_ref), (vw_s, cv_ref)):
        dst[0:wc, :] = cache[0].astype(BF16)
        dst[wc + l:lk, :] = jnp.zeros((lk - wc - l, ATT_W), BF16)
    oa = _attend(q_s[...], kw_s[...], vw_s[...], bias_ref, s_s, None)
    st_s[...] = s0_ref[0]
    _hgrn_rows(l, qh_s, kh_s, lf_s, vh_s, or_s, st_s)
    sf_ref[0] = st_s[...]
    x_new, words, lg = _merge_rows(oa, or_s[...], gh_s[...], x, ga_ref, gr_ref, wo_ref, g1_ref,
                                   nf_ref, sc2_ref, sh2_ref, wr_ref, br_ref)
    xo_ref[0] = x_new
    h2_ref[...] = words
    lg_ref[0] = lg


def _mixer_sample(x, res, nw, sc1, sh1, w_in_bf, lb, cache_k, cache_v, bias, s0_t, g_att, g_rec, w_out_bf,
                  g1, nf, sc2, sh2, wr2, br, rows, row0):
    b, l, d = x.shape
    wc = cache_k.shape[1]
    lk = bias.shape[2]
    has_res = res is not None
    row = pl.BlockSpec((1, l, d), lambda i: (i, 0, 0))
    half = pl.BlockSpec((1, l, ATT_W), lambda i: (i, 0, 0))
    old = pl.BlockSpec((1, wc, ATT_W), lambda i: (i, 0, 0))
    per_b = pl.BlockSpec((1, 1, d), lambda i: (i, 0, 0))
    const = lambda shp: pl.BlockSpec(shp, lambda i: (0,) * len(shp))
    state = pl.BlockSpec((1, REC_HEADS, REC_DIM, REC_DIM), lambda i: (i, 0, 0, 0))
    in_specs, args = [row], [x]
    if has_res:
        yg, gate, g2, first = res
        blk0 = first // l
        in_specs += [pl.BlockSpec((TOP_K, l, d // 2), lambda i: (0, blk0 + i, 0)),
                     pl.BlockSpec((1, l, TOP_K), lambda i: (i, 0, 0)), per_b]
        args += [yg, gate, g2]
    in_specs += [const((1, d)), per_b, per_b, const(w_in_bf.shape), const((1, REC_W)), old, old, const(bias.shape),
                 state, const((1, ATT_W)), const((1, REC_W)), const(w_out_bf.shape), per_b, const((1, d)),
                 per_b, per_b, const(wr2.shape), const((N_EXPERTS, 1)), pl.BlockSpec(memory_space=pl.ANY)]
    args += [nw.reshape(1, d), sc1, sh1, w_in_bf, lb.reshape(1, REC_W), cache_k, cache_v, bias, s0_t,
             g_att.reshape(1, ATT_W), g_rec.reshape(1, REC_W), w_out_bf, g1, nf.reshape(1, d), sc2, sh2, wr2,
             br.reshape(N_EXPERTS, 1), rows]
    tile = lambda dt: pltpu.VMEM((l, ATT_W), dt)
    return pl.pallas_call(
        functools.partial(_sample_mixer_kernel, has_res),
        grid=(b,),
        in_specs=in_specs,
        out_specs=[row, pl.BlockSpec((l, d // 2), lambda i: (row0 // l + i, 0)),
                   pl.BlockSpec((1, N_EXPERTS, l), lambda i: (i, 0, 0)), half, half, state],
        out_shape=[jax.ShapeDtypeStruct((b, l, d), F32), jax.ShapeDtypeStruct(rows.shape, U32),
                   jax.ShapeDtypeStruct((b, N_EXPERTS, l), F32),
                   jax.ShapeDtypeStruct((b, l, ATT_W), F32), jax.ShapeDtypeStruct((b, l, ATT_W), F32),
                   jax.ShapeDtypeStruct((b, REC_HEADS, REC_DIM, REC_DIM), F32)],
        scratch_shapes=[tile(BF16), pltpu.VMEM((lk, ATT_W), BF16), pltpu.VMEM((lk, ATT_W), BF16),
                        pltpu.VMEM((ATT_HEADS // 2, 2 * l, lk), F32),
                        tile(F32), tile(F32), tile(F32), tile(BF16), tile(F32), tile(F32),
                        pltpu.VMEM((REC_HEADS, REC_DIM, REC_DIM), F32)],
        input_output_aliases={len(args) - 1: 1},
        compiler_params=_cparams("parallel"),
        name="sample_mixer",
    )(*args)


def _route_kernel(lg_ref, idx_ref, gate_ref, rank_ref, cnt_ref, run_ref):
    i = pl.program_id(0)
    t = ROUTE_TILE

    @pl.when(i == 0)
    def _():
        run_ref[...] = jnp.zeros_like(run_ref)

    row = lax.broadcasted_iota(I32, (N_EXPERTS, t), 0)
    earlier = (lax.broadcasted_iota(I32, (t, t), 0) < lax.broadcasted_iota(I32, (t, t), 1)).astype(BF16)
    run = run_ref[:, 0:1]
    for sb in range(lg_ref.shape[1] // t):
        cols = slice(sb * t, (sb + 1) * t)
        x = lg_ref[:, cols]
        vals, idxs, hits = [], [], []
        for _ in range(TOP_K):
            m = jnp.max(x, axis=0, keepdims=True)
            ik = jnp.min(jnp.where(x == m, row, N_EXPERTS), axis=0, keepdims=True)
            hit = row == ik
            x = jnp.where(hit, -jnp.inf, x)
            vals.append(m)
            idxs.append(ik)
            hits.append(hit)
        es = [jnp.exp(v - vals[0]) for v in vals]
        tot = es[0] + es[1] + es[2] + es[3]
        gate_ref[:, cols] = jnp.concatenate([e / tot for e in es] + [jnp.zeros((GATE_ROWS - TOP_K, t), F32)], axis=0)
        idx_ref[:, cols] = jnp.concatenate(idxs, axis=0)
        chosen = (hits[0] | hits[1] | hits[2] | hits[3])
        onehot = jnp.where(chosen, 1.0, 0.0)
        before = jnp.dot(onehot.astype(BF16), earlier, preferred_element_type=F32) + run
        rank_ref[:, cols] = jnp.concatenate(
            [jnp.sum(jnp.where(h, before, 0.0), axis=0, keepdims=True) for h in hits], axis=0).astype(I32)
        run = run + jnp.sum(onehot, axis=1, keepdims=True)
    run_ref[...] = jnp.broadcast_to(run, run_ref.shape)
    cnt_ref[...] = jnp.broadcast_to(run, cnt_ref.shape)


def _route(lg_t):
    e, n = lg_t.shape
    t = max(m * ROUTE_TILE for m in range(1, ROUTE_MAX_SUB + 1) if n % (m * ROUTE_TILE) == 0)
    tok = pl.BlockSpec((TOP_K, t), lambda i: (0, i))
    idx, gate, rank, cnt = pl.pallas_call(
        _route_kernel,
        grid=(n // t,),
        in_specs=[pl.BlockSpec((e, t), lambda i: (0, i))],
        out_specs=[tok, pl.BlockSpec((GATE_ROWS, t), lambda i: (0, i)), tok, pl.BlockSpec((e, LANES), lambda i: (0, 0))],
        out_shape=[jax.ShapeDtypeStruct((TOP_K, n), I32), jax.ShapeDtypeStruct((GATE_ROWS, n), F32),
                   jax.ShapeDtypeStruct((TOP_K, n), I32), jax.ShapeDtypeStruct((e, LANES), F32)],
        scratch_shapes=[pltpu.VMEM((e, LANES), F32)],
        compiler_params=_cparams("arbitrary"),
        name="moe_route",
    )(lg_t)
    counts = cnt[:, 0].astype(I32)
    padded = (counts + MOE_ROWS - 1) // MOE_ROWS * MOE_ROWS
    pad_end = jnp.cumsum(padded)
    pad_start = pad_end - padded
    m = n * TOP_K
    nb = (m + N_EXPERTS * (MOE_ROWS - 1) + MOE_ROWS - 1) // MOE_ROWS
    block_start = jnp.arange(nb, dtype=I32) * MOE_ROWS
    block_expert = jnp.minimum(jnp.sum((pad_end[None, :] <= block_start[:, None]).astype(I32), axis=1), N_EXPERTS - 1)
    experts = jnp.arange(N_EXPERTS, dtype=I32)
    in_block = block_start[:, None] - pad_start[None, :]
    mine = block_expert[:, None] == experts[None, :]
    filled = jnp.sum(jnp.where(mine, jnp.clip(counts[None, :] - in_block, 0, MOE_ROWS), 0), axis=1)
    owns = jnp.any(block_expert[:, None] == experts[None, :], axis=0)
    later = owns[None, :] & (experts[None, :] > block_expert[:, None])
    nxt = jnp.min(jnp.where(later, experts[None, :], N_EXPERTS), axis=1)
    nxt = jnp.where(nxt == N_EXPERTS, -1, nxt)
    run = jnp.sum((owns[None, :] & (experts[None, :] < block_expert[:, None])).astype(I32), axis=1)
    block_info = jnp.concatenate([block_expert, nxt, run % 2, filled]).astype(I32)
    start_of = sum(jnp.where(idx == e, pad_start[e], 0) for e in range(N_EXPERTS))
    dest = start_of + rank
    return gate, dest, block_info, nb * MOE_ROWS


def _windows(dest, w, multiple, fill):
    kk, n = dest.shape
    nwin = n // w
    per = -(-nwin // SC_WORKERS)
    per = -(-per // multiple) * multiple
    idx = dest.reshape(kk, nwin, w).transpose(1, 0, 2)
    pad = jnp.broadcast_to(fill[None, None, :], (SC_WORKERS * per - nwin, kk, w)).astype(I32)
    return jnp.concatenate([idx, pad], axis=0)


def _sc_scatter_rows(x, idx, p_rows):
    n, c = x.shape
    nwin, kk, w = idx.shape
    per = nwin // SC_WORKERS
    nreal = n // w
    mesh = plsc.VectorSubcoreMesh(core_axis_name="c", subcore_axis_name="s")

    @functools.partial(
        pl.kernel, mesh=mesh,
        out_type=jax.ShapeDtypeStruct((p_rows, c), x.dtype),
        scratch_types=[pltpu.VMEM((per, kk, w), I32), pltpu.VMEM((2, w, c), x.dtype),
                       pltpu.SemaphoreType.DMA((2,)), pltpu.SemaphoreType.DMA((2,))],
        name="sc_scatter_rows",
    )
    def k(x_hbm, idx_hbm, out_hbm, idx_v, rows_v, lsem, ssem):
        wid = lax.axis_index("s") * 2 + lax.axis_index("c")
        g0 = wid * per
        pltpu.sync_copy(idx_hbm.at[pl.ds(g0, per)], idx_v)

        def load(j, b):
            g = jnp.minimum(g0 + j, nreal - 1)
            return pltpu.make_async_copy(x_hbm.at[pl.ds(g * w, w)], rows_v.at[b], lsem.at[b])

        def scat(j, b, q):
            return pltpu.make_async_copy(rows_v.at[b], out_hbm.at[idx_v.at[j, q]], ssem.at[b])

        @pl.loop(0, per + 2, step=2)
        def _(j0):
            for b in range(2):
                j = j0 + b

                @pl.when(jnp.logical_and(j >= 2, j < per + 2))
                def _():
                    for q in range(kk):
                        scat(j - 2, b, q).wait()

                @pl.when(j < per)
                def _():
                    load(j, b).start()

                @pl.when(jnp.logical_and(j >= 1, j < per + 1))
                def _():
                    load(j - 1, 1 - b).wait()
                    for q in range(kk):
                        scat(j - 1, 1 - b, q).start()

    return k(x, idx)


def _sc_gather_rows(y, idx):
    p, c = y.shape
    kk, n = idx.shape
    w = SC_GATHER_WIN
    nb = SC_GATHER_BUFS
    wins = LANES // w
    group = wins * kk
    assert group % nb == 0 and group >= nb and n % LANES == 0
    n_units = n // LANES
    max_units = -(-n_units // SC_WORKERS)
    mesh = plsc.VectorSubcoreMesh(core_axis_name="c", subcore_axis_name="s")

    @functools.partial(
        pl.kernel, mesh=mesh,
        out_type=jax.ShapeDtypeStruct((kk, n, c), y.dtype),
        scratch_types=[pltpu.VMEM((2, kk, LANES), I32), pltpu.VMEM((nb, w, c), y.dtype),
                       pltpu.SemaphoreType.DMA((nb,)), pltpu.SemaphoreType.DMA((nb,))],
        name="sc_gather_rows",
    )
    def k(y_hbm, idx_hbm, out_hbm, idx_v, rows_v, gsem, wsem):
        wid = lax.axis_index("s") * 2 + lax.axis_index("c")
        my_units = (n_units - wid + SC_WORKERS - 1) // SC_WORKERS
        n_items = my_units * group

        def gath(slot, jj, q, b):
            return pltpu.make_async_copy(y_hbm.at[idx_v.at[slot, q, pl.ds(jj * w, w)]], rows_v.at[b], gsem.at[b])

        def wr(i, b):
            row = (wid + (i // group) * SC_WORKERS) * LANES + ((i % group) // kk) * w
            return pltpu.make_async_copy(rows_v.at[b], out_hbm.at[i % kk, pl.ds(row, w)], wsem.at[b])

        @pl.loop(0, max_units + 1)
        def _(gi):
            slot = gi % 2

            @pl.when(gi < my_units)
            def _():
                pltpu.sync_copy(idx_hbm.at[:, pl.ds((wid + gi * SC_WORKERS) * LANES, LANES)], idx_v.at[slot])

            for l in range(group):
                i = gi * group + l
                b = l % nb
                bw = (l - (nb - 1)) % nb

                @pl.when(jnp.logical_and(i >= nb, i < n_items + nb))
                def _():
                    wr(i - nb, b).wait()

                @pl.when(i < n_items)
                def _():
                    gath(slot, l // kk, l % kk, b).start()

                @pl.when(jnp.logical_and(i >= nb - 1, i < n_items + nb - 1))
                def _():
                    gath(slot, 0, 0, bw).wait()
                    wr(i - (nb - 1), bw).start()

    return k(y, idx)


def _moe_kernel(layer, info_ref, x_ref, w1_hbm, b1_ref, w2_hbm, b2_ref, o_ref,
                w1f_ref, w2f_ref, w1b_ref, w2b_ref, sem):
    i = pl.program_id(0)
    nb = pl.num_programs(0)
    expert = info_ref[i]
    nxt = info_ref[nb + i]
    slot = info_ref[2 * nb + i]
    first = jnp.logical_or(i == 0, expert != info_ref[jnp.maximum(i - 1, 0)])

    def fetch(e, s):
        return (pltpu.make_async_copy(w1_hbm.at[layer, e], w1f_ref.at[s], sem.at[0, s]),
                pltpu.make_async_copy(w2_hbm.at[layer, e], w2f_ref.at[s], sem.at[1, s]))

    @pl.when(i == 0)
    def _():
        for cp in fetch(expert, slot):
            cp.start()

    @pl.when(first)
    def _():
        for cp in fetch(expert, slot):
            cp.wait()

        @pl.when(nxt >= 0)
        def _():
            for cp in fetch(nxt, 1 - slot):
                cp.start()

        w1b_ref[...] = w1f_ref[slot].astype(BF16)
        w2b_ref[...] = w2f_ref[slot].astype(BF16)

    units = (info_ref[3 * nb + i] + MOE_UNIT - 1) // MOE_UNIT
    for u in range(1, x_ref.shape[0] // MOE_UNIT + 1):
        @pl.when(units == u)
        def _():
            rows = slice(0, u * MOE_UNIT)
            _expert_block(x_ref.at[rows], w1b_ref, b1_ref, w2b_ref, b2_ref, o_ref.at[rows])


def _expert_block(x_ref, w1b_ref, b1_ref, w2b_ref, b2_ref, o_ref):
    lo, hi = _unpack_pairs(x_ref[...])
    x = jnp.concatenate([lo.astype(BF16), hi.astype(BF16)], axis=-1)
    f = w2b_ref.shape[0]
    fc = 512
    acc = None
    for c in range(f // fc):
        gt = jnp.dot(x, w1b_ref[:, c * fc:(c + 1) * fc], preferred_element_type=F32) + b1_ref[0, 0, :, c * fc:(c + 1) * fc]
        up = jnp.dot(x, w1b_ref[:, f + c * fc:f + (c + 1) * fc], preferred_element_type=F32) + b1_ref[0, 0, :, f + c * fc:f + (c + 1) * fc]
        gt = jnp.minimum(gt, SWIGLU_LIMIT)
        up = jnp.clip(up, -SWIGLU_LIMIT, SWIGLU_LIMIT)
        act = (gt * jax.nn.sigmoid(SWIGLU_ALPHA * gt) * (up + 1.0)).astype(BF16)
        part = jnp.dot(act, w2b_ref[c * fc:(c + 1) * fc, :], preferred_element_type=F32)
        acc = part if acc is None else acc + part
    o_ref[...] = _pack_pairs(acc + b2_ref[0, 0])


def _moe_experts(layer, block_expert, xs, p_rows, w1, b1, w2, b2):
    _, e, d, f2 = w1.shape
    f = f2 // 2
    nb = p_rows // MOE_ROWS
    grid_spec = pltpu.PrefetchScalarGridSpec(
        num_scalar_prefetch=1,
        grid=(nb,),
        in_specs=[pl.BlockSpec((MOE_ROWS, d // 2), lambda i, be: (i, 0)),
                  pl.BlockSpec(memory_space=pl.ANY),
                  pl.BlockSpec((1, 1, 1, f2), lambda i, be: (layer, be[i], 0, 0)),
                  pl.BlockSpec(memory_space=pl.ANY),
                  pl.BlockSpec((1, 1, 1, d), lambda i, be: (layer, be[i], 0, 0))],
        out_specs=pl.BlockSpec((MOE_ROWS, d // 2), lambda i, be: (i, 0)),
        scratch_shapes=[pltpu.VMEM((2, d, f2), F32), pltpu.VMEM((2, f, d), F32),
                        pltpu.VMEM((d, f2), BF16), pltpu.VMEM((f, d), BF16),
                        pltpu.SemaphoreType.DMA((2, 2))],
    )
    return pl.pallas_call(
        functools.partial(_moe_kernel, layer),
        grid_spec=grid_spec,
        out_shape=jax.ShapeDtypeStruct((p_rows, d // 2), U32),
        compiler_params=_cparams("arbitrary"),
        name="moe_experts",
    )(block_expert, xs, w1, b1.reshape(b1.shape[0], e, 1, f2), w2, b2.reshape(b2.shape[0], e, 1, d))


def _moe_dispatch(h2_words, lg_t):
    gate, dest, block_info, p_rows = _route(lg_t)
    spare = p_rows + jnp.arange(SC_SCATTER_WIN, dtype=I32)
    xs = _sc_scatter_rows(h2_words, _windows(dest, SC_SCATTER_WIN, 2, spare), p_rows + SC_SCATTER_WIN)
    return xs, gate, dest, block_info, p_rows


def _moe_compute(layer, dispatched, w1, b1, w2, b2):
    xs, gate, back, block_info, p_rows = dispatched
    ys = _moe_experts(layer, block_info, xs, p_rows, w1, b1, w2, b2)
    return _sc_gather_rows(ys, back), gate


def _final_kernel(x_ref, yg_ref, gate_ref, g2_ref, nw_ref, *rest):
    o_ref = rest[-1]
    x = x_ref[0] + g2_ref[0] * _combine(yg_ref, gate_ref)
    ms = jnp.mean(x * x, axis=-1, keepdims=True)
    o_ref[0] = x * lax.rsqrt(ms + EPS) * nw_ref[...]


def _final(x, res, nw, ts, out_buf, ob0, b_total):
    b, s, d = x.shape
    nj = s // ts
    yg, gate, g2, first = res
    blk0 = first // ts
    row = pl.BlockSpec((1, ts, d), lambda i, j: (i, j, 0))
    if gate.ndim == 2:
        gate_spec = pl.BlockSpec((GATE_ROWS, ts), lambda i, j: (0, blk0 + i * nj + j))
    else:
        gate_spec = pl.BlockSpec((1, ts, TOP_K), lambda i, j: (i, j, 0))
    in_specs = [row, pl.BlockSpec((TOP_K, ts, d // 2), lambda i, j: (0, blk0 + i * nj + j, 0)),
                gate_spec,
                pl.BlockSpec((1, 1, d), lambda i, j: (i, 0, 0)),
                pl.BlockSpec((1, d), lambda i, j: (0, 0))]
    args = [x, yg, gate, g2, nw.reshape(1, d)]
    aliases = {}
    if out_buf is not None:
        in_specs.append(pl.BlockSpec(memory_space=pl.ANY))
        args.append(out_buf)
        aliases = {len(args) - 1: 0}
    return pl.pallas_call(
        _final_kernel,
        grid=(b, nj),
        in_specs=in_specs,
        out_specs=pl.BlockSpec((1, ts, d), lambda i, j: (i + ob0, j, 0)),
        out_shape=jax.ShapeDtypeStruct((b_total, s, d), F32),
        input_output_aliases=aliases,
        compiler_params=_cparams("parallel", "parallel"),
        name="final_norm",
    )(*args)


def kernel(x_prompt, x_sample, c_prompt, c_sample, cache_k, cache_v, state_hgrn, norm_mix, norm_ffn, norm_final, w_ada, b_ada, w_in, rel_bias, hgrn_lb_logits, g_attn_out, g_hgrn_out, w_out, w_router, b_router, w_e1, b_e1, w_e2, b_e2):
    depth = w_in.shape[0]
    bp, sp, d = x_prompt.shape
    bs, ls, _ = x_sample.shape
    wc = cache_k.shape[2]
    n_p, n_s = bp * sp, bs * ls

    lbs = jax.nn.softmax(hgrn_lb_logits.astype(F32), axis=0)
    lbs = jnp.cumsum(lbs, axis=0) - lbs[0]

    mods = _modulation(jnp.concatenate([c_prompt, c_sample], axis=0), w_ada, b_ada)
    lk_s = -(-(wc + ls) // LANES) * LANES
    tail = min(LEFT, sp)
    rec_chunk = min(REC_CHUNK, sp)

    n_groups = PROMPT_GROUPS if bp % PROMPT_GROUPS == 0 else 1
    gb = bp // n_groups
    n_g = gb * sp
    x_g = [(x_prompt, g * gb) for g in range(n_groups)]
    x_s = x_sample
    res_g = [None] * n_groups
    res_s = None
    kp_l, vp_l, sp_l, ks_l, vs_l, ss_l = [], [], [], [], [], []
    for layer in range(depth):
        m6 = mods[layer].reshape(bp + bs, 1, N_MOD, d)
        mod_s = [m6[bp:, :, n, :] for n in range(N_MOD)]
        w_in_bf = w_in[layer].astype(BF16)
        w_out_bf = w_out[layer].astype(BF16)
        wr_hi, wr_mid, _ = _split3(w_router[layer].T)
        wr2 = jnp.concatenate([wr_hi, wr_mid], axis=0)
        table = rel_bias[layer]
        bias_p = _prompt_bias(table)

        def mixer_prompt(g):
            mod = [m6[g * gb:(g + 1) * gb, :, n, :] for n in range(N_MOD)]
            x_new, h2, lg, kt, vt, s_fin = _mixer_prompt(
                x_g[g][0], x_g[g][1], res_g[g], norm_mix[layer], mod[1], mod[0], w_in_bf, lbs[layer], bias_p,
                g_attn_out[layer], g_hgrn_out[layer], w_out_bf, mod[2], norm_ffn[layer], mod[4], mod[3],
                wr2, b_router[layer], tail, rec_chunk, n_s if g == n_groups - 1 else 0)
            x_g[g] = (x_new, 0)
            return h2, lg, kt, vt, s_fin, mod[5]

        def mixer_sample(rows):
            s0 = jnp.swapaxes(state_hgrn[layer].astype(F32), -1, -2)
            x_new, rows, lg, kt, vt, s_new = _mixer_sample(
                x_s, res_s, norm_mix[layer], mod_s[1], mod_s[0], w_in_bf, lbs[layer],
                cache_k[layer].reshape(bs, wc, ATT_W), cache_v[layer].reshape(bs, wc, ATT_W),
                _sample_bias(table, ls, wc, lk_s), s0, g_attn_out[layer], g_hgrn_out[layer], w_out_bf,
                mod_s[2], norm_ffn[layer], mod_s[4], mod_s[3], wr2, b_router[layer], rows, n_g)
            return x_new, rows, jnp.swapaxes(lg, 0, 1).reshape(N_EXPERTS, n_s), kt, vt, s_new

        dispatched, g2s, k_t, v_t, s_t = [], [], [], [], []
        for g in range(n_groups):
            h2, lg, kt, vt, s_fin, g2 = mixer_prompt(g)
            if g == n_groups - 1:
                x_s, h2, lg_s, ka_s, va_s, s_new = mixer_sample(h2)
                ks_l.append(ka_s.reshape(bs, ls, ATT_HEADS, ATT_DIM))
                vs_l.append(va_s.reshape(bs, ls, ATT_HEADS, ATT_DIM))
                ss_l.append(jnp.swapaxes(s_new, -1, -2))
                lg = jnp.concatenate([lg, lg_s], axis=1)
            dispatched.append(_moe_dispatch(h2, lg))
            g2s.append(g2)
            k_t.append(kt)
            v_t.append(vt)
            s_t.append(s_fin)
        kp_l.append(jnp.concatenate(k_t, axis=0).reshape(bp, tail, ATT_HEADS, ATT_DIM))
        vp_l.append(jnp.concatenate(v_t, axis=0).reshape(bp, tail, ATT_HEADS, ATT_DIM))
        sp_l.append(jnp.swapaxes(jnp.concatenate(s_t, axis=0), -1, -2))

        for g in range(n_groups):
            yg, gate = _moe_compute(layer, dispatched[g], w_e1, b_e1, w_e2, b_e2)
            res_g[g] = (yg, gate, g2s[g], 0)
            if g == n_groups - 1:
                res_s = (yg, gate[:TOP_K, n_g:].T.reshape(bs, ls, TOP_K), mod_s[5], n_g)

    y_prompt = None
    for g in range(n_groups):
        y_prompt = _final(x_g[g][0], res_g[g], norm_final, SEQ_TILE, y_prompt, g * gb, bp)
    y_sample = _final(x_s, res_s, norm_final, ls, None, 0, bs)
    return (y_prompt, y_sample, jnp.stack(kp_l), jnp.stack(vp_l), jnp.stack(sp_l),
            jnp.stack(ks_l), jnp.stack(vs_l), jnp.stack(ss_l))
```

```python
import functools

import jax
import jax.numpy as jnp
import numpy as np
from jax import lax
from jax.experimental import pallas as pl
from jax.experimental.pallas import tpu as pltpu
from jax.experimental.pallas import tpu_sc as plsc

F32 = jnp.float32
BF16 = jnp.bfloat16
I32 = jnp.int32
U32 = jnp.uint32

EPS = 1e-6
CHUNK = 64
LEFT_CHUNKS = 8
LEFT = LEFT_CHUNKS * CHUNK
MAX_REL = 2 * CHUNK
ATT_HEADS = 8
ATT_DIM = 64
ATT_W = ATT_HEADS * ATT_DIM
REC_HEADS = 4
REC_DIM = 128
REC_W = REC_HEADS * REC_DIM
N_EXPERTS = 32
TOP_K = 4
SWIGLU_ALPHA = 1.702
SWIGLU_LIMIT = 7.0
N_MOD = 6
NEG = -1e30
LOG2E = 1.4426950408889634

LANES = 128
ATT_QSUB = 128
ATT_WIN = LEFT + ATT_QSUB
SEQ_TILE = 512
REC_CHUNK = 128
MOE_ROWS = 1024
MOE_UNIT = 256
ROUTE_TILE = 256
ROUTE_MAX_SUB = 4
PROMPT_GROUPS = 2
GATE_ROWS = 8
VMEM_LIMIT = 56 * 1024 * 1024
SC_WORKERS = 32
SC_SCATTER_WIN = 64
SC_GATHER_WIN = 32
SC_GATHER_BUFS = 4


def _cparams(*sem):
    return pltpu.CompilerParams(dimension_semantics=sem, vmem_limit_bytes=VMEM_LIMIT)


def _split3(x):
    hi = x.astype(BF16)
    r1 = x - hi.astype(F32)
    mid = r1.astype(BF16)
    lo = (r1 - mid.astype(F32)).astype(BF16)
    return hi, mid, lo


def _bits(x):
    return lax.bitcast_convert_type(x, U32)


def _pack_pairs(x):
    r = _bits(x.astype(BF16).astype(F32))
    half = x.shape[1] // 2
    return jnp.bitwise_or(jnp.bitwise_and(r[:, half:], jnp.uint32(0xFFFF0000)),
                          jnp.right_shift(r[:, :half], jnp.uint32(16)))


def _unpack_pairs(word):
    lo = lax.bitcast_convert_type(jnp.left_shift(word, jnp.uint32(16)), F32)
    hi = lax.bitcast_convert_type(jnp.bitwise_and(word, jnp.uint32(0xFFFF0000)), F32)
    return lo, hi


def _mod_kernel(c_ref, w_ref, b_ref, o_ref):
    c = c_ref[...]
    act = (c * jax.nn.sigmoid(c)).astype(BF16)
    o_ref[0] = jnp.dot(act, w_ref[0].astype(BF16), preferred_element_type=F32) + b_ref[0]


def _modulation(c_all, w_ada, b_ada):
    depth, d, n6 = w_ada.shape
    rows = c_all.shape[0]
    tn = 1536
    return pl.pallas_call(
        _mod_kernel,
        grid=(depth, n6 // tn),
        in_specs=[pl.BlockSpec((rows, d), lambda l, j: (0, 0)),
                  pl.BlockSpec((1, d, tn), lambda l, j: (l, 0, j)),
                  pl.BlockSpec((1, 1, tn), lambda l, j: (l, 0, j))],
        out_specs=pl.BlockSpec((1, rows, tn), lambda l, j: (l, 0, j)),
        out_shape=jax.ShapeDtypeStruct((depth, rows, n6), F32),
        compiler_params=_cparams("parallel", "parallel"),
        name="adaln_modulation",
    )(c_all, w_ada, b_ada.reshape(depth, 1, n6))


def _combine(yg_ref, gate_ref):
    g = gate_ref[0] if len(gate_ref.shape) == 3 else gate_ref[...].T
    lo = hi = None
    for k in range(TOP_K):
        l, h = _unpack_pairs(yg_ref[k])
        gk = g[:, k:k + 1]
        lo = l * gk if lo is None else lo + l * gk
        hi = h * gk if hi is None else hi + h * gk
    return jnp.concatenate([lo, hi], axis=-1)


def _project_rows(x, nw_ref, sc_ref, sh_ref, w_ref, lb_ref,
                  qa_ref, ka_ref, va_ref, keep_f32, qh_ref, kh_ref, lf_ref, vh_ref, gh_ref):
    ms = jnp.mean(x * x, axis=-1, keepdims=True)
    h = (x * lax.rsqrt(ms + EPS) * nw_ref[...]) * (1.0 + sc_ref[0]) + sh_ref[0]
    hb = h.astype(BF16)

    def proj(g):
        return jnp.dot(hb, w_ref[:, g * ATT_W:(g + 1) * ATT_W], preferred_element_type=F32)

    qa_ref[...] = (proj(0) * (ATT_DIM ** -0.5)).astype(BF16)
    k = proj(1)
    ka_ref[...] = k.astype(BF16)
    keep_f32(0, k)
    v = proj(2)
    va_ref[...] = v.astype(BF16)
    keep_f32(1, v)
    qh = proj(3)
    qh_ref[...] = qh * jax.nn.sigmoid(qh)
    f = proj(4)
    lb = lb_ref[...]
    lf_ref[...] = jnp.log(lb + (1.0 - lb) * jax.nn.sigmoid(f))
    kh_ref[...] = (1.0 - lb) * jax.nn.sigmoid(-f)
    vh_ref[...] = proj(5).astype(BF16)
    gh_ref[...] = proj(6)


def _attend(q, kw, vw, bias_ref, s_ref, col_thr):
    lq, lk = q.shape[0], kw.shape[0]
    lane = lax.broadcasted_iota(I32, (lq, LANES), 1)
    lo = lane < ATT_DIM
    nt = (((1,), (1,)), ((), ()))
    if col_thr is not None:
        keep = lax.broadcasted_iota(I32, (2 * lq, lk), 1) >= col_thr
    ms = []
    for p in range(ATT_HEADS // 2):
        sl = slice(p * LANES, (p + 1) * LANES)
        qp = q[:, sl]
        z = jnp.zeros_like(qp)
        lhs = jnp.concatenate([jnp.where(lo, qp, z), jnp.where(lo, z, qp)], axis=0)
        s = lax.dot_general(lhs, kw[:, sl], nt, preferred_element_type=F32) + bias_ref[p]
        if col_thr is not None:
            s = jnp.where(keep, s, NEG)
        ms.append(jnp.max(s, axis=-1, keepdims=True))
        s_ref[p] = s
    outs = []
    for p in range(ATT_HEADS // 2):
        sl = slice(p * LANES, (p + 1) * LANES)
        e = jnp.exp(s_ref[p] - ms[p])
        l = jnp.sum(e, axis=-1, keepdims=True)
        o = jnp.dot(e.astype(BF16), vw[:, sl], preferred_element_type=F32) / l
        outs.append(jnp.where(lo, o[:lq], o[lq:]))
    return jnp.concatenate(outs, axis=-1)


def _band_attention(q_ref, kw_ref, vw_ref, bias_ref, s_ref, o_ref, first_tile):
    ts = q_ref.shape[0]

    def body(masked, r, carry):
        r0 = pl.multiple_of(r * ATT_QSUB, ATT_QSUB)
        q = q_ref[pl.ds(r0, ATT_QSUB), :]
        kw = kw_ref[pl.ds(r0, ATT_WIN), :]
        vw = vw_ref[pl.ds(r0, ATT_WIN), :]
        o_ref[pl.ds(r0, ATT_QSUB), :] = _attend(q, kw, vw, bias_ref, s_ref, (ts - r0) if masked else None)
        return carry

    @pl.when(first_tile)
    def _():
        lax.fori_loop(0, ts // ATT_QSUB, functools.partial(body, True), 0)

    @pl.when(jnp.logical_not(first_tile))
    def _():
        lax.fori_loop(0, ts // ATT_QSUB, functools.partial(body, False), 0)


def _toeplitz_bias(table, lq, lk, offset, valid):
    m = lq + lk - 1
    k = np.arange(m)
    diff = np.where(k < lk, k, k - m)
    g = table[:, np.clip(offset - diff, -MAX_REL, MAX_REL) + MAX_REL]
    skew = jnp.tile(g, (1, lq))[:, :lq * (m - 1)].reshape(-1, lq, m - 1)[:, :, :lk]
    bias = jnp.where(jnp.asarray(valid)[None], skew, NEG).astype(F32)
    return bias.reshape(ATT_HEADS // 2, 2 * lq, lk)


def _prompt_bias(table):
    t = np.arange(ATT_QSUB)[:, None]
    j = np.arange(ATT_WIN)[None, :]
    start = (t // CHUNK) * CHUNK
    return _toeplitz_bias(table, ATT_QSUB, ATT_WIN, LEFT, (j >= start) & (j < start + LEFT + CHUNK))


def _sample_bias(table, l, w, lk):
    valid = np.broadcast_to(np.arange(lk)[None, :] < w + l, (l, lk))
    return _toeplitz_bias(table, l, lk, w, valid)


def _neg_abs(x):
    return lax.bitcast_convert_type(jnp.bitwise_or(_bits(x), jnp.uint32(0x80000000)), F32)


def _hgrn_rows(chunk, q_ref, k_ref, lf_ref, v_ref, o_ref, st_ref):
    ts = q_ref.shape[0]
    n_chunks = ts // chunk

    row = lax.broadcasted_iota(I32, (chunk, chunk), 0)
    colm = lax.broadcasted_iota(I32, (chunk, chunk), 1)
    tri = (colm <= row).astype(BF16)
    xor = jnp.bitwise_xor(row, colm)
    lower = colm < row
    rid = lax.broadcasted_iota(I32, (chunk, REC_W), 0)

    levels = []
    b = chunk // 2
    while b >= 1:
        levels.append(b)
        b //= 2
    pair_mask = [xor == 0] + [lower & (xor >= b) & (xor < 2 * b) for b in levels]
    upper = [jnp.bitwise_and(rid, b) != 0 for b in levels]
    ph4 = jnp.bitwise_and(rid, 3)
    odd = jnp.bitwise_and(rid, 1) == 1
    nt = (((1,), (1,)), ((), ()))
    tn = (((0,), (0,)), ((), ()))
    sls = [slice(h * REC_DIM, (h + 1) * REC_DIM) for h in range(REC_HEADS)]

    def boundary(cum, b):
        if 2 * b >= 8:
            pieces = [jnp.broadcast_to(cum[m * 2 * b + b - 1:m * 2 * b + b, :], (2 * b, REC_W))
                      for m in range(chunk // (2 * b))]
            return pieces[0] if len(pieces) == 1 else jnp.concatenate(pieces, axis=0)
        if b == 2:
            up1 = pltpu.roll(cum, chunk - 1, 0)
            dn1 = pltpu.roll(cum, 1, 0)
            dn2 = pltpu.roll(cum, 2, 0)
            return jnp.where(ph4 == 0, up1, jnp.where(ph4 == 1, cum, jnp.where(ph4 == 2, dn1, dn2)))
        return jnp.where(odd, pltpu.roll(cum, 1, 0), cum)

    def one_chunk(c, carry):
        r0 = pl.multiple_of(c * chunk, chunk)
        rows = pl.ds(r0, chunk)
        q = q_ref[rows, :]
        k = k_ref[rows, :]
        v = v_ref[rows, :]
        lf = lf_ref[rows, :]
        cum = sum(jnp.dot(tri, part, preferred_element_type=F32) for part in _split3(lf)) * LOG2E
        last = cum[chunk - 1:chunk, :]
        q_in = (q * jnp.exp2(cum)).astype(BF16)
        k_out = (k * jnp.exp2(last - cum)).astype(BF16)
        decay = jnp.exp2(last)
        qb = q.astype(BF16)
        kb = k.astype(BF16)
        zs = []
        for n, b in enumerate(levels):
            w = jnp.exp2(_neg_abs(cum - boundary(cum, b)))
            zs.append((jnp.where(upper[n], q, k) * w).astype(BF16))
        sts = [st_ref[h] for h in range(REC_HEADS)]
        d0 = [lax.dot_general(qb[:, sl], kb[:, sl], nt, preferred_element_type=F32) for sl in sls]
        dl = [[lax.dot_general(z[:, sl], z[:, sl], nt, preferred_element_type=F32) for z in zs] for sl in sls]
        oi = [lax.dot_general(q_in[:, sl], st.astype(BF16), nt, preferred_element_type=F32)
              for sl, st in zip(sls, sts)]
        upd = [lax.dot_general(v[:, sl], k_out[:, sl], tn, preferred_element_type=F32) for sl in sls]
        outs = []
        for h, sl in enumerate(sls):
            a = jnp.where(pair_mask[0], d0[h], 0.0)
            for n in range(len(levels)):
                a = jnp.where(pair_mask[n + 1], dl[h][n], a)
            outs.append(jnp.dot(a.astype(BF16), v[:, sl], preferred_element_type=F32) + oi[h])
        o_ref[rows, :] = jnp.concatenate(outs, axis=-1)
        for h, sl in enumerate(sls):
            st_ref[h] = decay[:, sl] * sts[h] + upd[h]
        return carry

    lax.fori_loop(0, n_chunks, one_chunk, 0, unroll=2 if n_chunks % 2 == 0 else 1)


def _head_rms(x, width):
    lane = lax.broadcasted_iota(I32, (x.shape[0], LANES), 1)
    lo = lane < ATT_DIM
    outs = []
    for p in range(x.shape[1] // LANES):
        xp = x[:, p * LANES:(p + 1) * LANES]
        sq = xp * xp
        tot = jnp.sum(sq, axis=-1, keepdims=True)
        if width == LANES:
            scale = lax.rsqrt(tot * (1.0 / LANES) + EPS)
        else:
            s_lo = jnp.sum(jnp.where(lo, sq, 0.0), axis=-1, keepdims=True)
            scale = jnp.where(lo, lax.rsqrt(s_lo * (1.0 / ATT_DIM) + EPS),
                              lax.rsqrt((tot - s_lo) * (1.0 / ATT_DIM) + EPS))
        outs.append(xp * scale)
    return jnp.concatenate(outs, axis=-1)


def _merge_rows(oa, orr, gh, x, ga_ref, gr_ref, wo_ref, g1_ref, nf_ref, sc_ref, sh_ref, wr_ref, br_ref):
    a = _head_rms(oa, ATT_DIM) * ga_ref[...]
    r = _head_rms(orr, REC_DIM) * gr_ref[...] * (gh * jax.nn.sigmoid(gh))
    mix = jnp.dot(a.astype(BF16), wo_ref[0:ATT_W, :], preferred_element_type=F32)
    mix += jnp.dot(r.astype(BF16), wo_ref[ATT_W:ATT_W + REC_W, :], preferred_element_type=F32)
    x = x + g1_ref[0] * mix
    ms = jnp.mean(x * x, axis=-1, keepdims=True)
    h2 = (x * lax.rsqrt(ms + EPS) * nf_ref[...]) * (1.0 + sc_ref[0]) + sh_ref[0]
    hh = h2.astype(BF16)
    words = _pack_pairs(h2)
    hm = (h2 - hh.astype(F32)).astype(BF16)
    nt = (((1,), (1,)), ((), ()))
    l1 = lax.dot_general(wr_ref[...], hh, nt, preferred_element_type=F32)
    l2 = lax.dot_general(wr_ref[0:N_EXPERTS, :], hm, nt, preferred_element_type=F32)
    lg = l1[0:N_EXPERTS] + l1[N_EXPERTS:2 * N_EXPERTS] + l2 + br_ref[...]
    return x, words, lg


def _mixer_kernel(has_res, n_alias, chunk, *refs):
    n_in = 4 if has_res else 1
    x_ref = refs[0]
    (nw_ref, sc1_ref, sh1_ref, lb_ref, ga_ref, gr_ref, g1_ref, nf_ref, sc2_ref, sh2_ref, wr_ref, br_ref,
     win_hbm, wout_hbm, bias_hbm) = refs[n_in:n_in + 15]
    (xo_ref, h2_ref, lg_ref, kt_ref, vt_ref, sf_ref,
     win_v, wout_v, bias_v, q_s, kw_s, vw_s, s_s, qh_s, kh_s, lf_s, vh_s, gh_s, oa_s, or_s, st_s, sem
     ) = refs[n_in + 15 + n_alias:]
    j = pl.program_id(1)
    ts = x_ref.shape[1]
    last = j == pl.num_programs(1) - 1

    def keep_f32(which, rows):
        @pl.when(last)
        def _():
            (kt_ref, vt_ref)[which][0, 0] = rows.T

    @pl.when(j == 0)
    def _():
        copies = [pltpu.make_async_copy(src, dst, sem.at[n])
                  for n, (src, dst) in enumerate(((win_hbm, win_v), (wout_hbm, wout_v), (bias_hbm, bias_v)))]
        for cp in copies:
            cp.start()
        for cp in copies:
            cp.wait()
        st_s[...] = jnp.zeros_like(st_s)
        kw_s[0:ts, :] = jnp.zeros((ts, ATT_W), BF16)
        vw_s[0:ts, :] = jnp.zeros((ts, ATT_W), BF16)

    if has_res:
        yg_ref, gate_ref, g2_ref = refs[1:4]
        x = x_ref[0] + g2_ref[0] * _combine(yg_ref, gate_ref)
        xo_ref[0] = x
    else:
        x = x_ref[0]
    cur = pl.ds(ts, ts)
    _project_rows(x, nw_ref, sc1_ref, sh1_ref, win_v, lb_ref, q_s, kw_s.at[cur], vw_s.at[cur],
                  keep_f32, qh_s, kh_s, lf_s, vh_s, gh_s)
    _band_attention(q_s, kw_s, vw_s, bias_v, s_s, oa_s, j == 0)
    _hgrn_rows(chunk, qh_s, kh_s, lf_s, vh_s, or_s, st_s)
    x_in = xo_ref[0] if has_res else x_ref[0]
    x_new, words, lg = _merge_rows(oa_s[...], or_s[...], gh_s[...], x_in, ga_ref, gr_ref, wout_v, g1_ref,
                                   nf_ref, sc2_ref, sh2_ref, wr_ref, br_ref)
    xo_ref[0] = x_new
    h2_ref[...] = words
    lg_ref[...] = lg
    kw_s[0:ts, :] = kw_s[ts:2 * ts, :]
    vw_s[0:ts, :] = vw_s[ts:2 * ts, :]

    @pl.when(last)
    def _():
        sf_ref[0] = st_s[...]


def _mixer_prompt(x, xb0, res, nw, sc1, sh1, w_in_bf, lb, bias, g_att, g_rec, w_out_bf, g1, nf, sc2, sh2,
                  wr2, br, chunk, extra_rows, kv_slot, kv_shape, kv_bufs):
    _, s, d = x.shape
    b = sc1.shape[0]
    ts = LEFT
    nj = s // ts
    has_res = res is not None
    kv_l, kv_b = kv_slot
    row = pl.BlockSpec((1, ts, d), lambda i, j: (i, j, 0))
    per_b = pl.BlockSpec((1, 1, d), lambda i, j: (i, 0, 0))
    const = lambda shp: pl.BlockSpec(shp, lambda i, j: (0,) * len(shp))
    hbm = pl.BlockSpec(memory_space=pl.ANY)
    in_specs, args = [pl.BlockSpec((1, ts, d), lambda i, j: (i + xb0, j, 0))], [x]
    if has_res:
        yg, gate, g2, first = res
        blk0 = first // ts
        in_specs += [pl.BlockSpec((TOP_K, ts, d // 2), lambda i, j: (0, blk0 + i * nj + j, 0)),
                     pl.BlockSpec((GATE_ROWS, ts), lambda i, j: (0, blk0 + i * nj + j)), per_b]
        args += [yg, gate, g2]
    in_specs += [const((1, d)), per_b, per_b, const((1, REC_W)), const((1, ATT_W)), const((1, REC_W)),
                 per_b, const((1, d)), per_b, per_b, const(wr2.shape), const((N_EXPERTS, 1)), hbm, hbm, hbm]
    args += [nw.reshape(1, d), sc1, sh1, lb.reshape(1, REC_W), g_att.reshape(1, ATT_W), g_rec.reshape(1, REC_W),
             g1, nf.reshape(1, d), sc2, sh2, wr2, br.reshape(N_EXPERTS, 1), w_in_bf, w_out_bf, bias]
    aliases = {}
    if kv_bufs is not None:
        aliases = {len(args): 3, len(args) + 1: 4}
        in_specs += [hbm, hbm]
        args += list(kv_bufs)
    tail_spec = pl.BlockSpec((1, 1, ATT_W, ts), lambda i, j: (kv_l, kv_b + i, 0, 0))
    state_spec = pl.BlockSpec((1, REC_HEADS, REC_DIM, REC_DIM), lambda i, j: (i, 0, 0, 0))
    tile = lambda dt: pltpu.VMEM((ts, ATT_W), dt)
    return pl.pallas_call(
        functools.partial(_mixer_kernel, has_res, len(aliases), chunk),
        grid=(b, nj),
        in_specs=in_specs,
        out_specs=[row, pl.BlockSpec((ts, d // 2), lambda i, j: (i * nj + j, 0)),
                   pl.BlockSpec((N_EXPERTS, ts), lambda i, j: (0, i * nj + j)), tail_spec, tail_spec, state_spec],
        out_shape=[jax.ShapeDtypeStruct((b, s, d), F32), jax.ShapeDtypeStruct((b * s + extra_rows, d // 2), U32),
                   jax.ShapeDtypeStruct((N_EXPERTS, b * s), F32),
                   jax.ShapeDtypeStruct(kv_shape, F32), jax.ShapeDtypeStruct(kv_shape, F32),
                   jax.ShapeDtypeStruct((b, REC_HEADS, REC_DIM, REC_DIM), F32)],
        input_output_aliases=aliases,
        scratch_shapes=[pltpu.VMEM(w_in_bf.shape, BF16), pltpu.VMEM(w_out_bf.shape, BF16), pltpu.VMEM(bias.shape, F32),
                        tile(BF16), pltpu.VMEM((2 * ts, ATT_W), BF16), pltpu.VMEM((2 * ts, ATT_W), BF16),
                        pltpu.VMEM((ATT_HEADS // 2, 2 * ATT_QSUB, ATT_WIN), F32),
                        tile(F32), tile(F32), tile(F32), tile(BF16), tile(F32), tile(F32), tile(F32),
                        pltpu.VMEM((REC_HEADS, REC_DIM, REC_DIM), F32), pltpu.SemaphoreType.DMA((3,))],
        compiler_params=_cparams("parallel", "arbitrary"),
        name="prompt_mixer",
    )(*args)


def _sample_mixer_kernel(has_res, *refs):
    n_in = 4 if has_res else 1
    x_ref = refs[0]
    (nw_ref, sc1_ref, sh1_ref, w_ref, lb_ref, ck_ref, cv_ref, bias_ref, s0_ref, ga_ref, gr_ref, wo_ref,
     g1_ref, nf_ref, sc2_ref, sh2_ref, wr_ref, br_ref, rows_hbm,
     xo_ref, h2_ref, lg_ref, kt_ref, vt_ref, sf_ref,
     q_s, kw_s, vw_s, s_s, qh_s, kh_s, lf_s, vh_s, gh_s, or_s, st_s) = refs[n_in:]
    del rows_hbm
    if has_res:
        yg_ref, gate_ref, g2_ref = refs[1:4]
        x = x_ref[0] + g2_ref[0] * _combine(yg_ref, gate_ref)
    else:
        x = x_ref[0]
    wc, l, lk = ck_ref.shape[1], x_ref.shape[1], kw_s.shape[0]
    new = pl.ds(wc, l)
    def keep_f32(which, rows):
        (kt_ref, vt_ref)[which][0] = rows

    _project_rows(x, nw_ref, sc1_ref, sh1_ref, w_ref, lb_ref, q_s, kw_s.at[new], vw_s.at[new],
                  keep_f32, qh_s, kh_s, lf_s, vh_s, gh_s)
    for dst, cache in ((kw_s, ck_ref), (vw_s, cv_ref)):
        dst[0:wc, :] = cache[0].astype(BF16)
        dst[wc + l:lk, :] = jnp.zeros((lk - wc - l, ATT_W), BF16)
    oa = _attend(q_s[...], kw_s[...], vw_s[...], bias_ref, s_s, None)
    st_s[...] = s0_ref[0]
    _hgrn_rows(l, qh_s, kh_s, lf_s, vh_s, or_s, st_s)
    sf_ref[0] = st_s[...]
    x_new, words, lg = _merge_rows(oa, or_s[...], gh_s[...], x, ga_ref, gr_ref, wo_ref, g1_ref,
                                   nf_ref, sc2_ref, sh2_ref, wr_ref, br_ref)
    xo_ref[0] = x_new
    h2_ref[...] = words
    lg_ref[0] = lg


def _mixer_sample(x, res, nw, sc1, sh1, w_in_bf, lb, cache_k, cache_v, bias, s0_t, g_att, g_rec, w_out_bf,
                  g1, nf, sc2, sh2, wr2, br, rows, row0):
    b, l, d = x.shape
    wc = cache_k.shape[1]
    lk = bias.shape[2]
    has_res = res is not None
    row = pl.BlockSpec((1, l, d), lambda i: (i, 0, 0))
    half = pl.BlockSpec((1, l, ATT_W), lambda i: (i, 0, 0))
    old = pl.BlockSpec((1, wc, ATT_W), lambda i: (i, 0, 0))
    per_b = pl.BlockSpec((1, 1, d), lambda i: (i, 0, 0))
    const = lambda shp: pl.BlockSpec(shp, lambda i: (0,) * len(shp))
    state = pl.BlockSpec((1, REC_HEADS, REC_DIM, REC_DIM), lambda i: (i, 0, 0, 0))
    in_specs, args = [row], [x]
    if has_res:
        yg, gate, g2, first = res
        blk0 = first // l
        in_specs += [pl.BlockSpec((TOP_K, l, d // 2), lambda i: (0, blk0 + i, 0)),
                     pl.BlockSpec((1, l, TOP_K), lambda i: (i, 0, 0)), per_b]
        args += [yg, gate, g2]
    in_specs += [const((1, d)), per_b, per_b, const(w_in_bf.shape), const((1, REC_W)), old, old, const(bias.shape),
                 state, const((1, ATT_W)), const((1, REC_W)), const(w_out_bf.shape), per_b, const((1, d)),
                 per_b, per_b, const(wr2.shape), const((N_EXPERTS, 1)), pl.BlockSpec(memory_space=pl.ANY)]
    args += [nw.reshape(1, d), sc1, sh1, w_in_bf, lb.reshape(1, REC_W), cache_k, cache_v, bias, s0_t,
             g_att.reshape(1, ATT_W), g_rec.reshape(1, REC_W), w_out_bf, g1, nf.reshape(1, d), sc2, sh2, wr2,
             br.reshape(N_EXPERTS, 1), rows]
    tile = lambda dt: pltpu.VMEM((l, ATT_W), dt)
    return pl.pallas_call(
        functools.partial(_sample_mixer_kernel, has_res),
        grid=(b,),
        in_specs=in_specs,
        out_specs=[row, pl.BlockSpec((l, d // 2), lambda i: (row0 // l + i, 0)),
                   pl.BlockSpec((1, N_EXPERTS, l), lambda i: (i, 0, 0)), half, half, state],
        out_shape=[jax.ShapeDtypeStruct((b, l, d), F32), jax.ShapeDtypeStruct(rows.shape, U32),
                   jax.ShapeDtypeStruct((b, N_EXPERTS, l), F32),
                   jax.ShapeDtypeStruct((b, l, ATT_W), F32), jax.ShapeDtypeStruct((b, l, ATT_W), F32),
                   jax.ShapeDtypeStruct((b, REC_HEADS, REC_DIM, REC_DIM), F32)],
        scratch_shapes=[tile(BF16), pltpu.VMEM((lk, ATT_W), BF16), pltpu.VMEM((lk, ATT_W), BF16),
                        pltpu.VMEM((ATT_HEADS // 2, 2 * l, lk), F32),
                        tile(F32), tile(F32), tile(F32), tile(BF16), tile(F32), tile(F32),
                        pltpu.VMEM((REC_HEADS, REC_DIM, REC_DIM), F32)],
        input_output_aliases={len(args) - 1: 1},
        compiler_params=_cparams("parallel"),
        name="sample_mixer",
    )(*args)


def _route_kernel(lg_ref, idx_ref, gate_ref, rank_ref, cnt_ref, run_ref):
    i = pl.program_id(0)
    t = ROUTE_TILE

    @pl.when(i == 0)
    def _():
        run_ref[...] = jnp.zeros_like(run_ref)

    row = lax.broadcasted_iota(I32, (N_EXPERTS, t), 0)
    earlier = (lax.broadcasted_iota(I32, (t, t), 0) < lax.broadcasted_iota(I32, (t, t), 1)).astype(BF16)
    run = run_ref[:, 0:1]
    for sb in range(lg_ref.shape[1] // t):
        cols = slice(sb * t, (sb + 1) * t)
        x = lg_ref[:, cols]
        vals, idxs, hits = [], [], []
        for _ in range(TOP_K):
            m = jnp.max(x, axis=0, keepdims=True)
            ik = jnp.min(jnp.where(x == m, row, N_EXPERTS), axis=0, keepdims=True)
            hit = row == ik
            x = jnp.where(hit, -jnp.inf, x)
            vals.append(m)
            idxs.append(ik)
            hits.append(hit)
        es = [jnp.exp(v - vals[0]) for v in vals]
        tot = es[0] + es[1] + es[2] + es[3]
        gate_ref[:, cols] = jnp.concatenate([e / tot for e in es] + [jnp.zeros((GATE_ROWS - TOP_K, t), F32)], axis=0)
        idx_ref[:, cols] = jnp.concatenate(idxs, axis=0)
        chosen = (hits[0] | hits[1] | hits[2] | hits[3])
        onehot = jnp.where(chosen, 1.0, 0.0)
        before = jnp.dot(onehot.astype(BF16), earlier, preferred_element_type=F32) + run
        rank_ref[:, cols] = jnp.concatenate(
            [jnp.sum(jnp.where(h, before, 0.0), axis=0, keepdims=True) for h in hits], axis=0).astype(I32)
        run = run + jnp.sum(onehot, axis=1, keepdims=True)
    run_ref[...] = jnp.broadcast_to(run, run_ref.shape)
    cnt_ref[...] = jnp.broadcast_to(run, cnt_ref.shape)


def _route(lg_t):
    e, n = lg_t.shape
    t = max(m * ROUTE_TILE for m in range(1, ROUTE_MAX_SUB + 1) if n % (m * ROUTE_TILE) == 0)
    tok = pl.BlockSpec((TOP_K, t), lambda i: (0, i))
    idx, gate, rank, cnt = pl.pallas_call(
        _route_kernel,
        grid=(n // t,),
        in_specs=[pl.BlockSpec((e, t), lambda i: (0, i))],
        out_specs=[tok, pl.BlockSpec((GATE_ROWS, t), lambda i: (0, i)), tok, pl.BlockSpec((e, LANES), lambda i: (0, 0))],
        out_shape=[jax.ShapeDtypeStruct((TOP_K, n), I32), jax.ShapeDtypeStruct((GATE_ROWS, n), F32),
                   jax.ShapeDtypeStruct((TOP_K, n), I32), jax.ShapeDtypeStruct((e, LANES), F32)],
        scratch_shapes=[pltpu.VMEM((e, LANES), F32)],
        compiler_params=_cparams("arbitrary"),
        name="moe_route",
    )(lg_t)
    counts = cnt[:, 0].astype(I32)
    padded = (counts + MOE_ROWS - 1) // MOE_ROWS * MOE_ROWS
    pad_end = jnp.cumsum(padded)
    pad_start = pad_end - padded
    m = n * TOP_K
    nb = (m + N_EXPERTS * (MOE_ROWS - 1) + MOE_ROWS - 1) // MOE_ROWS
    block_start = jnp.arange(nb, dtype=I32) * MOE_ROWS
    block_expert = jnp.minimum(jnp.sum((pad_end[None, :] <= block_start[:, None]).astype(I32), axis=1), N_EXPERTS - 1)
    experts = jnp.arange(N_EXPERTS, dtype=I32)
    in_block = block_start[:, None] - pad_start[None, :]
    mine = block_expert[:, None] == experts[None, :]
    filled = jnp.sum(jnp.where(mine, jnp.clip(counts[None, :] - in_block, 0, MOE_ROWS), 0), axis=1)
    owns = jnp.any(block_expert[:, None] == experts[None, :], axis=0)
    later = owns[None, :] & (experts[None, :] > block_expert[:, None])
    nxt = jnp.min(jnp.where(later, experts[None, :], N_EXPERTS), axis=1)
    nxt = jnp.where(nxt == N_EXPERTS, -1, nxt)
    run = jnp.sum((owns[None, :] & (experts[None, :] < block_expert[:, None])).astype(I32), axis=1)
    block_info = jnp.concatenate([block_expert, nxt, run % 2, filled]).astype(I32)
    start_of = sum(jnp.where(idx == e, pad_start[e], 0) for e in range(N_EXPERTS))
    dest = start_of + rank
    return gate, dest, block_info, nb * MOE_ROWS


def _windows(dest, w, multiple, fill):
    kk, n = dest.shape
    nwin = n // w
    per = -(-nwin // SC_WORKERS)
    per = -(-per // multiple) * multiple
    idx = dest.reshape(kk, nwin, w).transpose(1, 0, 2)
    pad = jnp.broadcast_to(fill[None, None, :], (SC_WORKERS * per - nwin, kk, w)).astype(I32)
    return jnp.concatenate([idx, pad], axis=0)


def _sc_scatter_rows(x, idx, p_rows):
    n, c = x.shape
    nwin, kk, w = idx.shape
    per = nwin // SC_WORKERS
    nreal = n // w
    mesh = plsc.VectorSubcoreMesh(core_axis_name="c", subcore_axis_name="s")

    @functools.partial(
        pl.kernel, mesh=mesh,
        out_type=jax.ShapeDtypeStruct((p_rows, c), x.dtype),
        scratch_types=[pltpu.VMEM((per, kk, w), I32), pltpu.VMEM((2, w, c), x.dtype),
                       pltpu.SemaphoreType.DMA((2,)), pltpu.SemaphoreType.DMA((2,))],
        name="sc_scatter_rows",
    )
    def k(x_hbm, idx_hbm, out_hbm, idx_v, rows_v, lsem, ssem):
        wid = lax.axis_index("s") * 2 + lax.axis_index("c")
        g0 = wid * per
        pltpu.sync_copy(idx_hbm.at[pl.ds(g0, per)], idx_v)

        def load(j, b):
            g = jnp.minimum(g0 + j, nreal - 1)
            return pltpu.make_async_copy(x_hbm.at[pl.ds(g * w, w)], rows_v.at[b], lsem.at[b])

        def scat(j, b, q):
            return pltpu.make_async_copy(rows_v.at[b], out_hbm.at[idx_v.at[j, q]], ssem.at[b])

        @pl.loop(0, per + 2, step=2)
        def _(j0):
            for b in range(2):
                j = j0 + b

                @pl.when(jnp.logical_and(j >= 2, j < per + 2))
                def _():
                    for q in range(kk):
                        scat(j - 2, b, q).wait()

                @pl.when(j < per)
                def _():
                    load(j, b).start()

                @pl.when(jnp.logical_and(j >= 1, j < per + 1))
                def _():
                    load(j - 1, 1 - b).wait()
                    for q in range(kk):
                        scat(j - 1, 1 - b, q).start()

    return k(x, idx)


def _sc_gather_rows(y, idx):
    p, c = y.shape
    kk, n = idx.shape
    w = SC_GATHER_WIN
    nb = SC_GATHER_BUFS
    wins = LANES // w
    group = wins * kk
    assert group % nb == 0 and group >= nb and n % LANES == 0
    n_units = n // LANES
    max_units = -(-n_units // SC_WORKERS)
    mesh = plsc.VectorSubcoreMesh(core_axis_name="c", subcore_axis_name="s")

    @functools.partial(
        pl.kernel, mesh=mesh,
        out_type=jax.ShapeDtypeStruct((kk, n, c), y.dtype),
        scratch_types=[pltpu.VMEM((2, kk, LANES), I32), pltpu.VMEM((nb, w, c), y.dtype),
                       pltpu.SemaphoreType.DMA((nb,)), pltpu.SemaphoreType.DMA((nb,))],
        name="sc_gather_rows",
    )
    def k(y_hbm, idx_hbm, out_hbm, idx_v, rows_v, gsem, wsem):
        wid = lax.axis_index("s") * 2 + lax.axis_index("c")
        my_units = (n_units - wid + SC_WORKERS - 1) // SC_WORKERS
        n_items = my_units * group

        def gath(slot, jj, q, b):
            return pltpu.make_async_copy(y_hbm.at[idx_v.at[slot, q, pl.ds(jj * w, w)]], rows_v.at[b], gsem.at[b])

        def wr(i, b):
            row = (wid + (i // group) * SC_WORKERS) * LANES + ((i % group) // kk) * w
            return pltpu.make_async_copy(rows_v.at[b], out_hbm.at[i % kk, pl.ds(row, w)], wsem.at[b])

        @pl.loop(0, max_units + 1)
        def _(gi):
            slot = gi % 2

            @pl.when(gi < my_units)
            def _():
                pltpu.sync_copy(idx_hbm.at[:, pl.ds((wid + gi * SC_WORKERS) * LANES, LANES)], idx_v.at[slot])

            for l in range(group):
                i = gi * group + l
                b = l % nb
                bw = (l - (nb - 1)) % nb

                @pl.when(jnp.logical_and(i >= nb, i < n_items + nb))
                def _():
                    wr(i - nb, b).wait()

                @pl.when(i < n_items)
                def _():
                    gath(slot, l // kk, l % kk, b).start()

                @pl.when(jnp.logical_and(i >= nb - 1, i < n_items + nb - 1))
                def _():
                    gath(slot, 0, 0, bw).wait()
                    wr(i - (nb - 1), bw).start()

    return k(y, idx)


def _moe_kernel(layer, info_ref, x_ref, w1_hbm, b1_ref, w2_hbm, b2_ref, o_ref,
                w1f_ref, w2f_ref, w1b_ref, w2b_ref, sem):
    i = pl.program_id(0)
    nb = pl.num_programs(0)
    expert = info_ref[i]
    nxt = info_ref[nb + i]
    slot = info_ref[2 * nb + i]
    first = jnp.logical_or(i == 0, expert != info_ref[jnp.maximum(i - 1, 0)])

    def fetch(e, s):
        return (pltpu.make_async_copy(w1_hbm.at[layer, e], w1f_ref.at[s], sem.at[0, s]),
                pltpu.make_async_copy(w2_hbm.at[layer, e], w2f_ref.at[s], sem.at[1, s]))

    @pl.when(i == 0)
    def _():
        for cp in fetch(expert, slot):
            cp.start()

    @pl.when(first)
    def _():
        for cp in fetch(expert, slot):
            cp.wait()

        @pl.when(nxt >= 0)
        def _():
            for cp in fetch(nxt, 1 - slot):
                cp.start()

        w1b_ref[...] = w1f_ref[slot].astype(BF16)
        w2b_ref[...] = w2f_ref[slot].astype(BF16)

    units = (info_ref[3 * nb + i] + MOE_UNIT - 1) // MOE_UNIT
    for u in range(1, x_ref.shape[0] // MOE_UNIT + 1):
        @pl.when(units == u)
        def _():
            rows = slice(0, u * MOE_UNIT)
            _expert_block(x_ref.at[rows], w1b_ref, b1_ref, w2b_ref, b2_ref, o_ref.at[rows])


def _expert_block(x_ref, w1b_ref, b1_ref, w2b_ref, b2_ref, o_ref):
    lo, hi = _unpack_pairs(x_ref[...])
    x = jnp.concatenate([lo.astype(BF16), hi.astype(BF16)], axis=-1)
    f = w2b_ref.shape[0]
    fc = 512
    acc = None
    for c in range(f // fc):
        gt = jnp.dot(x, w1b_ref[:, c * fc:(c + 1) * fc], preferred_element_type=F32) + b1_ref[0, 0, :, c * fc:(c + 1) * fc]
        up = jnp.dot(x, w1b_ref[:, f + c * fc:f + (c + 1) * fc], preferred_element_type=F32) + b1_ref[0, 0, :, f + c * fc:f + (c + 1) * fc]
        gt = jnp.minimum(gt, SWIGLU_LIMIT)
        up = jnp.clip(up, -SWIGLU_LIMIT, SWIGLU_LIMIT)
        act = (gt * jax.nn.sigmoid(SWIGLU_ALPHA * gt) * (up + 1.0)).astype(BF16)
        part = jnp.dot(act, w2b_ref[c * fc:(c + 1) * fc, :], preferred_element_type=F32)
        acc = part if acc is None else acc + part
    o_ref[...] = _pack_pairs(acc + b2_ref[0, 0])


def _moe_experts(layer, block_expert, xs, p_rows, w1, b1, w2, b2):
    _, e, d, f2 = w1.shape
    f = f2 // 2
    nb = p_rows // MOE_ROWS
    grid_spec = pltpu.PrefetchScalarGridSpec(
        num_scalar_prefetch=1,
        grid=(nb,),
        in_specs=[pl.BlockSpec((MOE_ROWS, d // 2), lambda i, be: (i, 0)),
                  pl.BlockSpec(memory_space=pl.ANY),
                  pl.BlockSpec((1, 1, 1, f2), lambda i, be: (layer, be[i], 0, 0)),
                  pl.BlockSpec(memory_space=pl.ANY),
                  pl.BlockSpec((1, 1, 1, d), lambda i, be: (layer, be[i], 0, 0))],
        out_specs=pl.BlockSpec((MOE_ROWS, d // 2), lambda i, be: (i, 0)),
        scratch_shapes=[pltpu.VMEM((2, d, f2), F32), pltpu.VMEM((2, f, d), F32),
                        pltpu.VMEM((d, f2), BF16), pltpu.VMEM((f, d), BF16),
                        pltpu.SemaphoreType.DMA((2, 2))],
    )
    return pl.pallas_call(
        functools.partial(_moe_kernel, layer),
        grid_spec=grid_spec,
        out_shape=jax.ShapeDtypeStruct((p_rows, d // 2), U32),
        compiler_params=_cparams("arbitrary"),
        name="moe_experts",
    )(block_expert, xs, w1, b1.reshape(b1.shape[0], e, 1, f2), w2, b2.reshape(b2.shape[0], e, 1, d))


def _moe_dispatch(h2_words, lg_t):
    gate, dest, block_info, p_rows = _route(lg_t)
    spare = p_rows + jnp.arange(SC_SCATTER_WIN, dtype=I32)
    xs = _sc_scatter_rows(h2_words, _windows(dest, SC_SCATTER_WIN, 2, spare), p_rows + SC_SCATTER_WIN)
    return xs, gate, dest, block_info, p_rows


def _moe_compute(layer, dispatched, w1, b1, w2, b2):
    xs, gate, back, block_info, p_rows = dispatched
    ys = _moe_experts(layer, block_info, xs, p_rows, w1, b1, w2, b2)
    return _sc_gather_rows(ys, back), gate


def _final_kernel(x_ref, yg_ref, gate_ref, g2_ref, nw_ref, *rest):
    o_ref = rest[-1]
    x = x_ref[0] + g2_ref[0] * _combine(yg_ref, gate_ref)
    ms = jnp.mean(x * x, axis=-1, keepdims=True)
    o_ref[0] = x * lax.rsqrt(ms + EPS) * nw_ref[...]


def _final(x, res, nw, ts, out_buf, ob0, b_total):
    b, s, d = x.shape
    nj = s // ts
    yg, gate, g2, first = res
    blk0 = first // ts
    row = pl.BlockSpec((1, ts, d), lambda i, j: (i, j, 0))
    if gate.ndim == 2:
        gate_spec = pl.BlockSpec((GATE_ROWS, ts), lambda i, j: (0, blk0 + i * nj + j))
    else:
        gate_spec = pl.BlockSpec((1, ts, TOP_K), lambda i, j: (i, j, 0))
    in_specs = [row, pl.BlockSpec((TOP_K, ts, d // 2), lambda i, j: (0, blk0 + i * nj + j, 0)),
                gate_spec,
                pl.BlockSpec((1, 1, d), lambda i, j: (i, 0, 0)),
                pl.BlockSpec((1, d), lambda i, j: (0, 0))]
    args = [x, yg, gate, g2, nw.reshape(1, d)]
    aliases = {}
    if out_buf is not None:
        in_specs.append(pl.BlockSpec(memory_space=pl.ANY))
        args.append(out_buf)
        aliases = {len(args) - 1: 0}
    return pl.pallas_call(
        _final_kernel,
        grid=(b, nj),
        in_specs=in_specs,
        out_specs=pl.BlockSpec((1, ts, d), lambda i, j: (i + ob0, j, 0)),
        out_shape=jax.ShapeDtypeStruct((b_total, s, d), F32),
        input_output_aliases=aliases,
        compiler_params=_cparams("parallel", "parallel"),
        name="final_norm",
    )(*args)


def kernel(x_prompt, x_sample, c_prompt, c_sample, cache_k, cache_v, state_hgrn, norm_mix, norm_ffn, norm_final, w_ada, b_ada, w_in, rel_bias, hgrn_lb_logits, g_attn_out, g_hgrn_out, w_out, w_router, b_router, w_e1, b_e1, w_e2, b_e2):
    depth = w_in.shape[0]
    bp, sp, d = x_prompt.shape
    bs, ls, _ = x_sample.shape
    wc = cache_k.shape[2]
    n_p, n_s = bp * sp, bs * ls

    lbs = jax.nn.softmax(hgrn_lb_logits.astype(F32), axis=0)
    lbs = jnp.cumsum(lbs, axis=0) - lbs[0]

    mods = _modulation(jnp.concatenate([c_prompt, c_sample], axis=0), w_ada, b_ada)
    lk_s = -(-(wc + ls) // LANES) * LANES
    assert sp % LEFT == 0, "prompt length must be whole 512-row tiles"
    tail = LEFT
    rec_chunk = REC_CHUNK

    n_groups = PROMPT_GROUPS if bp % PROMPT_GROUPS == 0 else 1
    gb = bp // n_groups
    n_g = gb * sp
    x_g = [(x_prompt, g * gb) for g in range(n_groups)]
    x_s = x_sample
    res_g = [None] * n_groups
    res_s = None
    kv_bufs = [None]
    sp_l, ks_l, vs_l, ss_l = [], [], [], []
    for layer in range(depth):
        m6 = mods[layer].reshape(bp + bs, 1, N_MOD, d)
        mod_s = [m6[bp:, :, n, :] for n in range(N_MOD)]
        w_in_bf = w_in[layer].astype(BF16)
        w_out_bf = w_out[layer].astype(BF16)
        wr_hi, wr_mid, _ = _split3(w_router[layer].T)
        wr2 = jnp.concatenate([wr_hi, wr_mid], axis=0)
        table = rel_bias[layer]
        bias_p = _prompt_bias(table)

        def mixer_prompt(g):
            mod = [m6[g * gb:(g + 1) * gb, :, n, :] for n in range(N_MOD)]
            x_new, h2, lg, kbuf, vbuf, s_fin = _mixer_prompt(
                x_g[g][0], x_g[g][1], res_g[g], norm_mix[layer], mod[1], mod[0], w_in_bf, lbs[layer], bias_p,
                g_attn_out[layer], g_hgrn_out[layer], w_out_bf, mod[2], norm_ffn[layer], mod[4], mod[3],
                wr2, b_router[layer], rec_chunk, n_s if g == n_groups - 1 else 0,
                (layer, g * gb), (depth, bp, ATT_W, tail), kv_bufs[0])
            x_g[g] = (x_new, 0)
            kv_bufs[0] = (kbuf, vbuf)
            return h2, lg, s_fin, mod[5]

        def mixer_sample(rows):
            s0 = jnp.swapaxes(state_hgrn[layer].astype(F32), -1, -2)
            x_new, rows, lg, kt, vt, s_new = _mixer_sample(
                x_s, res_s, norm_mix[layer], mod_s[1], mod_s[0], w_in_bf, lbs[layer],
                cache_k[layer].reshape(bs, wc, ATT_W), cache_v[layer].reshape(bs, wc, ATT_W),
                _sample_bias(table, ls, wc, lk_s), s0, g_attn_out[layer], g_hgrn_out[layer], w_out_bf,
                mod_s[2], norm_ffn[layer], mod_s[4], mod_s[3], wr2, b_router[layer], rows, n_g)
            return x_new, rows, jnp.swapaxes(lg, 0, 1).reshape(N_EXPERTS, n_s), kt, vt, s_new

        dispatched, g2s, s_t = [], [], []
        for g in range(n_groups):
            h2, lg, s_fin, g2 = mixer_prompt(g)
            if g == n_groups - 1:
                x_s, h2, lg_s, ka_s, va_s, s_new = mixer_sample(h2)
                ks_l.append(ka_s.reshape(bs, ls, ATT_HEADS, ATT_DIM))
                vs_l.append(va_s.reshape(bs, ls, ATT_HEADS, ATT_DIM))
                ss_l.append(jnp.swapaxes(s_new, -1, -2))
                lg = jnp.concatenate([lg, lg_s], axis=1)
            dispatched.append(_moe_dispatch(h2, lg))
            g2s.append(g2)
            s_t.append(s_fin)
        sp_l.append(jnp.swapaxes(jnp.concatenate(s_t, axis=0), -1, -2))

        for g in range(n_groups):
            yg, gate = _moe_compute(layer, dispatched[g], w_e1, b_e1, w_e2, b_e2)
            res_g[g] = (yg, gate, g2s[g], 0)
            if g == n_groups - 1:
                res_s = (yg, gate[:TOP_K, n_g:].T.reshape(bs, ls, TOP_K), mod_s[5], n_g)

    y_prompt = None
    for g in range(n_groups):
        y_prompt = _final(x_g[g][0], res_g[g], norm_final, SEQ_TILE, y_prompt, g * gb, bp)
    y_sample = _final(x_s, res_s, norm_final, ls, None, 0, bs)
    new_k, new_v = [jnp.transpose(buf.reshape(depth, bp, ATT_HEADS, ATT_DIM, tail), (0, 1, 4, 2, 3))
                    for buf in kv_bufs[0]]
    return (y_prompt, y_sample, new_k, new_v, jnp.stack(sp_l),
            jnp.stack(ks_l), jnp.stack(vs_l), jnp.stack(ss_l))
```

```python
import functools

import jax
import jax.numpy as jnp
import numpy as np
from jax import lax
from jax.experimental import pallas as pl
from jax.experimental.pallas import tpu as pltpu
from jax.experimental.pallas import tpu_sc as plsc

F32 = jnp.float32
BF16 = jnp.bfloat16
I32 = jnp.int32
U32 = jnp.uint32

EPS = 1e-6
CHUNK = 64
LEFT_CHUNKS = 8
LEFT = LEFT_CHUNKS * CHUNK
MAX_REL = 2 * CHUNK
ATT_HEADS = 8
ATT_DIM = 64
ATT_W = ATT_HEADS * ATT_DIM
REC_HEADS = 4
REC_DIM = 128
REC_W = REC_HEADS * REC_DIM
N_EXPERTS = 32
TOP_K = 4
SWIGLU_ALPHA = 1.702
SWIGLU_LIMIT = 7.0
N_MOD = 6
NEG = -1e30
LOG2E = 1.4426950408889634

LANES = 128
ATT_QSUB = 128
ATT_WIN = LEFT + ATT_QSUB
SEQ_TILE = 512
REC_CHUNK = 128
MOE_ROWS = 1024
MOE_UNIT = 128
ROUTE_TILE = 256
ROUTE_MAX_SUB = 4
PROMPT_GROUPS = 2
GATE_ROWS = 8
VMEM_LIMIT = 56 * 1024 * 1024
SC_WORKERS = 32
SC_SCATTER_WIN = 64
SC_GATHER_WIN = 32
SC_GATHER_BUFS = 4


def _cparams(*sem):
    return pltpu.CompilerParams(dimension_semantics=sem, vmem_limit_bytes=VMEM_LIMIT)


def _split3(x):
    hi = x.astype(BF16)
    r1 = x - hi.astype(F32)
    mid = r1.astype(BF16)
    lo = (r1 - mid.astype(F32)).astype(BF16)
    return hi, mid, lo


def _bits(x):
    return lax.bitcast_convert_type(x, U32)


def _pack_pairs(x):
    r = _bits(x.astype(BF16).astype(F32))
    half = x.shape[1] // 2
    return jnp.bitwise_or(jnp.bitwise_and(r[:, half:], jnp.uint32(0xFFFF0000)),
                          jnp.right_shift(r[:, :half], jnp.uint32(16)))


def _unpack_pairs(word):
    lo = lax.bitcast_convert_type(jnp.left_shift(word, jnp.uint32(16)), F32)
    hi = lax.bitcast_convert_type(jnp.bitwise_and(word, jnp.uint32(0xFFFF0000)), F32)
    return lo, hi


def _mod_kernel(c_ref, w_ref, b_ref, o_ref):
    c = c_ref[...]
    act = (c * jax.nn.sigmoid(c)).astype(BF16)
    o_ref[0] = jnp.dot(act, w_ref[0].astype(BF16), preferred_element_type=F32) + b_ref[0]


def _modulation(c_all, w_ada, b_ada):
    depth, d, n6 = w_ada.shape
    rows = c_all.shape[0]
    tn = 1536
    return pl.pallas_call(
        _mod_kernel,
        grid=(depth, n6 // tn),
        in_specs=[pl.BlockSpec((rows, d), lambda l, j: (0, 0)),
                  pl.BlockSpec((1, d, tn), lambda l, j: (l, 0, j)),
                  pl.BlockSpec((1, 1, tn), lambda l, j: (l, 0, j))],
        out_specs=pl.BlockSpec((1, rows, tn), lambda l, j: (l, 0, j)),
        out_shape=jax.ShapeDtypeStruct((depth, rows, n6), F32),
        compiler_params=_cparams("parallel", "parallel"),
        name="adaln_modulation",
    )(c_all, w_ada, b_ada.reshape(depth, 1, n6))


def _combine(yg_ref, gate_ref):
    g = gate_ref[0] if len(gate_ref.shape) == 3 else gate_ref[...].T
    lo = hi = None
    for k in range(TOP_K):
        l, h = _unpack_pairs(yg_ref[k])
        gk = g[:, k:k + 1]
        lo = l * gk if lo is None else lo + l * gk
        hi = h * gk if hi is None else hi + h * gk
    return jnp.concatenate([lo, hi], axis=-1)


def _project_rows(x, nw_ref, sc_ref, sh_ref, w_ref, lb_ref,
                  qa_ref, ka_ref, va_ref, kt_ref, vt_ref, qh_ref, kh_ref, lf_ref, vh_ref, gh_ref):
    ms = jnp.mean(x * x, axis=-1, keepdims=True)
    h = (x * lax.rsqrt(ms + EPS) * nw_ref[...]) * (1.0 + sc_ref[0]) + sh_ref[0]
    hb = h.astype(BF16)

    def proj(g):
        return jnp.dot(hb, w_ref[:, g * ATT_W:(g + 1) * ATT_W], preferred_element_type=F32)

    qa_ref[...] = (proj(0) * (ATT_DIM ** -0.5)).astype(BF16)
    k = proj(1)
    ka_ref[...] = k.astype(BF16)
    kt_ref[...] = k
    v = proj(2)
    va_ref[...] = v.astype(BF16)
    vt_ref[...] = v
    qh = proj(3)
    qh_ref[...] = qh * jax.nn.sigmoid(qh)
    f = proj(4)
    lb = lb_ref[...]
    lf_ref[...] = jnp.log(lb + (1.0 - lb) * jax.nn.sigmoid(f))
    kh_ref[...] = (1.0 - lb) * jax.nn.sigmoid(-f)
    vh_ref[...] = proj(5).astype(BF16)
    gh_ref[...] = proj(6)


def _attend(q, kw, vw, bias_ref, s_ref, col_thr):
    lq, lk = q.shape[0], kw.shape[0]
    lane = lax.broadcasted_iota(I32, (lq, LANES), 1)
    lo = lane < ATT_DIM
    nt = (((1,), (1,)), ((), ()))
    if col_thr is not None:
        keep = lax.broadcasted_iota(I32, (2 * lq, lk), 1) >= col_thr
    ms = []
    for p in range(ATT_HEADS // 2):
        sl = slice(p * LANES, (p + 1) * LANES)
        qp = q[:, sl]
        z = jnp.zeros_like(qp)
        lhs = jnp.concatenate([jnp.where(lo, qp, z), jnp.where(lo, z, qp)], axis=0)
        s = lax.dot_general(lhs, kw[:, sl], nt, preferred_element_type=F32) + bias_ref[p]
        if col_thr is not None:
            s = jnp.where(keep, s, NEG)
        ms.append(jnp.max(s, axis=-1, keepdims=True))
        s_ref[p] = s
    outs = []
    for p in range(ATT_HEADS // 2):
        sl = slice(p * LANES, (p + 1) * LANES)
        e = jnp.exp(s_ref[p] - ms[p])
        l = jnp.sum(e, axis=-1, keepdims=True)
        o = jnp.dot(e.astype(BF16), vw[:, sl], preferred_element_type=F32) / l
        outs.append(jnp.where(lo, o[:lq], o[lq:]))
    return jnp.concatenate(outs, axis=-1)


def _band_attention(q_ref, kw_ref, vw_ref, bias_ref, s_ref, o_ref, first_tile):
    ts = q_ref.shape[0]

    def body(masked, r, carry):
        r0 = pl.multiple_of(r * ATT_QSUB, ATT_QSUB)
        q = q_ref[pl.ds(r0, ATT_QSUB), :]
        kw = kw_ref[pl.ds(r0, ATT_WIN), :]
        vw = vw_ref[pl.ds(r0, ATT_WIN), :]
        o_ref[pl.ds(r0, ATT_QSUB), :] = _attend(q, kw, vw, bias_ref, s_ref, (ts - r0) if masked else None)
        return carry

    @pl.when(first_tile)
    def _():
        lax.fori_loop(0, ts // ATT_QSUB, functools.partial(body, True), 0)

    @pl.when(jnp.logical_not(first_tile))
    def _():
        lax.fori_loop(0, ts // ATT_QSUB, functools.partial(body, False), 0)


def _toeplitz_bias(table, lq, lk, offset, valid):
    m = lq + lk - 1
    k = np.arange(m)
    diff = np.where(k < lk, k, k - m)
    g = table[:, np.clip(offset - diff, -MAX_REL, MAX_REL) + MAX_REL]
    skew = jnp.tile(g, (1, lq))[:, :lq * (m - 1)].reshape(-1, lq, m - 1)[:, :, :lk]
    bias = jnp.where(jnp.asarray(valid)[None], skew, NEG).astype(F32)
    return bias.reshape(ATT_HEADS // 2, 2 * lq, lk)


def _prompt_bias(table):
    t = np.arange(ATT_QSUB)[:, None]
    j = np.arange(ATT_WIN)[None, :]
    start = (t // CHUNK) * CHUNK
    return _toeplitz_bias(table, ATT_QSUB, ATT_WIN, LEFT, (j >= start) & (j < start + LEFT + CHUNK))


def _sample_bias(table, l, w, lk):
    valid = np.broadcast_to(np.arange(lk)[None, :] < w + l, (l, lk))
    return _toeplitz_bias(table, l, lk, w, valid)


def _neg_abs(x):
    return lax.bitcast_convert_type(jnp.bitwise_or(_bits(x), jnp.uint32(0x80000000)), F32)


def _hgrn_rows(chunk, q_ref, k_ref, lf_ref, v_ref, o_ref, st_ref):
    ts = q_ref.shape[0]
    n_chunks = ts // chunk

    row = lax.broadcasted_iota(I32, (chunk, chunk), 0)
    colm = lax.broadcasted_iota(I32, (chunk, chunk), 1)
    tri = (colm <= row).astype(BF16)
    xor = jnp.bitwise_xor(row, colm)
    lower = colm < row
    rid = lax.broadcasted_iota(I32, (chunk, REC_W), 0)

    levels = []
    b = chunk // 2
    while b >= 1:
        levels.append(b)
        b //= 2
    pair_mask = [xor == 0] + [lower & (xor >= b) & (xor < 2 * b) for b in levels]
    upper = [jnp.bitwise_and(rid, b) != 0 for b in levels]
    ph4 = jnp.bitwise_and(rid, 3)
    odd = jnp.bitwise_and(rid, 1) == 1
    nt = (((1,), (1,)), ((), ()))
    tn = (((0,), (0,)), ((), ()))
    sls = [slice(h * REC_DIM, (h + 1) * REC_DIM) for h in range(REC_HEADS)]

    def boundary(cum, b):
        if 2 * b >= 8:
            pieces = [jnp.broadcast_to(cum[m * 2 * b + b - 1:m * 2 * b + b, :], (2 * b, REC_W))
                      for m in range(chunk // (2 * b))]
            return pieces[0] if len(pieces) == 1 else jnp.concatenate(pieces, axis=0)
        if b == 2:
            up1 = pltpu.roll(cum, chunk - 1, 0)
            dn1 = pltpu.roll(cum, 1, 0)
            dn2 = pltpu.roll(cum, 2, 0)
            return jnp.where(ph4 == 0, up1, jnp.where(ph4 == 1, cum, jnp.where(ph4 == 2, dn1, dn2)))
        return jnp.where(odd, pltpu.roll(cum, 1, 0), cum)

    def one_chunk(c, carry):
        r0 = pl.multiple_of(c * chunk, chunk)
        rows = pl.ds(r0, chunk)
        q = q_ref[rows, :]
        k = k_ref[rows, :]
        v = v_ref[rows, :]
        lf = lf_ref[rows, :]
        cum = sum(jnp.dot(tri, part, preferred_element_type=F32) for part in _split3(lf)) * LOG2E
        last = cum[chunk - 1:chunk, :]
        q_in = (q * jnp.exp2(cum)).astype(BF16)
        k_out = (k * jnp.exp2(last - cum)).astype(BF16)
        decay = jnp.exp2(last)
        qb = q.astype(BF16)
        kb = k.astype(BF16)
        zs = []
        for n, b in enumerate(levels):
            w = jnp.exp2(_neg_abs(cum - boundary(cum, b)))
            zs.append((jnp.where(upper[n], q, k) * w).astype(BF16))
        sts = [st_ref[h] for h in range(REC_HEADS)]
        d0 = [lax.dot_general(qb[:, sl], kb[:, sl], nt, preferred_element_type=F32) for sl in sls]
        dl = [[lax.dot_general(z[:, sl], z[:, sl], nt, preferred_element_type=F32) for z in zs] for sl in sls]
        oi = [lax.dot_general(q_in[:, sl], st.astype(BF16), nt, preferred_element_type=F32)
              for sl, st in zip(sls, sts)]
        upd = [lax.dot_general(v[:, sl], k_out[:, sl], tn, preferred_element_type=F32) for sl in sls]
        outs = []
        for h, sl in enumerate(sls):
            a = jnp.where(pair_mask[0], d0[h], 0.0)
            for n in range(len(levels)):
                a = jnp.where(pair_mask[n + 1], dl[h][n], a)
            outs.append(jnp.dot(a.astype(BF16), v[:, sl], preferred_element_type=F32) + oi[h])
        o_ref[rows, :] = jnp.concatenate(outs, axis=-1)
        for h, sl in enumerate(sls):
            st_ref[h] = decay[:, sl] * sts[h] + upd[h]
        return carry

    lax.fori_loop(0, n_chunks, one_chunk, 0, unroll=2 if n_chunks % 2 == 0 else 1)


def _head_rms(x, width):
    lane = lax.broadcasted_iota(I32, (x.shape[0], LANES), 1)
    lo = lane < ATT_DIM
    outs = []
    for p in range(x.shape[1] // LANES):
        xp = x[:, p * LANES:(p + 1) * LANES]
        sq = xp * xp
        tot = jnp.sum(sq, axis=-1, keepdims=True)
        if width == LANES:
            scale = lax.rsqrt(tot * (1.0 / LANES) + EPS)
        else:
            s_lo = jnp.sum(jnp.where(lo, sq, 0.0), axis=-1, keepdims=True)
            scale = jnp.where(lo, lax.rsqrt(s_lo * (1.0 / ATT_DIM) + EPS),
                              lax.rsqrt((tot - s_lo) * (1.0 / ATT_DIM) + EPS))
        outs.append(xp * scale)
    return jnp.concatenate(outs, axis=-1)


def _merge_rows(oa, orr, gh, x, ga_ref, gr_ref, wo_ref, g1_ref, nf_ref, sc_ref, sh_ref, wr_ref, br_ref):
    a = _head_rms(oa, ATT_DIM) * ga_ref[...]
    r = _head_rms(orr, REC_DIM) * gr_ref[...] * (gh * jax.nn.sigmoid(gh))
    mix = jnp.dot(a.astype(BF16), wo_ref[0:ATT_W, :], preferred_element_type=F32)
    mix += jnp.dot(r.astype(BF16), wo_ref[ATT_W:ATT_W + REC_W, :], preferred_element_type=F32)
    x = x + g1_ref[0] * mix
    ms = jnp.mean(x * x, axis=-1, keepdims=True)
    h2 = (x * lax.rsqrt(ms + EPS) * nf_ref[...]) * (1.0 + sc_ref[0]) + sh_ref[0]
    hh = h2.astype(BF16)
    words = _pack_pairs(h2)
    hm = (h2 - hh.astype(F32)).astype(BF16)
    nt = (((1,), (1,)), ((), ()))
    l1 = lax.dot_general(wr_ref[...], hh, nt, preferred_element_type=F32)
    l2 = lax.dot_general(wr_ref[0:N_EXPERTS, :], hm, nt, preferred_element_type=F32)
    lg = l1[0:N_EXPERTS] + l1[N_EXPERTS:2 * N_EXPERTS] + l2 + br_ref[...]
    return x, words, lg


def _mixer_kernel(has_res, chunk, *refs):
    n_in = 4 if has_res else 1
    x_ref = refs[0]
    (nw_ref, sc1_ref, sh1_ref, lb_ref, ga_ref, gr_ref, g1_ref, nf_ref, sc2_ref, sh2_ref, wr_ref, br_ref,
     win_hbm, wout_hbm, bias_hbm,
     xo_ref, h2_ref, lg_ref, kt_ref, vt_ref, sf_ref,
     win_v, wout_v, bias_v, q_s, kw_s, vw_s, s_s, qh_s, kh_s, lf_s, vh_s, gh_s, oa_s, or_s, st_s, sem) = refs[n_in:]
    j = pl.program_id(1)
    ts = x_ref.shape[1]

    @pl.when(j == 0)
    def _():
        copies = [pltpu.make_async_copy(src, dst, sem.at[n])
                  for n, (src, dst) in enumerate(((win_hbm, win_v), (wout_hbm, wout_v), (bias_hbm, bias_v)))]
        for cp in copies:
            cp.start()
        for cp in copies:
            cp.wait()
        st_s[...] = jnp.zeros_like(st_s)
        kw_s[0:ts, :] = jnp.zeros((ts, ATT_W), BF16)
        vw_s[0:ts, :] = jnp.zeros((ts, ATT_W), BF16)

    if has_res:
        yg_ref, gate_ref, g2_ref = refs[1:4]
        x = x_ref[0] + g2_ref[0] * _combine(yg_ref, gate_ref)
        xo_ref[0] = x
    else:
        x = x_ref[0]
    cur = pl.ds(ts, ts)
    _project_rows(x, nw_ref, sc1_ref, sh1_ref, win_v, lb_ref, q_s, kw_s.at[cur], vw_s.at[cur],
                  kt_ref.at[0], vt_ref.at[0], qh_s, kh_s, lf_s, vh_s, gh_s)
    _band_attention(q_s, kw_s, vw_s, bias_v, s_s, oa_s, j == 0)
    _hgrn_rows(chunk, qh_s, kh_s, lf_s, vh_s, or_s, st_s)
    x_in = xo_ref[0] if has_res else x_ref[0]
    x_new, words, lg = _merge_rows(oa_s[...], or_s[...], gh_s[...], x_in, ga_ref, gr_ref, wout_v, g1_ref,
                                   nf_ref, sc2_ref, sh2_ref, wr_ref, br_ref)
    xo_ref[0] = x_new
    h2_ref[...] = words
    lg_ref[...] = lg
    kw_s[0:ts, :] = kw_s[ts:2 * ts, :]
    vw_s[0:ts, :] = vw_s[ts:2 * ts, :]

    @pl.when(j == pl.num_programs(1) - 1)
    def _():
        sf_ref[0] = st_s[...]


def _mixer_prompt(x, xb0, res, nw, sc1, sh1, w_in_bf, lb, bias, g_att, g_rec, w_out_bf, g1, nf, sc2, sh2,
                  wr2, br, tail, chunk, extra_rows):
    _, s, d = x.shape
    b = sc1.shape[0]
    ts = LEFT
    nj = s // ts
    n_tail = tail // ts
    has_res = res is not None
    row = pl.BlockSpec((1, ts, d), lambda i, j: (i, j, 0))
    per_b = pl.BlockSpec((1, 1, d), lambda i, j: (i, 0, 0))
    const = lambda shp: pl.BlockSpec(shp, lambda i, j: (0,) * len(shp))
    hbm = pl.BlockSpec(memory_space=pl.ANY)
    in_specs, args = [pl.BlockSpec((1, ts, d), lambda i, j: (i + xb0, j, 0))], [x]
    if has_res:
        yg, gate, g2, first = res
        blk0 = first // ts
        in_specs += [pl.BlockSpec((TOP_K, ts, d // 2), lambda i, j: (0, blk0 + i * nj + j, 0)),
                     pl.BlockSpec((GATE_ROWS, ts), lambda i, j: (0, blk0 + i * nj + j)), per_b]
        args += [yg, gate, g2]
    in_specs += [const((1, d)), per_b, per_b, const((1, REC_W)), const((1, ATT_W)), const((1, REC_W)),
                 per_b, const((1, d)), per_b, per_b, const(wr2.shape), const((N_EXPERTS, 1)), hbm, hbm, hbm]
    args += [nw.reshape(1, d), sc1, sh1, lb.reshape(1, REC_W), g_att.reshape(1, ATT_W), g_rec.reshape(1, REC_W),
             g1, nf.reshape(1, d), sc2, sh2, wr2, br.reshape(N_EXPERTS, 1), w_in_bf, w_out_bf, bias]
    tail_spec = pl.BlockSpec((1, ts, ATT_W), lambda i, j: (i, jnp.maximum(j - (nj - n_tail), 0), 0))
    state_spec = pl.BlockSpec((1, REC_HEADS, REC_DIM, REC_DIM), lambda i, j: (i, 0, 0, 0))
    tile = lambda dt: pltpu.VMEM((ts, ATT_W), dt)
    return pl.pallas_call(
        functools.partial(_mixer_kernel, has_res, chunk),
        grid=(b, nj),
        in_specs=in_specs,
        out_specs=[row, pl.BlockSpec((ts, d // 2), lambda i, j: (i * nj + j, 0)),
                   pl.BlockSpec((N_EXPERTS, ts), lambda i, j: (0, i * nj + j)), tail_spec, tail_spec, state_spec],
        out_shape=[jax.ShapeDtypeStruct((b, s, d), F32), jax.ShapeDtypeStruct((b * s + extra_rows, d // 2), U32),
                   jax.ShapeDtypeStruct((N_EXPERTS, b * s), F32),
                   jax.ShapeDtypeStruct((b, tail, ATT_W), F32), jax.ShapeDtypeStruct((b, tail, ATT_W), F32),
                   jax.ShapeDtypeStruct((b, REC_HEADS, REC_DIM, REC_DIM), F32)],
        scratch_shapes=[pltpu.VMEM(w_in_bf.shape, BF16), pltpu.VMEM(w_out_bf.shape, BF16), pltpu.VMEM(bias.shape, F32),
                        tile(BF16), pltpu.VMEM((2 * ts, ATT_W), BF16), pltpu.VMEM((2 * ts, ATT_W), BF16),
                        pltpu.VMEM((ATT_HEADS // 2, 2 * ATT_QSUB, ATT_WIN), F32),
                        tile(F32), tile(F32), tile(F32), tile(BF16), tile(F32), tile(F32), tile(F32),
                        pltpu.VMEM((REC_HEADS, REC_DIM, REC_DIM), F32), pltpu.SemaphoreType.DMA((3,))],
        compiler_params=_cparams("parallel", "arbitrary"),
        name="prompt_mixer",
    )(*args)


def _sample_mixer_kernel(has_res, *refs):
    n_in = 4 if has_res else 1
    x_ref = refs[0]
    (nw_ref, sc1_ref, sh1_ref, w_ref, lb_ref, ck_ref, cv_ref, bias_ref, s0_ref, ga_ref, gr_ref, wo_ref,
     g1_ref, nf_ref, sc2_ref, sh2_ref, wr_ref, br_ref, rows_hbm,
     xo_ref, h2_ref, lg_ref, kt_ref, vt_ref, sf_ref,
     q_s, kw_s, vw_s, s_s, qh_s, kh_s, lf_s, vh_s, gh_s, or_s, st_s) = refs[n_in:]
    del rows_hbm
    if has_res:
        yg_ref, gate_ref, g2_ref = refs[1:4]
        x = x_ref[0] + g2_ref[0] * _combine(yg_ref, gate_ref)
    else:
        x = x_ref[0]
    wc, l, lk = ck_ref.shape[1], x_ref.shape[1], kw_s.shape[0]
    new = pl.ds(wc, l)
    _project_rows(x, nw_ref, sc1_ref, sh1_ref, w_ref, lb_ref, q_s, kw_s.at[new], vw_s.at[new],
                  kt_ref.at[0], vt_ref.at[0], qh_s, kh_s, lf_s, vh_s, gh_s)
    for dst, cache in ((kw_s, ck_ref), (vw_s, cv_ref)):
        dst[0:wc, :] = cache[0].astype(BF16)
        dst[wc + l:lk, :] = jnp.zeros((lk - wc - l, ATT_W), BF16)
    oa = _attend(q_s[...], kw_s[...], vw_s[...], bias_ref, s_s, None)
    st_s[...] = s0_ref[0]
    _hgrn_rows(l, qh_s, kh_s, lf_s, vh_s, or_s, st_s)
    sf_ref[0] = st_s[...]
    x_new, words, lg = _merge_rows(oa, or_s[...], gh_s[...], x, ga_ref, gr_ref, wo_ref, g1_ref,
                                   nf_ref, sc2_ref, sh2_ref, wr_ref, br_ref)
    xo_ref[0] = x_new
    h2_ref[...] = words
    lg_ref[0] = lg


def _mixer_sample(x, res, nw, sc1, sh1, w_in_bf, lb, cache_k, cache_v, bias, s0_t, g_att, g_rec, w_out_bf,
                  g1, nf, sc2, sh2, wr2, br, rows, row0):
    b, l, d = x.shape
    wc = cache_k.shape[1]
    lk = bias.shape[2]
    has_res = res is not None
    row = pl.BlockSpec((1, l, d), lambda i: (i, 0, 0))
    half = pl.BlockSpec((1, l, ATT_W), lambda i: (i, 0, 0))
    old = pl.BlockSpec((1, wc, ATT_W), lambda i: (i, 0, 0))
    per_b = pl.BlockSpec((1, 1, d), lambda i: (i, 0, 0))
    const = lambda shp: pl.BlockSpec(shp, lambda i: (0,) * len(shp))
    state = pl.BlockSpec((1, REC_HEADS, REC_DIM, REC_DIM), lambda i: (i, 0, 0, 0))
    in_specs, args = [row], [x]
    if has_res:
        yg, gate, g2, first = res
        blk0 = first // l
        in_specs += [pl.BlockSpec((TOP_K, l, d // 2), lambda i: (0, blk0 + i, 0)),
                     pl.BlockSpec((1, l, TOP_K), lambda i: (i, 0, 0)), per_b]
        args += [yg, gate, g2]
    in_specs += [const((1, d)), per_b, per_b, const(w_in_bf.shape), const((1, REC_W)), old, old, const(bias.shape),
                 state, const((1, ATT_W)), const((1, REC_W)), const(w_out_bf.shape), per_b, const((1, d)),
                 per_b, per_b, const(wr2.shape), const((N_EXPERTS, 1)), pl.BlockSpec(memory_space=pl.ANY)]
    args += [nw.reshape(1, d), sc1, sh1, w_in_bf, lb.reshape(1, REC_W), cache_k, cache_v, bias, s0_t,
             g_att.reshape(1, ATT_W), g_rec.reshape(1, REC_W), w_out_bf, g1, nf.reshape(1, d), sc2, sh2, wr2,
             br.reshape(N_EXPERTS, 1), rows]
    tile = lambda dt: pltpu.VMEM((l, ATT_W), dt)
    return pl.pallas_call(
        functools.partial(_sample_mixer_kernel, has_res),
        grid=(b,),
        in_specs=in_specs,
        out_specs=[row, pl.BlockSpec((l, d // 2), lambda i: (row0 // l + i, 0)),
                   pl.BlockSpec((1, N_EXPERTS, l), lambda i: (i, 0, 0)), half, half, state],
        out_shape=[jax.ShapeDtypeStruct((b, l, d), F32), jax.ShapeDtypeStruct(rows.shape, U32),
                   jax.ShapeDtypeStruct((b, N_EXPERTS, l), F32),
                   jax.ShapeDtypeStruct((b, l, ATT_W), F32), jax.ShapeDtypeStruct((b, l, ATT_W), F32),
                   jax.ShapeDtypeStruct((b, REC_HEADS, REC_DIM, REC_DIM), F32)],
        scratch_shapes=[tile(BF16), pltpu.VMEM((lk, ATT_W), BF16), pltpu.VMEM((lk, ATT_W), BF16),
                        pltpu.VMEM((ATT_HEADS // 2, 2 * l, lk), F32),
                        tile(F32), tile(F32), tile(F32), tile(BF16), tile(F32), tile(F32),
                        pltpu.VMEM((REC_HEADS, REC_DIM, REC_DIM), F32)],
        input_output_aliases={len(args) - 1: 1},
        compiler_params=_cparams("parallel"),
        name="sample_mixer",
    )(*args)


def _route_kernel(lg_ref, idx_ref, gate_ref, rank_ref, cnt_ref, run_ref):
    i = pl.program_id(0)
    t = ROUTE_TILE

    @pl.when(i == 0)
    def _():
        run_ref[...] = jnp.zeros_like(run_ref)

    row = lax.broadcasted_iota(I32, (N_EXPERTS, t), 0)
    earlier = (lax.broadcasted_iota(I32, (t, t), 0) < lax.broadcasted_iota(I32, (t, t), 1)).astype(BF16)
    run = run_ref[:, 0:1]
    for sb in range(lg_ref.shape[1] // t):
        cols = slice(sb * t, (sb + 1) * t)
        x = lg_ref[:, cols]
        vals, idxs, hits = [], [], []
        for _ in range(TOP_K):
            m = jnp.max(x, axis=0, keepdims=True)
            ik = jnp.min(jnp.where(x == m, row, N_EXPERTS), axis=0, keepdims=True)
            hit = row == ik
            x = jnp.where(hit, -jnp.inf, x)
            vals.append(m)
            idxs.append(ik)
            hits.append(hit)
        es = [jnp.exp(v - vals[0]) for v in vals]
        tot = es[0] + es[1] + es[2] + es[3]
        gate_ref[:, cols] = jnp.concatenate([e / tot for e in es] + [jnp.zeros((GATE_ROWS - TOP_K, t), F32)], axis=0)
        idx_ref[:, cols] = jnp.concatenate(idxs, axis=0)
        chosen = (hits[0] | hits[1] | hits[2] | hits[3])
        onehot = jnp.where(chosen, 1.0, 0.0)
        before = jnp.dot(onehot.astype(BF16), earlier, preferred_element_type=F32) + run
        rank_ref[:, cols] = jnp.concatenate(
            [jnp.sum(jnp.where(h, before, 0.0), axis=0, keepdims=True) for h in hits], axis=0).astype(I32)
        run = run + jnp.sum(onehot, axis=1, keepdims=True)
    run_ref[...] = jnp.broadcast_to(run, run_ref.shape)
    cnt_ref[...] = jnp.broadcast_to(run, cnt_ref.shape)


def _route(lg_t):
    e, n = lg_t.shape
    t = max(m * ROUTE_TILE for m in range(1, ROUTE_MAX_SUB + 1) if n % (m * ROUTE_TILE) == 0)
    tok = pl.BlockSpec((TOP_K, t), lambda i: (0, i))
    idx, gate, rank, cnt = pl.pallas_call(
        _route_kernel,
        grid=(n // t,),
        in_specs=[pl.BlockSpec((e, t), lambda i: (0, i))],
        out_specs=[tok, pl.BlockSpec((GATE_ROWS, t), lambda i: (0, i)), tok, pl.BlockSpec((e, LANES), lambda i: (0, 0))],
        out_shape=[jax.ShapeDtypeStruct((TOP_K, n), I32), jax.ShapeDtypeStruct((GATE_ROWS, n), F32),
                   jax.ShapeDtypeStruct((TOP_K, n), I32), jax.ShapeDtypeStruct((e, LANES), F32)],
        scratch_shapes=[pltpu.VMEM((e, LANES), F32)],
        compiler_params=_cparams("arbitrary"),
        name="moe_route",
    )(lg_t)
    counts = cnt[:, 0].astype(I32)
    padded = (counts + MOE_ROWS - 1) // MOE_ROWS * MOE_ROWS
    pad_end = jnp.cumsum(padded)
    pad_start = pad_end - padded
    m = n * TOP_K
    nb = (m + N_EXPERTS * (MOE_ROWS - 1) + MOE_ROWS - 1) // MOE_ROWS
    block_start = jnp.arange(nb, dtype=I32) * MOE_ROWS
    block_expert = jnp.minimum(jnp.sum((pad_end[None, :] <= block_start[:, None]).astype(I32), axis=1), N_EXPERTS - 1)
    experts = jnp.arange(N_EXPERTS, dtype=I32)
    in_block = block_start[:, None] - pad_start[None, :]
    mine = block_expert[:, None] == experts[None, :]
    filled = jnp.sum(jnp.where(mine, jnp.clip(counts[None, :] - in_block, 0, MOE_ROWS), 0), axis=1)
    owns = jnp.any(block_expert[:, None] == experts[None, :], axis=0)
    later = owns[None, :] & (experts[None, :] > block_expert[:, None])
    nxt = jnp.min(jnp.where(later, experts[None, :], N_EXPERTS), axis=1)
    nxt = jnp.where(nxt == N_EXPERTS, -1, nxt)
    run = jnp.sum((owns[None, :] & (experts[None, :] < block_expert[:, None])).astype(I32), axis=1)
    block_info = jnp.concatenate([block_expert, nxt, run % 2, filled]).astype(I32)
    start_of = sum(jnp.where(idx == e, pad_start[e], 0) for e in range(N_EXPERTS))
    dest = start_of + rank
    return gate, dest, block_info, nb * MOE_ROWS


def _windows(dest, w, multiple, fill):
    kk, n = dest.shape
    nwin = n // w
    per = -(-nwin // SC_WORKERS)
    per = -(-per // multiple) * multiple
    idx = dest.reshape(kk, nwin, w).transpose(1, 0, 2)
    pad = jnp.broadcast_to(fill[None, None, :], (SC_WORKERS * per - nwin, kk, w)).astype(I32)
    return jnp.concatenate([idx, pad], axis=0)


def _sc_scatter_rows(x, idx, p_rows):
    n, c = x.shape
    nwin, kk, w = idx.shape
    per = nwin // SC_WORKERS
    nreal = n // w
    mesh = plsc.VectorSubcoreMesh(core_axis_name="c", subcore_axis_name="s")

    @functools.partial(
        pl.kernel, mesh=mesh,
        out_type=jax.ShapeDtypeStruct((p_rows, c), x.dtype),
        scratch_types=[pltpu.VMEM((per, kk, w), I32), pltpu.VMEM((2, w, c), x.dtype),
                       pltpu.SemaphoreType.DMA((2,)), pltpu.SemaphoreType.DMA((2,))],
        name="sc_scatter_rows",
    )
    def k(x_hbm, idx_hbm, out_hbm, idx_v, rows_v, lsem, ssem):
        wid = lax.axis_index("s") * 2 + lax.axis_index("c")
        g0 = wid * per
        pltpu.sync_copy(idx_hbm.at[pl.ds(g0, per)], idx_v)

        def load(j, b):
            g = jnp.minimum(g0 + j, nreal - 1)
            return pltpu.make_async_copy(x_hbm.at[pl.ds(g * w, w)], rows_v.at[b], lsem.at[b])

        def scat(j, b, q):
            return pltpu.make_async_copy(rows_v.at[b], out_hbm.at[idx_v.at[j, q]], ssem.at[b])

        @pl.loop(0, per + 2, step=2)
        def _(j0):
            for b in range(2):
                j = j0 + b

                @pl.when(jnp.logical_and(j >= 2, j < per + 2))
                def _():
                    for q in range(kk):
                        scat(j - 2, b, q).wait()

                @pl.when(j < per)
                def _():
                    load(j, b).start()

                @pl.when(jnp.logical_and(j >= 1, j < per + 1))
                def _():
                    load(j - 1, 1 - b).wait()
                    for q in range(kk):
                        scat(j - 1, 1 - b, q).start()

    return k(x, idx)


def _sc_gather_rows(y, idx):
    p, c = y.shape
    kk, n = idx.shape
    w = SC_GATHER_WIN
    nb = SC_GATHER_BUFS
    wins = LANES // w
    group = wins * kk
    assert group % nb == 0 and group >= nb and n % LANES == 0
    n_units = n // LANES
    max_units = -(-n_units // SC_WORKERS)
    mesh = plsc.VectorSubcoreMesh(core_axis_name="c", subcore_axis_name="s")

    @functools.partial(
        pl.kernel, mesh=mesh,
        out_type=jax.ShapeDtypeStruct((kk, n, c), y.dtype),
        scratch_types=[pltpu.VMEM((2, kk, LANES), I32), pltpu.VMEM((nb, w, c), y.dtype),
                       pltpu.SemaphoreType.DMA((nb,)), pltpu.SemaphoreType.DMA((nb,))],
        name="sc_gather_rows",
    )
    def k(y_hbm, idx_hbm, out_hbm, idx_v, rows_v, gsem, wsem):
        wid = lax.axis_index("s") * 2 + lax.axis_index("c")
        my_units = (n_units - wid + SC_WORKERS - 1) // SC_WORKERS
        n_items = my_units * group

        def gath(slot, jj, q, b):
            return pltpu.make_async_copy(y_hbm.at[idx_v.at[slot, q, pl.ds(jj * w, w)]], rows_v.at[b], gsem.at[b])

        def wr(i, b):
            row = (wid + (i // group) * SC_WORKERS) * LANES + ((i % group) // kk) * w
            return pltpu.make_async_copy(rows_v.at[b], out_hbm.at[i % kk, pl.ds(row, w)], wsem.at[b])

        @pl.loop(0, max_units + 1)
        def _(gi):
            slot = gi % 2

            @pl.when(gi < my_units)
            def _():
                pltpu.sync_copy(idx_hbm.at[:, pl.ds((wid + gi * SC_WORKERS) * LANES, LANES)], idx_v.at[slot])

            for l in range(group):
                i = gi * group + l
                b = l % nb
                bw = (l - (nb - 1)) % nb

                @pl.when(jnp.logical_and(i >= nb, i < n_items + nb))
                def _():
                    wr(i - nb, b).wait()

                @pl.when(i < n_items)
                def _():
                    gath(slot, l // kk, l % kk, b).start()

                @pl.when(jnp.logical_and(i >= nb - 1, i < n_items + nb - 1))
                def _():
                    gath(slot, 0, 0, bw).wait()
                    wr(i - (nb - 1), bw).start()

    return k(y, idx)


def _moe_kernel(layer, info_ref, x_ref, w1_hbm, b1_ref, w2_hbm, b2_ref, o_ref,
                w1f_ref, w2f_ref, w1b_ref, w2b_ref, sem):
    i = pl.program_id(0)
    nb = pl.num_programs(0)
    expert = info_ref[i]
    nxt = info_ref[nb + i]
    slot = info_ref[2 * nb + i]
    first = jnp.logical_or(i == 0, expert != info_ref[jnp.maximum(i - 1, 0)])

    def fetch(e, s):
        return (pltpu.make_async_copy(w1_hbm.at[layer, e], w1f_ref.at[s], sem.at[0, s]),
                pltpu.make_async_copy(w2_hbm.at[layer, e], w2f_ref.at[s], sem.at[1, s]))

    @pl.when(i == 0)
    def _():
        for cp in fetch(expert, slot):
            cp.start()

    @pl.when(first)
    def _():
        for cp in fetch(expert, slot):
            cp.wait()

        @pl.when(nxt >= 0)
        def _():
            for cp in fetch(nxt, 1 - slot):
                cp.start()

        w1b_ref[...] = w1f_ref[slot].astype(BF16)
        w2b_ref[...] = w2f_ref[slot].astype(BF16)

    units = (info_ref[3 * nb + i] + MOE_UNIT - 1) // MOE_UNIT
    for u in range(1, x_ref.shape[0] // MOE_UNIT + 1):
        @pl.when(units == u)
        def _():
            rows = slice(0, u * MOE_UNIT)
            _expert_block(x_ref.at[rows], w1b_ref, b1_ref, w2b_ref, b2_ref, o_ref.at[rows])


def _expert_block(x_ref, w1b_ref, b1_ref, w2b_ref, b2_ref, o_ref):
    lo, hi = _unpack_pairs(x_ref[...])
    x = jnp.concatenate([lo.astype(BF16), hi.astype(BF16)], axis=-1)
    f = w2b_ref.shape[0]
    fc = 512
    acc = None
    for c in range(f // fc):
        gt = jnp.dot(x, w1b_ref[:, c * fc:(c + 1) * fc], preferred_element_type=F32) + b1_ref[0, 0, :, c * fc:(c + 1) * fc]
        up = jnp.dot(x, w1b_ref[:, f + c * fc:f + (c + 1) * fc], preferred_element_type=F32) + b1_ref[0, 0, :, f + c * fc:f + (c + 1) * fc]
        gt = jnp.minimum(gt, SWIGLU_LIMIT)
        up = jnp.clip(up, -SWIGLU_LIMIT, SWIGLU_LIMIT)
        act = (gt * jax.nn.sigmoid(SWIGLU_ALPHA * gt) * (up + 1.0)).astype(BF16)
        part = jnp.dot(act, w2b_ref[c * fc:(c + 1) * fc, :], preferred_element_type=F32)
        acc = part if acc is None else acc + part
    o_ref[...] = _pack_pairs(acc + b2_ref[0, 0])


def _moe_experts(layer, block_expert, xs, p_rows, w1, b1, w2, b2):
    _, e, d, f2 = w1.shape
    f = f2 // 2
    nb = p_rows // MOE_ROWS
    grid_spec = pltpu.PrefetchScalarGridSpec(
        num_scalar_prefetch=1,
        grid=(nb,),
        in_specs=[pl.BlockSpec((MOE_ROWS, d // 2), lambda i, be: (i, 0)),
                  pl.BlockSpec(memory_space=pl.ANY),
                  pl.BlockSpec((1, 1, 1, f2), lambda i, be: (layer, be[i], 0, 0)),
                  pl.BlockSpec(memory_space=pl.ANY),
                  pl.BlockSpec((1, 1, 1, d), lambda i, be: (layer, be[i], 0, 0))],
        out_specs=pl.BlockSpec((MOE_ROWS, d // 2), lambda i, be: (i, 0)),
        scratch_shapes=[pltpu.VMEM((2, d, f2), F32), pltpu.VMEM((2, f, d), F32),
                        pltpu.VMEM((d, f2), BF16), pltpu.VMEM((f, d), BF16),
                        pltpu.SemaphoreType.DMA((2, 2))],
    )
    return pl.pallas_call(
        functools.partial(_moe_kernel, layer),
        grid_spec=grid_spec,
        out_shape=jax.ShapeDtypeStruct((p_rows, d // 2), U32),
        compiler_params=_cparams("arbitrary"),
        name="moe_experts",
    )(block_expert, xs, w1, b1.reshape(b1.shape[0], e, 1, f2), w2, b2.reshape(b2.shape[0], e, 1, d))


def _moe_dispatch(h2_words, lg_t):
    gate, dest, block_info, p_rows = _route(lg_t)
    spare = p_rows + jnp.arange(SC_SCATTER_WIN, dtype=I32)
    xs = _sc_scatter_rows(h2_words, _windows(dest, SC_SCATTER_WIN, 2, spare), p_rows + SC_SCATTER_WIN)
    return xs, gate, dest, block_info, p_rows


def _moe_compute(layer, dispatched, w1, b1, w2, b2):
    xs, gate, back, block_info, p_rows = dispatched
    ys = _moe_experts(layer, block_info, xs, p_rows, w1, b1, w2, b2)
    return _sc_gather_rows(ys, back), gate


def _final_kernel(x_ref, yg_ref, gate_ref, g2_ref, nw_ref, *rest):
    o_ref = rest[-1]
    x = x_ref[0] + g2_ref[0] * _combine(yg_ref, gate_ref)
    ms = jnp.mean(x * x, axis=-1, keepdims=True)
    o_ref[0] = x * lax.rsqrt(ms + EPS) * nw_ref[...]


def _final(x, res, nw, ts, out_buf, ob0, b_total):
    b, s, d = x.shape
    nj = s // ts
    yg, gate, g2, first = res
    blk0 = first // ts
    row = pl.BlockSpec((1, ts, d), lambda i, j: (i, j, 0))
    if gate.ndim == 2:
        gate_spec = pl.BlockSpec((GATE_ROWS, ts), lambda i, j: (0, blk0 + i * nj + j))
    else:
        gate_spec = pl.BlockSpec((1, ts, TOP_K), lambda i, j: (i, j, 0))
    in_specs = [row, pl.BlockSpec((TOP_K, ts, d // 2), lambda i, j: (0, blk0 + i * nj + j, 0)),
                gate_spec,
                pl.BlockSpec((1, 1, d), lambda i, j: (i, 0, 0)),
                pl.BlockSpec((1, d), lambda i, j: (0, 0))]
    args = [x, yg, gate, g2, nw.reshape(1, d)]
    aliases = {}
    if out_buf is not None:
        in_specs.append(pl.BlockSpec(memory_space=pl.ANY))
        args.append(out_buf)
        aliases = {len(args) - 1: 0}
    return pl.pallas_call(
        _final_kernel,
        grid=(b, nj),
        in_specs=in_specs,
        out_specs=pl.BlockSpec((1, ts, d), lambda i, j: (i + ob0, j, 0)),
        out_shape=jax.ShapeDtypeStruct((b_total, s, d), F32),
        input_output_aliases=aliases,
        compiler_params=_cparams("parallel", "parallel"),
        name="final_norm",
    )(*args)


def kernel(x_prompt, x_sample, c_prompt, c_sample, cache_k, cache_v, state_hgrn, norm_mix, norm_ffn, norm_final, w_ada, b_ada, w_in, rel_bias, hgrn_lb_logits, g_attn_out, g_hgrn_out, w_out, w_router, b_router, w_e1, b_e1, w_e2, b_e2):
    depth = w_in.shape[0]
    bp, sp, d = x_prompt.shape
    bs, ls, _ = x_sample.shape
    wc = cache_k.shape[2]
    n_p, n_s = bp * sp, bs * ls

    lbs = jax.nn.softmax(hgrn_lb_logits.astype(F32), axis=0)
    lbs = jnp.cumsum(lbs, axis=0) - lbs[0]

    mods = _modulation(jnp.concatenate([c_prompt, c_sample], axis=0), w_ada, b_ada)
    lk_s = -(-(wc + ls) // LANES) * LANES
    tail = min(LEFT, sp)
    rec_chunk = min(REC_CHUNK, sp)

    n_groups = PROMPT_GROUPS if bp % PROMPT_GROUPS == 0 else 1
    gb = bp // n_groups
    n_g = gb * sp
    x_g = [(x_prompt, g * gb) for g in range(n_groups)]
    x_s = x_sample
    res_g = [None] * n_groups
    res_s = None
    kp_l, vp_l, sp_l, ks_l, vs_l, ss_l = [], [], [], [], [], []
    for layer in range(depth):
        m6 = mods[layer].reshape(bp + bs, 1, N_MOD, d)
        mod_s = [m6[bp:, :, n, :] for n in range(N_MOD)]
        w_in_bf = w_in[layer].astype(BF16)
        w_out_bf = w_out[layer].astype(BF16)
        wr_hi, wr_mid, _ = _split3(w_router[layer].T)
        wr2 = jnp.concatenate([wr_hi, wr_mid], axis=0)
        table = rel_bias[layer]
        bias_p = _prompt_bias(table)

        def mixer_prompt(g):
            mod = [m6[g * gb:(g + 1) * gb, :, n, :] for n in range(N_MOD)]
            x_new, h2, lg, kt, vt, s_fin = _mixer_prompt(
                x_g[g][0], x_g[g][1], res_g[g], norm_mix[layer], mod[1], mod[0], w_in_bf, lbs[layer], bias_p,
                g_attn_out[layer], g_hgrn_out[layer], w_out_bf, mod[2], norm_ffn[layer], mod[4], mod[3],
                wr2, b_router[layer], tail, rec_chunk, n_s if g == n_groups - 1 else 0)
            x_g[g] = (x_new, 0)
            return h2, lg, kt, vt, s_fin, mod[5]

        def mixer_sample(rows):
            s0 = jnp.swapaxes(state_hgrn[layer].astype(F32), -1, -2)
            x_new, rows, lg, kt, vt, s_new = _mixer_sample(
                x_s, res_s, norm_mix[layer], mod_s[1], mod_s[0], w_in_bf, lbs[layer],
                cache_k[layer].reshape(bs, wc, ATT_W), cache_v[layer].reshape(bs, wc, ATT_W),
                _sample_bias(table, ls, wc, lk_s), s0, g_attn_out[layer], g_hgrn_out[layer], w_out_bf,
                mod_s[2], norm_ffn[layer], mod_s[4], mod_s[3], wr2, b_router[layer], rows, n_g)
            return x_new, rows, jnp.swapaxes(lg, 0, 1).reshape(N_EXPERTS, n_s), kt, vt, s_new

        dispatched, g2s, k_t, v_t, s_t = [], [], [], [], []
        for g in range(n_groups):
            h2, lg, kt, vt, s_fin, g2 = mixer_prompt(g)
            if g == n_groups - 1:
                x_s, h2, lg_s, ka_s, va_s, s_new = mixer_sample(h2)
                ks_l.append(ka_s.reshape(bs, ls, ATT_HEADS, ATT_DIM))
                vs_l.append(va_s.reshape(bs, ls, ATT_HEADS, ATT_DIM))
                ss_l.append(jnp.swapaxes(s_new, -1, -2))
                lg = jnp.concatenate([lg, lg_s], axis=1)
            dispatched.append(_moe_dispatch(h2, lg))
            g2s.append(g2)
            k_t.append(kt)
            v_t.append(vt)
            s_t.append(s_fin)
        kp_l.append(jnp.concatenate(k_t, axis=0).reshape(bp, tail, ATT_HEADS, ATT_DIM))
        vp_l.append(jnp.concatenate(v_t, axis=0).reshape(bp, tail, ATT_HEADS, ATT_DIM))
        sp_l.append(jnp.swapaxes(jnp.concatenate(s_t, axis=0), -1, -2))

        for g in range(n_groups):
            yg, gate = _moe_compute(layer, dispatched[g], w_e1, b_e1, w_e2, b_e2)
            res_g[g] = (yg, gate, g2s[g], 0)
            if g == n_groups - 1:
                res_s = (yg, gate[:TOP_K, n_g:].T.reshape(bs, ls, TOP_K), mod_s[5], n_g)

    y_prompt = None
    for g in range(n_groups):
        y_prompt = _final(x_g[g][0], res_g[g], norm_final, SEQ_TILE, y_prompt, g * gb, bp)
    y_sample = _final(x_s, res_s, norm_final, ls, None, 0, bs)
    return (y_prompt, y_sample, jnp.stack(kp_l), jnp.stack(vp_l), jnp.stack(sp_l),
            jnp.stack(ks_l), jnp.stack(vs_l), jnp.stack(ss_l))
```

```python
import functools

import jax
import jax.numpy as jnp
import numpy as np
from jax import lax
from jax.experimental import pallas as pl
from jax.experimental.pallas import tpu as pltpu
from jax.experimental.pallas import tpu_sc as plsc

F32 = jnp.float32
BF16 = jnp.bfloat16
I32 = jnp.int32
U32 = jnp.uint32

EPS = 1e-6
CHUNK = 64
LEFT_CHUNKS = 8
LEFT = LEFT_CHUNKS * CHUNK
MAX_REL = 2 * CHUNK
ATT_HEADS = 8
ATT_DIM = 64
ATT_W = ATT_HEADS * ATT_DIM
REC_HEADS = 4
REC_DIM = 128
REC_W = REC_HEADS * REC_DIM
N_EXPERTS = 32
TOP_K = 4
SWIGLU_ALPHA = 1.702
SWIGLU_LIMIT = 7.0
N_MOD = 6
NEG = -1e30
LOG2E = 1.4426950408889634

LANES = 128
ATT_QSUB = 128
ATT_WIN = LEFT + ATT_QSUB
SEQ_TILE = 1024
REC_CHUNK = 128
MOE_ROWS = 1024
MOE_UNIT = 256
ROUTE_TILE = 256
ROUTE_MAX_SUB = 8
PROMPT_GROUPS = 2
GATE_ROWS = 8
VMEM_LIMIT = 56 * 1024 * 1024
SC_WORKERS = 32
SC_SCATTER_WIN = 64
SC_GATHER_WIN = 32
SC_GATHER_BUFS = 4


def _cparams(*sem):
    return pltpu.CompilerParams(dimension_semantics=sem, vmem_limit_bytes=VMEM_LIMIT)


def _split3(x):
    hi = x.astype(BF16)
    r1 = x - hi.astype(F32)
    mid = r1.astype(BF16)
    lo = (r1 - mid.astype(F32)).astype(BF16)
    return hi, mid, lo


def _bits(x):
    return lax.bitcast_convert_type(x, U32)


def _pack_pairs(x):
    r = _bits(x.astype(BF16).astype(F32))
    half = x.shape[1] // 2
    return jnp.bitwise_or(jnp.bitwise_and(r[:, half:], jnp.uint32(0xFFFF0000)),
                          jnp.right_shift(r[:, :half], jnp.uint32(16)))


def _unpack_pairs(word):
    lo = lax.bitcast_convert_type(jnp.left_shift(word, jnp.uint32(16)), F32)
    hi = lax.bitcast_convert_type(jnp.bitwise_and(word, jnp.uint32(0xFFFF0000)), F32)
    return lo, hi


def _mod_kernel(c_ref, w_ref, b_ref, o_ref):
    c = c_ref[...]
    act = (c * jax.nn.sigmoid(c)).astype(BF16)
    o_ref[0] = jnp.dot(act, w_ref[0].astype(BF16), preferred_element_type=F32) + b_ref[0]


def _modulation(c_all, w_ada, b_ada):
    depth, d, n6 = w_ada.shape
    rows = c_all.shape[0]
    tn = 1536
    return pl.pallas_call(
        _mod_kernel,
        grid=(depth, n6 // tn),
        in_specs=[pl.BlockSpec((rows, d), lambda l, j: (0, 0)),
                  pl.BlockSpec((1, d, tn), lambda l, j: (l, 0, j)),
                  pl.BlockSpec((1, 1, tn), lambda l, j: (l, 0, j))],
        out_specs=pl.BlockSpec((1, rows, tn), lambda l, j: (l, 0, j)),
        out_shape=jax.ShapeDtypeStruct((depth, rows, n6), F32),
        compiler_params=_cparams("parallel", "parallel"),
        name="adaln_modulation",
    )(c_all, w_ada, b_ada.reshape(depth, 1, n6))


def _combine(yg_ref, gate_ref):
    g = gate_ref[0] if len(gate_ref.shape) == 3 else gate_ref[...].T
    lo = hi = None
    for k in range(TOP_K):
        l, h = _unpack_pairs(yg_ref[k])
        gk = g[:, k:k + 1]
        lo = l * gk if lo is None else lo + l * gk
        hi = h * gk if hi is None else hi + h * gk
    return jnp.concatenate([lo, hi], axis=-1)


def _project_rows(x, nw_ref, sc_ref, sh_ref, w_ref, lb_ref,
                  qa_ref, ka_ref, va_ref, kt_ref, vt_ref, qh_ref, kh_ref, lf_ref, vh_ref, gh_ref):
    ms = jnp.mean(x * x, axis=-1, keepdims=True)
    h = (x * lax.rsqrt(ms + EPS) * nw_ref[...]) * (1.0 + sc_ref[0]) + sh_ref[0]
    hb = h.astype(BF16)

    def proj(g):
        return jnp.dot(hb, w_ref[:, g * ATT_W:(g + 1) * ATT_W], preferred_element_type=F32)

    qa_ref[...] = (proj(0) * (ATT_DIM ** -0.5)).astype(BF16)
    k = proj(1)
    ka_ref[...] = k.astype(BF16)
    kt_ref[...] = k
    v = proj(2)
    va_ref[...] = v.astype(BF16)
    vt_ref[...] = v
    qh = proj(3)
    qh_ref[...] = qh * jax.nn.sigmoid(qh)
    f = proj(4)
    lb = lb_ref[...]
    lf_ref[...] = jnp.log(lb + (1.0 - lb) * jax.nn.sigmoid(f))
    kh_ref[...] = (1.0 - lb) * jax.nn.sigmoid(-f)
    vh_ref[...] = proj(5).astype(BF16)
    gh_ref[...] = proj(6)


def _attend(q, kw, vw, bias_ref, s_ref, col_thr):
    lq, lk = q.shape[0], kw.shape[0]
    lane = lax.broadcasted_iota(I32, (lq, LANES), 1)
    lo = lane < ATT_DIM
    nt = (((1,), (1,)), ((), ()))
    if col_thr is not None:
        keep = lax.broadcasted_iota(I32, (2 * lq, lk), 1) >= col_thr
    ms = []
    for p in range(ATT_HEADS // 2):
        sl = slice(p * LANES, (p + 1) * LANES)
        qp = q[:, sl]
        z = jnp.zeros_like(qp)
        lhs = jnp.concatenate([jnp.where(lo, qp, z), jnp.where(lo, z, qp)], axis=0)
        s = lax.dot_general(lhs, kw[:, sl], nt, preferred_element_type=F32) + bias_ref[p]
        if col_thr is not None:
            s = jnp.where(keep, s, NEG)
        ms.append(jnp.max(s, axis=-1, keepdims=True))
        s_ref[p] = s
    outs = []
    for p in range(ATT_HEADS // 2):
        sl = slice(p * LANES, (p + 1) * LANES)
        e = jnp.exp(s_ref[p] - ms[p])
        l = jnp.sum(e, axis=-1, keepdims=True)
        o = jnp.dot(e.astype(BF16), vw[:, sl], preferred_element_type=F32) / l
        outs.append(jnp.where(lo, o[:lq], o[lq:]))
    return jnp.concatenate(outs, axis=-1)


def _band_attention(q_ref, kw_ref, vw_ref, bias_ref, s_ref, o_ref, first_tile):
    ts = q_ref.shape[0]

    def body(masked, r, carry):
        r0 = pl.multiple_of(r * ATT_QSUB, ATT_QSUB)
        q = q_ref[pl.ds(r0, ATT_QSUB), :]
        kw = kw_ref[pl.ds(r0, ATT_WIN), :]
        vw = vw_ref[pl.ds(r0, ATT_WIN), :]
        o_ref[pl.ds(r0, ATT_QSUB), :] = _attend(q, kw, vw, bias_ref, s_ref, (ts - r0) if masked else None)
        return carry

    @pl.when(first_tile)
    def _():
        lax.fori_loop(0, ts // ATT_QSUB, functools.partial(body, True), 0)

    @pl.when(jnp.logical_not(first_tile))
    def _():
        lax.fori_loop(0, ts // ATT_QSUB, functools.partial(body, False), 0)


def _toeplitz_bias(table, lq, lk, offset, valid):
    m = lq + lk - 1
    k = np.arange(m)
    diff = np.where(k < lk, k, k - m)
    g = table[:, np.clip(offset - diff, -MAX_REL, MAX_REL) + MAX_REL]
    skew = jnp.tile(g, (1, lq))[:, :lq * (m - 1)].reshape(-1, lq, m - 1)[:, :, :lk]
    bias = jnp.where(jnp.asarray(valid)[None], skew, NEG).astype(F32)
    return bias.reshape(ATT_HEADS // 2, 2 * lq, lk)


def _prompt_bias(table):
    t = np.arange(ATT_QSUB)[:, None]
    j = np.arange(ATT_WIN)[None, :]
    start = (t // CHUNK) * CHUNK
    return _toeplitz_bias(table, ATT_QSUB, ATT_WIN, LEFT, (j >= start) & (j < start + LEFT + CHUNK))


def _sample_bias(table, l, w, lk):
    valid = np.broadcast_to(np.arange(lk)[None, :] < w + l, (l, lk))
    return _toeplitz_bias(table, l, lk, w, valid)


def _neg_abs(x):
    return lax.bitcast_convert_type(jnp.bitwise_or(_bits(x), jnp.uint32(0x80000000)), F32)


def _hgrn_rows(chunk, q_ref, k_ref, lf_ref, v_ref, o_ref, st_ref):
    ts = q_ref.shape[0]
    n_chunks = ts // chunk

    row = lax.broadcasted_iota(I32, (chunk, chunk), 0)
    colm = lax.broadcasted_iota(I32, (chunk, chunk), 1)
    tri = (colm <= row).astype(BF16)
    xor = jnp.bitwise_xor(row, colm)
    lower = colm < row
    rid = lax.broadcasted_iota(I32, (chunk, REC_W), 0)

    levels = []
    b = chunk // 2
    while b >= 1:
        levels.append(b)
        b //= 2
    pair_mask = [xor == 0] + [lower & (xor >= b) & (xor < 2 * b) for b in levels]
    upper = [jnp.bitwise_and(rid, b) != 0 for b in levels]
    ph4 = jnp.bitwise_and(rid, 3)
    odd = jnp.bitwise_and(rid, 1) == 1
    nt = (((1,), (1,)), ((), ()))
    tn = (((0,), (0,)), ((), ()))
    sls = [slice(h * REC_DIM, (h + 1) * REC_DIM) for h in range(REC_HEADS)]

    def boundary(cum, b):
        if 2 * b >= 8:
            pieces = [jnp.broadcast_to(cum[m * 2 * b + b - 1:m * 2 * b + b, :], (2 * b, REC_W))
                      for m in range(chunk // (2 * b))]
            return pieces[0] if len(pieces) == 1 else jnp.concatenate(pieces, axis=0)
        if b == 2:
            up1 = pltpu.roll(cum, chunk - 1, 0)
            dn1 = pltpu.roll(cum, 1, 0)
            dn2 = pltpu.roll(cum, 2, 0)
            return jnp.where(ph4 == 0, up1, jnp.where(ph4 == 1, cum, jnp.where(ph4 == 2, dn1, dn2)))
        return jnp.where(odd, pltpu.roll(cum, 1, 0), cum)

    def one_chunk(c, carry):
        r0 = pl.multiple_of(c * chunk, chunk)
        rows = pl.ds(r0, chunk)
        q = q_ref[rows, :]
        k = k_ref[rows, :]
        v = v_ref[rows, :]
        lf = lf_ref[rows, :]
        cum = sum(jnp.dot(tri, part, preferred_element_type=F32) for part in _split3(lf)) * LOG2E
        last = cum[chunk - 1:chunk, :]
        q_in = (q * jnp.exp2(cum)).astype(BF16)
        k_out = (k * jnp.exp2(last - cum)).astype(BF16)
        decay = jnp.exp2(last)
        qb = q.astype(BF16)
        kb = k.astype(BF16)
        zs = []
        for n, b in enumerate(levels):
            w = jnp.exp2(_neg_abs(cum - boundary(cum, b)))
            zs.append((jnp.where(upper[n], q, k) * w).astype(BF16))
        sts = [st_ref[h] for h in range(REC_HEADS)]
        d0 = [lax.dot_general(qb[:, sl], kb[:, sl], nt, preferred_element_type=F32) for sl in sls]
        dl = [[lax.dot_general(z[:, sl], z[:, sl], nt, preferred_element_type=F32) for z in zs] for sl in sls]
        oi = [lax.dot_general(q_in[:, sl], st.astype(BF16), nt, preferred_element_type=F32)
              for sl, st in zip(sls, sts)]
        upd = [lax.dot_general(v[:, sl], k_out[:, sl], tn, preferred_element_type=F32) for sl in sls]
        outs = []
        for h, sl in enumerate(sls):
            a = jnp.where(pair_mask[0], d0[h], 0.0)
            for n in range(len(levels)):
                a = jnp.where(pair_mask[n + 1], dl[h][n], a)
            outs.append(jnp.dot(a.astype(BF16), v[:, sl], preferred_element_type=F32) + oi[h])
        o_ref[rows, :] = jnp.concatenate(outs, axis=-1)
        for h, sl in enumerate(sls):
            st_ref[h] = decay[:, sl] * sts[h] + upd[h]
        return carry

    lax.fori_loop(0, n_chunks, one_chunk, 0, unroll=2 if n_chunks % 2 == 0 else 1)


def _head_rms(x, width):
    lane = lax.broadcasted_iota(I32, (x.shape[0], LANES), 1)
    lo = lane < ATT_DIM
    outs = []
    for p in range(x.shape[1] // LANES):
        xp = x[:, p * LANES:(p + 1) * LANES]
        sq = xp * xp
        tot = jnp.sum(sq, axis=-1, keepdims=True)
        if width == LANES:
            scale = lax.rsqrt(tot * (1.0 / LANES) + EPS)
        else:
            s_lo = jnp.sum(jnp.where(lo, sq, 0.0), axis=-1, keepdims=True)
            scale = jnp.where(lo, lax.rsqrt(s_lo * (1.0 / ATT_DIM) + EPS),
                              lax.rsqrt((tot - s_lo) * (1.0 / ATT_DIM) + EPS))
        outs.append(xp * scale)
    return jnp.concatenate(outs, axis=-1)


def _merge_rows(oa, orr, gh, x, ga_ref, gr_ref, wo_ref, g1_ref, nf_ref, sc_ref, sh_ref, wr_ref, br_ref):
    a = _head_rms(oa, ATT_DIM) * ga_ref[...]
    r = _head_rms(orr, REC_DIM) * gr_ref[...] * (gh * jax.nn.sigmoid(gh))
    mix = jnp.dot(a.astype(BF16), wo_ref[0:ATT_W, :], preferred_element_type=F32)
    mix += jnp.dot(r.astype(BF16), wo_ref[ATT_W:ATT_W + REC_W, :], preferred_element_type=F32)
    x = x + g1_ref[0] * mix
    ms = jnp.mean(x * x, axis=-1, keepdims=True)
    h2 = (x * lax.rsqrt(ms + EPS) * nf_ref[...]) * (1.0 + sc_ref[0]) + sh_ref[0]
    hh = h2.astype(BF16)
    words = _pack_pairs(h2)
    hm = (h2 - hh.astype(F32)).astype(BF16)
    nt = (((1,), (1,)), ((), ()))
    l1 = lax.dot_general(wr_ref[...], hh, nt, preferred_element_type=F32)
    l2 = lax.dot_general(wr_ref[0:N_EXPERTS, :], hm, nt, preferred_element_type=F32)
    lg = l1[0:N_EXPERTS] + l1[N_EXPERTS:2 * N_EXPERTS] + l2 + br_ref[...]
    return x, words, lg


def _mixer_kernel(has_res, chunk, *refs):
    n_in = 4 if has_res else 1
    x_ref = refs[0]
    (nw_ref, sc1_ref, sh1_ref, lb_ref, ga_ref, gr_ref, g1_ref, nf_ref, sc2_ref, sh2_ref, wr_ref, br_ref,
     win_hbm, wout_hbm, bias_hbm,
     xo_ref, h2_ref, lg_ref, kt_ref, vt_ref, sf_ref,
     win_v, wout_v, bias_v, q_s, kw_s, vw_s, s_s, qh_s, kh_s, lf_s, vh_s, gh_s, oa_s, or_s, st_s, sem) = refs[n_in:]
    j = pl.program_id(1)
    ts = x_ref.shape[1]

    @pl.when(j == 0)
    def _():
        copies = [pltpu.make_async_copy(src, dst, sem.at[n])
                  for n, (src, dst) in enumerate(((win_hbm, win_v), (wout_hbm, wout_v), (bias_hbm, bias_v)))]
        for cp in copies:
            cp.start()
        for cp in copies:
            cp.wait()
        st_s[...] = jnp.zeros_like(st_s)
        kw_s[0:ts, :] = jnp.zeros((ts, ATT_W), BF16)
        vw_s[0:ts, :] = jnp.zeros((ts, ATT_W), BF16)

    if has_res:
        yg_ref, gate_ref, g2_ref = refs[1:4]
        x = x_ref[0] + g2_ref[0] * _combine(yg_ref, gate_ref)
        xo_ref[0] = x
    else:
        x = x_ref[0]
    cur = pl.ds(ts, ts)
    _project_rows(x, nw_ref, sc1_ref, sh1_ref, win_v, lb_ref, q_s, kw_s.at[cur], vw_s.at[cur],
                  kt_ref.at[0], vt_ref.at[0], qh_s, kh_s, lf_s, vh_s, gh_s)
    _band_attention(q_s, kw_s, vw_s, bias_v, s_s, oa_s, j == 0)
    _hgrn_rows(chunk, qh_s, kh_s, lf_s, vh_s, or_s, st_s)
    x_in = xo_ref[0] if has_res else x_ref[0]
    x_new, words, lg = _merge_rows(oa_s[...], or_s[...], gh_s[...], x_in, ga_ref, gr_ref, wout_v, g1_ref,
                                   nf_ref, sc2_ref, sh2_ref, wr_ref, br_ref)
    xo_ref[0] = x_new
    h2_ref[...] = words
    lg_ref[...] = lg
    kw_s[0:ts, :] = kw_s[ts:2 * ts, :]
    vw_s[0:ts, :] = vw_s[ts:2 * ts, :]

    @pl.when(j == pl.num_programs(1) - 1)
    def _():
        sf_ref[0] = st_s[...]


def _mixer_prompt(x, xb0, res, nw, sc1, sh1, w_in_bf, lb, bias, g_att, g_rec, w_out_bf, g1, nf, sc2, sh2,
                  wr2, br, tail, chunk, extra_rows):
    _, s, d = x.shape
    b = sc1.shape[0]
    ts = LEFT
    nj = s // ts
    n_tail = tail // ts
    has_res = res is not None
    row = pl.BlockSpec((1, ts, d), lambda i, j: (i, j, 0))
    per_b = pl.BlockSpec((1, 1, d), lambda i, j: (i, 0, 0))
    const = lambda shp: pl.BlockSpec(shp, lambda i, j: (0,) * len(shp))
    hbm = pl.BlockSpec(memory_space=pl.ANY)
    in_specs, args = [pl.BlockSpec((1, ts, d), lambda i, j: (i + xb0, j, 0))], [x]
    if has_res:
        yg, gate, g2, first = res
        blk0 = first // ts
        in_specs += [pl.BlockSpec((TOP_K, ts, d // 2), lambda i, j: (0, blk0 + i * nj + j, 0)),
                     pl.BlockSpec((GATE_ROWS, ts), lambda i, j: (0, blk0 + i * nj + j)), per_b]
        args += [yg, gate, g2]
    in_specs += [const((1, d)), per_b, per_b, const((1, REC_W)), const((1, ATT_W)), const((1, REC_W)),
                 per_b, const((1, d)), per_b, per_b, const(wr2.shape), const((N_EXPERTS, 1)), hbm, hbm, hbm]
    args += [nw.reshape(1, d), sc1, sh1, lb.reshape(1, REC_W), g_att.reshape(1, ATT_W), g_rec.reshape(1, REC_W),
             g1, nf.reshape(1, d), sc2, sh2, wr2, br.reshape(N_EXPERTS, 1), w_in_bf, w_out_bf, bias]
    tail_spec = pl.BlockSpec((1, ts, ATT_W), lambda i, j: (i, jnp.maximum(j - (nj - n_tail), 0), 0))
    state_spec = pl.BlockSpec((1, REC_HEADS, REC_DIM, REC_DIM), lambda i, j: (i, 0, 0, 0))
    tile = lambda dt: pltpu.VMEM((ts, ATT_W), dt)
    return pl.pallas_call(
        functools.partial(_mixer_kernel, has_res, chunk),
        grid=(b, nj),
        in_specs=in_specs,
        out_specs=[row, pl.BlockSpec((ts, d // 2), lambda i, j: (i * nj + j, 0)),
                   pl.BlockSpec((N_EXPERTS, ts), lambda i, j: (0, i * nj + j)), tail_spec, tail_spec, state_spec],
        out_shape=[jax.ShapeDtypeStruct((b, s, d), F32), jax.ShapeDtypeStruct((b * s + extra_rows, d // 2), U32),
                   jax.ShapeDtypeStruct((N_EXPERTS, b * s), F32),
                   jax.ShapeDtypeStruct((b, tail, ATT_W), F32), jax.ShapeDtypeStruct((b, tail, ATT_W), F32),
                   jax.ShapeDtypeStruct((b, REC_HEADS, REC_DIM, REC_DIM), F32)],
        scratch_shapes=[pltpu.VMEM(w_in_bf.shape, BF16), pltpu.VMEM(w_out_bf.shape, BF16), pltpu.VMEM(bias.shape, F32),
                        tile(BF16), pltpu.VMEM((2 * ts, ATT_W), BF16), pltpu.VMEM((2 * ts, ATT_W), BF16),
                        pltpu.VMEM((ATT_HEADS // 2, 2 * ATT_QSUB, ATT_WIN), F32),
                        tile(F32), tile(F32), tile(F32), tile(BF16), tile(F32), tile(F32), tile(F32),
                        pltpu.VMEM((REC_HEADS, REC_DIM, REC_DIM), F32), pltpu.SemaphoreType.DMA((3,))],
        compiler_params=_cparams("parallel", "arbitrary"),
        name="prompt_mixer",
    )(*args)


def _sample_mixer_kernel(has_res, *refs):
    n_in = 4 if has_res else 1
    x_ref = refs[0]
    (nw_ref, sc1_ref, sh1_ref, w_ref, lb_ref, ck_ref, cv_ref, bias_ref, s0_ref, ga_ref, gr_ref, wo_ref,
     g1_ref, nf_ref, sc2_ref, sh2_ref, wr_ref, br_ref, rows_hbm,
     xo_ref, h2_ref, lg_ref, kt_ref, vt_ref, sf_ref,
     q_s, kw_s, vw_s, s_s, qh_s, kh_s, lf_s, vh_s, gh_s, or_s, st_s) = refs[n_in:]
    del rows_hbm
    if has_res:
        yg_ref, gate_ref, g2_ref = refs[1:4]
        x = x_ref[0] + g2_ref[0] * _combine(yg_ref, gate_ref)
    else:
        x = x_ref[0]
    wc, l, lk = ck_ref.shape[1], x_ref.shape[1], kw_s.shape[0]
    new = pl.ds(wc, l)
    _project_rows(x, nw_ref, sc1_ref, sh1_ref, w_ref, lb_ref, q_s, kw_s.at[new], vw_s.at[new],
                  kt_ref.at[0], vt_ref.at[0], qh_s, kh_s, lf_s, vh_s, gh_s)
    for dst, cache in ((kw_s, ck_ref), (vw_s, cv_ref)):
        dst[0:wc, :] = cache[0].astype(BF16)
        dst[wc + l:lk, :] = jnp.zeros((lk - wc - l, ATT_W), BF16)
    oa = _attend(q_s[...], kw_s[...], vw_s[...], bias_ref, s_s, None)
    st_s[...] = s0_ref[0]
    _hgrn_rows(l, qh_s, kh_s, lf_s, vh_s, or_s, st_s)
    sf_ref[0] = st_s[...]
    x_new, words, lg = _merge_rows(oa, or_s[...], gh_s[...], x, ga_ref, gr_ref, wo_ref, g1_ref,
                                   nf_ref, sc2_ref, sh2_ref, wr_ref, br_ref)
    xo_ref[0] = x_new
    h2_ref[...] = words
    lg_ref[0] = lg


def _mixer_sample(x, res, nw, sc1, sh1, w_in_bf, lb, cache_k, cache_v, bias, s0_t, g_att, g_rec, w_out_bf,
                  g1, nf, sc2, sh2, wr2, br, rows, row0):
    b, l, d = x.shape
    wc = cache_k.shape[1]
    lk = bias.shape[2]
    has_res = res is not None
    row = pl.BlockSpec((1, l, d), lambda i: (i, 0, 0))
    half = pl.BlockSpec((1, l, ATT_W), lambda i: (i, 0, 0))
    old = pl.BlockSpec((1, wc, ATT_W), lambda i: (i, 0, 0))
    per_b = pl.BlockSpec((1, 1, d), lambda i: (i, 0, 0))
    const = lambda shp: pl.BlockSpec(shp, lambda i: (0,) * len(shp))
    state = pl.BlockSpec((1, REC_HEADS, REC_DIM, REC_DIM), lambda i: (i, 0, 0, 0))
    in_specs, args = [row], [x]
    if has_res:
        yg, gate, g2, first = res
        blk0 = first // l
        in_specs += [pl.BlockSpec((TOP_K, l, d // 2), lambda i: (0, blk0 + i, 0)),
                     pl.BlockSpec((1, l, TOP_K), lambda i: (i, 0, 0)), per_b]
        args += [yg, gate, g2]
    in_specs += [const((1, d)), per_b, per_b, const(w_in_bf.shape), const((1, REC_W)), old, old, const(bias.shape),
                 state, const((1, ATT_W)), const((1, REC_W)), const(w_out_bf.shape), per_b, const((1, d)),
                 per_b, per_b, const(wr2.shape), const((N_EXPERTS, 1)), pl.BlockSpec(memory_space=pl.ANY)]
    args += [nw.reshape(1, d), sc1, sh1, w_in_bf, lb.reshape(1, REC_W), cache_k, cache_v, bias, s0_t,
             g_att.reshape(1, ATT_W), g_rec.reshape(1, REC_W), w_out_bf, g1, nf.reshape(1, d), sc2, sh2, wr2,
             br.reshape(N_EXPERTS, 1), rows]
    tile = lambda dt: pltpu.VMEM((l, ATT_W), dt)
    return pl.pallas_call(
        functools.partial(_sample_mixer_kernel, has_res),
        grid=(b,),
        in_specs=in_specs,
        out_specs=[row, pl.BlockSpec((l, d // 2), lambda i: (row0 // l + i, 0)),
                   pl.BlockSpec((1, N_EXPERTS, l), lambda i: (i, 0, 0)), half, half, state],
        out_shape=[jax.ShapeDtypeStruct((b, l, d), F32), jax.ShapeDtypeStruct(rows.shape, U32),
                   jax.ShapeDtypeStruct((b, N_EXPERTS, l), F32),
                   jax.ShapeDtypeStruct((b, l, ATT_W), F32), jax.ShapeDtypeStruct((b, l, ATT_W), F32),
                   jax.ShapeDtypeStruct((b, REC_HEADS, REC_DIM, REC_DIM), F32)],
        scratch_shapes=[tile(BF16), pltpu.VMEM((lk, ATT_W), BF16), pltpu.VMEM((lk, ATT_W), BF16),
                        pltpu.VMEM((ATT_HEADS // 2, 2 * l, lk), F32),
                        tile(F32), tile(F32), tile(F32), tile(BF16), tile(F32), tile(F32),
                        pltpu.VMEM((REC_HEADS, REC_DIM, REC_DIM), F32)],
        input_output_aliases={len(args) - 1: 1},
        compiler_params=_cparams("parallel"),
        name="sample_mixer",
    )(*args)


def _route_kernel(lg_ref, idx_ref, gate_ref, rank_ref, cnt_ref, run_ref):
    i = pl.program_id(0)
    t = ROUTE_TILE

    @pl.when(i == 0)
    def _():
        run_ref[...] = jnp.zeros_like(run_ref)

    row = lax.broadcasted_iota(I32, (N_EXPERTS, t), 0)
    earlier = (lax.broadcasted_iota(I32, (t, t), 0) < lax.broadcasted_iota(I32, (t, t), 1)).astype(BF16)
    run = run_ref[:, 0:1]
    for sb in range(lg_ref.shape[1] // t):
        cols = slice(sb * t, (sb + 1) * t)
        x = lg_ref[:, cols]
        vals, idxs, hits = [], [], []
        for _ in range(TOP_K):
            m = jnp.max(x, axis=0, keepdims=True)
            ik = jnp.min(jnp.where(x == m, row, N_EXPERTS), axis=0, keepdims=True)
            hit = row == ik
            x = jnp.where(hit, -jnp.inf, x)
            vals.append(m)
            idxs.append(ik)
            hits.append(hit)
        es = [jnp.exp(v - vals[0]) for v in vals]
        tot = es[0] + es[1] + es[2] + es[3]
        gate_ref[:, cols] = jnp.concatenate([e / tot for e in es] + [jnp.zeros((GATE_ROWS - TOP_K, t), F32)], axis=0)
        idx_ref[:, cols] = jnp.concatenate(idxs, axis=0)
        chosen = (hits[0] | hits[1] | hits[2] | hits[3])
        onehot = jnp.where(chosen, 1.0, 0.0)
        before = jnp.dot(onehot.astype(BF16), earlier, preferred_element_type=F32) + run
        rank_ref[:, cols] = jnp.concatenate(
            [jnp.sum(jnp.where(h, before, 0.0), axis=0, keepdims=True) for h in hits], axis=0).astype(I32)
        run = run + jnp.sum(onehot, axis=1, keepdims=True)
    run_ref[...] = jnp.broadcast_to(run, run_ref.shape)
    cnt_ref[...] = jnp.broadcast_to(run, cnt_ref.shape)


def _route(lg_t):
    e, n = lg_t.shape
    t = max(m * ROUTE_TILE for m in range(1, ROUTE_MAX_SUB + 1) if n % (m * ROUTE_TILE) == 0)
    tok = pl.BlockSpec((TOP_K, t), lambda i: (0, i))
    idx, gate, rank, cnt = pl.pallas_call(
        _route_kernel,
        grid=(n // t,),
        in_specs=[pl.BlockSpec((e, t), lambda i: (0, i))],
        out_specs=[tok, pl.BlockSpec((GATE_ROWS, t), lambda i: (0, i)), tok, pl.BlockSpec((e, LANES), lambda i: (0, 0))],
        out_shape=[jax.ShapeDtypeStruct((TOP_K, n), I32), jax.ShapeDtypeStruct((GATE_ROWS, n), F32),
                   jax.ShapeDtypeStruct((TOP_K, n), I32), jax.ShapeDtypeStruct((e, LANES), F32)],
        scratch_shapes=[pltpu.VMEM((e, LANES), F32)],
        compiler_params=_cparams("arbitrary"),
        name="moe_route",
    )(lg_t)
    counts = cnt[:, 0].astype(I32)
    padded = (counts + MOE_ROWS - 1) // MOE_ROWS * MOE_ROWS
    pad_end = jnp.cumsum(padded)
    pad_start = pad_end - padded
    m = n * TOP_K
    nb = (m + N_EXPERTS * (MOE_ROWS - 1) + MOE_ROWS - 1) // MOE_ROWS
    block_start = jnp.arange(nb, dtype=I32) * MOE_ROWS
    block_expert = jnp.minimum(jnp.sum((pad_end[None, :] <= block_start[:, None]).astype(I32), axis=1), N_EXPERTS - 1)
    experts = jnp.arange(N_EXPERTS, dtype=I32)
    in_block = block_start[:, None] - pad_start[None, :]
    mine = block_expert[:, None] == experts[None, :]
    filled = jnp.sum(jnp.where(mine, jnp.clip(counts[None, :] - in_block, 0, MOE_ROWS), 0), axis=1)
    owns = jnp.any(block_expert[:, None] == experts[None, :], axis=0)
    later = owns[None, :] & (experts[None, :] > block_expert[:, None])
    nxt = jnp.min(jnp.where(later, experts[None, :], N_EXPERTS), axis=1)
    nxt = jnp.where(nxt == N_EXPERTS, -1, nxt)
    run = jnp.sum((owns[None, :] & (experts[None, :] < block_expert[:, None])).astype(I32), axis=1)
    block_info = jnp.concatenate([block_expert, nxt, run % 2, filled]).astype(I32)
    start_of = sum(jnp.where(idx == e, pad_start[e], 0) for e in range(N_EXPERTS))
    dest = start_of + rank
    return gate, dest, block_info, nb * MOE_ROWS


def _windows(dest, w, multiple, fill):
    kk, n = dest.shape
    nwin = n // w
    per = -(-nwin // SC_WORKERS)
    per = -(-per // multiple) * multiple
    idx = dest.reshape(kk, nwin, w).transpose(1, 0, 2)
    pad = jnp.broadcast_to(fill[None, None, :], (SC_WORKERS * per - nwin, kk, w)).astype(I32)
    return jnp.concatenate([idx, pad], axis=0)


def _sc_scatter_rows(x, idx, p_rows):
    n, c = x.shape
    nwin, kk, w = idx.shape
    per = nwin // SC_WORKERS
    nreal = n // w
    mesh = plsc.VectorSubcoreMesh(core_axis_name="c", subcore_axis_name="s")

    @functools.partial(
        pl.kernel, mesh=mesh,
        out_type=jax.ShapeDtypeStruct((p_rows, c), x.dtype),
        scratch_types=[pltpu.VMEM((per, kk, w), I32), pltpu.VMEM((2, w, c), x.dtype),
                       pltpu.SemaphoreType.DMA((2,)), pltpu.SemaphoreType.DMA((2,))],
        name="sc_scatter_rows",
    )
    def k(x_hbm, idx_hbm, out_hbm, idx_v, rows_v, lsem, ssem):
        wid = lax.axis_index("s") * 2 + lax.axis_index("c")
        g0 = wid * per
        pltpu.sync_copy(idx_hbm.at[pl.ds(g0, per)], idx_v)

        def load(j, b):
            g = jnp.minimum(g0 + j, nreal - 1)
            return pltpu.make_async_copy(x_hbm.at[pl.ds(g * w, w)], rows_v.at[b], lsem.at[b])

        def scat(j, b, q):
            return pltpu.make_async_copy(rows_v.at[b], out_hbm.at[idx_v.at[j, q]], ssem.at[b])

        @pl.loop(0, per + 2, step=2)
        def _(j0):
            for b in range(2):
                j = j0 + b

                @pl.when(jnp.logical_and(j >= 2, j < per + 2))
                def _():
                    for q in range(kk):
                        scat(j - 2, b, q).wait()

                @pl.when(j < per)
                def _():
                    load(j, b).start()

                @pl.when(jnp.logical_and(j >= 1, j < per + 1))
                def _():
                    load(j - 1, 1 - b).wait()
                    for q in range(kk):
                        scat(j - 1, 1 - b, q).start()

    return k(x, idx)


def _sc_gather_rows(y, idx):
    p, c = y.shape
    kk, n = idx.shape
    w = SC_GATHER_WIN
    nb = SC_GATHER_BUFS
    wins = LANES // w
    group = wins * kk
    assert group % nb == 0 and group >= nb and n % LANES == 0
    n_units = n // LANES
    max_units = -(-n_units // SC_WORKERS)
    mesh = plsc.VectorSubcoreMesh(core_axis_name="c", subcore_axis_name="s")

    @functools.partial(
        pl.kernel, mesh=mesh,
        out_type=jax.ShapeDtypeStruct((kk, n, c), y.dtype),
        scratch_types=[pltpu.VMEM((2, kk, LANES), I32), pltpu.VMEM((nb, w, c), y.dtype),
                       pltpu.SemaphoreType.DMA((nb,)), pltpu.SemaphoreType.DMA((nb,))],
        name="sc_gather_rows",
    )
    def k(y_hbm, idx_hbm, out_hbm, idx_v, rows_v, gsem, wsem):
        wid = lax.axis_index("s") * 2 + lax.axis_index("c")
        my_units = (n_units - wid + SC_WORKERS - 1) // SC_WORKERS
        n_items = my_units * group

        def gath(slot, jj, q, b):
            return pltpu.make_async_copy(y_hbm.at[idx_v.at[slot, q, pl.ds(jj * w, w)]], rows_v.at[b], gsem.at[b])

        def wr(i, b):
            row = (wid + (i // group) * SC_WORKERS) * LANES + ((i % group) // kk) * w
            return pltpu.make_async_copy(rows_v.at[b], out_hbm.at[i % kk, pl.ds(row, w)], wsem.at[b])

        @pl.loop(0, max_units + 1)
        def _(gi):
            slot = gi % 2

            @pl.when(gi < my_units)
            def _():
                pltpu.sync_copy(idx_hbm.at[:, pl.ds((wid + gi * SC_WORKERS) * LANES, LANES)], idx_v.at[slot])

            for l in range(group):
                i = gi * group + l
                b = l % nb
                bw = (l - (nb - 1)) % nb

                @pl.when(jnp.logical_and(i >= nb, i < n_items + nb))
                def _():
                    wr(i - nb, b).wait()

                @pl.when(i < n_items)
                def _():
                    gath(slot, l // kk, l % kk, b).start()

                @pl.when(jnp.logical_and(i >= nb - 1, i < n_items + nb - 1))
                def _():
                    gath(slot, 0, 0, bw).wait()
                    wr(i - (nb - 1), bw).start()

    return k(y, idx)


def _moe_kernel(layer, info_ref, x_ref, w1_hbm, b1_ref, w2_hbm, b2_ref, o_ref,
                w1f_ref, w2f_ref, w1b_ref, w2b_ref, sem):
    i = pl.program_id(0)
    nb = pl.num_programs(0)
    expert = info_ref[i]
    nxt = info_ref[nb + i]
    slot = info_ref[2 * nb + i]
    first = jnp.logical_or(i == 0, expert != info_ref[jnp.maximum(i - 1, 0)])

    def fetch(e, s):
        return (pltpu.make_async_copy(w1_hbm.at[layer, e], w1f_ref.at[s], sem.at[0, s]),
                pltpu.make_async_copy(w2_hbm.at[layer, e], w2f_ref.at[s], sem.at[1, s]))

    @pl.when(i == 0)
    def _():
        for cp in fetch(expert, slot):
            cp.start()

    @pl.when(first)
    def _():
        for cp in fetch(expert, slot):
            cp.wait()

        @pl.when(nxt >= 0)
        def _():
            for cp in fetch(nxt, 1 - slot):
                cp.start()

        w1b_ref[...] = w1f_ref[slot].astype(BF16)
        w2b_ref[...] = w2f_ref[slot].astype(BF16)

    units = (info_ref[3 * nb + i] + MOE_UNIT - 1) // MOE_UNIT
    for u in range(1, x_ref.shape[0] // MOE_UNIT + 1):
        @pl.when(units == u)
        def _():
            rows = slice(0, u * MOE_UNIT)
            _expert_block(x_ref.at[rows], w1b_ref, b1_ref, w2b_ref, b2_ref, o_ref.at[rows])


def _expert_block(x_ref, w1b_ref, b1_ref, w2b_ref, b2_ref, o_ref):
    lo, hi = _unpack_pairs(x_ref[...])
    x = jnp.concatenate([lo.astype(BF16), hi.astype(BF16)], axis=-1)
    f = w2b_ref.shape[0]
    fc = 512
    acc = None
    for c in range(f // fc):
        gt = jnp.dot(x, w1b_ref[:, c * fc:(c + 1) * fc], preferred_element_type=F32) + b1_ref[0, 0, :, c * fc:(c + 1) * fc]
        up = jnp.dot(x, w1b_ref[:, f + c * fc:f + (c + 1) * fc], preferred_element_type=F32) + b1_ref[0, 0, :, f + c * fc:f + (c + 1) * fc]
        gt = jnp.minimum(gt, SWIGLU_LIMIT)
        up = jnp.clip(up, -SWIGLU_LIMIT, SWIGLU_LIMIT)
        act = (gt * jax.nn.sigmoid(SWIGLU_ALPHA * gt) * (up + 1.0)).astype(BF16)
        part = jnp.dot(act, w2b_ref[c * fc:(c + 1) * fc, :], preferred_element_type=F32)
        acc = part if acc is None else acc + part
    o_ref[...] = _pack_pairs(acc + b2_ref[0, 0])


def _moe_experts(layer, block_expert, xs, p_rows, w1, b1, w2, b2):
    _, e, d, f2 = w1.shape
    f = f2 // 2
    nb = p_rows // MOE_ROWS
    grid_spec = pltpu.PrefetchScalarGridSpec(
        num_scalar_prefetch=1,
        grid=(nb,),
        in_specs=[pl.BlockSpec((MOE_ROWS, d // 2), lambda i, be: (i, 0)),
                  pl.BlockSpec(memory_space=pl.ANY),
                  pl.BlockSpec((1, 1, 1, f2), lambda i, be: (layer, be[i], 0, 0)),
                  pl.BlockSpec(memory_space=pl.ANY),
                  pl.BlockSpec((1, 1, 1, d), lambda i, be: (layer, be[i], 0, 0))],
        out_specs=pl.BlockSpec((MOE_ROWS, d // 2), lambda i, be: (i, 0)),
        scratch_shapes=[pltpu.VMEM((2, d, f2), F32), pltpu.VMEM((2, f, d), F32),
                        pltpu.VMEM((d, f2), BF16), pltpu.VMEM((f, d), BF16),
                        pltpu.SemaphoreType.DMA((2, 2))],
    )
    return pl.pallas_call(
        functools.partial(_moe_kernel, layer),
        grid_spec=grid_spec,
        out_shape=jax.ShapeDtypeStruct((p_rows, d // 2), U32),
        compiler_params=_cparams("arbitrary"),
        name="moe_experts",
    )(block_expert, xs, w1, b1.reshape(b1.shape[0], e, 1, f2), w2, b2.reshape(b2.shape[0], e, 1, d))


def _moe_dispatch(h2_words, lg_t):
    gate, dest, block_info, p_rows = _route(lg_t)
    spare = p_rows + jnp.arange(SC_SCATTER_WIN, dtype=I32)
    xs = _sc_scatter_rows(h2_words, _windows(dest, SC_SCATTER_WIN, 2, spare), p_rows + SC_SCATTER_WIN)
    return xs, gate, dest, block_info, p_rows


def _moe_compute(layer, dispatched, w1, b1, w2, b2):
    xs, gate, back, block_info, p_rows = dispatched
    ys = _moe_experts(layer, block_info, xs, p_rows, w1, b1, w2, b2)
    return _sc_gather_rows(ys, back), gate


def _final_kernel(x_ref, yg_ref, gate_ref, g2_ref, nw_ref, *rest):
    o_ref = rest[-1]
    x = x_ref[0] + g2_ref[0] * _combine(yg_ref, gate_ref)
    ms = jnp.mean(x * x, axis=-1, keepdims=True)
    o_ref[0] = x * lax.rsqrt(ms + EPS) * nw_ref[...]


def _final(x, res, nw, ts, out_buf, ob0, b_total):
    b, s, d = x.shape
    nj = s // ts
    yg, gate, g2, first = res
    blk0 = first // ts
    row = pl.BlockSpec((1, ts, d), lambda i, j: (i, j, 0))
    if gate.ndim == 2:
        gate_spec = pl.BlockSpec((GATE_ROWS, ts), lambda i, j: (0, blk0 + i * nj + j))
    else:
        gate_spec = pl.BlockSpec((1, ts, TOP_K), lambda i, j: (i, j, 0))
    in_specs = [row, pl.BlockSpec((TOP_K, ts, d // 2), lambda i, j: (0, blk0 + i * nj + j, 0)),
                gate_spec,
                pl.BlockSpec((1, 1, d), lambda i, j: (i, 0, 0)),
                pl.BlockSpec((1, d), lambda i, j: (0, 0))]
    args = [x, yg, gate, g2, nw.reshape(1, d)]
    aliases = {}
    if out_buf is not None:
        in_specs.append(pl.BlockSpec(memory_space=pl.ANY))
        args.append(out_buf)
        aliases = {len(args) - 1: 0}
    return pl.pallas_call(
        _final_kernel,
        grid=(b, nj),
        in_specs=in_specs,
        out_specs=pl.BlockSpec((1, ts, d), lambda i, j: (i + ob0, j, 0)),
        out_shape=jax.ShapeDtypeStruct((b_total, s, d), F32),
        input_output_aliases=aliases,
        compiler_params=_cparams("parallel", "parallel"),
        name="final_norm",
    )(*args)


def kernel(x_prompt, x_sample, c_prompt, c_sample, cache_k, cache_v, state_hgrn, norm_mix, norm_ffn, norm_final, w_ada, b_ada, w_in, rel_bias, hgrn_lb_logits, g_attn_out, g_hgrn_out, w_out, w_router, b_router, w_e1, b_e1, w_e2, b_e2):
    depth = w_in.shape[0]
    bp, sp, d = x_prompt.shape
    bs, ls, _ = x_sample.shape
    wc = cache_k.shape[2]
    n_p, n_s = bp * sp, bs * ls

    lbs = jax.nn.softmax(hgrn_lb_logits.astype(F32), axis=0)
    lbs = jnp.cumsum(lbs, axis=0) - lbs[0]

    mods = _modulation(jnp.concatenate([c_prompt, c_sample], axis=0), w_ada, b_ada)
    lk_s = -(-(wc + ls) // LANES) * LANES
    tail = min(LEFT, sp)
    rec_chunk = min(REC_CHUNK, sp)

    n_groups = PROMPT_GROUPS if bp % PROMPT_GROUPS == 0 else 1
    gb = bp // n_groups
    n_g = gb * sp
    x_g = [(x_prompt, g * gb) for g in range(n_groups)]
    x_s = x_sample
    res_g = [None] * n_groups
    res_s = None
    kp_l, vp_l, sp_l, ks_l, vs_l, ss_l = [], [], [], [], [], []
    for layer in range(depth):
        m6 = mods[layer].reshape(bp + bs, 1, N_MOD, d)
        mod_s = [m6[bp:, :, n, :] for n in range(N_MOD)]
        w_in_bf = w_in[layer].astype(BF16)
        w_out_bf = w_out[layer].astype(BF16)
        wr_hi, wr_mid, _ = _split3(w_router[layer].T)
        wr2 = jnp.concatenate([wr_hi, wr_mid], axis=0)
        table = rel_bias[layer]
        bias_p = _prompt_bias(table)

        def mixer_prompt(g):
            mod = [m6[g * gb:(g + 1) * gb, :, n, :] for n in range(N_MOD)]
            x_new, h2, lg, kt, vt, s_fin = _mixer_prompt(
                x_g[g][0], x_g[g][1], res_g[g], norm_mix[layer], mod[1], mod[0], w_in_bf, lbs[layer], bias_p,
                g_attn_out[layer], g_hgrn_out[layer], w_out_bf, mod[2], norm_ffn[layer], mod[4], mod[3],
                wr2, b_router[layer], tail, rec_chunk, n_s if g == n_groups - 1 else 0)
            x_g[g] = (x_new, 0)
            return h2, lg, kt, vt, s_fin, mod[5]

        def mixer_sample(rows):
            s0 = jnp.swapaxes(state_hgrn[layer].astype(F32), -1, -2)
            x_new, rows, lg, kt, vt, s_new = _mixer_sample(
                x_s, res_s, norm_mix[layer], mod_s[1], mod_s[0], w_in_bf, lbs[layer],
                cache_k[layer].reshape(bs, wc, ATT_W), cache_v[layer].reshape(bs, wc, ATT_W),
                _sample_bias(table, ls, wc, lk_s), s0, g_attn_out[layer], g_hgrn_out[layer], w_out_bf,
                mod_s[2], norm_ffn[layer], mod_s[4], mod_s[3], wr2, b_router[layer], rows, n_g)
            return x_new, rows, jnp.swapaxes(lg, 0, 1).reshape(N_EXPERTS, n_s), kt, vt, s_new

        dispatched, g2s, k_t, v_t, s_t = [], [], [], [], []
        for g in range(n_groups):
            h2, lg, kt, vt, s_fin, g2 = mixer_prompt(g)
            if g == n_groups - 1:
                x_s, h2, lg_s, ka_s, va_s, s_new = mixer_sample(h2)
                ks_l.append(ka_s.reshape(bs, ls, ATT_HEADS, ATT_DIM))
                vs_l.append(va_s.reshape(bs, ls, ATT_HEADS, ATT_DIM))
                ss_l.append(jnp.swapaxes(s_new, -1, -2))
                lg = jnp.concatenate([lg, lg_s], axis=1)
            dispatched.append(_moe_dispatch(h2, lg))
            g2s.append(g2)
            k_t.append(kt)
            v_t.append(vt)
            s_t.append(s_fin)
        kp_l.append(jnp.concatenate(k_t, axis=0).reshape(bp, tail, ATT_HEADS, ATT_DIM))
        vp_l.append(jnp.concatenate(v_t, axis=0).reshape(bp, tail, ATT_HEADS, ATT_DIM))
        sp_l.append(jnp.swapaxes(jnp.concatenate(s_t, axis=0), -1, -2))

        for g in range(n_groups):
            yg, gate = _moe_compute(layer, dispatched[g], w_e1, b_e1, w_e2, b_e2)
            res_g[g] = (yg, gate, g2s[g], 0)
            if g == n_groups - 1:
                res_s = (yg, gate[:TOP_K, n_g:].T.reshape(bs, ls, TOP_K), mod_s[5], n_g)

    y_prompt = None
    for g in range(n_groups):
        y_prompt = _final(x_g[g][0], res_g[g], norm_final, SEQ_TILE, y_prompt, g * gb, bp)
    y_sample = _final(x_s, res_s, norm_final, ls, None, 0, bs)
    return (y_prompt, y_sample, jnp.stack(kp_l), jnp.stack(vp_l), jnp.stack(sp_l),
            jnp.stack(ks_l), jnp.stack(vs_l), jnp.stack(ss_l))
```

```python
import functools

import jax
import jax.numpy as jnp
import numpy as np
from jax import lax
from jax.experimental import pallas as pl
from jax.experimental.pallas import tpu as pltpu
from jax.experimental.pallas import tpu_sc as plsc

F32 = jnp.float32
BF16 = jnp.bfloat16
I32 = jnp.int32
U32 = jnp.uint32

EPS = 1e-6
CHUNK = 64
LEFT_CHUNKS = 8
LEFT = LEFT_CHUNKS * CHUNK
MAX_REL = 2 * CHUNK
ATT_HEADS = 8
ATT_DIM = 64
ATT_W = ATT_HEADS * ATT_DIM
REC_HEADS = 4
REC_DIM = 128
REC_W = REC_HEADS * REC_DIM
N_EXPERTS = 32
TOP_K = 4
SWIGLU_ALPHA = 1.702
SWIGLU_LIMIT = 7.0
N_MOD = 6
NEG = -1e30
LOG2E = 1.4426950408889634

LANES = 128
ATT_QSUB = 128
ATT_WIN = LEFT + ATT_QSUB
SEQ_TILE = 1024
REC_CHUNK = 128
MOE_ROWS = 1024
MOE_UNIT = 256
ROUTE_TILE = 256
ROUTE_MAX_SUB = 8
PROMPT_GROUPS = 2
GATE_ROWS = 8
VMEM_LIMIT = 56 * 1024 * 1024
SC_WORKERS = 32
SC_SCATTER_WIN = 64
SC_GATHER_WIN = 32
SC_GATHER_BUFS = 4


def _cparams(*sem):
    return pltpu.CompilerParams(dimension_semantics=sem, vmem_limit_bytes=VMEM_LIMIT)


def _split3(x):
    hi = x.astype(BF16)
    r1 = x - hi.astype(F32)
    mid = r1.astype(BF16)
    lo = (r1 - mid.astype(F32)).astype(BF16)
    return hi, mid, lo


def _bits(x):
    return lax.bitcast_convert_type(x, U32)


def _pack_pairs(x):
    r = _bits(x.astype(BF16).astype(F32))
    half = x.shape[1] // 2
    return jnp.bitwise_or(jnp.bitwise_and(r[:, half:], jnp.uint32(0xFFFF0000)),
                          jnp.right_shift(r[:, :half], jnp.uint32(16)))


def _unpack_pairs(word):
    lo = lax.bitcast_convert_type(jnp.left_shift(word, jnp.uint32(16)), F32)
    hi = lax.bitcast_convert_type(jnp.bitwise_and(word, jnp.uint32(0xFFFF0000)), F32)
    return lo, hi


def _mod_kernel(c_ref, w_ref, b_ref, o_ref):
    c = c_ref[...]
    act = (c * jax.nn.sigmoid(c)).astype(BF16)
    o_ref[0] = jnp.dot(act, w_ref[0].astype(BF16), preferred_element_type=F32) + b_ref[0]


def _modulation(c_all, w_ada, b_ada):
    depth, d, n6 = w_ada.shape
    rows = c_all.shape[0]
    tn = 1536
    return pl.pallas_call(
        _mod_kernel,
        grid=(depth, n6 // tn),
        in_specs=[pl.BlockSpec((rows, d), lambda l, j: (0, 0)),
                  pl.BlockSpec((1, d, tn), lambda l, j: (l, 0, j)),
                  pl.BlockSpec((1, 1, tn), lambda l, j: (l, 0, j))],
        out_specs=pl.BlockSpec((1, rows, tn), lambda l, j: (l, 0, j)),
        out_shape=jax.ShapeDtypeStruct((depth, rows, n6), F32),
        compiler_params=_cparams("parallel", "parallel"),
        name="adaln_modulation",
    )(c_all, w_ada, b_ada.reshape(depth, 1, n6))


def _combine(yg_ref, gate_ref):
    g = gate_ref[0] if len(gate_ref.shape) == 3 else gate_ref[...].T
    lo = hi = None
    for k in range(TOP_K):
        l, h = _unpack_pairs(yg_ref[k])
        gk = g[:, k:k + 1]
        lo = l * gk if lo is None else lo + l * gk
        hi = h * gk if hi is None else hi + h * gk
    return jnp.concatenate([lo, hi], axis=-1)


def _project_rows(x, nw_ref, sc_ref, sh_ref, w_ref, lb_ref,
                  qa_ref, ka_ref, va_ref, kt_ref, vt_ref, qh_ref, kh_ref, lf_ref, vh_ref, gh_ref):
    ms = jnp.mean(x * x, axis=-1, keepdims=True)
    h = (x * lax.rsqrt(ms + EPS) * nw_ref[...]) * (1.0 + sc_ref[0]) + sh_ref[0]
    hb = h.astype(BF16)

    def proj(g):
        return jnp.dot(hb, w_ref[:, g * ATT_W:(g + 1) * ATT_W], preferred_element_type=F32)

    qa_ref[...] = (proj(0) * (ATT_DIM ** -0.5)).astype(BF16)
    k = proj(1)
    ka_ref[...] = k.astype(BF16)
    kt_ref[...] = k
    v = proj(2)
    va_ref[...] = v.astype(BF16)
    vt_ref[...] = v
    qh = proj(3)
    qh_ref[...] = qh * jax.nn.sigmoid(qh)
    f = proj(4)
    lb = lb_ref[...]
    lf_ref[...] = jnp.log(lb + (1.0 - lb) * jax.nn.sigmoid(f))
    kh_ref[...] = (1.0 - lb) * jax.nn.sigmoid(-f)
    vh_ref[...] = proj(5).astype(BF16)
    gh_ref[...] = proj(6)


def _attend(q, kw, vw, bias_ref, s_ref, col_thr):
    lq, lk = q.shape[0], kw.shape[0]
    lane = lax.broadcasted_iota(I32, (lq, LANES), 1)
    lo = lane < ATT_DIM
    nt = (((1,), (1,)), ((), ()))
    if col_thr is not None:
        keep = lax.broadcasted_iota(I32, (2 * lq, lk), 1) >= col_thr
    ms = []
    for p in range(ATT_HEADS // 2):
        sl = slice(p * LANES, (p + 1) * LANES)
        qp = q[:, sl]
        z = jnp.zeros_like(qp)
        lhs = jnp.concatenate([jnp.where(lo, qp, z), jnp.where(lo, z, qp)], axis=0)
        s = lax.dot_general(lhs, kw[:, sl], nt, preferred_element_type=F32) + bias_ref[p]
        if col_thr is not None:
            s = jnp.where(keep, s, NEG)
        ms.append(jnp.max(s, axis=-1, keepdims=True))
        s_ref[p] = s
    outs = []
    for p in range(ATT_HEADS // 2):
        sl = slice(p * LANES, (p + 1) * LANES)
        e = jnp.exp(s_ref[p] - ms[p])
        l = jnp.sum(e, axis=-1, keepdims=True)
        o = jnp.dot(e.astype(BF16), vw[:, sl], preferred_element_type=F32) / l
        outs.append(jnp.where(lo, o[:lq], o[lq:]))
    return jnp.concatenate(outs, axis=-1)


def _band_attention(q_ref, kw_ref, vw_ref, bias_ref, s_ref, o_ref, first_tile):
    ts = q_ref.shape[0]

    def body(masked, r, carry):
        r0 = pl.multiple_of(r * ATT_QSUB, ATT_QSUB)
        q = q_ref[pl.ds(r0, ATT_QSUB), :]
        kw = kw_ref[pl.ds(r0, ATT_WIN), :]
        vw = vw_ref[pl.ds(r0, ATT_WIN), :]
        o_ref[pl.ds(r0, ATT_QSUB), :] = _attend(q, kw, vw, bias_ref, s_ref, (ts - r0) if masked else None)
        return carry

    @pl.when(first_tile)
    def _():
        lax.fori_loop(0, ts // ATT_QSUB, functools.partial(body, True), 0)

    @pl.when(jnp.logical_not(first_tile))
    def _():
        lax.fori_loop(0, ts // ATT_QSUB, functools.partial(body, False), 0)


def _toeplitz_bias(table, lq, lk, offset, valid):
    m = lq + lk - 1
    k = np.arange(m)
    diff = np.where(k < lk, k, k - m)
    g = table[:, np.clip(offset - diff, -MAX_REL, MAX_REL) + MAX_REL]
    skew = jnp.tile(g, (1, lq))[:, :lq * (m - 1)].reshape(-1, lq, m - 1)[:, :, :lk]
    bias = jnp.where(jnp.asarray(valid)[None], skew, NEG).astype(F32)
    return bias.reshape(ATT_HEADS // 2, 2 * lq, lk)


def _prompt_bias(table):
    t = np.arange(ATT_QSUB)[:, None]
    j = np.arange(ATT_WIN)[None, :]
    start = (t // CHUNK) * CHUNK
    return _toeplitz_bias(table, ATT_QSUB, ATT_WIN, LEFT, (j >= start) & (j < start + LEFT + CHUNK))


def _sample_bias(table, l, w, lk):
    valid = np.broadcast_to(np.arange(lk)[None, :] < w + l, (l, lk))
    return _toeplitz_bias(table, l, lk, w, valid)


def _neg_abs(x):
    return lax.bitcast_convert_type(jnp.bitwise_or(_bits(x), jnp.uint32(0x80000000)), F32)


def _hgrn_rows(chunk, q_ref, k_ref, lf_ref, v_ref, o_ref, st_ref):
    ts = q_ref.shape[0]
    n_chunks = ts // chunk

    row = lax.broadcasted_iota(I32, (chunk, chunk), 0)
    colm = lax.broadcasted_iota(I32, (chunk, chunk), 1)
    tri = (colm <= row).astype(BF16)
    xor = jnp.bitwise_xor(row, colm)
    lower = colm < row
    rid = lax.broadcasted_iota(I32, (chunk, REC_W), 0)

    levels = []
    b = chunk // 2
    while b >= 1:
        levels.append(b)
        b //= 2
    pair_mask = [xor == 0] + [lower & (xor >= b) & (xor < 2 * b) for b in levels]
    upper = [jnp.bitwise_and(rid, b) != 0 for b in levels]
    ph4 = jnp.bitwise_and(rid, 3)
    odd = jnp.bitwise_and(rid, 1) == 1
    nt = (((1,), (1,)), ((), ()))
    tn = (((0,), (0,)), ((), ()))
    sls = [slice(h * REC_DIM, (h + 1) * REC_DIM) for h in range(REC_HEADS)]

    def boundary(cum, b):
        if 2 * b >= 8:
            pieces = [jnp.broadcast_to(cum[m * 2 * b + b - 1:m * 2 * b + b, :], (2 * b, REC_W))
                      for m in range(chunk // (2 * b))]
            return pieces[0] if len(pieces) == 1 else jnp.concatenate(pieces, axis=0)
        if b == 2:
            up1 = pltpu.roll(cum, chunk - 1, 0)
            dn1 = pltpu.roll(cum, 1, 0)
            dn2 = pltpu.roll(cum, 2, 0)
            return jnp.where(ph4 == 0, up1, jnp.where(ph4 == 1, cum, jnp.where(ph4 == 2, dn1, dn2)))
        return jnp.where(odd, pltpu.roll(cum, 1, 0), cum)

    def one_chunk(c, carry):
        r0 = pl.multiple_of(c * chunk, chunk)
        rows = pl.ds(r0, chunk)
        q = q_ref[rows, :]
        k = k_ref[rows, :]
        v = v_ref[rows, :]
        lf = lf_ref[rows, :]
        cum = sum(jnp.dot(tri, part, preferred_element_type=F32) for part in _split3(lf)) * LOG2E
        last = cum[chunk - 1:chunk, :]
        q_in = (q * jnp.exp2(cum)).astype(BF16)
        k_out = (k * jnp.exp2(last - cum)).astype(BF16)
        decay = jnp.exp2(last)
        qb = q.astype(BF16)
        kb = k.astype(BF16)
        zs = []
        for n, b in enumerate(levels):
            w = jnp.exp2(_neg_abs(cum - boundary(cum, b)))
            zs.append((jnp.where(upper[n], q, k) * w).astype(BF16))
        sts = [st_ref[h] for h in range(REC_HEADS)]
        d0 = [lax.dot_general(qb[:, sl], kb[:, sl], nt, preferred_element_type=F32) for sl in sls]
        dl = [[lax.dot_general(z[:, sl], z[:, sl], nt, preferred_element_type=F32) for z in zs] for sl in sls]
        oi = [lax.dot_general(q_in[:, sl], st.astype(BF16), nt, preferred_element_type=F32)
              for sl, st in zip(sls, sts)]
        upd = [lax.dot_general(v[:, sl], k_out[:, sl], tn, preferred_element_type=F32) for sl in sls]
        outs = []
        for h, sl in enumerate(sls):
            a = jnp.where(pair_mask[0], d0[h], 0.0)
            for n in range(len(levels)):
                a = jnp.where(pair_mask[n + 1], dl[h][n], a)
            outs.append(jnp.dot(a.astype(BF16), v[:, sl], preferred_element_type=F32) + oi[h])
        o_ref[rows, :] = jnp.concatenate(outs, axis=-1)
        for h, sl in enumerate(sls):
            st_ref[h] = decay[:, sl] * sts[h] + upd[h]
        return carry

    lax.fori_loop(0, n_chunks, one_chunk, 0, unroll=2 if n_chunks % 2 == 0 else 1)


def _head_rms(x, width):
    lane = lax.broadcasted_iota(I32, (x.shape[0], LANES), 1)
    lo = lane < ATT_DIM
    outs = []
    for p in range(x.shape[1] // LANES):
        xp = x[:, p * LANES:(p + 1) * LANES]
        sq = xp * xp
        tot = jnp.sum(sq, axis=-1, keepdims=True)
        if width == LANES:
            scale = lax.rsqrt(tot * (1.0 / LANES) + EPS)
        else:
            s_lo = jnp.sum(jnp.where(lo, sq, 0.0), axis=-1, keepdims=True)
            scale = jnp.where(lo, lax.rsqrt(s_lo * (1.0 / ATT_DIM) + EPS),
                              lax.rsqrt((tot - s_lo) * (1.0 / ATT_DIM) + EPS))
        outs.append(xp * scale)
    return jnp.concatenate(outs, axis=-1)


def _merge_rows(oa, orr, gh, x, ga_ref, gr_ref, wo_ref, g1_ref, nf_ref, sc_ref, sh_ref, wr_ref, br_ref):
    a = _head_rms(oa, ATT_DIM) * ga_ref[...]
    r = _head_rms(orr, REC_DIM) * gr_ref[...] * (gh * jax.nn.sigmoid(gh))
    mix = jnp.dot(a.astype(BF16), wo_ref[0:ATT_W, :], preferred_element_type=F32)
    mix += jnp.dot(r.astype(BF16), wo_ref[ATT_W:ATT_W + REC_W, :], preferred_element_type=F32)
    x = x + g1_ref[0] * mix
    ms = jnp.mean(x * x, axis=-1, keepdims=True)
    h2 = (x * lax.rsqrt(ms + EPS) * nf_ref[...]) * (1.0 + sc_ref[0]) + sh_ref[0]
    hh = h2.astype(BF16)
    words = _pack_pairs(h2)
    hm = (h2 - hh.astype(F32)).astype(BF16)
    nt = (((1,), (1,)), ((), ()))
    l1 = lax.dot_general(wr_ref[...], hh, nt, preferred_element_type=F32)
    l2 = lax.dot_general(wr_ref[0:N_EXPERTS, :], hm, nt, preferred_element_type=F32)
    lg = l1[0:N_EXPERTS] + l1[N_EXPERTS:2 * N_EXPERTS] + l2 + br_ref[...]
    return x, words, lg


def _mixer_kernel(has_res, chunk, *refs):
    n_in = 4 if has_res else 1
    x_ref = refs[0]
    (nw_ref, sc1_ref, sh1_ref, lb_ref, ga_ref, gr_ref, g1_ref, nf_ref, sc2_ref, sh2_ref, wr_ref, br_ref,
     win_hbm, wout_hbm, bias_hbm,
     xo_ref, h2_ref, lg_ref, kt_ref, vt_ref, sf_ref,
     win_v, wout_v, bias_v, q_s, kw_s, vw_s, s_s, qh_s, kh_s, lf_s, vh_s, gh_s, oa_s, or_s, st_s, sem) = refs[n_in:]
    j = pl.program_id(1)
    ts = x_ref.shape[1]

    @pl.when(j == 0)
    def _():
        copies = [pltpu.make_async_copy(src, dst, sem.at[n])
                  for n, (src, dst) in enumerate(((win_hbm, win_v), (wout_hbm, wout_v), (bias_hbm, bias_v)))]
        for cp in copies:
            cp.start()
        for cp in copies:
            cp.wait()
        st_s[...] = jnp.zeros_like(st_s)
        kw_s[0:ts, :] = jnp.zeros((ts, ATT_W), BF16)
        vw_s[0:ts, :] = jnp.zeros((ts, ATT_W), BF16)

    if has_res:
        yg_ref, gate_ref, g2_ref = refs[1:4]
        x = x_ref[0] + g2_ref[0] * _combine(yg_ref, gate_ref)
        xo_ref[0] = x
    else:
        x = x_ref[0]
    cur = pl.ds(ts, ts)
    _project_rows(x, nw_ref, sc1_ref, sh1_ref, win_v, lb_ref, q_s, kw_s.at[cur], vw_s.at[cur],
                  kt_ref.at[0], vt_ref.at[0], qh_s, kh_s, lf_s, vh_s, gh_s)
    _band_attention(q_s, kw_s, vw_s, bias_v, s_s, oa_s, j == 0)
    _hgrn_rows(chunk, qh_s, kh_s, lf_s, vh_s, or_s, st_s)
    x_in = xo_ref[0] if has_res else x_ref[0]
    x_new, words, lg = _merge_rows(oa_s[...], or_s[...], gh_s[...], x_in, ga_ref, gr_ref, wout_v, g1_ref,
                                   nf_ref, sc2_ref, sh2_ref, wr_ref, br_ref)
    xo_ref[0] = x_new
    h2_ref[...] = words
    lg_ref[...] = lg
    kw_s[0:ts, :] = kw_s[ts:2 * ts, :]
    vw_s[0:ts, :] = vw_s[ts:2 * ts, :]

    @pl.when(j == pl.num_programs(1) - 1)
    def _():
        sf_ref[0] = st_s[...]


def _mixer_prompt(x, xb0, res, nw, sc1, sh1, w_in_bf, lb, bias, g_att, g_rec, w_out_bf, g1, nf, sc2, sh2,
                  wr2, br, tail, chunk, extra_rows):
    _, s, d = x.shape
    b = sc1.shape[0]
    ts = LEFT
    nj = s // ts
    n_tail = tail // ts
    has_res = res is not None
    row = pl.BlockSpec((1, ts, d), lambda i, j: (i, j, 0))
    per_b = pl.BlockSpec((1, 1, d), lambda i, j: (i, 0, 0))
    const = lambda shp: pl.BlockSpec(shp, lambda i, j: (0,) * len(shp))
    hbm = pl.BlockSpec(memory_space=pl.ANY)
    in_specs, args = [pl.BlockSpec((1, ts, d), lambda i, j: (i + xb0, j, 0))], [x]
    if has_res:
        yg, gate, g2, first = res
        blk0 = first // ts
        in_specs += [pl.BlockSpec((TOP_K, ts, d // 2), lambda i, j: (0, blk0 + i * nj + j, 0)),
                     pl.BlockSpec((GATE_ROWS, ts), lambda i, j: (0, blk0 + i * nj + j)), per_b]
        args += [yg, gate, g2]
    in_specs += [const((1, d)), per_b, per_b, const((1, REC_W)), const((1, ATT_W)), const((1, REC_W)),
                 per_b, const((1, d)), per_b, per_b, const(wr2.shape), const((N_EXPERTS, 1)), hbm, hbm, hbm]
    args += [nw.reshape(1, d), sc1, sh1, lb.reshape(1, REC_W), g_att.reshape(1, ATT_W), g_rec.reshape(1, REC_W),
             g1, nf.reshape(1, d), sc2, sh2, wr2, br.reshape(N_EXPERTS, 1), w_in_bf, w_out_bf, bias]
    tail_spec = pl.BlockSpec((1, ts, ATT_W), lambda i, j: (i, jnp.maximum(j - (nj - n_tail), 0), 0))
    state_spec = pl.BlockSpec((1, REC_HEADS, REC_DIM, REC_DIM), lambda i, j: (i, 0, 0, 0))
    tile = lambda dt: pltpu.VMEM((ts, ATT_W), dt)
    return pl.pallas_call(
        functools.partial(_mixer_kernel, has_res, chunk),
        grid=(b, nj),
        in_specs=in_specs,
        out_specs=[row, pl.BlockSpec((ts, d // 2), lambda i, j: (i * nj + j, 0)),
                   pl.BlockSpec((N_EXPERTS, ts), lambda i, j: (0, i * nj + j)), tail_spec, tail_spec, state_spec],
        out_shape=[jax.ShapeDtypeStruct((b, s, d), F32), jax.ShapeDtypeStruct((b * s + extra_rows, d // 2), U32),
                   jax.ShapeDtypeStruct((N_EXPERTS, b * s), F32),
                   jax.ShapeDtypeStruct((b, tail, ATT_W), F32), jax.ShapeDtypeStruct((b, tail, ATT_W), F32),
                   jax.ShapeDtypeStruct((b, REC_HEADS, REC_DIM, REC_DIM), F32)],
        scratch_shapes=[pltpu.VMEM(w_in_bf.shape, BF16), pltpu.VMEM(w_out_bf.shape, BF16), pltpu.VMEM(bias.shape, F32),
                        tile(BF16), pltpu.VMEM((2 * ts, ATT_W), BF16), pltpu.VMEM((2 * ts, ATT_W), BF16),
                        pltpu.VMEM((ATT_HEADS // 2, 2 * ATT_QSUB, ATT_WIN), F32),
                        tile(F32), tile(F32), tile(F32), tile(BF16), tile(F32), tile(F32), tile(F32),
                        pltpu.VMEM((REC_HEADS, REC_DIM, REC_DIM), F32), pltpu.SemaphoreType.DMA((3,))],
        compiler_params=_cparams("parallel", "arbitrary"),
        name="prompt_mixer",
    )(*args)


def _sample_mixer_kernel(has_res, *refs):
    n_in = 4 if has_res else 1
    x_ref = refs[0]
    (nw_ref, sc1_ref, sh1_ref, w_ref, lb_ref, ck_ref, cv_ref, bias_ref, s0_ref, ga_ref, gr_ref, wo_ref,
     g1_ref, nf_ref, sc2_ref, sh2_ref, wr_ref, br_ref, rows_hbm,
     xo_ref, h2_ref, lg_ref, kt_ref, vt_ref, sf_ref,
     q_s, kw_s, vw_s, s_s, qh_s, kh_s, lf_s, vh_s, gh_s, or_s, st_s) = refs[n_in:]
    del rows_hbm
    if has_res:
        yg_ref, gate_ref, g2_ref = refs[1:4]
        x = x_ref[0] + g2_ref[0] * _combine(yg_ref, gate_ref)
    else:
        x = x_ref[0]
    wc, l, lk = ck_ref.shape[1], x_ref.shape[1], kw_s.shape[0]
    new = pl.ds(wc, l)
    _project_rows(x, nw_ref, sc1_ref, sh1_ref, w_ref, lb_ref, q_s, kw_s.at[new], vw_s.at[new],
                  kt_ref.at[0], vt_ref.at[0], qh_s, kh_s, lf_s, vh_s, gh_s)
    for dst, cache in ((kw_s, ck_ref), (vw_s, cv_ref)):
        dst[0:wc, :] = cache[0].astype(BF16)
        dst[wc + l:lk, :] = jnp.zeros((lk - wc - l, ATT_W), BF16)
    oa = _attend(q_s[...], kw_s[...], vw_s[...], bias_ref, s_s, None)
    st_s[...] = s0_ref[0]
    _hgrn_rows(l, qh_s, kh_s, lf_s, vh_s, or_s, st_s)
    sf_ref[0] = st_s[...]
    x_new, words, lg = _merge_rows(oa, or_s[...], gh_s[...], x, ga_ref, gr_ref, wo_ref, g1_ref,
                                   nf_ref, sc2_ref, sh2_ref, wr_ref, br_ref)
    xo_ref[0] = x_new
    h2_ref[...] = words
    lg_ref[0] = lg


def _mixer_sample(x, res, nw, sc1, sh1, w_in_bf, lb, cache_k, cache_v, bias, s0_t, g_att, g_rec, w_out_bf,
                  g1, nf, sc2, sh2, wr2, br, rows, row0):
    b, l, d = x.shape
    wc = cache_k.shape[1]
    lk = bias.shape[2]
    has_res = res is not None
    row = pl.BlockSpec((1, l, d), lambda i: (i, 0, 0))
    half = pl.BlockSpec((1, l, ATT_W), lambda i: (i, 0, 0))
    old = pl.BlockSpec((1, wc, ATT_W), lambda i: (i, 0, 0))
    per_b = pl.BlockSpec((1, 1, d), lambda i: (i, 0, 0))
    const = lambda shp: pl.BlockSpec(shp, lambda i: (0,) * len(shp))
    state = pl.BlockSpec((1, REC_HEADS, REC_DIM, REC_DIM), lambda i: (i, 0, 0, 0))
    in_specs, args = [row], [x]
    if has_res:
        yg, gate, g2, first = res
        blk0 = first // l
        in_specs += [pl.BlockSpec((TOP_K, l, d // 2), lambda i: (0, blk0 + i, 0)),
                     pl.BlockSpec((1, l, TOP_K), lambda i: (i, 0, 0)), per_b]
        args += [yg, gate, g2]
    in_specs += [const((1, d)), per_b, per_b, const(w_in_bf.shape), const((1, REC_W)), old, old, const(bias.shape),
                 state, const((1, ATT_W)), const((1, REC_W)), const(w_out_bf.shape), per_b, const((1, d)),
                 per_b, per_b, const(wr2.shape), const((N_EXPERTS, 1)), pl.BlockSpec(memory_space=pl.ANY)]
    args += [nw.reshape(1, d), sc1, sh1, w_in_bf, lb.reshape(1, REC_W), cache_k, cache_v, bias, s0_t,
             g_att.reshape(1, ATT_W), g_rec.reshape(1, REC_W), w_out_bf, g1, nf.reshape(1, d), sc2, sh2, wr2,
             br.reshape(N_EXPERTS, 1), rows]
    tile = lambda dt: pltpu.VMEM((l, ATT_W), dt)
    return pl.pallas_call(
        functools.partial(_sample_mixer_kernel, has_res),
        grid=(b,),
        in_specs=in_specs,
        out_specs=[row, pl.BlockSpec((l, d // 2), lambda i: (row0 // l + i, 0)),
                   pl.BlockSpec((1, N_EXPERTS, l), lambda i: (i, 0, 0)), half, half, state],
        out_shape=[jax.ShapeDtypeStruct((b, l, d), F32), jax.ShapeDtypeStruct(rows.shape, U32),
                   jax.ShapeDtypeStruct((b, N_EXPERTS, l), F32),
                   jax.ShapeDtypeStruct((b, l, ATT_W), F32), jax.ShapeDtypeStruct((b, l, ATT_W), F32),
                   jax.ShapeDtypeStruct((b, REC_HEADS, REC_DIM, REC_DIM), F32)],
        scratch_shapes=[tile(BF16), pltpu.VMEM((lk, ATT_W), BF16), pltpu.VMEM((lk, ATT_W), BF16),
                        pltpu.VMEM((ATT_HEADS // 2, 2 * l, lk), F32),
                        tile(F32), tile(F32), tile(F32), tile(BF16), tile(F32), tile(F32),
                        pltpu.VMEM((REC_HEADS, REC_DIM, REC_DIM), F32)],
        input_output_aliases={len(args) - 1: 1},
        compiler_params=_cparams("parallel"),
        name="sample_mixer",
    )(*args)


def _route_kernel(lg_ref, idx_ref, gate_ref, rank_ref, cnt_ref, run_ref):
    i = pl.program_id(0)
    t = ROUTE_TILE

    @pl.when(i == 0)
    def _():
        run_ref[...] = jnp.zeros_like(run_ref)

    row = lax.broadcasted_iota(I32, (N_EXPERTS, t), 0)
    earlier = (lax.broadcasted_iota(I32, (t, t), 0) < lax.broadcasted_iota(I32, (t, t), 1)).astype(BF16)
    run = run_ref[:, 0:1]
    for sb in range(lg_ref.shape[1] // t):
        cols = slice(sb * t, (sb + 1) * t)
        x = lg_ref[:, cols]
        vals, idxs, hits = [], [], []
        for _ in range(TOP_K):
            m = jnp.max(x, axis=0, keepdims=True)
            ik = jnp.min(jnp.where(x == m, row, N_EXPERTS), axis=0, keepdims=True)
            hit = row == ik
            x = jnp.where(hit, -jnp.inf, x)
            vals.append(m)
            idxs.append(ik)
            hits.append(hit)
        es = [jnp.exp(v - vals[0]) for v in vals]
        tot = es[0] + es[1] + es[2] + es[3]
        gate_ref[:, cols] = jnp.concatenate([e / tot for e in es] + [jnp.zeros((GATE_ROWS - TOP_K, t), F32)], axis=0)
        idx_ref[:, cols] = jnp.concatenate(idxs, axis=0)
        chosen = (hits[0] | hits[1] | hits[2] | hits[3])
        onehot = jnp.where(chosen, 1.0, 0.0)
        before = jnp.dot(onehot.astype(BF16), earlier, preferred_element_type=F32) + run
        rank_ref[:, cols] = jnp.concatenate(
            [jnp.sum(jnp.where(h, before, 0.0), axis=0, keepdims=True) for h in hits], axis=0).astype(I32)
        run = run + jnp.sum(onehot, axis=1, keepdims=True)
    run_ref[...] = jnp.broadcast_to(run, run_ref.shape)
    cnt_ref[...] = jnp.broadcast_to(run, cnt_ref.shape)


def _route(lg_t):
    e, n = lg_t.shape
    t = max(m * ROUTE_TILE for m in range(1, ROUTE_MAX_SUB + 1) if n % (m * ROUTE_TILE) == 0)
    tok = pl.BlockSpec((TOP_K, t), lambda i: (0, i))
    idx, gate, rank, cnt = pl.pallas_call(
        _route_kernel,
        grid=(n // t,),
        in_specs=[pl.BlockSpec((e, t), lambda i: (0, i))],
        out_specs=[tok, pl.BlockSpec((GATE_ROWS, t), lambda i: (0, i)), tok, pl.BlockSpec((e, LANES), lambda i: (0, 0))],
        out_shape=[jax.ShapeDtypeStruct((TOP_K, n), I32), jax.ShapeDtypeStruct((GATE_ROWS, n), F32),
                   jax.ShapeDtypeStruct((TOP_K, n), I32), jax.ShapeDtypeStruct((e, LANES), F32)],
        scratch_shapes=[pltpu.VMEM((e, LANES), F32)],
        compiler_params=_cparams("arbitrary"),
        name="moe_route",
    )(lg_t)
    counts = cnt[:, 0].astype(I32)
    padded = (counts + MOE_ROWS - 1) // MOE_ROWS * MOE_ROWS
    pad_end = jnp.cumsum(padded)
    pad_start = pad_end - padded
    m = n * TOP_K
    nb = (m + N_EXPERTS * (MOE_ROWS - 1) + MOE_ROWS - 1) // MOE_ROWS
    block_start = jnp.arange(nb, dtype=I32) * MOE_ROWS
    block_expert = jnp.minimum(jnp.sum((pad_end[None, :] <= block_start[:, None]).astype(I32), axis=1), N_EXPERTS - 1)
    experts = jnp.arange(N_EXPERTS, dtype=I32)
    in_block = block_start[:, None] - pad_start[None, :]
    mine = block_expert[:, None] == experts[None, :]
    filled = jnp.sum(jnp.where(mine, jnp.clip(counts[None, :] - in_block, 0, MOE_ROWS), 0), axis=1)
    owns = jnp.any(block_expert[:, None] == experts[None, :], axis=0)
    later = owns[None, :] & (experts[None, :] > block_expert[:, None])
    nxt = jnp.min(jnp.where(later, experts[None, :], N_EXPERTS), axis=1)
    nxt = jnp.where(nxt == N_EXPERTS, -1, nxt)
    run = jnp.sum((owns[None, :] & (experts[None, :] < block_expert[:, None])).astype(I32), axis=1)
    block_info = jnp.concatenate([block_expert, nxt, run % 2, filled]).astype(I32)
    start_of = sum(jnp.where(idx == e, pad_start[e], 0) for e in range(N_EXPERTS))
    dest = start_of + rank
    return gate, dest, block_info, nb * MOE_ROWS


def _windows(dest, w, multiple, fill):
    kk, n = dest.shape
    nwin = n // w
    per = -(-nwin // SC_WORKERS)
    per = -(-per // multiple) * multiple
    idx = dest.reshape(kk, nwin, w).transpose(1, 0, 2)
    pad = jnp.broadcast_to(fill[None, None, :], (SC_WORKERS * per - nwin, kk, w)).astype(I32)
    return jnp.concatenate([idx, pad], axis=0)


def _sc_scatter_rows(x, idx, p_rows):
    n, c = x.shape
    nwin, kk, w = idx.shape
    per = nwin // SC_WORKERS
    nreal = n // w
    mesh = plsc.VectorSubcoreMesh(core_axis_name="c", subcore_axis_name="s")

    @functools.partial(
        pl.kernel, mesh=mesh,
        out_type=jax.ShapeDtypeStruct((p_rows, c), x.dtype),
        scratch_types=[pltpu.VMEM((per, kk, w), I32), pltpu.VMEM((2, w, c), x.dtype),
                       pltpu.SemaphoreType.DMA((2,)), pltpu.SemaphoreType.DMA((2,))],
        name="sc_scatter_rows",
    )
    def k(x_hbm, idx_hbm, out_hbm, idx_v, rows_v, lsem, ssem):
        wid = lax.axis_index("s") * 2 + lax.axis_index("c")
        g0 = wid * per
        pltpu.sync_copy(idx_hbm.at[pl.ds(g0, per)], idx_v)

        def load(j, b):
            g = jnp.minimum(g0 + j, nreal - 1)
            return pltpu.make_async_copy(x_hbm.at[pl.ds(g * w, w)], rows_v.at[b], lsem.at[b])

        def scat(j, b, q):
            return pltpu.make_async_copy(rows_v.at[b], out_hbm.at[idx_v.at[j, q]], ssem.at[b])

        @pl.loop(0, per + 2, step=2)
        def _(j0):
            for b in range(2):
                j = j0 + b

                @pl.when(jnp.logical_and(j >= 2, j < per + 2))
                def _():
                    for q in range(kk):
                        scat(j - 2, b, q).wait()

                @pl.when(j < per)
                def _():
                    load(j, b).start()

                @pl.when(jnp.logical_and(j >= 1, j < per + 1))
                def _():
                    load(j - 1, 1 - b).wait()
                    for q in range(kk):
                        scat(j - 1, 1 - b, q).start()

    return k(x, idx)


def _sc_gather_rows(y, idx):
    p, c = y.shape
    kk, n = idx.shape
    w = SC_GATHER_WIN
    nb = SC_GATHER_BUFS
    wins = LANES // w
    group = wins * kk
    assert group % nb == 0 and group >= nb and n % LANES == 0
    n_units = n // LANES
    max_units = -(-n_units // SC_WORKERS)
    mesh = plsc.VectorSubcoreMesh(core_axis_name="c", subcore_axis_name="s")

    @functools.partial(
        pl.kernel, mesh=mesh,
        out_type=jax.ShapeDtypeStruct((kk, n, c), y.dtype),
        scratch_types=[pltpu.VMEM((2, kk, LANES), I32), pltpu.VMEM((nb, w, c), y.dtype),
                       pltpu.SemaphoreType.DMA((nb,)), pltpu.SemaphoreType.DMA((nb,))],
        name="sc_gather_rows",
    )
    def k(y_hbm, idx_hbm, out_hbm, idx_v, rows_v, gsem, wsem):
        wid = lax.axis_index("s") * 2 + lax.axis_index("c")
        my_units = (n_units - wid + SC_WORKERS - 1) // SC_WORKERS
        n_items = my_units * group

        def gath(slot, jj, q, b):
            return pltpu.make_async_copy(y_hbm.at[idx_v.at[slot, q, pl.ds(jj * w, w)]], rows_v.at[b], gsem.at[b])

        def wr(i, b):
            row = (wid + (i // group) * SC_WORKERS) * LANES + ((i % group) // kk) * w
            return pltpu.make_async_copy(rows_v.at[b], out_hbm.at[i % kk, pl.ds(row, w)], wsem.at[b])

        @pl.loop(0, max_units + 1)
        def _(gi):
            slot = gi % 2

            @pl.when(gi < my_units)
            def _():
                pltpu.sync_copy(idx_hbm.at[:, pl.ds((wid + gi * SC_WORKERS) * LANES, LANES)], idx_v.at[slot])

            for l in range(group):
                i = gi * group + l
                b = l % nb
                bw = (l - (nb - 1)) % nb

                @pl.when(jnp.logical_and(i >= nb, i < n_items + nb))
                def _():
                    wr(i - nb, b).wait()

                @pl.when(i < n_items)
                def _():
                    gath(slot, l // kk, l % kk, b).start()

                @pl.when(jnp.logical_and(i >= nb - 1, i < n_items + nb - 1))
                def _():
                    gath(slot, 0, 0, bw).wait()
                    wr(i - (nb - 1), bw).start()

    return k(y, idx)


def _moe_kernel(layer, info_ref, x_ref, w1_hbm, b1_ref, w2_hbm, b2_ref, o_ref,
                w1f_ref, w2f_ref, w1b_ref, w2b_ref, sem):
    i = pl.program_id(0)
    nb = pl.num_programs(0)
    expert = info_ref[i]
    nxt = info_ref[nb + i]
    slot = info_ref[2 * nb + i]
    first = jnp.logical_or(i == 0, expert != info_ref[jnp.maximum(i - 1, 0)])

    def fetch(e, s):
        return (pltpu.make_async_copy(w1_hbm.at[layer, e], w1f_ref.at[s], sem.at[0, s]),
                pltpu.make_async_copy(w2_hbm.at[layer, e], w2f_ref.at[s], sem.at[1, s]))

    @pl.when(i == 0)
    def _():
        for cp in fetch(expert, slot):
            cp.start()

    @pl.when(first)
    def _():
        for cp in fetch(expert, slot):
            cp.wait()

        @pl.when(nxt >= 0)
        def _():
            for cp in fetch(nxt, 1 - slot):
                cp.start()

        w1b_ref[...] = w1f_ref[slot].astype(BF16)
        w2b_ref[...] = w2f_ref[slot].astype(BF16)

    units = (info_ref[3 * nb + i] + MOE_UNIT - 1) // MOE_UNIT
    for u in range(1, x_ref.shape[0] // MOE_UNIT + 1):
        @pl.when(units == u)
        def _():
            rows = slice(0, u * MOE_UNIT)
            _expert_block(x_ref.at[rows], w1b_ref, b1_ref, w2b_ref, b2_ref, o_ref.at[rows])


def _expert_block(x_ref, w1b_ref, b1_ref, w2b_ref, b2_ref, o_ref):
    lo, hi = _unpack_pairs(x_ref[...])
    x = jnp.concatenate([lo.astype(BF16), hi.astype(BF16)], axis=-1)
    f = w2b_ref.shape[0]
    fc = 512
    acc = None
    for c in range(f // fc):
        gt = jnp.dot(x, w1b_ref[:, c * fc:(c + 1) * fc], preferred_element_type=F32) + b1_ref[0, 0, :, c * fc:(c + 1) * fc]
        up = jnp.dot(x, w1b_ref[:, f + c * fc:f + (c + 1) * fc], preferred_element_type=F32) + b1_ref[0, 0, :, f + c * fc:f + (c + 1) * fc]
        gt = jnp.minimum(gt, SWIGLU_LIMIT)
        up = jnp.clip(up, -SWIGLU_LIMIT, SWIGLU_LIMIT)
        act = (gt * jax.nn.sigmoid(SWIGLU_ALPHA * gt) * (up + 1.0)).astype(BF16)
        part = jnp.dot(act, w2b_ref[c * fc:(c + 1) * fc, :], preferred_element_type=F32)
        acc = part if acc is None else acc + part
    o_ref[...] = _pack_pairs(acc + b2_ref[0, 0])


def _moe_experts(layer, block_expert, xs, p_rows, w1, b1, w2, b2):
    _, e, d, f2 = w1.shape
    f = f2 // 2
    nb = p_rows // MOE_ROWS
    grid_spec = pltpu.PrefetchScalarGridSpec(
        num_scalar_prefetch=1,
        grid=(nb,),
        in_specs=[pl.BlockSpec((MOE_ROWS, d // 2), lambda i, be: (i, 0)),
                  pl.BlockSpec(memory_space=pl.ANY),
                  pl.BlockSpec((1, 1, 1, f2), lambda i, be: (layer, be[i], 0, 0)),
                  pl.BlockSpec(memory_space=pl.ANY),
                  pl.BlockSpec((1, 1, 1, d), lambda i, be: (layer, be[i], 0, 0))],
        out_specs=pl.BlockSpec((MOE_ROWS, d // 2), lambda i, be: (i, 0)),
        scratch_shapes=[pltpu.VMEM((2, d, f2), F32), pltpu.VMEM((2, f, d), F32),
                        pltpu.VMEM((d, f2), BF16), pltpu.VMEM((f, d), BF16),
                        pltpu.SemaphoreType.DMA((2, 2))],
    )
    return pl.pallas_call(
        functools.partial(_moe_kernel, layer),
        grid_spec=grid_spec,
        out_shape=jax.ShapeDtypeStruct((p_rows, d // 2), U32),
        compiler_params=_cparams("arbitrary"),
        name="moe_experts",
    )(block_expert, xs, w1, b1.reshape(b1.shape[0], e, 1, f2), w2, b2.reshape(b2.shape[0], e, 1, d))


def _moe_dispatch(h2_words, lg_t):
    gate, dest, block_info, p_rows = _route(lg_t)
    spare = p_rows + jnp.arange(SC_SCATTER_WIN, dtype=I32)
    xs = _sc_scatter_rows(h2_words, _windows(dest, SC_SCATTER_WIN, 2, spare), p_rows + SC_SCATTER_WIN)
    return xs, gate, dest, block_info, p_rows


def _moe_compute(layer, dispatched, w1, b1, w2, b2):
    xs, gate, back, block_info, p_rows = dispatched
    ys = _moe_experts(layer, block_info, xs, p_rows, w1, b1, w2, b2)
    return _sc_gather_rows(ys, back), gate


def _final_kernel(x_ref, yg_ref, gate_ref, g2_ref, nw_ref, *rest):
    o_ref = rest[-1]
    x = x_ref[0] + g2_ref[0] * _combine(yg_ref, gate_ref)
    ms = jnp.mean(x * x, axis=-1, keepdims=True)
    o_ref[0] = x * lax.rsqrt(ms + EPS) * nw_ref[...]


def _final(x, res, nw, ts, out_buf, ob0, b_total):
    b, s, d = x.shape
    nj = s // ts
    yg, gate, g2, first = res
    blk0 = first // ts
    row = pl.BlockSpec((1, ts, d), lambda i, j: (i, j, 0))
    if gate.ndim == 2:
        gate_spec = pl.BlockSpec((GATE_ROWS, ts), lambda i, j: (0, blk0 + i * nj + j))
    else:
        gate_spec = pl.BlockSpec((1, ts, TOP_K), lambda i, j: (i, j, 0))
    in_specs = [row, pl.BlockSpec((TOP_K, ts, d // 2), lambda i, j: (0, blk0 + i * nj + j, 0)),
                gate_spec,
                pl.BlockSpec((1, 1, d), lambda i, j: (i, 0, 0)),
                pl.BlockSpec((1, d), lambda i, j: (0, 0))]
    args = [x, yg, gate, g2, nw.reshape(1, d)]
    aliases = {}
    if out_buf is not None:
        in_specs.append(pl.BlockSpec(memory_space=pl.ANY))
        args.append(out_buf)
        aliases = {len(args) - 1: 0}
    return pl.pallas_call(
        _final_kernel,
        grid=(b, nj),
        in_specs=in_specs,
        out_specs=pl.BlockSpec((1, ts, d), lambda i, j: (i + ob0, j, 0)),
        out_shape=jax.ShapeDtypeStruct((b_total, s, d), F32),
        input_output_aliases=aliases,
        compiler_params=_cparams("parallel", "parallel"),
        name="final_norm",
    )(*args)


def kernel(x_prompt, x_sample, c_prompt, c_sample, cache_k, cache_v, state_hgrn, norm_mix, norm_ffn, norm_final, w_ada, b_ada, w_in, rel_bias, hgrn_lb_logits, g_attn_out, g_hgrn_out, w_out, w_router, b_router, w_e1, b_e1, w_e2, b_e2):
    depth = w_in.shape[0]
    bp, sp, d = x_prompt.shape
    bs, ls, _ = x_sample.shape
    wc = cache_k.shape[2]
    n_p, n_s = bp * sp, bs * ls

    lbs = jax.nn.softmax(hgrn_lb_logits.astype(F32), axis=0)
    lbs = jnp.cumsum(lbs, axis=0) - lbs[0]

    mods = _modulation(jnp.concatenate([c_prompt, c_sample], axis=0), w_ada, b_ada)
    lk_s = -(-(wc + ls) // LANES) * LANES
    tail = min(LEFT, sp)
    rec_chunk = min(REC_CHUNK, sp)

    n_groups = PROMPT_GROUPS if bp % PROMPT_GROUPS == 0 else 1
    gb = bp // n_groups
    n_g = gb * sp
    x_g = [(x_prompt, g * gb) for g in range(n_groups)]
    x_s = x_sample
    res_g = [None] * n_groups
    res_s = None
    kp_l, vp_l, sp_l, ks_l, vs_l, ss_l = [], [], [], [], [], []
    for layer in range(depth):
        m6 = mods[layer].reshape(bp + bs, 1, N_MOD, d)
        mod_s = [m6[bp:, :, n, :] for n in range(N_MOD)]
        w_in_bf = w_in[layer].astype(BF16)
        w_out_bf = w_out[layer].astype(BF16)
        wr_hi, wr_mid, _ = _split3(w_router[layer].T)
        wr2 = jnp.concatenate([wr_hi, wr_mid], axis=0)
        table = rel_bias[layer]
        bias_p = _prompt_bias(table)

        def mixer_prompt(g):
            mod = [m6[g * gb:(g + 1) * gb, :, n, :] for n in range(N_MOD)]
            x_new, h2, lg, kt, vt, s_fin = _mixer_prompt(
                x_g[g][0], x_g[g][1], res_g[g], norm_mix[layer], mod[1], mod[0], w_in_bf, lbs[layer], bias_p,
                g_attn_out[layer], g_hgrn_out[layer], w_out_bf, mod[2], norm_ffn[layer], mod[4], mod[3],
                wr2, b_router[layer], tail, rec_chunk, n_s if g == n_groups - 1 else 0)
            x_g[g] = (x_new, 0)
            return h2, lg, kt, vt, s_fin, mod[5]

        def mixer_sample(rows):
            s0 = jnp.swapaxes(state_hgrn[layer].astype(F32), -1, -2)
            x_new, rows, lg, kt, vt, s_new = _mixer_sample(
                x_s, res_s, norm_mix[layer], mod_s[1], mod_s[0], w_in_bf, lbs[layer],
                cache_k[layer].reshape(bs, wc, ATT_W), cache_v[layer].reshape(bs, wc, ATT_W),
                _sample_bias(table, ls, wc, lk_s), s0, g_attn_out[layer], g_hgrn_out[layer], w_out_bf,
                mod_s[2], norm_ffn[layer], mod_s[4], mod_s[3], wr2, b_router[layer], rows, n_g)
            return x_new, rows, jnp.swapaxes(lg, 0, 1).reshape(N_EXPERTS, n_s), kt, vt, s_new

        dispatched, g2s, k_t, v_t, s_t = [], [], [], [], []
        for g in range(n_groups):
            h2, lg, kt, vt, s_fin, g2 = mixer_prompt(g)
            if g == n_groups - 1:
                x_s, h2, lg_s, ka_s, va_s, s_new = mixer_sample(h2)
                ks_l.append(ka_s.reshape(bs, ls, ATT_HEADS, ATT_DIM))
                vs_l.append(va_s.reshape(bs, ls, ATT_HEADS, ATT_DIM))
                ss_l.append(jnp.swapaxes(s_new, -1, -2))
                lg = jnp.concatenate([lg, lg_s], axis=1)
            dispatched.append(_moe_dispatch(h2, lg))
            g2s.append(g2)
            k_t.append(kt)
            v_t.append(vt)
            s_t.append(s_fin)
        kp_l.append(jnp.concatenate(k_t, axis=0).reshape(bp, tail, ATT_HEADS, ATT_DIM))
        vp_l.append(jnp.concatenate(v_t, axis=0).reshape(bp, tail, ATT_HEADS, ATT_DIM))
        sp_l.append(jnp.swapaxes(jnp.concatenate(s_t, axis=0), -1, -2))

        order = range(n_groups) if layer < depth - 1 else reversed(range(n_groups))
        for g in order:
            yg, gate = _moe_compute(layer, dispatched[g], w_e1, b_e1, w_e2, b_e2)
            res_g[g] = (yg, gate, g2s[g], 0)
            if g == n_groups - 1:
                res_s = (yg, gate[:TOP_K, n_g:].T.reshape(bs, ls, TOP_K), mod_s[5], n_g)

    y_prompt = None
    y_sample = _final(x_s, res_s, norm_final, ls, None, 0, bs)
    for g in reversed(range(n_groups)):
        y_prompt = _final(x_g[g][0], res_g[g], norm_final, SEQ_TILE, y_prompt, g * gb, bp)
    return (y_prompt, y_sample, jnp.stack(kp_l), jnp.stack(vp_l), jnp.stack(sp_l),
            jnp.stack(ks_l), jnp.stack(vs_l), jnp.stack(ss_l))
```
